```python
import jax, jax.numpy as jnp
from jax import lax
import numpy as np

D_MODEL = 2048
BATCH = 8
SEQ = 8192
DEPTH = 2

N_META = 16
BLOCK = 128
META_PAD = BLOCK - N_META
SB_HEADS = 8
SB_HEAD_DIM = 128
SB_W = SB_HEADS * SB_HEAD_DIM
MLA_HEADS = 8
Q_LORA = 512
KV_LORA = 256
NOPE_DIM = 128
ROPE_DIM = 64
V_DIM = 128
MLA_W = MLA_HEADS * V_DIM
ROPE_THETA = 10000.0
D_FF = 5632
CONV_W = 3
EPS = 1e-6
IN_WIDTHS = (SB_W, SB_W, SB_W, Q_LORA, KV_LORA, ROPE_DIM, D_MODEL, D_MODEL)
D_IN = SB_W * 3 + Q_LORA + KV_LORA + ROPE_DIM + 2 * D_MODEL

kernel_name = "hybrid_stickbreak_mla_convglu"


def _rmsnorm(x, g):
    xf = x.astype(jnp.float32)
    y = xf * lax.rsqrt(jnp.mean(xf * xf, axis=-1, keepdims=True) + EPS)
    return (y * g.astype(jnp.float32)).astype(x.dtype)


def _split(t, widths):
    pieces, off = [], 0
    for w in widths:
        pieces.append(t[..., off:off + w])
        off += w
    return pieces


def _heads(t, n_heads):
    b, l, _ = t.shape
    return t.reshape(b, l, n_heads, -1).transpose(0, 2, 1, 3)


def _merge_heads(t):
    b, h, l, d = t.shape
    return t.transpose(0, 2, 1, 3).reshape(b, l, h * d)


def _rope(t, pos):
    half = t.shape[-1] // 2
    freqs = ROPE_THETA ** (-jnp.arange(half, dtype=jnp.float32) / half)
    ang = pos.astype(jnp.float32)[:, None] * freqs[None, :]
    cos, sin = jnp.cos(ang), jnp.sin(ang)
    tf = t.astype(jnp.float32)
    t1, t2 = tf[..., :half], tf[..., half:]
    return jnp.concatenate([t1 * cos - t2 * sin, t1 * sin + t2 * cos], axis=-1).astype(t.dtype)


def _pad_seq(t, axis):
    widths = [(0, 0)] * t.ndim
    widths[axis] = (META_PAD, 0)
    return jnp.pad(t, widths)


def _to_blocks(t):
    b, h, lp, d = t.shape
    return t.reshape(b, h, lp // BLOCK, BLOCK, d).transpose(2, 0, 1, 3, 4)


def _from_blocks(t):
    nb, b, h, blk, d = t.shape
    return t.transpose(1, 2, 0, 3, 4).reshape(b, h, nb * blk, d)


def _stick_breaking_attn(q, k, v):
    qp, kp, vp = _pad_seq(q, 2), _pad_seq(k, 2), _pad_seq(v, 2)
    lp = qp.shape[2]
    k_idx = jnp.arange(lp)
    kf = kp.astype(jnp.float32)
    scale = SB_HEAD_DIM ** -0.5

    def one_block(args):
        q_blk, blk = args
        q_idx = blk * BLOCK + jnp.arange(BLOCK)
        logits = jnp.einsum('bhqd,bhkd->bhqk', q_blk.astype(jnp.float32), kf) * scale
        mask = (k_idx[None, :] < q_idx[:, None]) & (k_idx[None, :] >= META_PAD)
        log_1m_beta = jnp.where(mask, jax.nn.log_sigmoid(-logits), 0.0)
        later = lax.cumsum(log_1m_beta, axis=3, reverse=True) - log_1m_beta
        w = jnp.where(mask, jnp.exp(jax.nn.log_sigmoid(logits) + later), 0.0)
        return jnp.einsum('bhqk,bhkd->bhqd', w.astype(vp.dtype), vp)

    out = lax.map(one_block, (_to_blocks(qp), jnp.arange(lp // BLOCK)))
    return _from_blocks(out)[:, :, META_PAD:]


def _mla_attn(q_nope, q_rope, k_nope, k_rope, v):
    qn, qr = _pad_seq(q_nope, 2), _pad_seq(q_rope, 2)
    knf = _pad_seq(k_nope, 2).astype(jnp.float32)
    krf = _pad_seq(k_rope, 1).astype(jnp.float32)
    vp = _pad_seq(v, 2)
    lp = qn.shape[2]
    k_idx = jnp.arange(lp)
    scale = (NOPE_DIM + ROPE_DIM) ** -0.5

    def one_block(args):
        qn_blk, qr_blk, blk = args
        q_idx = blk * BLOCK + jnp.arange(BLOCK)
        s = (jnp.einsum('bhqd,bhkd->bhqk', qn_blk.astype(jnp.float32), knf)
             + jnp.einsum('bhqr,bkr->bhqk', qr_blk.astype(jnp.float32), krf)) * scale
        mask = (k_idx[None, :] <= q_idx[:, None]) & (k_idx[None, :] >= META_PAD)
        p = jax.nn.softmax(jnp.where(mask, s, -1e30), axis=-1)
        return jnp.einsum('bhqk,bhkd->bhqd', p.astype(vp.dtype), vp)

    out = lax.map(one_block, (_to_blocks(qn), _to_blocks(qr), jnp.arange(lp // BLOCK)))
    return _from_blocks(out)[:, :, META_PAD:]


def _causal_dwconv(u, w, b):
    l = u.shape[1]
    up = jnp.pad(u, ((0, 0), (CONV_W - 1, 0), (0, 0)))
    out = b
    for i in range(CONV_W):
        out = out + up[:, i:i + l] * w[i]
    return out


def _mixer(x, norm_mix, w_in, q_norm, w_uq, kv_norm, w_ukv, w_sb_out, w_mla_out, w_o):
    b, l, _ = x.shape
    h = _rmsnorm(x, norm_mix)
    proj = h @ w_in
    q_sb, k_sb, v_sb, c_q, c_kv, k_rope, g_sb, g_mla = _split(proj, IN_WIDTHS)
    pos = jnp.arange(l)
    o_sb = _stick_breaking_attn(_heads(q_sb, SB_HEADS), _heads(k_sb, SB_HEADS), _heads(v_sb, SB_HEADS))
    branch_sb = _merge_heads(o_sb) @ w_sb_out
    q = (_rmsnorm(c_q, q_norm) @ w_uq).reshape(b, l, MLA_HEADS, NOPE_DIM + ROPE_DIM).transpose(0, 2, 1, 3)
    q_nope, q_rope = q[..., :NOPE_DIM], _rope(q[..., NOPE_DIM:], pos)
    kv = (_rmsnorm(c_kv, kv_norm) @ w_ukv).reshape(b, l, MLA_HEADS, NOPE_DIM + V_DIM).transpose(0, 2, 1, 3)
    k_nope, v = kv[..., :NOPE_DIM], kv[..., NOPE_DIM:]
    o_mla = _mla_attn(q_nope, q_rope, k_nope, _rope(k_rope, pos), v)
    branch_mla = _merge_heads(o_mla) @ w_mla_out
    merged = jax.nn.sigmoid(g_sb) * branch_sb + jax.nn.sigmoid(g_mla) * branch_mla
    return merged @ w_o


def _conv_ffn(x, norm_ffn, w_up, conv_w, conv_b, w_down):
    h = _rmsnorm(x, norm_ffn)
    u = _causal_dwconv(h @ w_up, conv_w, conv_b)
    a, g = u[..., :D_FF], u[..., D_FF:]
    return (jax.nn.silu(a) * g) @ w_down


def _fwd_setup_inputs(seed: int = 0) -> dict:
    key = jax.random.key(seed)
    ks = jax.random.split(key, 20)
    f32 = jnp.float32

    def nrm(k, shape, scale):
        return jax.random.normal(k, shape, f32) * scale

    def gain(k, shape):
        return 1.0 + 0.02 * jax.random.normal(k, shape, f32)

    return {
        "x": nrm(ks[0], (BATCH, SEQ, D_MODEL), 1.0),
        "meta_tokens": nrm(ks[1], (N_META, D_MODEL), 1.0),
        "norm_mix": gain(ks[2], (DEPTH, D_MODEL)),
        "w_in": nrm(ks[3], (DEPTH, D_MODEL, D_IN), D_MODEL ** -0.5),
        "q_norm": gain(ks[4], (DEPTH, Q_LORA)),
        "w_uq": nrm(ks[5], (DEPTH, Q_LORA, MLA_HEADS * (NOPE_DIM + ROPE_DIM)), Q_LORA ** -0.5),
        "kv_norm": gain(ks[6], (DEPTH, KV_LORA)),
        "w_ukv": nrm(ks[7], (DEPTH, KV_LORA, MLA_HEADS * (NOPE_DIM + V_DIM)), KV_LORA ** -0.5),
        "w_sb_out": nrm(ks[8], (DEPTH, SB_W, D_MODEL), SB_W ** -0.5),
        "w_mla_out": nrm(ks[9], (DEPTH, MLA_W, D_MODEL), MLA_W ** -0.5),
        "w_o": nrm(ks[10], (DEPTH, D_MODEL, D_MODEL), D_MODEL ** -0.5),
        "norm_ffn": gain(ks[11], (DEPTH, D_MODEL)),
        "w_up": nrm(ks[12], (DEPTH, D_MODEL, 2 * D_FF), D_MODEL ** -0.5),
        "conv_w": nrm(ks[13], (DEPTH, CONV_W, 2 * D_FF), CONV_W ** -0.5),
        "conv_b": nrm(ks[14], (DEPTH, 2 * D_FF), 0.02),
        "w_down": nrm(ks[15], (DEPTH, D_FF, D_MODEL), D_FF ** -0.5),
        "final_norm": gain(ks[16], (D_MODEL,)),
    }


def _fwd_reference(x, meta_tokens, norm_mix, w_in, q_norm, w_uq, kv_norm, w_ukv, w_sb_out, w_mla_out,
              w_o, norm_ffn, w_up, conv_w, conv_b, w_down, final_norm):
    b = x.shape[0]
    meta = jnp.broadcast_to(meta_tokens[None].astype(x.dtype), (b, N_META, x.shape[-1]))
    h = jnp.concatenate([meta, x], axis=1)
    for i in range(DEPTH):
        h = h + _mixer(h, norm_mix[i], w_in[i], q_norm[i], w_uq[i], kv_norm[i], w_ukv[i],
                       w_sb_out[i], w_mla_out[i], w_o[i])
        h = h + _conv_ffn(h, norm_ffn[i], w_up[i], conv_w[i], conv_b[i], w_down[i])
    return _rmsnorm(h[:, N_META:], final_norm)


import jax as _jax
import jax.numpy as _jnp

TWIN_FORMAT = 'train_step'
FWD_PARAMS = ['x', 'meta_tokens', 'norm_mix', 'w_in', 'q_norm', 'w_uq', 'kv_norm', 'w_ukv', 'w_sb_out', 'w_mla_out', 'w_o', 'norm_ffn', 'w_up', 'conv_w', 'conv_b', 'w_down', 'final_norm']
TWIN_WEIGHTS = ['meta_tokens', 'norm_mix', 'w_in', 'q_norm', 'w_uq', 'kv_norm', 'w_ukv', 'w_sb_out', 'w_mla_out', 'w_o', 'norm_ffn', 'w_up', 'conv_w', 'conv_b', 'w_down', 'final_norm']
TWIN_DIFF_INPUT = 'x'
TWIN_INPUTS = ['x', 'meta_tokens', 'norm_mix', 'w_in', 'q_norm', 'w_uq', 'kv_norm', 'w_ukv', 'w_sb_out', 'w_mla_out', 'w_o', 'norm_ffn', 'w_up', 'conv_w', 'conv_b', 'w_down', 'final_norm', 'loss_target', 'm_meta_tokens', 'm_norm_mix', 'm_w_in', 'm_q_norm', 'm_w_uq', 'm_kv_norm', 'm_w_ukv', 'm_w_sb_out', 'm_w_mla_out', 'm_w_o', 'm_norm_ffn', 'm_w_up', 'm_conv_w', 'm_conv_b', 'm_w_down', 'm_final_norm', 'v_meta_tokens', 'v_norm_mix', 'v_w_in', 'v_q_norm', 'v_w_uq', 'v_kv_norm', 'v_w_ukv', 'v_w_sb_out', 'v_w_mla_out', 'v_w_o', 'v_norm_ffn', 'v_w_up', 'v_conv_w', 'v_conv_b', 'v_w_down', 'v_final_norm']
TWIN_OUTPUTS = ['loss', 'grad_x', 'grad_meta_tokens', 'grad_norm_mix', 'grad_w_in', 'grad_q_norm', 'grad_w_uq', 'grad_kv_norm', 'grad_w_ukv', 'grad_w_sb_out', 'grad_w_mla_out', 'grad_w_o', 'grad_norm_ffn', 'grad_w_up', 'grad_conv_w', 'grad_conv_b', 'grad_w_down', 'grad_final_norm', 'delta_meta_tokens', 'delta_norm_mix', 'delta_w_in', 'delta_q_norm', 'delta_w_uq', 'delta_kv_norm', 'delta_w_ukv', 'delta_w_sb_out', 'delta_w_mla_out', 'delta_w_o', 'delta_norm_ffn', 'delta_w_up', 'delta_conv_w', 'delta_conv_b', 'delta_w_down', 'delta_final_norm', 'new_m_meta_tokens', 'new_m_norm_mix', 'new_m_w_in', 'new_m_q_norm', 'new_m_w_uq', 'new_m_kv_norm', 'new_m_w_ukv', 'new_m_w_sb_out', 'new_m_w_mla_out', 'new_m_w_o', 'new_m_norm_ffn', 'new_m_w_up', 'new_m_conv_w', 'new_m_conv_b', 'new_m_w_down', 'new_m_final_norm', 'new_v_meta_tokens', 'new_v_norm_mix', 'new_v_w_in', 'new_v_q_norm', 'new_v_w_uq', 'new_v_kv_norm', 'new_v_w_ukv', 'new_v_w_sb_out', 'new_v_w_mla_out', 'new_v_w_o', 'new_v_norm_ffn', 'new_v_w_up', 'new_v_conv_w', 'new_v_conv_b', 'new_v_w_down', 'new_v_final_norm']
TWIN_LEAF_KINDS = {'loss': 'loss', 'grad_x': 'grad_x', 'grad_meta_tokens': 'grad_w', 'grad_norm_mix': 'grad_w', 'grad_w_in': 'grad_w', 'grad_q_norm': 'grad_w', 'grad_w_uq': 'grad_w', 'grad_kv_norm': 'grad_w', 'grad_w_ukv': 'grad_w', 'grad_w_sb_out': 'grad_w', 'grad_w_mla_out': 'grad_w', 'grad_w_o': 'grad_w', 'grad_norm_ffn': 'grad_w', 'grad_w_up': 'grad_w', 'grad_conv_w': 'grad_w', 'grad_conv_b': 'grad_w', 'grad_w_down': 'grad_w', 'grad_final_norm': 'grad_w', 'delta_meta_tokens': 'delta_w', 'delta_norm_mix': 'delta_w', 'delta_w_in': 'delta_w', 'delta_q_norm': 'delta_w', 'delta_w_uq': 'delta_w', 'delta_kv_norm': 'delta_w', 'delta_w_ukv': 'delta_w', 'delta_w_sb_out': 'delta_w', 'delta_w_mla_out': 'delta_w', 'delta_w_o': 'delta_w', 'delta_norm_ffn': 'delta_w', 'delta_w_up': 'delta_w', 'delta_conv_w': 'delta_w', 'delta_conv_b': 'delta_w', 'delta_w_down': 'delta_w', 'delta_final_norm': 'delta_w', 'new_m_meta_tokens': 'new_m', 'new_m_norm_mix': 'new_m', 'new_m_w_in': 'new_m', 'new_m_q_norm': 'new_m', 'new_m_w_uq': 'new_m', 'new_m_kv_norm': 'new_m', 'new_m_w_ukv': 'new_m', 'new_m_w_sb_out': 'new_m', 'new_m_w_mla_out': 'new_m', 'new_m_w_o': 'new_m', 'new_m_norm_ffn': 'new_m', 'new_m_w_up': 'new_m', 'new_m_conv_w': 'new_m', 'new_m_conv_b': 'new_m', 'new_m_w_down': 'new_m', 'new_m_final_norm': 'new_m', 'new_v_meta_tokens': 'new_v', 'new_v_norm_mix': 'new_v', 'new_v_w_in': 'new_v', 'new_v_q_norm': 'new_v', 'new_v_w_uq': 'new_v', 'new_v_kv_norm': 'new_v', 'new_v_w_ukv': 'new_v', 'new_v_w_sb_out': 'new_v', 'new_v_w_mla_out': 'new_v', 'new_v_w_o': 'new_v', 'new_v_norm_ffn': 'new_v', 'new_v_w_up': 'new_v', 'new_v_conv_w': 'new_v', 'new_v_conv_b': 'new_v', 'new_v_w_down': 'new_v', 'new_v_final_norm': 'new_v'}


def _forward(args):
    return _fwd_reference(*[args[k] for k in FWD_PARAMS])


def _output_shape():
    def fwd():
        inp = _fwd_setup_inputs(0)
        return _fwd_reference(*[inp[k] for k in FWD_PARAMS])
    out = _jax.eval_shape(fwd)
    return out.shape, out.dtype

N_MICROBATCH = 1
ADAM_LR = 0.001
ADAM_B1 = 0.9
ADAM_B2 = 0.999
ADAM_EPS = 1e-08
ADAM_WD = 0.01
ADAM_STEP = 10
PER_EXAMPLE_BATCH_AXIS = {'x': 0, 'loss_target': 0}
SHARED_INPUTS = []
_WEIGHT_DTYPES = {'meta_tokens': _jnp.float32, 'norm_mix': _jnp.float32, 'w_in': _jnp.float32, 'q_norm': _jnp.float32, 'w_uq': _jnp.float32, 'kv_norm': _jnp.float32, 'w_ukv': _jnp.float32, 'w_sb_out': _jnp.float32, 'w_mla_out': _jnp.float32, 'w_o': _jnp.float32, 'norm_ffn': _jnp.float32, 'w_up': _jnp.float32, 'conv_w': _jnp.float32, 'conv_b': _jnp.float32, 'w_down': _jnp.float32, 'final_norm': _jnp.float32}
MOMENT_SCALE = {'meta_tokens': 4.270305e-03, 'norm_mix': 6.087424e-02, 'w_in': 3.078709e-02, 'q_norm': 2.297812e-02, 'w_uq': 1.316459e-02, 'kv_norm': 4.876842e-02, 'w_ukv': 1.599815e-02, 'w_sb_out': 4.582635e-02, 'w_mla_out': 1.277572e-02, 'w_o': 4.735648e-02, 'norm_ffn': 9.332289e-02, 'w_up': 3.879568e-02, 'conv_w': 3.860422e-02, 'conv_b': 3.817222e-02, 'w_down': 6.331674e-02, 'final_norm': 3.197890e+01}


def _to_microbatches(a, axis):
    t = _jnp.moveaxis(a, axis, 0)
    t = t.reshape((N_MICROBATCH, t.shape[0] // N_MICROBATCH) + t.shape[1:])
    return _jnp.moveaxis(t, 1, axis + 1)


def setup_inputs(seed: int = 0) -> dict:
    inp = _fwd_setup_inputs(seed)
    key = _jax.random.fold_in(_jax.random.key(seed), 7919)
    shape, _ = _output_shape()
    out = dict(inp)
    out["loss_target"] = _jax.random.normal(_jax.random.fold_in(key, 0), shape, _jnp.float32)
    for i, name in enumerate(TWIN_WEIGHTS):
        w = inp[name].astype(_jnp.float32)
        if MOMENT_SCALE is None:
            s = _jnp.sqrt(_jnp.mean(_jnp.square(w)) + 1e-30)
        else:
            s = MOMENT_SCALE[name]
        km, kv = _jax.random.split(_jax.random.fold_in(key, i + 1))
        out[name] = w
        out["m_" + name] = s * _jax.random.normal(km, w.shape, _jnp.float32)
        out["v_" + name] = (s * s) * _jax.random.uniform(kv, w.shape, _jnp.float32, 0.5, 1.5)
    if N_MICROBATCH > 1:
        for name, axis in PER_EXAMPLE_BATCH_AXIS.items():
            out[name] = _to_microbatches(out[name], axis)
    return {'x': out['x'], 'meta_tokens': out['meta_tokens'], 'norm_mix': out['norm_mix'], 'w_in': out['w_in'], 'q_norm': out['q_norm'], 'w_uq': out['w_uq'], 'kv_norm': out['kv_norm'], 'w_ukv': out['w_ukv'], 'w_sb_out': out['w_sb_out'], 'w_mla_out': out['w_mla_out'], 'w_o': out['w_o'], 'norm_ffn': out['norm_ffn'], 'w_up': out['w_up'], 'conv_w': out['conv_w'], 'conv_b': out['conv_b'], 'w_down': out['w_down'], 'final_norm': out['final_norm'], 'loss_target': out['loss_target'], 'm_meta_tokens': out['m_meta_tokens'], 'm_norm_mix': out['m_norm_mix'], 'm_w_in': out['m_w_in'], 'm_q_norm': out['m_q_norm'], 'm_w_uq': out['m_w_uq'], 'm_kv_norm': out['m_kv_norm'], 'm_w_ukv': out['m_w_ukv'], 'm_w_sb_out': out['m_w_sb_out'], 'm_w_mla_out': out['m_w_mla_out'], 'm_w_o': out['m_w_o'], 'm_norm_ffn': out['m_norm_ffn'], 'm_w_up': out['m_w_up'], 'm_conv_w': out['m_conv_w'], 'm_conv_b': out['m_conv_b'], 'm_w_down': out['m_w_down'], 'm_final_norm': out['m_final_norm'], 'v_meta_tokens': out['v_meta_tokens'], 'v_norm_mix': out['v_norm_mix'], 'v_w_in': out['v_w_in'], 'v_q_norm': out['v_q_norm'], 'v_w_uq': out['v_w_uq'], 'v_kv_norm': out['v_kv_norm'], 'v_w_ukv': out['v_w_ukv'], 'v_w_sb_out': out['v_w_sb_out'], 'v_w_mla_out': out['v_w_mla_out'], 'v_w_o': out['v_w_o'], 'v_norm_ffn': out['v_norm_ffn'], 'v_w_up': out['v_w_up'], 'v_conv_w': out['v_conv_w'], 'v_conv_b': out['v_conv_b'], 'v_w_down': out['v_w_down'], 'v_final_norm': out['v_final_norm']}


def _loss(weights, diff, rest, loss_target):
    with _jax.named_scope("forward"):
        args = {**rest, TWIN_DIFF_INPUT: diff, **{k: w.astype(_WEIGHT_DTYPES[k]) for k, w in weights.items()}}
        y = _forward(args)
    with _jax.named_scope("loss_head"):
        err = _jnp.square(y.astype(_jnp.float32) - loss_target)
        return 0.5 * _jnp.sum(_jnp.mean(err, axis=-1)) if err.ndim else 0.5 * err


def _adamw(w, g, m, v):
    m = ADAM_B1 * m + (1.0 - ADAM_B1) * g
    v = ADAM_B2 * v + (1.0 - ADAM_B2) * _jnp.square(g)
    m_hat = m / (1.0 - ADAM_B1 ** ADAM_STEP)
    v_hat = v / (1.0 - ADAM_B2 ** ADAM_STEP)
    delta = -ADAM_LR * (m_hat / (_jnp.sqrt(v_hat) + ADAM_EPS) + ADAM_WD * w)
    return delta, m, v


def reference(x, meta_tokens, norm_mix, w_in, q_norm, w_uq, kv_norm, w_ukv, w_sb_out, w_mla_out, w_o, norm_ffn, w_up, conv_w, conv_b, w_down, final_norm, loss_target, m_meta_tokens, m_norm_mix, m_w_in, m_q_norm, m_w_uq, m_kv_norm, m_w_ukv, m_w_sb_out, m_w_mla_out, m_w_o, m_norm_ffn, m_w_up, m_conv_w, m_conv_b, m_w_down, m_final_norm, v_meta_tokens, v_norm_mix, v_w_in, v_q_norm, v_w_uq, v_kv_norm, v_w_ukv, v_w_sb_out, v_w_mla_out, v_w_o, v_norm_ffn, v_w_up, v_conv_w, v_conv_b, v_w_down, v_final_norm):
    given = dict(x=x, meta_tokens=meta_tokens, norm_mix=norm_mix, w_in=w_in, q_norm=q_norm, w_uq=w_uq, kv_norm=kv_norm, w_ukv=w_ukv, w_sb_out=w_sb_out, w_mla_out=w_mla_out, w_o=w_o, norm_ffn=norm_ffn, w_up=w_up, conv_w=conv_w, conv_b=conv_b, w_down=w_down, final_norm=final_norm, loss_target=loss_target, m_meta_tokens=m_meta_tokens, m_norm_mix=m_norm_mix, m_w_in=m_w_in, m_q_norm=m_q_norm, m_w_uq=m_w_uq, m_kv_norm=m_kv_norm, m_w_ukv=m_w_ukv, m_w_sb_out=m_w_sb_out, m_w_mla_out=m_w_mla_out, m_w_o=m_w_o, m_norm_ffn=m_norm_ffn, m_w_up=m_w_up, m_conv_w=m_conv_w, m_conv_b=m_conv_b, m_w_down=m_w_down, m_final_norm=m_final_norm, v_meta_tokens=v_meta_tokens, v_norm_mix=v_norm_mix, v_w_in=v_w_in, v_q_norm=v_q_norm, v_w_uq=v_w_uq, v_kv_norm=v_kv_norm, v_w_ukv=v_w_ukv, v_w_sb_out=v_w_sb_out, v_w_mla_out=v_w_mla_out, v_w_o=v_w_o, v_norm_ffn=v_norm_ffn, v_w_up=v_w_up, v_conv_w=v_conv_w, v_conv_b=v_conv_b, v_w_down=v_w_down, v_final_norm=v_final_norm)
    weights = {n: given[n] for n in TWIN_WEIGHTS}
    shared = {n: given[n] for n in SHARED_INPUTS}
    per_example = {n: given[n] for n in ['x']}
    grad_fn = _jax.value_and_grad(_loss, argnums=(0, 1))

    def one_microbatch(ex, loss_target):
        ex = dict(ex)
        diff = ex.pop(TWIN_DIFF_INPUT)
        return grad_fn(weights, diff, {**shared, **ex}, loss_target)

    if N_MICROBATCH == 1:
        loss, (grad_w, grad_x) = one_microbatch(per_example, given["loss_target"])
    else:
        def body(carry, xs):
            loss_sum, grad_sum = carry
            l_k, (gw_k, gx_k) = one_microbatch(xs[0], xs[1])
            with _jax.named_scope("update"):
                return (loss_sum + l_k, _jax.tree.map(_jnp.add, grad_sum, gw_k)), gx_k

        init = (_jnp.zeros((), _jnp.float32), _jax.tree.map(_jnp.zeros_like, weights))
        (loss, grad_w), grad_x = _jax.lax.scan(body, init, (per_example, given["loss_target"]))
    with _jax.named_scope("update"):
        delta_w, new_m, new_v = {}, {}, {}
        for n in TWIN_WEIGHTS:
            delta_w[n], new_m[n], new_v[n] = _adamw(weights[n], grad_w[n], given["m_" + n], given["v_" + n])
    return (loss, grad_x, *[grad_w[n] for n in TWIN_WEIGHTS], *[delta_w[n] for n in TWIN_WEIGHTS],
            *[new_m[n] for n in TWIN_WEIGHTS], *[new_v[n] for n in TWIN_WEIGHTS])
```

```python
import jax
import jax.numpy as jnp
from jax import lax
from jax.experimental import pallas as pl
from jax.experimental.pallas import tpu as pltpu

F32 = jnp.float32
BF16 = jnp.bfloat16

EPS = 1e-6
ROPE_THETA = 10000.0
ADAM_LR = 0.001
ADAM_B1 = 0.9
ADAM_B2 = 0.999
ADAM_EPS = 1e-08
ADAM_WD = 0.01
ADAM_STEP = 10
NEG = -1e30
LANES = 128
PACK_W = 1024
V7X_VMEM_LIMIT = 48 * 1024 * 1024
V7X_VMEM_LIMIT_BIG = 58 * 1024 * 1024
MESH_AXES = ("x", "y", "c")
N_DEV = 8
FLIPS = [(0, 0, 1), (0, 1, 0), (0, 1, 1), (1, 0, 0), (1, 0, 1), (1, 1, 0), (1, 1, 1)]


class _Dims:
    def __init__(self, d_model=2048, seq=8192, depth=2, n_meta=16, block=128, sb_heads=8, hd=128,
                 mla_heads=8, q_lora=512, kv_lora=256, nope=128, rope=64, vdim=128, d_ff=5632):
        self.d, self.seq, self.depth, self.n_meta, self.block = d_model, seq, depth, n_meta, block
        self.sb_heads, self.hd, self.mla_heads = sb_heads, hd, mla_heads
        self.q_lora, self.kv_lora, self.nope, self.rope, self.vdim, self.f = q_lora, kv_lora, nope, rope, vdim, d_ff
        assert hd == LANES and nope == LANES and vdim == LANES and 2 * rope == LANES
        self.pad = block - n_meta
        self.lp = self.pad + n_meta + seq
        self.first_tok = self.pad + n_meta
        assert self.first_tok == block and self.lp % block == 0 and self.lp // block <= LANES
        self.sbw = sb_heads * hd
        self.mlaw = mla_heads * vdim
        self.wa = 3 * self.sbw
        self.d_in = 3 * self.sbw + q_lora + kv_lora + rope + 2 * d_model
        self.tg = min(1024, d_model)
        self.kr_off = q_lora + kv_lora
        raw = self.kr_off + LANES
        self.g_off = -(-raw // self.tg) * self.tg
        self.wb = self.g_off + 2 * d_model
        self.qw = mla_heads * 2 * LANES


PROD = _Dims()


def _pick(n, prefs):
    for p in prefs:
        if n % p == 0:
            return p
    return n


def _cparams(sem, limit=V7X_VMEM_LIMIT):
    return pltpu.CompilerParams(dimension_semantics=sem, vmem_limit_bytes=limit)


def _mm(a, b, *, ta=False, tb=False, out_dtype=F32, name):
    if ta:
        kdim, m = a.shape
    else:
        m, kdim = a.shape
    if tb:
        n, k2 = b.shape
    else:
        k2, n = b.shape
    assert kdim == k2, (a.shape, b.shape, ta, tb)
    tm = _pick(m, (640, 512, 256, 128))
    tn = _pick(n, (1024, 512, 384, 256, 128))
    tk = _pick(kdim, (512, 640, 256, 128))
    nk = kdim // tk
    dn = (((0 if ta else 1,), (1 if tb else 0,)), ((), ()))

    def body(a_ref, b_ref, o_ref, acc_ref):
        k = pl.program_id(2)

        @pl.when(k == 0)
        def _():
            acc_ref[...] = jnp.zeros_like(acc_ref)

        acc_ref[...] += lax.dot_general(a_ref[...].astype(BF16), b_ref[...].astype(BF16), dn,
                                        preferred_element_type=F32)

        @pl.when(k == nk - 1)
        def _():
            o_ref[...] = acc_ref[...].astype(out_dtype)

    a_spec = pl.BlockSpec((tk, tm), lambda i, j, k: (k, i)) if ta else pl.BlockSpec((tm, tk), lambda i, j, k: (i, k))
    b_spec = pl.BlockSpec((tn, tk), lambda i, j, k: (j, k)) if tb else pl.BlockSpec((tk, tn), lambda i, j, k: (k, j))
    return pl.pallas_call(
        body, name=name, grid=(m // tm, n // tn, nk), in_specs=[a_spec, b_spec],
        out_specs=pl.BlockSpec((tm, tn), lambda i, j, k: (i, j)),
        out_shape=jax.ShapeDtypeStruct((m, n), out_dtype),
        scratch_shapes=[pltpu.VMEM((tm, tn), F32)],
        compiler_params=_cparams(("parallel", "parallel", "arbitrary")),
    )(a, b)


def _norm_fwd(x, g, *, width, cidx, add=None, name):
    rows = x.shape[0]
    tr = _pick(rows, (256, 128))
    has_add = add is not None

    def body(*refs):
        if has_add:
            x_ref, a_ref, g_ref, xn_ref, y_ref, r_ref = refs
            xv = x_ref[...] + a_ref[...]
            xn_ref[...] = xv
        else:
            x_ref, g_ref, y_ref, r_ref = refs
            xv = x_ref[...]
        r = lax.rsqrt(jnp.mean(xv * xv, axis=1, keepdims=True) + EPS)
        y_ref[...] = (xv * r * g_ref[...]).astype(BF16)
        r_ref[...] = r

    blk = pl.BlockSpec((tr, width), lambda i: (i, 0))
    in_specs = [pl.BlockSpec((tr, width), lambda i: (i, cidx))]
    args = [x]
    if has_add:
        in_specs.append(blk)
        args.append(add)
    in_specs.append(pl.BlockSpec((1, width), lambda i: (0, 0)))
    args.append(g.reshape(1, width))
    out_specs = [blk, pl.BlockSpec((tr, 1), lambda i: (i, 0))]
    out_shape = [jax.ShapeDtypeStruct((rows, width), BF16), jax.ShapeDtypeStruct((rows, 1), F32)]
    if has_add:
        out_specs.insert(0, blk)
        out_shape.insert(0, jax.ShapeDtypeStruct((rows, width), F32))
    return pl.pallas_call(body, name=name, grid=(rows // tr,), in_specs=in_specs, out_specs=out_specs,
                          out_shape=out_shape, compiler_params=_cparams(("parallel",)))(*args)


def _norm_bwd(dy, x, r, g, *, width, cidx, dres=None, out_dtype=F32, name):
    rows = x.shape[0]
    tr = _pick(rows, (256, 128))
    has_res = dres is not None

    def body(*refs):
        if has_res:
            dy_ref, x_ref, r_ref, g_ref, dr_ref, dx_ref, dg_ref = refs
        else:
            dy_ref, x_ref, r_ref, g_ref, dx_ref, dg_ref = refs
        i = pl.program_id(0)

        @pl.when(i == 0)
        def _():
            dg_ref[...] = jnp.zeros_like(dg_ref)

        dyv, xv, rv = dy_ref[...], x_ref[...], r_ref[...]
        gy = dyv * g_ref[...]
        c = jnp.mean(gy * xv, axis=1, keepdims=True)
        dx = rv * gy - xv * (rv * rv * rv) * c
        if has_res:
            dx = dx + dr_ref[...]
        dx_ref[...] = dx.astype(out_dtype)
        dg_ref[...] += jnp.sum(dyv * xv * rv, axis=0, keepdims=True)

    blk = pl.BlockSpec((tr, width), lambda i: (i, 0))
    in_specs = [blk, pl.BlockSpec((tr, width), lambda i: (i, cidx)), pl.BlockSpec((tr, 1), lambda i: (i, 0)),
                pl.BlockSpec((1, width), lambda i: (0, 0))]
    args = [dy, x, r, g.reshape(1, width)]
    if has_res:
        in_specs.append(blk)
        args.append(dres)
    return pl.pallas_call(
        body, name=name, grid=(rows // tr,), in_specs=in_specs,
        out_specs=[blk, pl.BlockSpec((1, width), lambda i: (0, 0))],
        out_shape=[jax.ShapeDtypeStruct((rows, width), out_dtype), jax.ShapeDtypeStruct((1, width), F32)],
        compiler_params=_cparams(("arbitrary",)))(*args)


def _split3(x):
    h1 = x.astype(BF16)
    r1 = x - h1.astype(F32)
    h2 = r1.astype(BF16)
    h3 = (r1 - h2.astype(F32)).astype(BF16)
    return h1, h2, h3


def _cum(x, tri):
    h1, h2, h3 = _split3(x)
    dot = lambda h: jnp.dot(h, tri, preferred_element_type=F32)
    return dot(h1) + dot(h2) + dot(h3)


def _dot_nt(a, b):
    return lax.dot_general(a, b, (((1,), (1,)), ((), ())), preferred_element_type=F32)


def _dot_tn(a, b):
    return lax.dot_general(a, b, (((0,), (0,)), ((), ())), preferred_element_type=F32)


def _sb_fwd(qkv, d, name):
    nh, hd, lp, t = d.sb_heads, d.hd, d.lp, d.block
    scale = hd ** -0.5
    pad = d.pad

    def body(q_ref, k_ref, v_ref, o_ref, c_ref):
        i = pl.program_id(1)
        q = q_ref[...]
        row = i * t + lax.broadcasted_iota(jnp.int32, (t, t), 0)
        colb = lax.broadcasted_iota(jnp.int32, (t, t), 1)
        tri = (lax.broadcasted_iota(jnp.int32, (t, t), 0) > colb).astype(BF16)
        lane = lax.broadcasted_iota(jnp.int32, (t, LANES), 1)

        c_ref[...] = jnp.zeros_like(c_ref)

        def step(jj, carry):
            acc, run = carry
            j = i - jj
            off = pl.multiple_of(j * t, t)
            k = k_ref[pl.ds(off, t), :]
            v = v_ref[pl.ds(off, t), :]
            z = _dot_nt(q, k) * scale
            col = j * t + colb
            mask = (col < row) & (col >= pad)
            e = jnp.exp(-jnp.abs(z))
            sp = jnp.maximum(z, 0.0) + jnp.log(1.0 + e)
            l1 = jnp.where(mask, -sp, 0.0)
            later = _cum(l1, tri) + run
            w = jnp.where(mask, jnp.exp(z - sp + later), 0.0)
            acc = acc + jnp.dot(w.astype(BF16), v, preferred_element_type=F32)
            c_ref[...] = jnp.where(lane == j, run, c_ref[...])
            run = run + jnp.sum(l1, axis=1, keepdims=True)
            return acc, run

        acc, _ = lax.fori_loop(0, i + 1, step, (jnp.zeros((t, hd), F32), jnp.zeros((t, 1), F32)))
        o_ref[...] = acc

    return pl.pallas_call(
        body, name=name, grid=(nh, lp // t),
        in_specs=[pl.BlockSpec((t, hd), lambda h, i: (i, h)),
                  pl.BlockSpec((lp, hd), lambda h, i: (0, nh + h)),
                  pl.BlockSpec((lp, hd), lambda h, i: (0, 2 * nh + h))],
        out_specs=[pl.BlockSpec((t, hd), lambda h, i: (i, h)),
                   pl.BlockSpec((None, t, LANES), lambda h, i: (h, i, 0))],
        out_shape=[jax.ShapeDtypeStruct((lp, nh * hd), F32), jax.ShapeDtypeStruct((nh, lp, LANES), F32)],
        compiler_params=_cparams(("parallel", "arbitrary")),
    )(qkv, qkv, qkv)


def _sb_bwd(qkv, do, carry, d, name):
    nh, hd, lp, t = d.sb_heads, d.hd, d.lp, d.block
    scale = hd ** -0.5
    pad = d.pad

    def body(q_ref, k_ref, v_ref, do_ref, c_ref, dq_ref, dk_ref, dv_ref):
        i = pl.program_id(1)

        @pl.when(i == 0)
        def _():
            dk_ref[...] = jnp.zeros_like(dk_ref)
            dv_ref[...] = jnp.zeros_like(dv_ref)

        q = q_ref[...]
        dob = do_ref[...].astype(BF16)
        ctile = c_ref[...]
        row = i * t + lax.broadcasted_iota(jnp.int32, (t, t), 0)
        rowi = lax.broadcasted_iota(jnp.int32, (t, t), 0)
        colb = lax.broadcasted_iota(jnp.int32, (t, t), 1)
        tri_suf = (rowi > colb).astype(BF16)
        tri_pre = (rowi < colb).astype(BF16)
        lane = lax.broadcasted_iota(jnp.int32, (t, LANES), 1)

        def step(j, carry):
            dq, pc = carry
            off = pl.multiple_of(j * t, t)
            k = k_ref[pl.ds(off, t), :]
            v = v_ref[pl.ds(off, t), :]
            z = _dot_nt(q, k) * scale
            col = j * t + colb
            mask = (col < row) & (col >= pad)
            e = jnp.exp(-jnp.abs(z))
            sp = jnp.maximum(z, 0.0) + jnp.log(1.0 + e)
            l1 = jnp.where(mask, -sp, 0.0)
            run = jnp.sum(jnp.where(lane == j, ctile, 0.0), axis=1, keepdims=True)
            later = _cum(l1, tri_suf) + run
            w = jnp.where(mask, jnp.exp(z - sp + later), 0.0)
            dw = _dot_nt(dob, v)
            g = w * dw
            p = _cum(g, tri_pre) + pc
            inv = 1.0 / (1.0 + e)
            sig = jnp.where(z >= 0.0, inv, e * inv)
            dz = jnp.where(mask, g * (1.0 - sig) - sig * p, 0.0) * scale
            dzb = dz.astype(BF16)
            dq = dq + jnp.dot(dzb, k, preferred_element_type=F32)
            dk_ref[pl.ds(off, t), :] += _dot_tn(dzb, q)
            dv_ref[pl.ds(off, t), :] += _dot_tn(w.astype(BF16), dob)
            pc = pc + jnp.sum(g, axis=1, keepdims=True)
            return dq, pc

        dq, _ = lax.fori_loop(0, i + 1, step, (jnp.zeros((t, hd), F32), jnp.zeros((t, 1), F32)))
        dq_ref[...] = dq.astype(BF16)

    w3 = nh * hd
    return pl.pallas_call(
        body, name=name, grid=(nh, lp // t),
        in_specs=[pl.BlockSpec((t, hd), lambda h, i: (i, h)),
                  pl.BlockSpec((lp, hd), lambda h, i: (0, nh + h)),
                  pl.BlockSpec((lp, hd), lambda h, i: (0, 2 * nh + h)),
                  pl.BlockSpec((t, hd), lambda h, i: (i, h)),
                  pl.BlockSpec((None, t, LANES), lambda h, i: (h, i, 0))],
        out_specs=[pl.BlockSpec((t, hd), lambda h, i: (i, h)),
                   pl.BlockSpec((lp, hd), lambda h, i: (0, h)),
                   pl.BlockSpec((lp, hd), lambda h, i: (0, h))],
        out_shape=[jax.ShapeDtypeStruct((lp, w3), BF16), jax.ShapeDtypeStruct((lp, w3), F32),
                   jax.ShapeDtypeStruct((lp, w3), F32)],
        compiler_params=_cparams(("arbitrary", "arbitrary")),
    )(qkv, qkv, qkv, do, carry)


def _mla_prep_fwd(qraw, projb, ctab, stab, d, name):
    lp, nh = d.lp, d.mla_heads
    tr = _pick(lp, (256, 128))
    kidx = d.kr_off // LANES

    def rope(u, c, s):
        return u * c + pltpu.roll(u, LANES // 2, 1) * s

    def body(q_ref, k_ref, c_ref, s_ref, qm_ref, kr_ref):
        c, s = c_ref[...], s_ref[...]
        for h in range(nh):
            base = 2 * LANES * h
            qm_ref[:, base:base + LANES] = q_ref[:, base:base + LANES].astype(BF16)
            qm_ref[:, base + LANES:base + 2 * LANES] = rope(q_ref[:, base + LANES:base + 2 * LANES], c, s).astype(BF16)
        kr_ref[...] = rope(k_ref[...], c, s).astype(BF16)

    tab = pl.BlockSpec((tr, LANES), lambda i: (i, 0))
    return pl.pallas_call(
        body, name=name, grid=(lp // tr,),
        in_specs=[pl.BlockSpec((tr, d.qw), lambda i: (i, 0)), pl.BlockSpec((tr, LANES), lambda i: (i, kidx)), tab, tab],
        out_specs=[pl.BlockSpec((tr, d.qw), lambda i: (i, 0)), tab],
        out_shape=[jax.ShapeDtypeStruct((lp, d.qw), BF16), jax.ShapeDtypeStruct((lp, LANES), BF16)],
        compiler_params=_cparams(("parallel",)))(qraw, projb, ctab, stab)


def _mla_prep_bwd(dqm, dkr, ctab, stab, d, name):
    lp, nh = d.lp, d.mla_heads
    tr = _pick(lp, (256, 128))

    def unrope(g, c, s):
        return g * c + pltpu.roll(g * s, LANES // 2, 1)

    def body(dq_ref, dk_ref, c_ref, s_ref, o_ref, ok_ref):
        c, s = c_ref[...], s_ref[...]
        for h in range(nh):
            base = 2 * LANES * h
            o_ref[:, base:base + LANES] = dq_ref[:, base:base + LANES].astype(BF16)
            o_ref[:, base + LANES:base + 2 * LANES] = unrope(dq_ref[:, base + LANES:base + 2 * LANES], c, s).astype(BF16)
        ok_ref[...] = unrope(dk_ref[...], c, s).astype(BF16)

    tab = pl.BlockSpec((tr, LANES), lambda i: (i, 0))
    wide = pl.BlockSpec((tr, d.qw), lambda i: (i, 0))
    return pl.pallas_call(
        body, name=name, grid=(lp // tr,), in_specs=[wide, tab, tab, tab], out_specs=[wide, tab],
        out_shape=[jax.ShapeDtypeStruct((lp, d.qw), BF16), jax.ShapeDtypeStruct((lp, LANES), BF16)],
        compiler_params=_cparams(("parallel",)))(dqm, dkr, ctab, stab)


def _mla_fwd(qm, kv, kr, d, name):
    nh, lp, t = d.mla_heads, d.lp, d.block
    scale = (d.nope + d.rope) ** -0.5
    pad = d.pad

    def body(q_ref, kn_ref, v_ref, kr_ref, o_ref, lse_ref):
        i = pl.program_id(1)
        qn = q_ref[:, :LANES]
        qr = q_ref[:, LANES:]
        row = i * t + lax.broadcasted_iota(jnp.int32, (t, t), 0)
        colb = lax.broadcasted_iota(jnp.int32, (t, t), 1)

        def step(j, carry):
            acc, m, l = carry
            off = pl.multiple_of(j * t, t)
            s = (_dot_nt(qn, kn_ref[pl.ds(off, t), :]) + _dot_nt(qr, kr_ref[pl.ds(off, t), :])) * scale
            col = j * t + colb
            s = jnp.where((col <= row) & (col >= pad), s, NEG)
            m_new = jnp.maximum(m, jnp.max(s, axis=1, keepdims=True))
            alpha = jnp.exp(m - m_new)
            p = jnp.exp(s - m_new)
            l = alpha * l + jnp.sum(p, axis=1, keepdims=True)
            acc = alpha * acc + jnp.dot(p.astype(BF16), v_ref[pl.ds(off, t), :], preferred_element_type=F32)
            return acc, m_new, l

        init = (jnp.zeros((t, LANES), F32), jnp.full((t, 1), NEG, F32), jnp.zeros((t, 1), F32))
        acc, m, l = lax.fori_loop(0, i + 1, step, init)
        rowv = i * t + lax.broadcasted_iota(jnp.int32, (t, LANES), 0)
        o_ref[...] = jnp.where(rowv >= pad, acc / l, 0.0)
        lse_ref[...] = m + jnp.log(l)

    return pl.pallas_call(
        body, name=name, grid=(nh, lp // t),
        in_specs=[pl.BlockSpec((t, 2 * LANES), lambda h, i: (i, h)),
                  pl.BlockSpec((lp, LANES), lambda h, i: (0, h)),
                  pl.BlockSpec((lp, LANES), lambda h, i: (0, nh + h)),
                  pl.BlockSpec((lp, LANES), lambda h, i: (0, 0))],
        out_specs=[pl.BlockSpec((t, LANES), lambda h, i: (i, h)),
                   pl.BlockSpec((None, t, 1), lambda h, i: (h, i, 0))],
        out_shape=[jax.ShapeDtypeStruct((lp, nh * LANES), F32), jax.ShapeDtypeStruct((nh, lp, 1), F32)],
        compiler_params=_cparams(("parallel", "arbitrary")),
    )(qm, kv, kv, kr)


def _mla_bwd(qm, kv, kr, o, do, lse, d, name):
    nh, lp, t = d.mla_heads, d.lp, d.block
    scale = (d.nope + d.rope) ** -0.5
    pad = d.pad

    def body(q_ref, kn_ref, v_ref, kr_ref, o_ref, do_ref, lse_ref, dq_ref, dkn_ref, dv_ref, dkr_ref):
        h = pl.program_id(0)
        i = pl.program_id(1)

        @pl.when(i == 0)
        def _():
            dkn_ref[...] = jnp.zeros_like(dkn_ref)
            dv_ref[...] = jnp.zeros_like(dv_ref)

        @pl.when((i == 0) & (h == 0))
        def _():
            dkr_ref[...] = jnp.zeros_like(dkr_ref)

        qn = q_ref[:, :LANES]
        qr = q_ref[:, LANES:]
        dof = do_ref[...]
        dob = dof.astype(BF16)
        delta = jnp.sum(dof * o_ref[...], axis=1, keepdims=True)
        lse = lse_ref[...]
        row = i * t + lax.broadcasted_iota(jnp.int32, (t, t), 0)
        colb = lax.broadcasted_iota(jnp.int32, (t, t), 1)

        def step(j, carry):
            dqn, dqr = carry
            off = pl.multiple_of(j * t, t)
            kn = kn_ref[pl.ds(off, t), :]
            krj = kr_ref[pl.ds(off, t), :]
            v = v_ref[pl.ds(off, t), :]
            s = (_dot_nt(qn, kn) + _dot_nt(qr, krj)) * scale
            col = j * t + colb
            mask = (col <= row) & (col >= pad)
            p = jnp.where(mask, jnp.exp(jnp.where(mask, s, NEG) - lse), 0.0)
            dp = _dot_nt(dob, v)
            dsb = (p * (dp - delta) * scale).astype(BF16)
            dqn = dqn + jnp.dot(dsb, kn, preferred_element_type=F32)
            dqr = dqr + jnp.dot(dsb, krj, preferred_element_type=F32)
            dkn_ref[pl.ds(off, t), :] += _dot_tn(dsb, qn)
            dkr_ref[pl.ds(off, t), :] += _dot_tn(dsb, qr)
            dv_ref[pl.ds(off, t), :] += _dot_tn(p.astype(BF16), dob)
            return dqn, dqr

        dqn, dqr = lax.fori_loop(0, i + 1, step, (jnp.zeros((t, LANES), F32), jnp.zeros((t, LANES), F32)))
        dq_ref[:, :LANES] = dqn
        dq_ref[:, LANES:] = dqr

    return pl.pallas_call(
        body, name=name, grid=(nh, lp // t),
        in_specs=[pl.BlockSpec((t, 2 * LANES), lambda h, i: (i, h)),
                  pl.BlockSpec((lp, LANES), lambda h, i: (0, h)),
                  pl.BlockSpec((lp, LANES), lambda h, i: (0, nh + h)),
                  pl.BlockSpec((lp, LANES), lambda h, i: (0, 0)),
                  pl.BlockSpec((t, LANES), lambda h, i: (i, h)),
                  pl.BlockSpec((t, LANES), lambda h, i: (i, h)),
                  pl.BlockSpec((None, t, 1), lambda h, i: (h, i, 0))],
        out_specs=[pl.BlockSpec((t, 2 * LANES), lambda h, i: (i, h)),
                   pl.BlockSpec((lp, LANES), lambda h, i: (0, h)),
                   pl.BlockSpec((lp, LANES), lambda h, i: (0, h)),
                   pl.BlockSpec((lp, LANES), lambda h, i: (0, 0))],
        out_shape=[jax.ShapeDtypeStruct((lp, nh * 2 * LANES), F32), jax.ShapeDtypeStruct((lp, nh * LANES), F32),
                   jax.ShapeDtypeStruct((lp, nh * LANES), F32), jax.ShapeDtypeStruct((lp, LANES), F32)],
        compiler_params=_cparams(("arbitrary", "arbitrary"), V7X_VMEM_LIMIT_BIG),
    )(qm, kv, kv, kr, o, do, lse)


def _sigmoid(x):
    return 1.0 / (1.0 + jnp.exp(-x))


def _gate_fwd(projb, b_sb, b_mla, d, name):
    lp, tg = d.lp, d.tg
    tr = _pick(lp, (256, 128))
    o1, o2 = d.g_off // tg, (d.g_off + d.d) // tg

    def body(g1_ref, g2_ref, b1_ref, b2_ref, o_ref):
        o_ref[...] = (_sigmoid(g1_ref[...]) * b1_ref[...] + _sigmoid(g2_ref[...]) * b2_ref[...]).astype(BF16)

    blk = pl.BlockSpec((tr, tg), lambda i, j: (i, j))
    return pl.pallas_call(
        body, name=name, grid=(lp // tr, d.d // tg),
        in_specs=[pl.BlockSpec((tr, tg), lambda i, j: (i, o1 + j)), pl.BlockSpec((tr, tg), lambda i, j: (i, o2 + j)),
                  blk, blk],
        out_specs=blk, out_shape=jax.ShapeDtypeStruct((lp, d.d), BF16),
        compiler_params=_cparams(("parallel", "parallel")))(projb, projb, b_sb, b_mla)


def _gate_bwd(dm, projb, b_sb, b_mla, d, name):
    lp, tg = d.lp, d.tg
    tr = _pick(lp, (256, 128))
    o1, o2 = d.g_off // tg, (d.g_off + d.d) // tg

    def body(dm_ref, g1_ref, g2_ref, b1_ref, b2_ref, db1_ref, db2_ref, dg1_ref, dg2_ref):
        dmv = dm_ref[...]
        s1, s2 = _sigmoid(g1_ref[...]), _sigmoid(g2_ref[...])
        db1_ref[...] = (dmv * s1).astype(BF16)
        db2_ref[...] = (dmv * s2).astype(BF16)
        dg1_ref[...] = (dmv * b1_ref[...] * s1 * (1.0 - s1)).astype(BF16)
        dg2_ref[...] = (dmv * b2_ref[...] * s2 * (1.0 - s2)).astype(BF16)

    blk = pl.BlockSpec((tr, tg), lambda i, j: (i, j))
    out = jax.ShapeDtypeStruct((lp, d.d), BF16)
    return pl.pallas_call(
        body, name=name, grid=(lp // tr, d.d // tg),
        in_specs=[blk, pl.BlockSpec((tr, tg), lambda i, j: (i, o1 + j)), pl.BlockSpec((tr, tg), lambda i, j: (i, o2 + j)),
                  blk, blk],
        out_specs=[blk] * 4, out_shape=[out] * 4,
        compiler_params=_cparams(("parallel", "parallel")))(dm, projb, projb, b_sb, b_mla)


HALO = 8


def _conv_tiles(d):
    return _pick(d.lp, (640, 512, 256, 128)), _pick(d.f, (512, 256, 128))


def _convglu_fwd(up, cw, cb, d, name):
    lp, f = d.lp, d.f
    tr, tc = _conv_tiles(d)
    nf = f // tc
    hb = tr // HALO
    pad = d.pad

    def body(a_ref, g_ref, pa_ref, pg_ref, wa_ref, wg_ref, ba_ref, bg_ref, o_ref, xa, xg):
        i = pl.program_id(1)
        keep = (i > 0).astype(F32)
        xa[0:HALO, :] = pa_ref[...] * keep
        xg[0:HALO, :] = pg_ref[...] * keep
        xa[HALO:, :] = a_ref[...]
        xg[HALO:, :] = g_ref[...]

        def conv(x, w_ref, b_ref):
            return (b_ref[...] + x[pl.ds(HALO - 2, tr), :] * w_ref[0:1, :] + x[pl.ds(HALO - 1, tr), :] * w_ref[1:2, :]
                    + x[pl.ds(HALO, tr), :] * w_ref[2:3, :])

        ua = conv(xa, wa_ref, ba_ref)
        ug = conv(xg, wg_ref, bg_ref)
        row = i * tr + lax.broadcasted_iota(jnp.int32, (tr, tc), 0)
        o_ref[...] = jnp.where(row >= pad, ua * _sigmoid(ua) * ug, 0.0).astype(BF16)

    prev = lambda j, i: (jnp.maximum(i * hb - 1, 0), j)
    prevg = lambda j, i: (jnp.maximum(i * hb - 1, 0), nf + j)
    return pl.pallas_call(
        body, name=name, grid=(nf, lp // tr),
        in_specs=[pl.BlockSpec((tr, tc), lambda j, i: (i, j)), pl.BlockSpec((tr, tc), lambda j, i: (i, nf + j)),
                  pl.BlockSpec((HALO, tc), prev), pl.BlockSpec((HALO, tc), prevg),
                  pl.BlockSpec((3, tc), lambda j, i: (0, j)), pl.BlockSpec((3, tc), lambda j, i: (0, nf + j)),
                  pl.BlockSpec((1, tc), lambda j, i: (0, j)), pl.BlockSpec((1, tc), lambda j, i: (0, nf + j))],
        out_specs=pl.BlockSpec((tr, tc), lambda j, i: (i, j)),
        out_shape=jax.ShapeDtypeStruct((lp, f), BF16),
        scratch_shapes=[pltpu.VMEM((tr + HALO, tc), F32), pltpu.VMEM((tr + HALO, tc), F32)],
        compiler_params=_cparams(("parallel", "arbitrary")))(up, up, up, up, cw, cw, cb, cb)


def _convglu_bwd(up, dact, cw, cb, d, name):
    lp, f = d.lp, d.f
    tr, tc = _conv_tiles(d)
    nf = f // tc
    hb = tr // HALO
    nrow = lp // tr
    pad = d.pad
    te = tr + HALO

    def body(a_ref, g_ref, pa_ref, pg_ref, na_ref, ng_ref, da_ref, nd_ref, wa_ref, wg_ref, ba_ref, bg_ref,
             oa_ref, og_ref, sa_ref, sg_ref, xa, xg, xd, ya, yg):
        i = pl.program_id(1)

        @pl.when(i == 0)
        def _():
            sa_ref[...] = jnp.zeros_like(sa_ref)
            sg_ref[...] = jnp.zeros_like(sg_ref)

        keep_p = (i > 0).astype(F32)
        keep_n = (i < nrow - 1).astype(F32)
        xa[0:HALO, :] = pa_ref[...] * keep_p
        xg[0:HALO, :] = pg_ref[...] * keep_p
        xa[HALO:HALO + tr, :] = a_ref[...]
        xg[HALO:HALO + tr, :] = g_ref[...]
        xa[HALO + tr:, :] = na_ref[...] * keep_n
        xg[HALO + tr:, :] = ng_ref[...] * keep_n
        xd[0:tr, :] = da_ref[...]
        xd[tr:, :] = nd_ref[...] * keep_n

        def conv(x, w_ref, b_ref):
            return (b_ref[...] + x[pl.ds(HALO - 2, te), :] * w_ref[0:1, :] + x[pl.ds(HALO - 1, te), :] * w_ref[1:2, :]
                    + x[pl.ds(HALO, te), :] * w_ref[2:3, :])

        ua = conv(xa, wa_ref, ba_ref)
        ug = conv(xg, wg_ref, bg_ref)
        sg = _sigmoid(ua)
        dact = xd[...]
        ya[...] = dact * ug * (sg * (1.0 + ua * (1.0 - sg)))
        yg[...] = dact * (ua * sg)
        row = i * tr + lax.broadcasted_iota(jnp.int32, (tr, tc), 0)

        def back(y, x, w_ref, o_ref, s_ref):
            y0 = y[pl.ds(0, tr), :]
            dup = y0 * w_ref[2:3, :] + y[pl.ds(1, tr), :] * w_ref[1:2, :] + y[pl.ds(2, tr), :] * w_ref[0:1, :]
            o_ref[...] = jnp.where(row >= pad, dup, 0.0).astype(BF16)
            for tap in range(3):
                s_ref[tap:tap + 1, :] += jnp.sum(y0 * x[pl.ds(HALO - 2 + tap, tr), :], axis=0, keepdims=True)
            s_ref[3:4, :] += jnp.sum(y0, axis=0, keepdims=True)

        back(ya, xa, wa_ref, oa_ref, sa_ref)
        back(yg, xg, wg_ref, og_ref, sg_ref)

    last8 = lp // HALO - 1
    prev = lambda j, i: (jnp.maximum(i * hb - 1, 0), j)
    prevg = lambda j, i: (jnp.maximum(i * hb - 1, 0), nf + j)
    nxt = lambda j, i: (jnp.minimum((i + 1) * hb, last8), j)
    nxtg = lambda j, i: (jnp.minimum((i + 1) * hb, last8), nf + j)
    halo = lambda m: pl.BlockSpec((HALO, tc), m)
    main = pl.BlockSpec((tr, tc), lambda j, i: (i, j))
    sums = pl.BlockSpec((8, tc), lambda j, i: (0, j))
    return pl.pallas_call(
        body, name=name, grid=(nf, nrow),
        in_specs=[main, pl.BlockSpec((tr, tc), lambda j, i: (i, nf + j)), halo(prev), halo(prevg), halo(nxt), halo(nxtg),
                  main, halo(nxt),
                  pl.BlockSpec((3, tc), lambda j, i: (0, j)), pl.BlockSpec((3, tc), lambda j, i: (0, nf + j)),
                  pl.BlockSpec((1, tc), lambda j, i: (0, j)), pl.BlockSpec((1, tc), lambda j, i: (0, nf + j))],
        out_specs=[main, main, sums, sums],
        out_shape=[jax.ShapeDtypeStruct((lp, f), BF16), jax.ShapeDtypeStruct((lp, f), BF16),
                   jax.ShapeDtypeStruct((8, f), F32), jax.ShapeDtypeStruct((8, f), F32)],
        scratch_shapes=[pltpu.VMEM((tr + 2 * HALO, tc), F32), pltpu.VMEM((tr + 2 * HALO, tc), F32),
                        pltpu.VMEM((te, tc), F32), pltpu.VMEM((te, tc), F32), pltpu.VMEM((te, tc), F32)],
        compiler_params=_cparams(("parallel", "arbitrary")))(up, up, up, up, up, up, dact, dact, cw, cw, cb, cb)


def _head(h, target, g, d, name):
    lp, dm, t = d.lp, d.d, d.block
    inv_d = 1.0 / dm

    def body(h_ref, t_ref, g_ref, dh_ref, loss_ref, dg_ref):
        i = pl.program_id(0)

        @pl.when(i == 0)
        def _():
            dh_ref[...] = jnp.zeros_like(dh_ref)
            loss_ref[...] = jnp.zeros_like(loss_ref)
            dg_ref[...] = jnp.zeros_like(dg_ref)

        @pl.when(i > 0)
        def _():
            x, gv = h_ref[...], g_ref[...]
            r = lax.rsqrt(jnp.mean(x * x, axis=1, keepdims=True) + EPS)
            xh = x * r
            err = xh * gv - t_ref[...]
            loss_ref[...] += 0.5 * inv_d * jnp.sum(err * err)
            dy = err * inv_d
            gy = dy * gv
            c = jnp.mean(gy * x, axis=1, keepdims=True)
            dh_ref[...] = r * gy - x * (r * r * r) * c
            dg_ref[...] += jnp.sum(dy * xh, axis=0, keepdims=True)

    blk = pl.BlockSpec((t, dm), lambda i: (i, 0))
    return pl.pallas_call(
        body, name=name, grid=(lp // t,),
        in_specs=[blk, pl.BlockSpec((t, dm), lambda i: (jnp.maximum(i - 1, 0), 0)), pl.BlockSpec((1, dm), lambda i: (0, 0))],
        out_specs=[blk, pl.BlockSpec((8, LANES), lambda i: (0, 0)), pl.BlockSpec((1, dm), lambda i: (0, 0))],
        out_shape=[jax.ShapeDtypeStruct((lp, dm), F32), jax.ShapeDtypeStruct((8, LANES), F32),
                   jax.ShapeDtypeStruct((1, dm), F32)],
        compiler_params=_cparams(("arbitrary",)))(h, target, g.reshape(1, dm))


def _exchange(xs, *, scatter, name):
    n = len(xs)
    nf = len(FLIPS)

    def body(*refs):
        ins, outs = refs[:n], refs[n:2 * n]
        send_sems, recv_sems, loc_sems = refs[2 * n:]
        x, y, c = lax.axis_index("x"), lax.axis_index("y"), lax.axis_index("c")
        me = 4 * x + 2 * y + c
        sends, recvs, locs = [], [], []
        for a in range(n):
            src_me = ins[a].at[me] if scatter else ins[a]
            loc = pltpu.make_async_copy(src_me, outs[a].at[me], loc_sems.at[a])
            loc.start()
            locs.append(loc)
            for k, (fx, fy, fc) in enumerate(FLIPS):
                px, py, pc = x ^ fx, y ^ fy, c ^ fc
                peer = 4 * px + 2 * py + pc
                src = ins[a].at[peer] if scatter else ins[a]
                cp = pltpu.make_async_remote_copy(
                    src_ref=src, dst_ref=outs[a].at[me], send_sem=send_sems.at[a * nf + k],
                    recv_sem=recv_sems.at[a * nf + k], device_id=(px, py, pc), device_id_type=pl.DeviceIdType.MESH)
                cp.start()
                sends.append(cp)
                recvs.append(pltpu.make_async_remote_copy(
                    src_ref=src, dst_ref=outs[a].at[peer], send_sem=send_sems.at[a * nf + k],
                    recv_sem=recv_sems.at[a * nf + k], device_id=(px, py, pc), device_id_type=pl.DeviceIdType.MESH))
        for cp in recvs:
            cp.wait_recv()
        for cp in sends:
            cp.wait_send()
        for loc in locs:
            loc.wait()

    hbm = pl.BlockSpec(memory_space=pltpu.HBM)
    out_shape = [jax.ShapeDtypeStruct(((N_DEV,) + tuple(x.shape[1:])) if scatter else ((N_DEV,) + tuple(x.shape)), x.dtype)
                 for x in xs]
    return pl.pallas_call(
        body, name=name, in_specs=[hbm] * n, out_specs=[hbm] * n, out_shape=out_shape,
        scratch_shapes=[pltpu.SemaphoreType.DMA((n * nf,)), pltpu.SemaphoreType.DMA((n * nf,)),
                        pltpu.SemaphoreType.DMA((n,))],
    )(*xs)


def _adamw(parts, w, m, v, name):
    rows = w.shape[0]
    tr = _pick(rows, (128, 64, 32, 16, 8))
    c1 = 1.0 - ADAM_B1 ** ADAM_STEP
    c2 = 1.0 - ADAM_B2 ** ADAM_STEP

    def body(p_ref, w_ref, m_ref, v_ref, g_ref, d_ref, mo_ref, vo_ref):
        g = p_ref[0]
        for q in range(1, N_DEV):
            g = g + p_ref[q]
        mn = ADAM_B1 * m_ref[...] + (1.0 - ADAM_B1) * g
        vn = ADAM_B2 * v_ref[...] + (1.0 - ADAM_B2) * (g * g)
        g_ref[...] = g
        mo_ref[...] = mn
        vo_ref[...] = vn
        d_ref[...] = -ADAM_LR * ((mn / c1) / (jnp.sqrt(vn / c2) + ADAM_EPS) + ADAM_WD * w_ref[...])

    blk = pl.BlockSpec((tr, PACK_W), lambda i: (i, 0))
    out = jax.ShapeDtypeStruct((rows, PACK_W), F32)
    return pl.pallas_call(
        body, name=name, grid=(rows // tr,),
        in_specs=[pl.BlockSpec((N_DEV, tr, PACK_W), lambda i: (0, i, 0)), blk, blk, blk],
        out_specs=[blk] * 4, out_shape=[out] * 4, compiler_params=_cparams(("parallel",)))(parts, w, m, v)


BIG = ["w_in", "w_uq", "w_ukv", "w_sb_out", "w_mla_out", "w_o", "w_up", "w_down"]
ROW_SHARDED = {"w_o", "w_down"}
SMALL = ["conv_w", "meta_tokens"]
SHARDED = BIG + SMALL
REPL = ["norm_mix", "q_norm", "kv_norm", "norm_ffn", "conv_b", "final_norm"]


def _pack_rows(flat, row_mult):
    unit = PACK_W * row_mult
    total = -(-flat.shape[0] // unit) * unit
    return jnp.pad(flat, (0, total - flat.shape[0])).reshape(-1, PACK_W)


def _pack_rows_batched(flat2, row_mult):
    unit = PACK_W * row_mult
    total = -(-flat2.shape[1] // unit) * unit
    return jnp.pad(flat2, ((0, 0), (0, total - flat2.shape[1]))).reshape(flat2.shape[0], -1, PACK_W)


def _full_from_slots(slots, name):
    if name == "meta_tokens":
        return jnp.transpose(slots, (1, 0, 2)).reshape(slots.shape[1], -1)
    if name in ROW_SHARDED:
        return jnp.transpose(slots, (1, 0, 2, 3)).reshape(slots.shape[1], -1, slots.shape[3])
    return jnp.transpose(slots, (1, 2, 0, 3)).reshape(slots.shape[1], slots.shape[2], -1)


def _slots_from_full(full, name):
    if name == "meta_tokens":
        r, n = full.shape
        return jnp.transpose(full.reshape(r, N_DEV, n // N_DEV), (1, 0, 2)).reshape(N_DEV, -1)
    if name in ROW_SHARDED:
        l, rr, n = full.shape
        return jnp.transpose(full.reshape(l, N_DEV, rr // N_DEV, n), (1, 0, 2, 3)).reshape(N_DEV, -1)
    l, k, n = full.shape
    return jnp.transpose(full.reshape(l, k, N_DEV, n // N_DEV), (2, 0, 1, 3)).reshape(N_DEV, -1)


def _swap_halves(t):
    half = t.shape[-1] // 2
    return jnp.concatenate([t[..., half:], t[..., :half]], axis=-1)


def _in_offsets(d):
    widths = (d.sbw, d.sbw, d.sbw, d.q_lora, d.kv_lora, d.rope, d.d, d.d)
    offs, o = [], 0
    for w in widths:
        offs.append((o, o + w))
        o += w
    return offs


def _prime_weights(full, layer, d):
    offs = _in_offsets(d)
    w_in = full["w_in"][layer]
    cols = lambda k: w_in[:, offs[k][0]:offs[k][1]]
    kr = cols(5)
    zpad = jnp.zeros((d.d, d.g_off - d.kr_off - LANES), w_in.dtype)
    w_inb = jnp.concatenate([cols(3), cols(4), kr, _swap_halves(kr), zpad, cols(6), cols(7)], axis=1)
    w_ina = w_in[:, :d.wa]
    uq = full["w_uq"][layer].reshape(d.q_lora, d.mla_heads, d.nope + d.rope)
    rope = uq[..., d.nope:]
    w_uq = jnp.concatenate([uq[..., :d.nope], rope, _swap_halves(rope)], axis=-1).reshape(d.q_lora, d.qw)
    ukv = full["w_ukv"][layer].reshape(d.kv_lora, d.mla_heads, d.nope + d.vdim)
    w_ukv = jnp.concatenate([ukv[..., :d.nope].reshape(d.kv_lora, -1), ukv[..., d.nope:].reshape(d.kv_lora, -1)], axis=1)
    return dict(w_ina=w_ina, w_inb=w_inb, w_in=jnp.concatenate([w_ina, w_inb], axis=1), w_uq=w_uq, w_ukv=w_ukv,
                w_sb_out=full["w_sb_out"][layer], w_mla_out=full["w_mla_out"][layer], w_o=full["w_o"][layer],
                w_up=full["w_up"][layer], w_down=full["w_down"][layer])


def _unprime_grads(g, d):
    gi = g["w_in"]
    b = gi[:, d.wa:]
    kr = b[:, d.kr_off:d.kr_off + d.rope] + _swap_halves(b[:, d.kr_off + d.rope:d.kr_off + 2 * d.rope])
    w_in = jnp.concatenate([gi[:, :d.wa], b[:, :d.kr_off], kr, b[:, d.g_off:]], axis=1)
    uq = g["w_uq"].reshape(d.q_lora, d.mla_heads, 2 * LANES)
    rope = uq[..., d.nope:d.nope + d.rope] + _swap_halves(uq[..., d.nope + d.rope:])
    w_uq = jnp.concatenate([uq[..., :d.nope], rope], axis=-1).reshape(d.q_lora, -1)
    hw = d.mla_heads * d.nope
    ukv = g["w_ukv"]
    w_ukv = jnp.concatenate([ukv[:, :hw].reshape(d.kv_lora, d.mla_heads, d.nope),
                             ukv[:, hw:].reshape(d.kv_lora, d.mla_heads, d.vdim)], axis=-1).reshape(d.kv_lora, -1)
    return dict(g, w_in=w_in, w_uq=w_uq, w_ukv=w_ukv)


def _layer_fwd(h, add, w, norm_mix, q_norm, kv_norm, norm_ffn, cw, cb, ctab, stab, d, tag):
    s = {}
    if add is None:
        s["h"] = h
        s["hn"], s["r1"] = _norm_fwd(h, norm_mix, width=d.d, cidx=0, name=f"norm_mix_{tag}")
    else:
        s["h"], s["hn"], s["r1"] = _norm_fwd(h, norm_mix, width=d.d, cidx=0, add=add, name=f"norm_mix_{tag}")
    s["pa"] = _mm(s["hn"], w["w_ina"], out_dtype=BF16, name=f"proj_a_{tag}")
    s["pb"] = _mm(s["hn"], w["w_inb"], name=f"proj_b_{tag}")
    s["o_sb"], s["carry"] = _sb_fwd(s["pa"], d, f"sb_fwd_{tag}")
    s["cqn"], s["rq"] = _norm_fwd(s["pb"], q_norm, width=d.q_lora, cidx=0, name=f"norm_q_{tag}")
    s["ckn"], s["rk"] = _norm_fwd(s["pb"], kv_norm, width=d.kv_lora, cidx=d.q_lora // d.kv_lora, name=f"norm_kv_{tag}")
    qraw = _mm(s["cqn"], w["w_uq"], name=f"uq_{tag}")
    s["kv"] = _mm(s["ckn"], w["w_ukv"], out_dtype=BF16, name=f"ukv_{tag}")
    s["qm"], s["kr"] = _mla_prep_fwd(qraw, s["pb"], ctab, stab, d, f"mla_prep_{tag}")
    s["o_mla"], s["lse"] = _mla_fwd(s["qm"], s["kv"], s["kr"], d, f"mla_fwd_{tag}")
    s["b_sb"] = _mm(s["o_sb"], w["w_sb_out"], name=f"sb_out_{tag}")
    s["b_mla"] = _mm(s["o_mla"], w["w_mla_out"], name=f"mla_out_{tag}")
    s["merged"] = _gate_fwd(s["pb"], s["b_sb"], s["b_mla"], d, f"gate_{tag}")
    mix = _mm(s["merged"], w["w_o"], name=f"w_o_{tag}")
    s["h1"], s["hn2"], s["r2"] = _norm_fwd(s["h"], norm_ffn, width=d.d, cidx=0, add=mix, name=f"norm_ffn_{tag}")
    s["up"] = _mm(s["hn2"], w["w_up"], name=f"w_up_{tag}")
    s["act"] = _convglu_fwd(s["up"], cw, cb, d, f"convglu_{tag}")
    ffn = _mm(s["act"], w["w_down"], name=f"w_down_{tag}")
    return s, ffn


def _layer_bwd(dh2, s, w, norm_mix, q_norm, kv_norm, norm_ffn, cw, cb, ctab, stab, d, tag):
    g = {}
    dact = _mm(dh2, w["w_down"], tb=True, name=f"d_act_{tag}")
    g["w_down"] = _mm(s["act"], dh2, ta=True, name=f"g_w_down_{tag}")
    dup_a, dup_g, sums_a, sums_g = _convglu_bwd(s["up"], dact, cw, cb, d, f"convglu_bwd_{tag}")
    dup = jnp.concatenate([dup_a, dup_g], axis=1)
    g["conv_w"] = jnp.concatenate([sums_a[0:3], sums_g[0:3]], axis=1)
    g["conv_b"] = jnp.concatenate([sums_a[3], sums_g[3]], axis=0)
    g["w_up"] = _mm(s["hn2"], dup, ta=True, name=f"g_w_up_{tag}")
    dhn2 = _mm(dup, w["w_up"], tb=True, name=f"d_hn2_{tag}")
    dh1, g["norm_ffn"] = _norm_bwd(dhn2, s["h1"], s["r2"], norm_ffn, width=d.d, cidx=0, dres=dh2, name=f"norm_ffn_bwd_{tag}")
    dmerged = _mm(dh1, w["w_o"], tb=True, name=f"d_merged_{tag}")
    g["w_o"] = _mm(s["merged"], dh1, ta=True, name=f"g_w_o_{tag}")
    db_sb, db_mla, dg_sb, dg_mla = _gate_bwd(dmerged, s["pb"], s["b_sb"], s["b_mla"], d, f"gate_bwd_{tag}")
    do_sb = _mm(db_sb, w["w_sb_out"], tb=True, name=f"d_o_sb_{tag}")
    g["w_sb_out"] = _mm(s["o_sb"], db_sb, ta=True, name=f"g_w_sb_out_{tag}")
    do_mla = _mm(db_mla, w["w_mla_out"], tb=True, name=f"d_o_mla_{tag}")
    g["w_mla_out"] = _mm(s["o_mla"], db_mla, ta=True, name=f"g_w_mla_out_{tag}")
    dq_sb, dk_sb, dv_sb = _sb_bwd(s["pa"], do_sb, s["carry"], d, f"sb_bwd_{tag}")
    dqm, dkn, dv, dkr = _mla_bwd(s["qm"], s["kv"], s["kr"], s["o_mla"], do_mla, s["lse"], d, f"mla_bwd_{tag}")
    dqraw, dkr128 = _mla_prep_bwd(dqm, dkr, ctab, stab, d, f"mla_prep_bwd_{tag}")
    dkv = jnp.concatenate([dkn.astype(BF16), dv.astype(BF16)], axis=1)
    dcqn = _mm(dqraw, w["w_uq"], tb=True, name=f"d_cq_{tag}")
    g["w_uq"] = _mm(s["cqn"], dqraw, ta=True, name=f"g_w_uq_{tag}")
    dckn = _mm(dkv, w["w_ukv"], tb=True, name=f"d_ckv_{tag}")
    g["w_ukv"] = _mm(s["ckn"], dkv, ta=True, name=f"g_w_ukv_{tag}")
    dcq, g["q_norm"] = _norm_bwd(dcqn, s["pb"], s["rq"], q_norm, width=d.q_lora, cidx=0, out_dtype=BF16, name=f"norm_q_bwd_{tag}")
    dckv, g["kv_norm"] = _norm_bwd(dckn, s["pb"], s["rk"], kv_norm, width=d.kv_lora, cidx=d.q_lora // d.kv_lora,
                                   out_dtype=BF16, name=f"norm_kv_bwd_{tag}")
    zpad = jnp.zeros((d.lp, d.g_off - d.kr_off - LANES), BF16)
    dproj = jnp.concatenate([dq_sb, dk_sb.astype(BF16), dv_sb.astype(BF16), dcq, dckv, dkr128, zpad, dg_sb, dg_mla], axis=1)
    g["w_in"] = _mm(s["hn"], dproj, ta=True, name=f"g_w_in_{tag}")
    dhn = _mm(dproj, w["w_in"], tb=True, name=f"d_hn_{tag}")
    dh, g["norm_mix"] = _norm_bwd(dhn, s["h"], s["r1"], norm_mix, width=d.d, cidx=0, dres=dh1, name=f"norm_mix_bwd_{tag}")
    return dh, g


def _step(d, x, p, m, v, loss_target):
    x = x.reshape(d.seq, d.d)
    target = loss_target.reshape(d.seq, d.d)

    big_local = _pack_rows(jnp.concatenate([p[n].astype(BF16).reshape(-1) for n in BIG]), 16)
    small_local = _pack_rows(jnp.concatenate([p[n].reshape(-1) for n in SMALL]), 8)
    big_all, small_all = _exchange([big_local, small_local], scatter=False, name="gather_weights")
    big_all = big_all.reshape(N_DEV, -1)
    small_all = small_all.reshape(N_DEV, -1)
    full, off = {}, 0
    for n in BIG:
        size = p[n].size
        full[n] = _full_from_slots(big_all[:, off:off + size].reshape((N_DEV,) + p[n].shape), n)
        off += size
    off = 0
    for n in SMALL:
        size = p[n].size
        full[n] = _full_from_slots(small_all[:, off:off + size].reshape((N_DEV,) + p[n].shape), n)
        off += size

    pos = jnp.arange(d.lp, dtype=F32) - d.pad
    half = d.rope // 2
    freqs = ROPE_THETA ** (-jnp.arange(half, dtype=F32) / half)
    ang = pos[:, None] * freqs[None, :]
    cos, sin = jnp.cos(ang), jnp.sin(ang)
    zero = jnp.zeros((d.lp, LANES - d.rope), F32)
    ctab = jnp.concatenate([cos, cos, zero], axis=1)
    stab = jnp.concatenate([-sin, sin, zero], axis=1)

    h = jnp.concatenate([jnp.zeros((d.pad, d.d), F32), full["meta_tokens"], x], axis=0)
    ws = [_prime_weights(full, l, d) for l in range(d.depth)]
    saved, add = [], None
    for l in range(d.depth):
        s, add = _layer_fwd(h, add, ws[l], p["norm_mix"][l], p["q_norm"][l], p["kv_norm"][l], p["norm_ffn"][l],
                            full["conv_w"][l], p["conv_b"][l].reshape(1, -1), ctab, stab, d, f"l{l}")
        saved.append(s)
        h = s["h1"]
    h_out = _residual_add(h, add, "final_add")
    dh, loss_part, g_final = _head(h_out, target, p["final_norm"], d, "head")

    grads = [None] * d.depth
    for l in reversed(range(d.depth)):
        dh, g = _layer_bwd(dh, saved[l], ws[l], p["norm_mix"][l], p["q_norm"][l], p["kv_norm"][l], p["norm_ffn"][l],
                           full["conv_w"][l], p["conv_b"][l].reshape(1, -1), ctab, stab, d, f"l{l}")
        grads[l] = _unprime_grads(g, d)
    grad_x = dh[d.first_tok:].reshape(1, d.seq, d.d)

    gfull = {n: jnp.stack([grads[l][n] for l in range(d.depth)]) for n in BIG + ["conv_w"]}
    gfull["meta_tokens"] = dh[d.pad:d.first_tok]
    gsend = _pack_rows_batched(jnp.concatenate([_slots_from_full(gfull[n], n) for n in SHARDED], axis=1), 128)
    (grecv,) = _exchange([gsend], scatter=True, name="scatter_grads")
    pack = lambda t: _pack_rows(jnp.concatenate([t[n].reshape(-1) for n in SHARDED]), 128)
    outs_sh = _adamw(grecv, pack(p), pack(m), pack(v), "adamw_sharded")

    grep = {n: jnp.stack([grads[l][n].reshape(-1) for l in range(d.depth)]) for n in REPL if n != "final_norm"}
    grep["final_norm"] = g_final.reshape(-1)
    rflat = jnp.concatenate([grep[n].reshape(-1) for n in REPL] + [loss_part[0, 0:1]])
    (rparts,) = _exchange([_pack_rows(rflat, 8)], scatter=False, name="gather_small_grads")
    packr = lambda t: _pack_rows(jnp.concatenate([t[n].reshape(-1) for n in REPL] + [jnp.zeros((1,), F32)]), 8)
    outs_rp = _adamw(rparts, packr(p), packr(m), packr(v), "adamw_replicated")

    def unpack(flat, names, extra=0):
        res, off = {}, 0
        flat = flat.reshape(-1)
        for n in names:
            res[n] = flat[off:off + p[n].size].reshape(p[n].shape)
            off += p[n].size
        return res, flat[off:off + extra]

    results = []
    loss = None
    for k in range(4):
        sh, _ = unpack(outs_sh[k], SHARDED)
        rp, tail = unpack(outs_rp[k], REPL, 1)
        if k == 0:
            loss = tail[0]
        results.append({**sh, **rp})
    return loss, grad_x, results


def _residual_add(h, add, name):
    rows, width = h.shape
    tr = _pick(rows, (256, 128))

    def body(h_ref, a_ref, o_ref):
        o_ref[...] = h_ref[...] + a_ref[...]

    blk = pl.BlockSpec((tr, width), lambda i: (i, 0))
    return pl.pallas_call(body, name=name, grid=(rows // tr,), in_specs=[blk, blk], out_specs=blk,
                          out_shape=jax.ShapeDtypeStruct((rows, width), F32),
                          compiler_params=_cparams(("parallel",)))(h, add)


WEIGHTS = ["meta_tokens", "norm_mix", "w_in", "q_norm", "w_uq", "kv_norm", "w_ukv", "w_sb_out", "w_mla_out", "w_o",
           "norm_ffn", "w_up", "conv_w", "conv_b", "w_down", "final_norm"]


def _run(d, x, weights, loss_target, moments_m, moments_v):
    p = dict(zip(WEIGHTS, weights))
    m = dict(zip(WEIGHTS, moments_m))
    v = dict(zip(WEIGHTS, moments_v))
    loss, grad_x, res = _step(d, x, p, m, v, loss_target)
    out = [loss, grad_x]
    for k in range(4):
        out += [res[k][n] for n in WEIGHTS]
    return tuple(out)


def kernel(x, meta_tokens, norm_mix, w_in, q_norm, w_uq, kv_norm, w_ukv, w_sb_out, w_mla_out, w_o, norm_ffn, w_up, conv_w, conv_b, w_down, final_norm, loss_target, m_meta_tokens, m_norm_mix, m_w_in, m_q_norm, m_w_uq, m_kv_norm, m_w_ukv, m_w_sb_out, m_w_mla_out, m_w_o, m_norm_ffn, m_w_up, m_conv_w, m_conv_b, m_w_down, m_final_norm, v_meta_tokens, v_norm_mix, v_w_in, v_q_norm, v_w_uq, v_kv_norm, v_w_ukv, v_w_sb_out, v_w_mla_out, v_w_o, v_norm_ffn, v_w_up, v_conv_w, v_conv_b, v_w_down, v_final_norm):
    weights = [meta_tokens, norm_mix, w_in, q_norm, w_uq, kv_norm, w_ukv, w_sb_out, w_mla_out, w_o, norm_ffn, w_up,
               conv_w, conv_b, w_down, final_norm]
    ms = [m_meta_tokens, m_norm_mix, m_w_in, m_q_norm, m_w_uq, m_kv_norm, m_w_ukv, m_w_sb_out, m_w_mla_out, m_w_o,
          m_norm_ffn, m_w_up, m_conv_w, m_conv_b, m_w_down, m_final_norm]
    vs = [v_meta_tokens, v_norm_mix, v_w_in, v_q_norm, v_w_uq, v_kv_norm, v_w_ukv, v_w_sb_out, v_w_mla_out, v_w_o,
          v_norm_ffn, v_w_up, v_conv_w, v_conv_b, v_w_down, v_final_norm]
    return _run(PROD, x, weights, loss_target, ms, vs)
```

```python
import jax
import jax.numpy as jnp
from jax import lax
from jax.experimental import pallas as pl
from jax.experimental.pallas import tpu as pltpu

F32 = jnp.float32
BF16 = jnp.bfloat16

EPS = 1e-6
ROPE_THETA = 10000.0
ADAM_LR = 0.001
ADAM_B1 = 0.9
ADAM_B2 = 0.999
ADAM_EPS = 1e-08
ADAM_WD = 0.01
ADAM_STEP = 10
NEG = -1e30
LANES = 128
PACK_W = 1024
V7X_VMEM_LIMIT = 48 * 1024 * 1024
V7X_VMEM_LIMIT_BIG = 58 * 1024 * 1024
MESH_AXES = ("x", "y", "c")
N_DEV = 8
FLIPS = [(0, 0, 1), (0, 1, 0), (0, 1, 1), (1, 0, 0), (1, 0, 1), (1, 1, 0), (1, 1, 1)]


class _Dims:
    def __init__(self, d_model=2048, seq=8192, depth=2, n_meta=16, block=128, sb_heads=8, hd=128,
                 mla_heads=8, q_lora=512, kv_lora=256, nope=128, rope=64, vdim=128, d_ff=5632, tq=None):
        self.d, self.seq, self.depth, self.n_meta, self.block = d_model, seq, depth, n_meta, block
        self.sb_heads, self.hd, self.mla_heads = sb_heads, hd, mla_heads
        self.q_lora, self.kv_lora, self.nope, self.rope, self.vdim, self.f = q_lora, kv_lora, nope, rope, vdim, d_ff
        assert hd == LANES and nope == LANES and vdim == LANES and 2 * rope == LANES
        self.pad = block - n_meta
        self.lp = self.pad + n_meta + seq
        self.first_tok = self.pad + n_meta
        assert self.first_tok == block and self.lp % block == 0 and self.lp // block <= LANES
        self.tq = tq or next(t for t in (640, 512, 256, 128) if self.lp % t == 0)
        assert self.tq % block == 0 and self.lp % self.tq == 0
        self.sbw = sb_heads * hd
        self.mlaw = mla_heads * vdim
        self.wa = 3 * self.sbw
        self.d_in = 3 * self.sbw + q_lora + kv_lora + rope + 2 * d_model
        self.tg = min(1024, d_model)
        self.kr_off = q_lora + kv_lora
        raw = self.kr_off + LANES
        self.g_off = -(-raw // self.tg) * self.tg
        self.wb = self.g_off + 2 * d_model
        self.qw = mla_heads * 2 * LANES


PROD = _Dims()


def _pick(n, prefs):
    for p in prefs:
        if n % p == 0:
            return p
    return n


def _cparams(sem, limit=V7X_VMEM_LIMIT):
    return pltpu.CompilerParams(dimension_semantics=sem, vmem_limit_bytes=limit)


def _mm(a, b, *, ta=False, tb=False, out_dtype=F32, name):
    if ta:
        kdim, m = a.shape
    else:
        m, kdim = a.shape
    if tb:
        n, k2 = b.shape
    else:
        k2, n = b.shape
    assert kdim == k2, (a.shape, b.shape, ta, tb)
    tm = _pick(m, (640, 512, 256, 128))
    tn = _pick(n, (1024, 512, 384, 256, 128))
    tk = _pick(kdim, (512, 640, 256, 128))
    nk = kdim // tk
    dn = (((0 if ta else 1,), (1 if tb else 0,)), ((), ()))

    def body(a_ref, b_ref, o_ref, acc_ref):
        k = pl.program_id(2)

        @pl.when(k == 0)
        def _():
            acc_ref[...] = jnp.zeros_like(acc_ref)

        acc_ref[...] += lax.dot_general(a_ref[...].astype(BF16), b_ref[...].astype(BF16), dn,
                                        preferred_element_type=F32)

        @pl.when(k == nk - 1)
        def _():
            o_ref[...] = acc_ref[...].astype(out_dtype)

    a_spec = pl.BlockSpec((tk, tm), lambda i, j, k: (k, i)) if ta else pl.BlockSpec((tm, tk), lambda i, j, k: (i, k))
    b_spec = pl.BlockSpec((tn, tk), lambda i, j, k: (j, k)) if tb else pl.BlockSpec((tk, tn), lambda i, j, k: (k, j))
    return pl.pallas_call(
        body, name=name, grid=(m // tm, n // tn, nk), in_specs=[a_spec, b_spec],
        out_specs=pl.BlockSpec((tm, tn), lambda i, j, k: (i, j)),
        out_shape=jax.ShapeDtypeStruct((m, n), out_dtype),
        scratch_shapes=[pltpu.VMEM((tm, tn), F32)],
        compiler_params=_cparams(("parallel", "parallel", "arbitrary")),
    )(a, b)


def _norm_fwd(x, g, *, width, cidx, add=None, name):
    rows = x.shape[0]
    tr = _pick(rows, (256, 128))
    has_add = add is not None

    def body(*refs):
        if has_add:
            x_ref, a_ref, g_ref, xn_ref, y_ref, r_ref = refs
            xv = x_ref[...] + a_ref[...]
            xn_ref[...] = xv
        else:
            x_ref, g_ref, y_ref, r_ref = refs
            xv = x_ref[...]
        r = lax.rsqrt(jnp.mean(xv * xv, axis=1, keepdims=True) + EPS)
        y_ref[...] = (xv * r * g_ref[...]).astype(BF16)
        r_ref[...] = r

    blk = pl.BlockSpec((tr, width), lambda i: (i, 0))
    in_specs = [pl.BlockSpec((tr, width), lambda i: (i, cidx))]
    args = [x]
    if has_add:
        in_specs.append(blk)
        args.append(add)
    in_specs.append(pl.BlockSpec((1, width), lambda i: (0, 0)))
    args.append(g.reshape(1, width))
    out_specs = [blk, pl.BlockSpec((tr, 1), lambda i: (i, 0))]
    out_shape = [jax.ShapeDtypeStruct((rows, width), BF16), jax.ShapeDtypeStruct((rows, 1), F32)]
    if has_add:
        out_specs.insert(0, blk)
        out_shape.insert(0, jax.ShapeDtypeStruct((rows, width), F32))
    return pl.pallas_call(body, name=name, grid=(rows // tr,), in_specs=in_specs, out_specs=out_specs,
                          out_shape=out_shape, compiler_params=_cparams(("parallel",)))(*args)


def _norm_bwd(dy, x, r, g, *, width, cidx, dres=None, out_dtype=F32, name):
    rows = x.shape[0]
    tr = _pick(rows, (256, 128))
    has_res = dres is not None

    def body(*refs):
        if has_res:
            dy_ref, x_ref, r_ref, g_ref, dr_ref, dx_ref, dg_ref = refs
        else:
            dy_ref, x_ref, r_ref, g_ref, dx_ref, dg_ref = refs
        i = pl.program_id(0)

        @pl.when(i == 0)
        def _():
            dg_ref[...] = jnp.zeros_like(dg_ref)

        dyv, xv, rv = dy_ref[...], x_ref[...], r_ref[...]
        gy = dyv * g_ref[...]
        c = jnp.mean(gy * xv, axis=1, keepdims=True)
        dx = rv * gy - xv * (rv * rv * rv) * c
        if has_res:
            dx = dx + dr_ref[...]
        dx_ref[...] = dx.astype(out_dtype)
        dg_ref[...] += jnp.sum(dyv * xv * rv, axis=0, keepdims=True)

    blk = pl.BlockSpec((tr, width), lambda i: (i, 0))
    in_specs = [blk, pl.BlockSpec((tr, width), lambda i: (i, cidx)), pl.BlockSpec((tr, 1), lambda i: (i, 0)),
                pl.BlockSpec((1, width), lambda i: (0, 0))]
    args = [dy, x, r, g.reshape(1, width)]
    if has_res:
        in_specs.append(blk)
        args.append(dres)
    return pl.pallas_call(
        body, name=name, grid=(rows // tr,), in_specs=in_specs,
        out_specs=[blk, pl.BlockSpec((1, width), lambda i: (0, 0))],
        out_shape=[jax.ShapeDtypeStruct((rows, width), out_dtype), jax.ShapeDtypeStruct((1, width), F32)],
        compiler_params=_cparams(("arbitrary",)))(*args)


def _split3(x):
    h1 = x.astype(BF16)
    r1 = x - h1.astype(F32)
    h2 = r1.astype(BF16)
    h3 = (r1 - h2.astype(F32)).astype(BF16)
    return h1, h2, h3


def _cum(x, tri):
    h1, h2, h3 = _split3(x)
    dot = lambda h: jnp.dot(h, tri, preferred_element_type=F32)
    return dot(h1) + dot(h2) + dot(h3)


def _dot_nt(a, b):
    return lax.dot_general(a, b, (((1,), (1,)), ((), ())), preferred_element_type=F32)


def _dot_tn(a, b):
    return lax.dot_general(a, b, (((0,), (0,)), ((), ())), preferred_element_type=F32)


def _sb_fwd(qkv, d, name):
    nh, hd, lp, tq, t = d.sb_heads, d.hd, d.lp, d.tq, d.block
    r = tq // t
    scale = hd ** -0.5
    pad = d.pad

    def body(q_ref, k_ref, v_ref, o_ref, c_ref):
        i = pl.program_id(1)
        q = q_ref[...]
        row = i * tq + lax.broadcasted_iota(jnp.int32, (tq, t), 0)
        colb = lax.broadcasted_iota(jnp.int32, (tq, t), 1)
        tri = (lax.broadcasted_iota(jnp.int32, (t, t), 0)
               > lax.broadcasted_iota(jnp.int32, (t, t), 1)).astype(BF16)
        lane = lax.broadcasted_iota(jnp.int32, (tq, LANES), 1)

        c_ref[...] = jnp.zeros_like(c_ref)

        def make_step(masked):
            def step(j, carry):
                acc, run = carry
                off = pl.multiple_of(j * t, t)
                k = k_ref[pl.ds(off, t), :]
                v = v_ref[pl.ds(off, t), :]
                z = _dot_nt(q, k) * scale
                e = jnp.exp(-jnp.abs(z))
                sp = jnp.maximum(z, 0.0) + jnp.log(1.0 + e)
                if masked:
                    col = j * t + colb
                    mask = (col < row) & (col >= pad)
                    l1 = jnp.where(mask, -sp, 0.0)
                else:
                    l1 = -sp
                w = jnp.exp(z - sp + _cum(l1, tri) + run)
                if masked:
                    w = jnp.where(mask, w, 0.0)
                acc = acc + jnp.dot(w.astype(BF16), v, preferred_element_type=F32)
                c_ref[...] = jnp.where(lane == j, run, c_ref[...])
                run = run + jnp.sum(l1, axis=1, keepdims=True)
                return acc, run
            return step

        masked, plain = make_step(True), make_step(False)
        hi = (i + 1) * r
        lo = jnp.maximum(i * r, 1)
        carry = (jnp.zeros((tq, hd), F32), jnp.zeros((tq, 1), F32))
        carry = lax.fori_loop(0, hi - lo, lambda jj, c: masked(hi - 1 - jj, c), carry)
        carry = lax.fori_loop(0, lo - 1, lambda jj, c: plain(lo - 1 - jj, c), carry)
        acc, _ = masked(0, carry)
        o_ref[...] = acc

    return pl.pallas_call(
        body, name=name, grid=(nh, lp // tq),
        in_specs=[pl.BlockSpec((tq, hd), lambda h, i: (i, h)),
                  pl.BlockSpec((lp, hd), lambda h, i: (0, nh + h)),
                  pl.BlockSpec((lp, hd), lambda h, i: (0, 2 * nh + h))],
        out_specs=[pl.BlockSpec((tq, hd), lambda h, i: (i, h)),
                   pl.BlockSpec((None, tq, LANES), lambda h, i: (h, i, 0))],
        out_shape=[jax.ShapeDtypeStruct((lp, nh * hd), F32), jax.ShapeDtypeStruct((nh, lp, LANES), F32)],
        compiler_params=_cparams(("parallel", "arbitrary")),
    )(qkv, qkv, qkv)


def _sb_bwd(qkv, do, carry, d, name):
    nh, hd, lp, tq, t = d.sb_heads, d.hd, d.lp, d.tq, d.block
    r = tq // t
    scale = hd ** -0.5
    pad = d.pad

    def body(q_ref, k_ref, v_ref, do_ref, c_ref, dq_ref, dk_ref, dv_ref):
        i = pl.program_id(1)

        @pl.when(i == 0)
        def _():
            dk_ref[...] = jnp.zeros_like(dk_ref)
            dv_ref[...] = jnp.zeros_like(dv_ref)

        q = q_ref[...]
        dob = do_ref[...].astype(BF16)
        row = i * tq + lax.broadcasted_iota(jnp.int32, (tq, t), 0)
        colb = lax.broadcasted_iota(jnp.int32, (tq, t), 1)
        ri = lax.broadcasted_iota(jnp.int32, (t, t), 0)
        ci = lax.broadcasted_iota(jnp.int32, (t, t), 1)
        tri_suf = (ri > ci).astype(BF16)
        tri_pre = (ri < ci).astype(BF16)
        lane = lax.broadcasted_iota(jnp.int32, (tq, LANES), 1)

        def make_step(masked):
            def step(j, carry):
                dq, pc = carry
                off = pl.multiple_of(j * t, t)
                k = k_ref[pl.ds(off, t), :]
                v = v_ref[pl.ds(off, t), :]
                z = _dot_nt(q, k) * scale
                e = jnp.exp(-jnp.abs(z))
                sp = jnp.maximum(z, 0.0) + jnp.log(1.0 + e)
                if masked:
                    col = j * t + colb
                    mask = (col < row) & (col >= pad)
                    l1 = jnp.where(mask, -sp, 0.0)
                else:
                    l1 = -sp
                run = jnp.sum(jnp.where(lane == j, c_ref[...], 0.0), axis=1, keepdims=True)
                w = jnp.exp(z - sp + _cum(l1, tri_suf) + run)
                if masked:
                    w = jnp.where(mask, w, 0.0)
                g = w * _dot_nt(dob, v)
                p = _cum(g, tri_pre) + pc
                inv = 1.0 / (1.0 + e)
                sig = jnp.where(z >= 0.0, inv, e * inv)
                dz = (g * (1.0 - sig) - sig * p) * scale
                if masked:
                    dz = jnp.where(mask, dz, 0.0)
                dzb = dz.astype(BF16)
                dq = dq + jnp.dot(dzb, k, preferred_element_type=F32)
                dk_ref[pl.ds(off, t), :] += _dot_tn(dzb, q)
                dv_ref[pl.ds(off, t), :] += _dot_tn(w.astype(BF16), dob)
                pc = pc + jnp.sum(g, axis=1, keepdims=True)
                return dq, pc
            return step

        masked, plain = make_step(True), make_step(False)
        lo = jnp.maximum(i * r, 1)
        carry = masked(0, (jnp.zeros((tq, hd), F32), jnp.zeros((tq, 1), F32)))
        carry = lax.fori_loop(1, lo, plain, carry)
        dq, _ = lax.fori_loop(lo, (i + 1) * r, masked, carry)
        dq_ref[...] = dq.astype(BF16)

    w3 = nh * hd
    return pl.pallas_call(
        body, name=name, grid=(nh, lp // tq),
        in_specs=[pl.BlockSpec((tq, hd), lambda h, i: (i, h)),
                  pl.BlockSpec((lp, hd), lambda h, i: (0, nh + h)),
                  pl.BlockSpec((lp, hd), lambda h, i: (0, 2 * nh + h)),
                  pl.BlockSpec((tq, hd), lambda h, i: (i, h)),
                  pl.BlockSpec((None, tq, LANES), lambda h, i: (h, i, 0))],
        out_specs=[pl.BlockSpec((tq, hd), lambda h, i: (i, h)),
                   pl.BlockSpec((lp, hd), lambda h, i: (0, h)),
                   pl.BlockSpec((lp, hd), lambda h, i: (0, h))],
        out_shape=[jax.ShapeDtypeStruct((lp, w3), BF16), jax.ShapeDtypeStruct((lp, w3), F32),
                   jax.ShapeDtypeStruct((lp, w3), F32)],
        compiler_params=_cparams(("arbitrary", "arbitrary")),
    )(qkv, qkv, qkv, do, carry)


def _mla_prep_fwd(qraw, projb, ctab, stab, d, name):
    lp, nh = d.lp, d.mla_heads
    tr = _pick(lp, (256, 128))
    kidx = d.kr_off // LANES

    def rope(u, c, s):
        return u * c + pltpu.roll(u, LANES // 2, 1) * s

    def body(q_ref, k_ref, c_ref, s_ref, qm_ref, kr_ref):
        c, s = c_ref[...], s_ref[...]
        for h in range(nh):
            base = 2 * LANES * h
            qm_ref[:, base:base + LANES] = q_ref[:, base:base + LANES].astype(BF16)
            qm_ref[:, base + LANES:base + 2 * LANES] = rope(q_ref[:, base + LANES:base + 2 * LANES], c, s).astype(BF16)
        kr_ref[...] = rope(k_ref[...], c, s).astype(BF16)

    tab = pl.BlockSpec((tr, LANES), lambda i: (i, 0))
    return pl.pallas_call(
        body, name=name, grid=(lp // tr,),
        in_specs=[pl.BlockSpec((tr, d.qw), lambda i: (i, 0)), pl.BlockSpec((tr, LANES), lambda i: (i, kidx)), tab, tab],
        out_specs=[pl.BlockSpec((tr, d.qw), lambda i: (i, 0)), tab],
        out_shape=[jax.ShapeDtypeStruct((lp, d.qw), BF16), jax.ShapeDtypeStruct((lp, LANES), BF16)],
        compiler_params=_cparams(("parallel",)))(qraw, projb, ctab, stab)


def _mla_prep_bwd(dqm, dkr, ctab, stab, d, name):
    lp, nh = d.lp, d.mla_heads
    tr = _pick(lp, (256, 128))

    def unrope(g, c, s):
        return g * c + pltpu.roll(g * s, LANES // 2, 1)

    def body(dq_ref, dk_ref, c_ref, s_ref, o_ref, ok_ref):
        c, s = c_ref[...], s_ref[...]
        for h in range(nh):
            base = 2 * LANES * h
            o_ref[:, base:base + LANES] = dq_ref[:, base:base + LANES].astype(BF16)
            o_ref[:, base + LANES:base + 2 * LANES] = unrope(dq_ref[:, base + LANES:base + 2 * LANES], c, s).astype(BF16)
        ok_ref[...] = unrope(dk_ref[...], c, s).astype(BF16)

    tab = pl.BlockSpec((tr, LANES), lambda i: (i, 0))
    wide = pl.BlockSpec((tr, d.qw), lambda i: (i, 0))
    return pl.pallas_call(
        body, name=name, grid=(lp // tr,), in_specs=[wide, tab, tab, tab], out_specs=[wide, tab],
        out_shape=[jax.ShapeDtypeStruct((lp, d.qw), BF16), jax.ShapeDtypeStruct((lp, LANES), BF16)],
        compiler_params=_cparams(("parallel",)))(dqm, dkr, ctab, stab)


def _mla_fwd(qm, kv, kr, d, name):
    nh, lp, t = d.mla_heads, d.lp, d.tq
    scale = (d.nope + d.rope) ** -0.5
    pad = d.pad

    def body(q_ref, kn_ref, v_ref, kr_ref, o_ref, lse_ref):
        i = pl.program_id(1)
        qn = q_ref[:, :LANES]
        qr = q_ref[:, LANES:]
        row = i * t + lax.broadcasted_iota(jnp.int32, (t, t), 0)
        colb = lax.broadcasted_iota(jnp.int32, (t, t), 1)

        def make_step(masked):
            def step(j, carry):
                acc, m, l = carry
                off = pl.multiple_of(j * t, t)
                s = (_dot_nt(qn, kn_ref[pl.ds(off, t), :]) + _dot_nt(qr, kr_ref[pl.ds(off, t), :])) * scale
                if masked:
                    col = j * t + colb
                    s = jnp.where((col <= row) & (col >= pad), s, NEG)
                m_new = jnp.maximum(m, jnp.max(s, axis=1, keepdims=True))
                alpha = jnp.exp(m - m_new)
                p = jnp.exp(s - m_new)
                l = alpha * l + jnp.sum(p, axis=1, keepdims=True)
                acc = alpha * acc + jnp.dot(p.astype(BF16), v_ref[pl.ds(off, t), :], preferred_element_type=F32)
                return acc, m_new, l
            return step

        masked, plain = make_step(True), make_step(False)
        carry = masked(0, (jnp.zeros((t, LANES), F32), jnp.full((t, 1), NEG, F32), jnp.zeros((t, 1), F32)))
        carry = lax.fori_loop(1, i, plain, carry)
        acc, m, l = lax.fori_loop(i, i + jnp.minimum(i, 1), masked, carry)
        rowv = i * t + lax.broadcasted_iota(jnp.int32, (t, LANES), 0)
        o_ref[...] = jnp.where(rowv >= pad, acc / l, 0.0)
        lse_ref[...] = m + jnp.log(l)

    return pl.pallas_call(
        body, name=name, grid=(nh, lp // t),
        in_specs=[pl.BlockSpec((t, 2 * LANES), lambda h, i: (i, h)),
                  pl.BlockSpec((lp, LANES), lambda h, i: (0, h)),
                  pl.BlockSpec((lp, LANES), lambda h, i: (0, nh + h)),
                  pl.BlockSpec((lp, LANES), lambda h, i: (0, 0))],
        out_specs=[pl.BlockSpec((t, LANES), lambda h, i: (i, h)),
                   pl.BlockSpec((None, t, 1), lambda h, i: (h, i, 0))],
        out_shape=[jax.ShapeDtypeStruct((lp, nh * LANES), F32), jax.ShapeDtypeStruct((nh, lp, 1), F32)],
        compiler_params=_cparams(("parallel", "arbitrary")),
    )(qm, kv, kv, kr)


def _mla_bwd(qm, kv, kr, o, do, lse, d, name):
    nh, lp, t = d.mla_heads, d.lp, d.tq
    scale = (d.nope + d.rope) ** -0.5
    pad = d.pad

    def body(q_ref, kn_ref, v_ref, kr_ref, o_ref, do_ref, lse_ref, dq_ref, dkn_ref, dv_ref, dkr_ref):
        h = pl.program_id(0)
        i = pl.program_id(1)

        @pl.when(i == 0)
        def _():
            dkn_ref[...] = jnp.zeros_like(dkn_ref)
            dv_ref[...] = jnp.zeros_like(dv_ref)

        @pl.when((i == 0) & (h == 0))
        def _():
            dkr_ref[...] = jnp.zeros_like(dkr_ref)

        qn = q_ref[:, :LANES]
        qr = q_ref[:, LANES:]
        dof = do_ref[...]
        dob = dof.astype(BF16)
        delta = jnp.sum(dof * o_ref[...], axis=1, keepdims=True)
        lse = lse_ref[...]
        row = i * t + lax.broadcasted_iota(jnp.int32, (t, t), 0)
        colb = lax.broadcasted_iota(jnp.int32, (t, t), 1)

        def make_step(masked):
            def step(j, carry):
                dqn, dqr = carry
                off = pl.multiple_of(j * t, t)
                kn = kn_ref[pl.ds(off, t), :]
                krj = kr_ref[pl.ds(off, t), :]
                v = v_ref[pl.ds(off, t), :]
                s = (_dot_nt(qn, kn) + _dot_nt(qr, krj)) * scale
                if masked:
                    col = j * t + colb
                    mask = (col <= row) & (col >= pad)
                    p = jnp.where(mask, jnp.exp(jnp.where(mask, s, NEG) - lse), 0.0)
                else:
                    p = jnp.exp(s - lse)
                dp = _dot_nt(dob, v)
                dsb = (p * (dp - delta) * scale).astype(BF16)
                dqn = dqn + jnp.dot(dsb, kn, preferred_element_type=F32)
                dqr = dqr + jnp.dot(dsb, krj, preferred_element_type=F32)
                dkn_ref[pl.ds(off, t), :] += _dot_tn(dsb, qn)
                dkr_ref[pl.ds(off, t), :] += _dot_tn(dsb, qr)
                dv_ref[pl.ds(off, t), :] += _dot_tn(p.astype(BF16), dob)
                return dqn, dqr
            return step

        masked, plain = make_step(True), make_step(False)
        carry = masked(0, (jnp.zeros((t, LANES), F32), jnp.zeros((t, LANES), F32)))
        carry = lax.fori_loop(1, i, plain, carry)
        dqn, dqr = lax.fori_loop(i, i + jnp.minimum(i, 1), masked, carry)
        dq_ref[:, :LANES] = dqn
        dq_ref[:, LANES:] = dqr

    return pl.pallas_call(
        body, name=name, grid=(nh, lp // t),
        in_specs=[pl.BlockSpec((t, 2 * LANES), lambda h, i: (i, h)),
                  pl.BlockSpec((lp, LANES), lambda h, i: (0, h), pipeline_mode=pl.Buffered(1)),
                  pl.BlockSpec((lp, LANES), lambda h, i: (0, nh + h), pipeline_mode=pl.Buffered(1)),
                  pl.BlockSpec((lp, LANES), lambda h, i: (0, 0), pipeline_mode=pl.Buffered(1)),
                  pl.BlockSpec((t, LANES), lambda h, i: (i, h)),
                  pl.BlockSpec((t, LANES), lambda h, i: (i, h)),
                  pl.BlockSpec((None, t, 1), lambda h, i: (h, i, 0))],
        out_specs=[pl.BlockSpec((t, 2 * LANES), lambda h, i: (i, h)),
                   pl.BlockSpec((lp, LANES), lambda h, i: (0, h)),
                   pl.BlockSpec((lp, LANES), lambda h, i: (0, h)),
                   pl.BlockSpec((lp, LANES), lambda h, i: (0, 0))],
        out_shape=[jax.ShapeDtypeStruct((lp, nh * 2 * LANES), F32), jax.ShapeDtypeStruct((lp, nh * LANES), F32),
                   jax.ShapeDtypeStruct((lp, nh * LANES), F32), jax.ShapeDtypeStruct((lp, LANES), F32)],
        compiler_params=_cparams(("arbitrary", "arbitrary"), V7X_VMEM_LIMIT_BIG),
    )(qm, kv, kv, kr, o, do, lse)


def _sigmoid(x):
    return 1.0 / (1.0 + jnp.exp(-x))


def _gate_fwd(projb, b_sb, b_mla, d, name):
    lp, tg = d.lp, d.tg
    tr = _pick(lp, (256, 128))
    o1, o2 = d.g_off // tg, (d.g_off + d.d) // tg

    def body(g1_ref, g2_ref, b1_ref, b2_ref, o_ref):
        o_ref[...] = (_sigmoid(g1_ref[...]) * b1_ref[...] + _sigmoid(g2_ref[...]) * b2_ref[...]).astype(BF16)

    blk = pl.BlockSpec((tr, tg), lambda i, j: (i, j))
    return pl.pallas_call(
        body, name=name, grid=(lp // tr, d.d // tg),
        in_specs=[pl.BlockSpec((tr, tg), lambda i, j: (i, o1 + j)), pl.BlockSpec((tr, tg), lambda i, j: (i, o2 + j)),
                  blk, blk],
        out_specs=blk, out_shape=jax.ShapeDtypeStruct((lp, d.d), BF16),
        compiler_params=_cparams(("parallel", "parallel")))(projb, projb, b_sb, b_mla)


def _gate_bwd(dm, projb, b_sb, b_mla, d, name):
    lp, tg = d.lp, d.tg
    tr = _pick(lp, (256, 128))
    o1, o2 = d.g_off // tg, (d.g_off + d.d) // tg

    def body(dm_ref, g1_ref, g2_ref, b1_ref, b2_ref, db1_ref, db2_ref, dg1_ref, dg2_ref):
        dmv = dm_ref[...]
        s1, s2 = _sigmoid(g1_ref[...]), _sigmoid(g2_ref[...])
        db1_ref[...] = (dmv * s1).astype(BF16)
        db2_ref[...] = (dmv * s2).astype(BF16)
        dg1_ref[...] = (dmv * b1_ref[...] * s1 * (1.0 - s1)).astype(BF16)
        dg2_ref[...] = (dmv * b2_ref[...] * s2 * (1.0 - s2)).astype(BF16)

    blk = pl.BlockSpec((tr, tg), lambda i, j: (i, j))
    out = jax.ShapeDtypeStruct((lp, d.d), BF16)
    return pl.pallas_call(
        body, name=name, grid=(lp // tr, d.d // tg),
        in_specs=[blk, pl.BlockSpec((tr, tg), lambda i, j: (i, o1 + j)), pl.BlockSpec((tr, tg), lambda i, j: (i, o2 + j)),
                  blk, blk],
        out_specs=[blk] * 4, out_shape=[out] * 4,
        compiler_params=_cparams(("parallel", "parallel")))(dm, projb, projb, b_sb, b_mla)


HALO = 8


def _conv_tiles(d):
    return _pick(d.lp, (640, 512, 256, 128)), _pick(d.f, (512, 256, 128))


def _convglu_fwd(up, cw, cb, d, name):
    lp, f = d.lp, d.f
    tr, tc = _conv_tiles(d)
    nf = f // tc
    hb = tr // HALO
    pad = d.pad

    def body(a_ref, g_ref, pa_ref, pg_ref, wa_ref, wg_ref, ba_ref, bg_ref, o_ref, xa, xg):
        i = pl.program_id(1)
        keep = (i > 0).astype(F32)
        xa[0:HALO, :] = pa_ref[...] * keep
        xg[0:HALO, :] = pg_ref[...] * keep
        xa[HALO:, :] = a_ref[...]
        xg[HALO:, :] = g_ref[...]

        def conv(x, w_ref, b_ref):
            return (b_ref[...] + x[pl.ds(HALO - 2, tr), :] * w_ref[0:1, :] + x[pl.ds(HALO - 1, tr), :] * w_ref[1:2, :]
                    + x[pl.ds(HALO, tr), :] * w_ref[2:3, :])

        ua = conv(xa, wa_ref, ba_ref)
        ug = conv(xg, wg_ref, bg_ref)
        row = i * tr + lax.broadcasted_iota(jnp.int32, (tr, tc), 0)
        o_ref[...] = jnp.where(row >= pad, ua * _sigmoid(ua) * ug, 0.0).astype(BF16)

    prev = lambda j, i: (jnp.maximum(i * hb - 1, 0), j)
    prevg = lambda j, i: (jnp.maximum(i * hb - 1, 0), nf + j)
    return pl.pallas_call(
        body, name=name, grid=(nf, lp // tr),
        in_specs=[pl.BlockSpec((tr, tc), lambda j, i: (i, j)), pl.BlockSpec((tr, tc), lambda j, i: (i, nf + j)),
                  pl.BlockSpec((HALO, tc), prev), pl.BlockSpec((HALO, tc), prevg),
                  pl.BlockSpec((3, tc), lambda j, i: (0, j)), pl.BlockSpec((3, tc), lambda j, i: (0, nf + j)),
                  pl.BlockSpec((1, tc), lambda j, i: (0, j)), pl.BlockSpec((1, tc), lambda j, i: (0, nf + j))],
        out_specs=pl.BlockSpec((tr, tc), lambda j, i: (i, j)),
        out_shape=jax.ShapeDtypeStruct((lp, f), BF16),
        scratch_shapes=[pltpu.VMEM((tr + HALO, tc), F32), pltpu.VMEM((tr + HALO, tc), F32)],
        compiler_params=_cparams(("parallel", "arbitrary")))(up, up, up, up, cw, cw, cb, cb)


def _convglu_bwd(up, dact, cw, cb, d, name):
    lp, f = d.lp, d.f
    tr, tc = _conv_tiles(d)
    nf = f // tc
    hb = tr // HALO
    nrow = lp // tr
    pad = d.pad
    te = tr + HALO

    def body(a_ref, g_ref, pa_ref, pg_ref, na_ref, ng_ref, da_ref, nd_ref, wa_ref, wg_ref, ba_ref, bg_ref,
             oa_ref, og_ref, sa_ref, sg_ref, xa, xg, xd, ya, yg):
        i = pl.program_id(1)

        @pl.when(i == 0)
        def _():
            sa_ref[...] = jnp.zeros_like(sa_ref)
            sg_ref[...] = jnp.zeros_like(sg_ref)

        keep_p = (i > 0).astype(F32)
        keep_n = (i < nrow - 1).astype(F32)
        xa[0:HALO, :] = pa_ref[...] * keep_p
        xg[0:HALO, :] = pg_ref[...] * keep_p
        xa[HALO:HALO + tr, :] = a_ref[...]
        xg[HALO:HALO + tr, :] = g_ref[...]
        xa[HALO + tr:, :] = na_ref[...] * keep_n
        xg[HALO + tr:, :] = ng_ref[...] * keep_n
        xd[0:tr, :] = da_ref[...]
        xd[tr:, :] = nd_ref[...] * keep_n

        def conv(x, w_ref, b_ref):
            return (b_ref[...] + x[pl.ds(HALO - 2, te), :] * w_ref[0:1, :] + x[pl.ds(HALO - 1, te), :] * w_ref[1:2, :]
                    + x[pl.ds(HALO, te), :] * w_ref[2:3, :])

        ua = conv(xa, wa_ref, ba_ref)
        ug = conv(xg, wg_ref, bg_ref)
        sg = _sigmoid(ua)
        dact = xd[...]
        ya[...] = dact * ug * (sg * (1.0 + ua * (1.0 - sg)))
        yg[...] = dact * (ua * sg)
        row = i * tr + lax.broadcasted_iota(jnp.int32, (tr, tc), 0)

        def back(y, x, w_ref, o_ref, s_ref):
            y0 = y[pl.ds(0, tr), :]
            dup = y0 * w_ref[2:3, :] + y[pl.ds(1, tr), :] * w_ref[1:2, :] + y[pl.ds(2, tr), :] * w_ref[0:1, :]
            o_ref[...] = jnp.where(row >= pad, dup, 0.0).astype(BF16)
            for tap in range(3):
                s_ref[tap:tap + 1, :] += jnp.sum(y0 * x[pl.ds(HALO - 2 + tap, tr), :], axis=0, keepdims=True)
            s_ref[3:4, :] += jnp.sum(y0, axis=0, keepdims=True)

        back(ya, xa, wa_ref, oa_ref, sa_ref)
        back(yg, xg, wg_ref, og_ref, sg_ref)

    last8 = lp // HALO - 1
    prev = lambda j, i: (jnp.maximum(i * hb - 1, 0), j)
    prevg = lambda j, i: (jnp.maximum(i * hb - 1, 0), nf + j)
    nxt = lambda j, i: (jnp.minimum((i + 1) * hb, last8), j)
    nxtg = lambda j, i: (jnp.minimum((i + 1) * hb, last8), nf + j)
    halo = lambda m: pl.BlockSpec((HALO, tc), m)
    main = pl.BlockSpec((tr, tc), lambda j, i: (i, j))
    sums = pl.BlockSpec((8, tc), lambda j, i: (0, j))
    return pl.pallas_call(
        body, name=name, grid=(nf, nrow),
        in_specs=[main, pl.BlockSpec((tr, tc), lambda j, i: (i, nf + j)), halo(prev), halo(prevg), halo(nxt), halo(nxtg),
                  main, halo(nxt),
                  pl.BlockSpec((3, tc), lambda j, i: (0, j)), pl.BlockSpec((3, tc), lambda j, i: (0, nf + j)),
                  pl.BlockSpec((1, tc), lambda j, i: (0, j)), pl.BlockSpec((1, tc), lambda j, i: (0, nf + j))],
        out_specs=[main, main, sums, sums],
        out_shape=[jax.ShapeDtypeStruct((lp, f), BF16), jax.ShapeDtypeStruct((lp, f), BF16),
                   jax.ShapeDtypeStruct((8, f), F32), jax.ShapeDtypeStruct((8, f), F32)],
        scratch_shapes=[pltpu.VMEM((tr + 2 * HALO, tc), F32), pltpu.VMEM((tr + 2 * HALO, tc), F32),
                        pltpu.VMEM((te, tc), F32), pltpu.VMEM((te, tc), F32), pltpu.VMEM((te, tc), F32)],
        compiler_params=_cparams(("parallel", "arbitrary")))(up, up, up, up, up, up, dact, dact, cw, cw, cb, cb)


def _head(h, target, g, d, name):
    lp, dm, t = d.lp, d.d, d.block
    inv_d = 1.0 / dm

    def body(h_ref, t_ref, g_ref, dh_ref, loss_ref, dg_ref):
        i = pl.program_id(0)

        @pl.when(i == 0)
        def _():
            dh_ref[...] = jnp.zeros_like(dh_ref)
            loss_ref[...] = jnp.zeros_like(loss_ref)
            dg_ref[...] = jnp.zeros_like(dg_ref)

        @pl.when(i > 0)
        def _():
            x, gv = h_ref[...], g_ref[...]
            r = lax.rsqrt(jnp.mean(x * x, axis=1, keepdims=True) + EPS)
            xh = x * r
            err = xh * gv - t_ref[...]
            loss_ref[...] += 0.5 * inv_d * jnp.sum(err * err)
            dy = err * inv_d
            gy = dy * gv
            c = jnp.mean(gy * x, axis=1, keepdims=True)
            dh_ref[...] = r * gy - x * (r * r * r) * c
            dg_ref[...] += jnp.sum(dy * xh, axis=0, keepdims=True)

    blk = pl.BlockSpec((t, dm), lambda i: (i, 0))
    return pl.pallas_call(
        body, name=name, grid=(lp // t,),
        in_specs=[blk, pl.BlockSpec((t, dm), lambda i: (jnp.maximum(i - 1, 0), 0)), pl.BlockSpec((1, dm), lambda i: (0, 0))],
        out_specs=[blk, pl.BlockSpec((8, LANES), lambda i: (0, 0)), pl.BlockSpec((1, dm), lambda i: (0, 0))],
        out_shape=[jax.ShapeDtypeStruct((lp, dm), F32), jax.ShapeDtypeStruct((8, LANES), F32),
                   jax.ShapeDtypeStruct((1, dm), F32)],
        compiler_params=_cparams(("arbitrary",)))(h, target, g.reshape(1, dm))


def _exchange(xs, *, scatter, name):
    n = len(xs)
    nf = len(FLIPS)

    def body(*refs):
        ins, outs = refs[:n], refs[n:2 * n]
        send_sems, recv_sems, loc_sems = refs[2 * n:]
        x, y, c = lax.axis_index("x"), lax.axis_index("y"), lax.axis_index("c")
        me = 4 * x + 2 * y + c
        sends, recvs, locs = [], [], []
        for a in range(n):
            src_me = ins[a].at[me] if scatter else ins[a]
            loc = pltpu.make_async_copy(src_me, outs[a].at[me], loc_sems.at[a])
            loc.start()
            locs.append(loc)
            for k, (fx, fy, fc) in enumerate(FLIPS):
                px, py, pc = x ^ fx, y ^ fy, c ^ fc
                peer = 4 * px + 2 * py + pc
                src = ins[a].at[peer] if scatter else ins[a]
                cp = pltpu.make_async_remote_copy(
                    src_ref=src, dst_ref=outs[a].at[me], send_sem=send_sems.at[a * nf + k],
                    recv_sem=recv_sems.at[a * nf + k], device_id=(px, py, pc), device_id_type=pl.DeviceIdType.MESH)
                cp.start()
                sends.append(cp)
                recvs.append(pltpu.make_async_remote_copy(
                    src_ref=src, dst_ref=outs[a].at[peer], send_sem=send_sems.at[a * nf + k],
                    recv_sem=recv_sems.at[a * nf + k], device_id=(px, py, pc), device_id_type=pl.DeviceIdType.MESH))
        for cp in recvs:
            cp.wait_recv()
        for cp in sends:
            cp.wait_send()
        for loc in locs:
            loc.wait()

    hbm = pl.BlockSpec(memory_space=pltpu.HBM)
    out_shape = [jax.ShapeDtypeStruct(((N_DEV,) + tuple(x.shape[1:])) if scatter else ((N_DEV,) + tuple(x.shape)), x.dtype)
                 for x in xs]
    return pl.pallas_call(
        body, name=name, in_specs=[hbm] * n, out_specs=[hbm] * n, out_shape=out_shape,
        scratch_shapes=[pltpu.SemaphoreType.DMA((n * nf,)), pltpu.SemaphoreType.DMA((n * nf,)),
                        pltpu.SemaphoreType.DMA((n,))],
    )(*xs)


def _adamw(parts, w, m, v, name):
    rows = w.shape[0]
    tr = _pick(rows, (128, 64, 32, 16, 8))
    c1 = 1.0 - ADAM_B1 ** ADAM_STEP
    c2 = 1.0 - ADAM_B2 ** ADAM_STEP

    def body(p_ref, w_ref, m_ref, v_ref, g_ref, d_ref, mo_ref, vo_ref):
        g = p_ref[0].astype(F32)
        for q in range(1, N_DEV):
            g = g + p_ref[q].astype(F32)
        mn = ADAM_B1 * m_ref[...] + (1.0 - ADAM_B1) * g
        vn = ADAM_B2 * v_ref[...] + (1.0 - ADAM_B2) * (g * g)
        g_ref[...] = g
        mo_ref[...] = mn
        vo_ref[...] = vn
        d_ref[...] = -ADAM_LR * ((mn / c1) / (jnp.sqrt(vn / c2) + ADAM_EPS) + ADAM_WD * w_ref[...])

    blk = pl.BlockSpec((tr, PACK_W), lambda i: (i, 0))
    out = jax.ShapeDtypeStruct((rows, PACK_W), F32)
    return pl.pallas_call(
        body, name=name, grid=(rows // tr,),
        in_specs=[pl.BlockSpec((N_DEV, tr, PACK_W), lambda i: (0, i, 0)), blk, blk, blk],
        out_specs=[blk] * 4, out_shape=[out] * 4, compiler_params=_cparams(("parallel",)))(parts, w, m, v)


BIG = ["w_in", "w_uq", "w_ukv", "w_sb_out", "w_mla_out", "w_o", "w_up", "w_down"]
ROW_SHARDED = {"w_o", "w_down"}
SMALL = ["conv_w", "meta_tokens"]
SHARDED = BIG + SMALL
REPL = ["norm_mix", "q_norm", "kv_norm", "norm_ffn", "conv_b", "final_norm"]


def _pack_rows(flat, row_mult):
    unit = PACK_W * row_mult
    total = -(-flat.shape[0] // unit) * unit
    return jnp.pad(flat, (0, total - flat.shape[0])).reshape(-1, PACK_W)


def _pack_rows_batched(flat2, row_mult):
    unit = PACK_W * row_mult
    total = -(-flat2.shape[1] // unit) * unit
    return jnp.pad(flat2, ((0, 0), (0, total - flat2.shape[1]))).reshape(flat2.shape[0], -1, PACK_W)


def _padded_cols(c):
    return -(-c // LANES) * LANES


def _pad_block(a, name):
    c = a.shape[-1]
    if name in ROW_SHARDED or c % LANES == 0:
        return a
    return jnp.pad(a, [(0, 0)] * (a.ndim - 1) + [(0, _padded_cols(c) - c)])


def _padded_shape(shape, name):
    if name in ROW_SHARDED:
        return tuple(shape)
    return tuple(shape[:-1]) + (_padded_cols(shape[-1]),)


def _full_from_slots(slots, name, c):
    if name in ROW_SHARDED:
        return jnp.transpose(slots, (1, 0, 2, 3)).reshape(slots.shape[1], -1, slots.shape[3])
    return jnp.concatenate([slots[q][..., :c] for q in range(N_DEV)], axis=-1)


def _slots_from_full(full, name):
    if name in ROW_SHARDED:
        l, rr, n = full.shape
        return jnp.transpose(full.reshape(l, N_DEV, rr // N_DEV, n), (1, 0, 2, 3)).reshape(N_DEV, -1)
    c = full.shape[-1] // N_DEV
    return jnp.stack([_pad_block(full[..., q * c:(q + 1) * c], name).reshape(-1) for q in range(N_DEV)])


def _swap_halves(t):
    half = t.shape[-1] // 2
    return jnp.concatenate([t[..., half:], t[..., :half]], axis=-1)


def _in_offsets(d):
    widths = (d.sbw, d.sbw, d.sbw, d.q_lora, d.kv_lora, d.rope, d.d, d.d)
    offs, o = [], 0
    for w in widths:
        offs.append((o, o + w))
        o += w
    return offs


def _prime_weights(full, layer, d):
    offs = _in_offsets(d)
    w_in = full["w_in"][layer]
    cols = lambda k: w_in[:, offs[k][0]:offs[k][1]]
    kr = cols(5)
    zpad = jnp.zeros((d.d, d.g_off - d.kr_off - LANES), w_in.dtype)
    w_inb = jnp.concatenate([cols(3), cols(4), kr, _swap_halves(kr), zpad, cols(6), cols(7)], axis=1)
    w_ina = w_in[:, :d.wa]
    uq = full["w_uq"][layer].reshape(d.q_lora, d.mla_heads, d.nope + d.rope)
    rope = uq[..., d.nope:]
    w_uq = jnp.concatenate([uq[..., :d.nope], rope, _swap_halves(rope)], axis=-1).reshape(d.q_lora, d.qw)
    ukv = full["w_ukv"][layer].reshape(d.kv_lora, d.mla_heads, d.nope + d.vdim)
    w_ukv = jnp.concatenate([ukv[..., :d.nope].reshape(d.kv_lora, -1), ukv[..., d.nope:].reshape(d.kv_lora, -1)], axis=1)
    return dict(w_ina=w_ina, w_inb=w_inb, w_in=jnp.concatenate([w_ina, w_inb], axis=1), w_uq=w_uq, w_ukv=w_ukv,
                w_sb_out=full["w_sb_out"][layer], w_mla_out=full["w_mla_out"][layer], w_o=full["w_o"][layer],
                w_up=full["w_up"][layer], w_down=full["w_down"][layer])


def _unprime_grads(g, d):
    gi = g["w_in"]
    b = gi[:, d.wa:]
    kr = b[:, d.kr_off:d.kr_off + d.rope] + _swap_halves(b[:, d.kr_off + d.rope:d.kr_off + 2 * d.rope])
    w_in = jnp.concatenate([gi[:, :d.wa], b[:, :d.kr_off], kr, b[:, d.g_off:]], axis=1)
    uq = g["w_uq"].reshape(d.q_lora, d.mla_heads, 2 * LANES)
    rope = uq[..., d.nope:d.nope + d.rope] + _swap_halves(uq[..., d.nope + d.rope:])
    w_uq = jnp.concatenate([uq[..., :d.nope], rope], axis=-1).reshape(d.q_lora, -1)
    hw = d.mla_heads * d.nope
    ukv = g["w_ukv"]
    w_ukv = jnp.concatenate([ukv[:, :hw].reshape(d.kv_lora, d.mla_heads, d.nope),
                             ukv[:, hw:].reshape(d.kv_lora, d.mla_heads, d.vdim)], axis=-1).reshape(d.kv_lora, -1)
    return dict(g, w_in=w_in, w_uq=w_uq, w_ukv=w_ukv)


def _layer_fwd(h, add, w, norm_mix, q_norm, kv_norm, norm_ffn, cw, cb, ctab, stab, d, tag):
    s = {}
    if add is None:
        s["h"] = h
        s["hn"], s["r1"] = _norm_fwd(h, norm_mix, width=d.d, cidx=0, name=f"norm_mix_{tag}")
    else:
        s["h"], s["hn"], s["r1"] = _norm_fwd(h, norm_mix, width=d.d, cidx=0, add=add, name=f"norm_mix_{tag}")
    s["pa"] = _mm(s["hn"], w["w_ina"], out_dtype=BF16, name=f"proj_a_{tag}")
    s["pb"] = _mm(s["hn"], w["w_inb"], name=f"proj_b_{tag}")
    s["o_sb"], s["carry"] = _sb_fwd(s["pa"], d, f"sb_fwd_{tag}")
    s["cqn"], s["rq"] = _norm_fwd(s["pb"], q_norm, width=d.q_lora, cidx=0, name=f"norm_q_{tag}")
    s["ckn"], s["rk"] = _norm_fwd(s["pb"], kv_norm, width=d.kv_lora, cidx=d.q_lora // d.kv_lora, name=f"norm_kv_{tag}")
    qraw = _mm(s["cqn"], w["w_uq"], name=f"uq_{tag}")
    s["kv"] = _mm(s["ckn"], w["w_ukv"], out_dtype=BF16, name=f"ukv_{tag}")
    s["qm"], s["kr"] = _mla_prep_fwd(qraw, s["pb"], ctab, stab, d, f"mla_prep_{tag}")
    s["o_mla"], s["lse"] = _mla_fwd(s["qm"], s["kv"], s["kr"], d, f"mla_fwd_{tag}")
    s["b_sb"] = _mm(s["o_sb"], w["w_sb_out"], name=f"sb_out_{tag}")
    s["b_mla"] = _mm(s["o_mla"], w["w_mla_out"], name=f"mla_out_{tag}")
    s["merged"] = _gate_fwd(s["pb"], s["b_sb"], s["b_mla"], d, f"gate_{tag}")
    mix = _mm(s["merged"], w["w_o"], name=f"w_o_{tag}")
    s["h1"], s["hn2"], s["r2"] = _norm_fwd(s["h"], norm_ffn, width=d.d, cidx=0, add=mix, name=f"norm_ffn_{tag}")
    s["up"] = _mm(s["hn2"], w["w_up"], name=f"w_up_{tag}")
    s["act"] = _convglu_fwd(s["up"], cw, cb, d, f"convglu_{tag}")
    ffn = _mm(s["act"], w["w_down"], name=f"w_down_{tag}")
    return s, ffn


def _layer_bwd(dh2, s, w, norm_mix, q_norm, kv_norm, norm_ffn, cw, cb, ctab, stab, d, tag):
    g = {}
    dact = _mm(dh2, w["w_down"], tb=True, name=f"d_act_{tag}")
    g["w_down"] = _mm(s["act"], dh2, ta=True, name=f"g_w_down_{tag}")
    dup_a, dup_g, sums_a, sums_g = _convglu_bwd(s["up"], dact, cw, cb, d, f"convglu_bwd_{tag}")
    dup = jnp.concatenate([dup_a, dup_g], axis=1)
    g["conv_w"] = jnp.concatenate([sums_a[0:3], sums_g[0:3]], axis=1)
    g["conv_b"] = jnp.concatenate([sums_a[3], sums_g[3]], axis=0)
    g["w_up"] = _mm(s["hn2"], dup, ta=True, name=f"g_w_up_{tag}")
    dhn2 = _mm(dup, w["w_up"], tb=True, name=f"d_hn2_{tag}")
    dh1, g["norm_ffn"] = _norm_bwd(dhn2, s["h1"], s["r2"], norm_ffn, width=d.d, cidx=0, dres=dh2, name=f"norm_ffn_bwd_{tag}")
    dmerged = _mm(dh1, w["w_o"], tb=True, name=f"d_merged_{tag}")
    g["w_o"] = _mm(s["merged"], dh1, ta=True, name=f"g_w_o_{tag}")
    db_sb, db_mla, dg_sb, dg_mla = _gate_bwd(dmerged, s["pb"], s["b_sb"], s["b_mla"], d, f"gate_bwd_{tag}")
    do_sb = _mm(db_sb, w["w_sb_out"], tb=True, name=f"d_o_sb_{tag}")
    g["w_sb_out"] = _mm(s["o_sb"], db_sb, ta=True, name=f"g_w_sb_out_{tag}")
    do_mla = _mm(db_mla, w["w_mla_out"], tb=True, name=f"d_o_mla_{tag}")
    g["w_mla_out"] = _mm(s["o_mla"], db_mla, ta=True, name=f"g_w_mla_out_{tag}")
    dq_sb, dk_sb, dv_sb = _sb_bwd(s["pa"], do_sb, s["carry"], d, f"sb_bwd_{tag}")
    dqm, dkn, dv, dkr = _mla_bwd(s["qm"], s["kv"], s["kr"], s["o_mla"], do_mla, s["lse"], d, f"mla_bwd_{tag}")
    dqraw, dkr128 = _mla_prep_bwd(dqm, dkr, ctab, stab, d, f"mla_prep_bwd_{tag}")
    dkv = jnp.concatenate([dkn.astype(BF16), dv.astype(BF16)], axis=1)
    dcqn = _mm(dqraw, w["w_uq"], tb=True, name=f"d_cq_{tag}")
    g["w_uq"] = _mm(s["cqn"], dqraw, ta=True, name=f"g_w_uq_{tag}")
    dckn = _mm(dkv, w["w_ukv"], tb=True, name=f"d_ckv_{tag}")
    g["w_ukv"] = _mm(s["ckn"], dkv, ta=True, name=f"g_w_ukv_{tag}")
    dcq, g["q_norm"] = _norm_bwd(dcqn, s["pb"], s["rq"], q_norm, width=d.q_lora, cidx=0, out_dtype=BF16, name=f"norm_q_bwd_{tag}")
    dckv, g["kv_norm"] = _norm_bwd(dckn, s["pb"], s["rk"], kv_norm, width=d.kv_lora, cidx=d.q_lora // d.kv_lora,
                                   out_dtype=BF16, name=f"norm_kv_bwd_{tag}")
    zpad = jnp.zeros((d.lp, d.g_off - d.kr_off - LANES), BF16)
    dproj = jnp.concatenate([dq_sb, dk_sb.astype(BF16), dv_sb.astype(BF16), dcq, dckv, dkr128, zpad, dg_sb, dg_mla], axis=1)
    g["w_in"] = _mm(s["hn"], dproj, ta=True, name=f"g_w_in_{tag}")
    dhn = _mm(dproj, w["w_in"], tb=True, name=f"d_hn_{tag}")
    dh, g["norm_mix"] = _norm_bwd(dhn, s["h"], s["r1"], norm_mix, width=d.d, cidx=0, dres=dh1, name=f"norm_mix_bwd_{tag}")
    return dh, g


def _step(d, x, p, m, v, loss_target):
    x = x.reshape(d.seq, d.d)
    target = loss_target.reshape(d.seq, d.d)

    pshape = {n: _padded_shape(p[n].shape, n) for n in SHARDED}
    psize = {n: 1 for n in SHARDED}
    for n in SHARDED:
        for s_ in pshape[n]:
            psize[n] *= s_
    pack_big = lambda t, dt: _pack_rows(jnp.concatenate([_pad_block(t[n], n).astype(dt).reshape(-1) for n in BIG]), 128)
    pack_small = lambda t: _pack_rows(jnp.concatenate([_pad_block(t[n], n).reshape(-1) for n in SMALL]), 8)
    big_all, small_all = _exchange([pack_big(p, BF16), pack_small(p)], scatter=False, name="gather_weights")
    full = {}
    for names, slots in ((BIG, big_all.reshape(N_DEV, -1)), (SMALL, small_all.reshape(N_DEV, -1))):
        off = 0
        for n in names:
            full[n] = _full_from_slots(slots[:, off:off + psize[n]].reshape((N_DEV,) + pshape[n]), n, p[n].shape[-1])
            off += psize[n]

    pos = jnp.arange(d.lp, dtype=F32) - d.pad
    half = d.rope // 2
    freqs = ROPE_THETA ** (-jnp.arange(half, dtype=F32) / half)
    ang = pos[:, None] * freqs[None, :]
    cos, sin = jnp.cos(ang), jnp.sin(ang)
    zero = jnp.zeros((d.lp, LANES - d.rope), F32)
    ctab = jnp.concatenate([cos, cos, zero], axis=1)
    stab = jnp.concatenate([-sin, sin, zero], axis=1)

    h = jnp.concatenate([jnp.zeros((d.pad, d.d), F32), full["meta_tokens"], x], axis=0)
    ws = [_prime_weights(full, l, d) for l in range(d.depth)]
    saved, add = [], None
    for l in range(d.depth):
        s, add = _layer_fwd(h, add, ws[l], p["norm_mix"][l], p["q_norm"][l], p["kv_norm"][l], p["norm_ffn"][l],
                            full["conv_w"][l], p["conv_b"][l].reshape(1, -1), ctab, stab, d, f"l{l}")
        saved.append(s)
        h = s["h1"]
    h_out = _residual_add(h, add, "final_add")
    dh, loss_part, g_final = _head(h_out, target, p["final_norm"], d, "head")

    grads = [None] * d.depth
    for l in reversed(range(d.depth)):
        dh, g = _layer_bwd(dh, saved[l], ws[l], p["norm_mix"][l], p["q_norm"][l], p["kv_norm"][l], p["norm_ffn"][l],
                           full["conv_w"][l], p["conv_b"][l].reshape(1, -1), ctab, stab, d, f"l{l}")
        grads[l] = _unprime_grads(g, d)
    grad_x = dh[d.first_tok:].reshape(1, d.seq, d.d)

    gfull = {n: jnp.stack([grads[l][n] for l in range(d.depth)]) for n in BIG + ["conv_w"]}
    gfull["meta_tokens"] = dh[d.pad:d.first_tok]
    gbig = _pack_rows_batched(jnp.concatenate([_slots_from_full(gfull[n].astype(BF16), n) for n in BIG], axis=1), 128)
    gsmall = _pack_rows_batched(jnp.concatenate([_slots_from_full(gfull[n], n) for n in SMALL], axis=1), 8)
    rbig, rsmall = _exchange([gbig, gsmall], scatter=True, name="scatter_grads")
    outs_big = _adamw(rbig, pack_big(p, F32), pack_big(m, F32), pack_big(v, F32), "adamw_big")
    outs_small = _adamw(rsmall, pack_small(p), pack_small(m), pack_small(v), "adamw_small")

    grep = {n: jnp.stack([grads[l][n].reshape(-1) for l in range(d.depth)]) for n in REPL if n != "final_norm"}
    grep["final_norm"] = g_final.reshape(-1)
    rflat = jnp.concatenate([grep[n].reshape(-1) for n in REPL] + [loss_part[0, 0:1]])
    (rparts,) = _exchange([_pack_rows(rflat, 8)], scatter=False, name="gather_small_grads")
    packr = lambda t: _pack_rows(jnp.concatenate([t[n].reshape(-1) for n in REPL] + [jnp.zeros((1,), F32)]), 8)
    outs_rp = _adamw(rparts, packr(p), packr(m), packr(v), "adamw_replicated")

    def unpack(flat, names, extra=0):
        res, off = {}, 0
        flat = flat.reshape(-1)
        for n in names:
            shape = pshape.get(n, p[n].shape)
            size = psize.get(n, p[n].size)
            res[n] = flat[off:off + size].reshape(shape)[..., :p[n].shape[-1]]
            off += size
        return res, flat[off:off + extra]

    results = []
    loss = None
    for k in range(4):
        big, _ = unpack(outs_big[k], BIG)
        small, _ = unpack(outs_small[k], SMALL)
        rp, tail = unpack(outs_rp[k], REPL, 1)
        if k == 0:
            loss = tail[0]
        results.append({**big, **small, **rp})
    return loss, grad_x, results


def _residual_add(h, add, name):
    rows, width = h.shape
    tr = _pick(rows, (256, 128))

    def body(h_ref, a_ref, o_ref):
        o_ref[...] = h_ref[...] + a_ref[...]

    blk = pl.BlockSpec((tr, width), lambda i: (i, 0))
    return pl.pallas_call(body, name=name, grid=(rows // tr,), in_specs=[blk, blk], out_specs=blk,
                          out_shape=jax.ShapeDtypeStruct((rows, width), F32),
                          compiler_params=_cparams(("parallel",)))(h, add)


WEIGHTS = ["meta_tokens", "norm_mix", "w_in", "q_norm", "w_uq", "kv_norm", "w_ukv", "w_sb_out", "w_mla_out", "w_o",
           "norm_ffn", "w_up", "conv_w", "conv_b", "w_down", "final_norm"]


def _run(d, x, weights, loss_target, moments_m, moments_v):
    p = dict(zip(WEIGHTS, weights))
    m = dict(zip(WEIGHTS, moments_m))
    v = dict(zip(WEIGHTS, moments_v))
    loss, grad_x, res = _step(d, x, p, m, v, loss_target)
    out = [loss, grad_x]
    for k in range(4):
        out += [res[k][n] for n in WEIGHTS]
    return tuple(out)


def kernel(x, meta_tokens, norm_mix, w_in, q_norm, w_uq, kv_norm, w_ukv, w_sb_out, w_mla_out, w_o, norm_ffn, w_up, conv_w, conv_b, w_down, final_norm, loss_target, m_meta_tokens, m_norm_mix, m_w_in, m_q_norm, m_w_uq, m_kv_norm, m_w_ukv, m_w_sb_out, m_w_mla_out, m_w_o, m_norm_ffn, m_w_up, m_conv_w, m_conv_b, m_w_down, m_final_norm, v_meta_tokens, v_norm_mix, v_w_in, v_q_norm, v_w_uq, v_kv_norm, v_w_ukv, v_w_sb_out, v_w_mla_out, v_w_o, v_norm_ffn, v_w_up, v_conv_w, v_conv_b, v_w_down, v_final_norm):
    weights = [meta_tokens, norm_mix, w_in, q_norm, w_uq, kv_norm, w_ukv, w_sb_out, w_mla_out, w_o, norm_ffn, w_up,
               conv_w, conv_b, w_down, final_norm]
    ms = [m_meta_tokens, m_norm_mix, m_w_in, m_q_norm, m_w_uq, m_kv_norm, m_w_ukv, m_w_sb_out, m_w_mla_out, m_w_o,
          m_norm_ffn, m_w_up, m_conv_w, m_conv_b, m_w_down, m_final_norm]
    vs = [v_meta_tokens, v_norm_mix, v_w_in, v_q_norm, v_w_uq, v_kv_norm, v_w_ukv, v_w_sb_out, v_w_mla_out, v_w_o,
          v_norm_ffn, v_w_up, v_conv_w, v_conv_b, v_w_down, v_final_norm]
    return _run(PROD, x, weights, loss_target, ms, vs)
```

```python
import jax
import jax.numpy as jnp
from jax import lax
from jax.experimental import pallas as pl
from jax.experimental.pallas import tpu as pltpu

F32 = jnp.float32
BF16 = jnp.bfloat16

EPS = 1e-6
ROPE_THETA = 10000.0
ADAM_LR = 0.001
ADAM_B1 = 0.9
ADAM_B2 = 0.999
ADAM_EPS = 1e-08
ADAM_WD = 0.01
ADAM_STEP = 10
NEG = -1e30
DEAD = -110.0
LANES = 128
PACK_W = 1024
V7X_VMEM_LIMIT = 48 * 1024 * 1024
V7X_VMEM_LIMIT_BIG = 58 * 1024 * 1024
MESH_AXES = ("x", "y", "c")
N_DEV = 8
FLIPS = [(0, 0, 1), (0, 1, 0), (0, 1, 1), (1, 0, 0), (1, 0, 1), (1, 1, 0), (1, 1, 1)]


class _Dims:
    def __init__(self, d_model=2048, seq=8192, depth=2, n_meta=16, block=128, sb_heads=8, hd=128,
                 mla_heads=8, q_lora=512, kv_lora=256, nope=128, rope=64, vdim=128, d_ff=5632, tq=None):
        self.d, self.seq, self.depth, self.n_meta, self.block = d_model, seq, depth, n_meta, block
        self.sb_heads, self.hd, self.mla_heads = sb_heads, hd, mla_heads
        self.q_lora, self.kv_lora, self.nope, self.rope, self.vdim, self.f = q_lora, kv_lora, nope, rope, vdim, d_ff
        assert hd == LANES and nope == LANES and vdim == LANES and 2 * rope == LANES
        self.pad = block - n_meta
        self.lp = self.pad + n_meta + seq
        self.first_tok = self.pad + n_meta
        assert self.first_tok == block and self.lp % block == 0 and self.lp // block < LANES
        self.tq = tq or next(t for t in (640, 512, 256, 128) if self.lp % t == 0)
        assert self.tq % block == 0 and self.lp % self.tq == 0
        self.sbw = sb_heads * hd
        self.mlaw = mla_heads * vdim
        self.wa = 3 * self.sbw
        self.d_in = 3 * self.sbw + q_lora + kv_lora + rope + 2 * d_model
        self.tg = min(1024, d_model)
        self.kr_off = q_lora + kv_lora
        raw = self.kr_off + LANES
        self.g_off = -(-raw // self.tg) * self.tg
        self.wb = self.g_off + 2 * d_model
        self.qw = mla_heads * 2 * LANES


PROD = _Dims()


def _pick(n, prefs):
    for p in prefs:
        if n % p == 0:
            return p
    return n


def _cparams(sem, limit=V7X_VMEM_LIMIT):
    return pltpu.CompilerParams(dimension_semantics=sem, vmem_limit_bytes=limit)


def _mm(a, b, *, ta=False, tb=False, out_dtype=F32, name):
    if ta:
        kdim, m = a.shape
    else:
        m, kdim = a.shape
    if tb:
        n, k2 = b.shape
    else:
        k2, n = b.shape
    assert kdim == k2, (a.shape, b.shape, ta, tb)
    tm = _pick(m, (640, 512, 256, 128))
    tn = _pick(n, (1024, 512, 384, 256, 128))
    tk = _pick(kdim, (2816, 2048, 1664, 1408, 1024, 640, 512, 256, 128))
    nk = kdim // tk
    dn = (((0 if ta else 1,), (1 if tb else 0,)), ((), ()))

    def dot(a_ref, b_ref):
        return lax.dot_general(a_ref[...].astype(BF16), b_ref[...].astype(BF16), dn, preferred_element_type=F32)

    def body_one(a_ref, b_ref, o_ref):
        o_ref[...] = dot(a_ref, b_ref).astype(out_dtype)

    def body_acc(a_ref, b_ref, o_ref, acc_ref):
        k = pl.program_id(2)

        @pl.when(k == 0)
        def _():
            acc_ref[...] = dot(a_ref, b_ref)

        @pl.when((k > 0) & (k < nk - 1))
        def _():
            acc_ref[...] += dot(a_ref, b_ref)

        @pl.when(k == nk - 1)
        def _():
            o_ref[...] = (acc_ref[...] + dot(a_ref, b_ref)).astype(out_dtype)

    a_spec = pl.BlockSpec((tk, tm), lambda i, j, k: (k, i)) if ta else pl.BlockSpec((tm, tk), lambda i, j, k: (i, k))
    b_spec = pl.BlockSpec((tn, tk), lambda i, j, k: (j, k)) if tb else pl.BlockSpec((tk, tn), lambda i, j, k: (k, j))
    return pl.pallas_call(
        body_one if nk == 1 else body_acc, name=name, grid=(m // tm, n // tn, nk), in_specs=[a_spec, b_spec],
        out_specs=pl.BlockSpec((tm, tn), lambda i, j, k: (i, j)),
        out_shape=jax.ShapeDtypeStruct((m, n), out_dtype),
        scratch_shapes=[] if nk == 1 else [pltpu.VMEM((tm, tn), F32)],
        compiler_params=_cparams(("parallel", "parallel", "arbitrary")),
    )(a, b)


def _norm_fwd(x, g, *, width, cidx, add=None, name):
    rows = x.shape[0]
    tr = _pick(rows, (256, 128))
    has_add = add is not None

    def body(*refs):
        if has_add:
            x_ref, a_ref, g_ref, xn_ref, y_ref, r_ref = refs
            xv = x_ref[...] + a_ref[...]
            xn_ref[...] = xv
        else:
            x_ref, g_ref, y_ref, r_ref = refs
            xv = x_ref[...]
        r = lax.rsqrt(jnp.mean(xv * xv, axis=1, keepdims=True) + EPS)
        y_ref[...] = (xv * r * g_ref[...]).astype(BF16)
        r_ref[...] = r

    blk = pl.BlockSpec((tr, width), lambda i: (i, 0))
    in_specs = [pl.BlockSpec((tr, width), lambda i: (i, cidx))]
    args = [x]
    if has_add:
        in_specs.append(blk)
        args.append(add)
    in_specs.append(pl.BlockSpec((1, width), lambda i: (0, 0)))
    args.append(g.reshape(1, width))
    out_specs = [blk, pl.BlockSpec((tr, 1), lambda i: (i, 0))]
    out_shape = [jax.ShapeDtypeStruct((rows, width), BF16), jax.ShapeDtypeStruct((rows, 1), F32)]
    if has_add:
        out_specs.insert(0, blk)
        out_shape.insert(0, jax.ShapeDtypeStruct((rows, width), F32))
    return pl.pallas_call(body, name=name, grid=(rows // tr,), in_specs=in_specs, out_specs=out_specs,
                          out_shape=out_shape, compiler_params=_cparams(("parallel",)))(*args)


def _norm_bwd(dy, x, r, g, *, width, cidx, dres=None, out_dtype=F32, name):
    rows = x.shape[0]
    tr = _pick(rows, (256, 128))
    has_res = dres is not None

    def body(*refs):
        if has_res:
            dy_ref, x_ref, r_ref, g_ref, dr_ref, dx_ref, dg_ref = refs
        else:
            dy_ref, x_ref, r_ref, g_ref, dx_ref, dg_ref = refs
        i = pl.program_id(0)

        @pl.when(i == 0)
        def _():
            dg_ref[...] = jnp.zeros_like(dg_ref)

        dyv, xv, rv = dy_ref[...], x_ref[...], r_ref[...]
        gy = dyv * g_ref[...]
        c = jnp.mean(gy * xv, axis=1, keepdims=True)
        dx = rv * gy - xv * (rv * rv * rv) * c
        if has_res:
            dx = dx + dr_ref[...]
        dx_ref[...] = dx.astype(out_dtype)
        dg_ref[...] += jnp.sum(dyv * xv * rv, axis=0, keepdims=True)

    blk = pl.BlockSpec((tr, width), lambda i: (i, 0))
    in_specs = [blk, pl.BlockSpec((tr, width), lambda i: (i, cidx)), pl.BlockSpec((tr, 1), lambda i: (i, 0)),
                pl.BlockSpec((1, width), lambda i: (0, 0))]
    args = [dy, x, r, g.reshape(1, width)]
    if has_res:
        in_specs.append(blk)
        args.append(dres)
    return pl.pallas_call(
        body, name=name, grid=(rows // tr,), in_specs=in_specs,
        out_specs=[blk, pl.BlockSpec((1, width), lambda i: (0, 0))],
        out_shape=[jax.ShapeDtypeStruct((rows, width), out_dtype), jax.ShapeDtypeStruct((1, width), F32)],
        compiler_params=_cparams(("arbitrary",)))(*args)


def _split3(x):
    h1 = x.astype(BF16)
    r1 = x - h1.astype(F32)
    h2 = r1.astype(BF16)
    h3 = (r1 - h2.astype(F32)).astype(BF16)
    return h1, h2, h3


def _cum(x, tri):
    h1, h2, h3 = _split3(x)
    dot = lambda h: jnp.dot(h, tri, preferred_element_type=F32)
    return dot(h1) + dot(h2) + dot(h3)


def _dot_nt(a, b):
    return lax.dot_general(a, b, (((1,), (1,)), ((), ())), preferred_element_type=F32)


def _dot_tn(a, b):
    return lax.dot_general(a, b, (((0,), (0,)), ((), ())), preferred_element_type=F32)


def _sb_fwd(qkv, d, name):
    nh, hd, lp, tq, t = d.sb_heads, d.hd, d.lp, d.tq, d.block
    r = tq // t
    scale = hd ** -0.5
    pad = d.pad

    def body(q_ref, k_ref, v_ref, o_ref, c_ref):
        i = pl.program_id(1)
        q = q_ref[...]
        row = i * tq + lax.broadcasted_iota(jnp.int32, (tq, t), 0)
        colb = lax.broadcasted_iota(jnp.int32, (tq, t), 1)
        tri = (lax.broadcasted_iota(jnp.int32, (t, t), 0)
               > lax.broadcasted_iota(jnp.int32, (t, t), 1)).astype(BF16)
        lane = lax.broadcasted_iota(jnp.int32, (tq, LANES), 1)

        c_ref[...] = jnp.zeros_like(c_ref)

        def make_step(masked):
            def step(j, carry):
                acc, run = carry
                off = pl.multiple_of(j * t, t)
                k = k_ref[pl.ds(off, t), :]
                v = v_ref[pl.ds(off, t), :]
                z = _dot_nt(q, k) * scale
                e = jnp.exp(-jnp.abs(z))
                sp = jnp.maximum(z, 0.0) + jnp.log(1.0 + e)
                if masked:
                    col = j * t + colb
                    mask = (col < row) & (col >= pad)
                    spm = jnp.where(mask, sp, 0.0)
                else:
                    spm = sp
                w = jnp.exp(z - sp - _cum(spm, tri) + run)
                if masked:
                    w = jnp.where(mask, w, 0.0)
                acc = acc + jnp.dot(w.astype(BF16), v, preferred_element_type=F32)
                c_ref[...] = jnp.where(lane == j, run, c_ref[...])
                run = run - jnp.sum(spm, axis=1, keepdims=True)
                return acc, run
            return step

        masked, plain = make_step(True), make_step(False)
        hi = (i + 1) * r
        lo = jnp.maximum(i * r, 1)
        carry = (jnp.zeros((tq, hd), F32), jnp.zeros((tq, 1), F32))
        acc, run = lax.fori_loop(0, hi - lo, lambda jj, c: masked(hi - 1 - jj, c), carry)

        def alive(run):
            return (jnp.max(run) >= DEAD).astype(jnp.int32)

        def cond(st):
            return (st[0] >= 1) & (st[1] > 0)

        def wbody(st):
            j, _, acc, run = st
            acc, run = plain(j, (acc, run))
            return j - 1, alive(run), acc, run

        j, live, acc, run = lax.while_loop(cond, wbody, (lo - 1, alive(run), acc, run))
        acc, run = lax.fori_loop(0, live, lambda _, c: masked(0, c), (acc, run))
        o_ref[...] = acc
        first = jnp.where(live > 0, 0, j + 1).astype(F32)
        c_ref[...] = jnp.where(lane == LANES - 1, first, c_ref[...])

    return pl.pallas_call(
        body, name=name, grid=(nh, lp // tq),
        in_specs=[pl.BlockSpec((tq, hd), lambda h, i: (i, h)),
                  pl.BlockSpec((lp, hd), lambda h, i: (0, nh + h)),
                  pl.BlockSpec((lp, hd), lambda h, i: (0, 2 * nh + h))],
        out_specs=[pl.BlockSpec((tq, hd), lambda h, i: (i, h)),
                   pl.BlockSpec((None, tq, LANES), lambda h, i: (h, i, 0))],
        out_shape=[jax.ShapeDtypeStruct((lp, nh * hd), F32), jax.ShapeDtypeStruct((nh, lp, LANES), F32)],
        compiler_params=_cparams(("parallel", "arbitrary")),
    )(qkv, qkv, qkv)


def _sb_bwd(qkv, do, carry, d, name):
    nh, hd, lp, tq, t = d.sb_heads, d.hd, d.lp, d.tq, d.block
    r = tq // t
    scale = hd ** -0.5
    pad = d.pad

    def body(q_ref, k_ref, v_ref, do_ref, c_ref, dq_ref, dk_ref, dv_ref):
        i = pl.program_id(1)

        @pl.when(i == 0)
        def _():
            dk_ref[...] = jnp.zeros_like(dk_ref)
            dv_ref[...] = jnp.zeros_like(dv_ref)

        q = q_ref[...]
        dob = do_ref[...].astype(BF16)
        row = i * tq + lax.broadcasted_iota(jnp.int32, (tq, t), 0)
        colb = lax.broadcasted_iota(jnp.int32, (tq, t), 1)
        ri = lax.broadcasted_iota(jnp.int32, (t, t), 0)
        ci = lax.broadcasted_iota(jnp.int32, (t, t), 1)
        tri_suf = (ri > ci).astype(BF16)
        tri_pre = (ri < ci).astype(BF16)
        lane = lax.broadcasted_iota(jnp.int32, (tq, LANES), 1)

        def make_step(masked):
            def step(j, carry):
                dq, pc = carry
                off = pl.multiple_of(j * t, t)
                k = k_ref[pl.ds(off, t), :]
                v = v_ref[pl.ds(off, t), :]
                z = _dot_nt(q, k) * scale
                e = jnp.exp(-jnp.abs(z))
                sp = jnp.maximum(z, 0.0) + jnp.log(1.0 + e)
                if masked:
                    col = j * t + colb
                    mask = (col < row) & (col >= pad)
                    spm = jnp.where(mask, sp, 0.0)
                else:
                    spm = sp
                run = jnp.sum(jnp.where(lane == j, c_ref[...], 0.0), axis=1, keepdims=True)
                w = jnp.exp(z - sp - _cum(spm, tri_suf) + run)
                if masked:
                    w = jnp.where(mask, w, 0.0)
                g = w * _dot_nt(dob, v)
                p = _cum(g, tri_pre) + pc
                inv = 1.0 / (1.0 + e)
                sig = jnp.where(z >= 0.0, inv, e * inv)
                dz = (g * (1.0 - sig) - sig * p) * scale
                if masked:
                    dz = jnp.where(mask, dz, 0.0)
                dzb = dz.astype(BF16)
                dq = dq + jnp.dot(dzb, k, preferred_element_type=F32)
                dk_ref[pl.ds(off, t), :] += _dot_tn(dzb, q)
                dv_ref[pl.ds(off, t), :] += _dot_tn(w.astype(BF16), dob)
                pc = pc + jnp.sum(g, axis=1, keepdims=True)
                return dq, pc
            return step

        masked, plain = make_step(True), make_step(False)
        lo = jnp.maximum(i * r, 1)
        first = jnp.max(jnp.where(lane == LANES - 1, c_ref[...], 0.0)).astype(jnp.int32)
        carry = (jnp.zeros((tq, hd), F32), jnp.zeros((tq, 1), F32))
        carry = lax.fori_loop(0, (first == 0).astype(jnp.int32), lambda _, c: masked(0, c), carry)
        carry = lax.fori_loop(jnp.maximum(first, 1), lo, plain, carry)
        dq, _ = lax.fori_loop(lo, (i + 1) * r, masked, carry)
        dq_ref[...] = dq.astype(BF16)

    w3 = nh * hd
    return pl.pallas_call(
        body, name=name, grid=(nh, lp // tq),
        in_specs=[pl.BlockSpec((tq, hd), lambda h, i: (i, h)),
                  pl.BlockSpec((lp, hd), lambda h, i: (0, nh + h)),
                  pl.BlockSpec((lp, hd), lambda h, i: (0, 2 * nh + h)),
                  pl.BlockSpec((tq, hd), lambda h, i: (i, h)),
                  pl.BlockSpec((None, tq, LANES), lambda h, i: (h, i, 0))],
        out_specs=[pl.BlockSpec((tq, hd), lambda h, i: (i, h)),
                   pl.BlockSpec((lp, hd), lambda h, i: (0, h)),
                   pl.BlockSpec((lp, hd), lambda h, i: (0, h))],
        out_shape=[jax.ShapeDtypeStruct((lp, w3), BF16), jax.ShapeDtypeStruct((lp, w3), F32),
                   jax.ShapeDtypeStruct((lp, w3), F32)],
        compiler_params=_cparams(("arbitrary", "arbitrary")),
    )(qkv, qkv, qkv, do, carry)


def _mla_prep_fwd(qraw, projb, ctab, stab, d, name):
    lp, nh = d.lp, d.mla_heads
    tr = _pick(lp, (256, 128))
    kidx = d.kr_off // LANES

    def rope(u, c, s):
        return u * c + pltpu.roll(u, LANES // 2, 1) * s

    def body(q_ref, k_ref, c_ref, s_ref, qm_ref, kr_ref):
        c, s = c_ref[...], s_ref[...]
        for h in range(nh):
            base = 2 * LANES * h
            qm_ref[:, base:base + LANES] = q_ref[:, base:base + LANES].astype(BF16)
            qm_ref[:, base + LANES:base + 2 * LANES] = rope(q_ref[:, base + LANES:base + 2 * LANES], c, s).astype(BF16)
        kr_ref[...] = rope(k_ref[...], c, s).astype(BF16)

    tab = pl.BlockSpec((tr, LANES), lambda i: (i, 0))
    return pl.pallas_call(
        body, name=name, grid=(lp // tr,),
        in_specs=[pl.BlockSpec((tr, d.qw), lambda i: (i, 0)), pl.BlockSpec((tr, LANES), lambda i: (i, kidx)), tab, tab],
        out_specs=[pl.BlockSpec((tr, d.qw), lambda i: (i, 0)), tab],
        out_shape=[jax.ShapeDtypeStruct((lp, d.qw), BF16), jax.ShapeDtypeStruct((lp, LANES), BF16)],
        compiler_params=_cparams(("parallel",)))(qraw, projb, ctab, stab)


def _mla_prep_bwd(dqm, dkr, ctab, stab, d, name):
    lp, nh = d.lp, d.mla_heads
    tr = _pick(lp, (256, 128))

    def unrope(g, c, s):
        return g * c + pltpu.roll(g * s, LANES // 2, 1)

    def body(dq_ref, dk_ref, c_ref, s_ref, o_ref, ok_ref):
        c, s = c_ref[...], s_ref[...]
        for h in range(nh):
            base = 2 * LANES * h
            o_ref[:, base:base + LANES] = dq_ref[:, base:base + LANES].astype(BF16)
            o_ref[:, base + LANES:base + 2 * LANES] = unrope(dq_ref[:, base + LANES:base + 2 * LANES], c, s).astype(BF16)
        ok_ref[...] = unrope(dk_ref[...], c, s).astype(BF16)

    tab = pl.BlockSpec((tr, LANES), lambda i: (i, 0))
    wide = pl.BlockSpec((tr, d.qw), lambda i: (i, 0))
    return pl.pallas_call(
        body, name=name, grid=(lp // tr,), in_specs=[wide, tab, tab, tab], out_specs=[wide, tab],
        out_shape=[jax.ShapeDtypeStruct((lp, d.qw), BF16), jax.ShapeDtypeStruct((lp, LANES), BF16)],
        compiler_params=_cparams(("parallel",)))(dqm, dkr, ctab, stab)


def _mla_fwd(qm, kv, kr, d, name):
    nh, lp, t = d.mla_heads, d.lp, d.tq
    scale = (d.nope + d.rope) ** -0.5
    pad = d.pad

    def body(q_ref, kn_ref, v_ref, kr_ref, o_ref, lse_ref):
        i = pl.program_id(1)
        qn = q_ref[:, :LANES]
        qr = q_ref[:, LANES:]
        row = i * t + lax.broadcasted_iota(jnp.int32, (t, t), 0)
        colb = lax.broadcasted_iota(jnp.int32, (t, t), 1)

        def make_step(masked):
            def step(j, carry):
                acc, m, l = carry
                off = pl.multiple_of(j * t, t)
                s = (_dot_nt(qn, kn_ref[pl.ds(off, t), :]) + _dot_nt(qr, kr_ref[pl.ds(off, t), :])) * scale
                if masked:
                    col = j * t + colb
                    s = jnp.where((col <= row) & (col >= pad), s, NEG)
                m_new = jnp.maximum(m, jnp.max(s, axis=1, keepdims=True))
                alpha = jnp.exp(m - m_new)
                p = jnp.exp(s - m_new)
                l = alpha * l + jnp.sum(p, axis=1, keepdims=True)
                acc = alpha * acc + jnp.dot(p.astype(BF16), v_ref[pl.ds(off, t), :], preferred_element_type=F32)
                return acc, m_new, l
            return step

        masked, plain = make_step(True), make_step(False)
        carry = masked(0, (jnp.zeros((t, LANES), F32), jnp.full((t, 1), NEG, F32), jnp.zeros((t, 1), F32)))
        carry = lax.fori_loop(1, i, plain, carry)
        acc, m, l = lax.fori_loop(i, i + jnp.minimum(i, 1), masked, carry)
        rowv = i * t + lax.broadcasted_iota(jnp.int32, (t, LANES), 0)
        o_ref[...] = jnp.where(rowv >= pad, acc / l, 0.0)
        lse_ref[...] = m + jnp.log(l)

    return pl.pallas_call(
        body, name=name, grid=(nh, lp // t),
        in_specs=[pl.BlockSpec((t, 2 * LANES), lambda h, i: (i, h)),
                  pl.BlockSpec((lp, LANES), lambda h, i: (0, h)),
                  pl.BlockSpec((lp, LANES), lambda h, i: (0, nh + h)),
                  pl.BlockSpec((lp, LANES), lambda h, i: (0, 0))],
        out_specs=[pl.BlockSpec((t, LANES), lambda h, i: (i, h)),
                   pl.BlockSpec((None, t, 1), lambda h, i: (h, i, 0))],
        out_shape=[jax.ShapeDtypeStruct((lp, nh * LANES), F32), jax.ShapeDtypeStruct((nh, lp, 1), F32)],
        compiler_params=_cparams(("parallel", "arbitrary")),
    )(qm, kv, kv, kr)


def _mla_bwd(qm, kv, kr, o, do, lse, d, name):
    nh, lp, t = d.mla_heads, d.lp, d.tq
    scale = (d.nope + d.rope) ** -0.5
    pad = d.pad

    def body(q_ref, kn_ref, v_ref, kr_ref, o_ref, do_ref, lse_ref, dq_ref, dkn_ref, dv_ref, dkr_ref):
        h = pl.program_id(0)
        i = pl.program_id(1)

        @pl.when(i == 0)
        def _():
            dkn_ref[...] = jnp.zeros_like(dkn_ref)
            dv_ref[...] = jnp.zeros_like(dv_ref)

        @pl.when((i == 0) & (h == 0))
        def _():
            dkr_ref[...] = jnp.zeros_like(dkr_ref)

        qn = q_ref[:, :LANES]
        qr = q_ref[:, LANES:]
        dof = do_ref[...]
        dob = dof.astype(BF16)
        delta = jnp.sum(dof * o_ref[...], axis=1, keepdims=True)
        lse = lse_ref[...]
        row = i * t + lax.broadcasted_iota(jnp.int32, (t, t), 0)
        colb = lax.broadcasted_iota(jnp.int32, (t, t), 1)

        def make_step(masked):
            def step(j, carry):
                dqn, dqr = carry
                off = pl.multiple_of(j * t, t)
                kn = kn_ref[pl.ds(off, t), :]
                krj = kr_ref[pl.ds(off, t), :]
                v = v_ref[pl.ds(off, t), :]
                s = (_dot_nt(qn, kn) + _dot_nt(qr, krj)) * scale
                if masked:
                    col = j * t + colb
                    mask = (col <= row) & (col >= pad)
                    p = jnp.where(mask, jnp.exp(jnp.where(mask, s, NEG) - lse), 0.0)
                else:
                    p = jnp.exp(s - lse)
                dp = _dot_nt(dob, v)
                dsb = (p * (dp - delta) * scale).astype(BF16)
                dqn = dqn + jnp.dot(dsb, kn, preferred_element_type=F32)
                dqr = dqr + jnp.dot(dsb, krj, preferred_element_type=F32)
                dkn_ref[pl.ds(off, t), :] += _dot_tn(dsb, qn)
                dkr_ref[pl.ds(off, t), :] += _dot_tn(dsb, qr)
                dv_ref[pl.ds(off, t), :] += _dot_tn(p.astype(BF16), dob)
                return dqn, dqr
            return step

        masked, plain = make_step(True), make_step(False)
        carry = masked(0, (jnp.zeros((t, LANES), F32), jnp.zeros((t, LANES), F32)))
        carry = lax.fori_loop(1, i, plain, carry)
        dqn, dqr = lax.fori_loop(i, i + jnp.minimum(i, 1), masked, carry)
        dq_ref[:, :LANES] = dqn
        dq_ref[:, LANES:] = dqr

    return pl.pallas_call(
        body, name=name, grid=(nh, lp // t),
        in_specs=[pl.BlockSpec((t, 2 * LANES), lambda h, i: (i, h)),
                  pl.BlockSpec((lp, LANES), lambda h, i: (0, h), pipeline_mode=pl.Buffered(1)),
                  pl.BlockSpec((lp, LANES), lambda h, i: (0, nh + h), pipeline_mode=pl.Buffered(1)),
                  pl.BlockSpec((lp, LANES), lambda h, i: (0, 0), pipeline_mode=pl.Buffered(1)),
                  pl.BlockSpec((t, LANES), lambda h, i: (i, h)),
                  pl.BlockSpec((t, LANES), lambda h, i: (i, h)),
                  pl.BlockSpec((None, t, 1), lambda h, i: (h, i, 0))],
        out_specs=[pl.BlockSpec((t, 2 * LANES), lambda h, i: (i, h)),
                   pl.BlockSpec((lp, LANES), lambda h, i: (0, h)),
                   pl.BlockSpec((lp, LANES), lambda h, i: (0, h)),
                   pl.BlockSpec((lp, LANES), lambda h, i: (0, 0))],
        out_shape=[jax.ShapeDtypeStruct((lp, nh * 2 * LANES), F32), jax.ShapeDtypeStruct((lp, nh * LANES), F32),
                   jax.ShapeDtypeStruct((lp, nh * LANES), F32), jax.ShapeDtypeStruct((lp, LANES), F32)],
        compiler_params=_cparams(("arbitrary", "arbitrary"), V7X_VMEM_LIMIT_BIG),
    )(qm, kv, kv, kr, o, do, lse)


def _sigmoid(x):
    return 1.0 / (1.0 + jnp.exp(-x))


def _gate_fwd(projb, b_sb, b_mla, d, name):
    lp, tg = d.lp, d.tg
    tr = _pick(lp, (256, 128))
    o1, o2 = d.g_off // tg, (d.g_off + d.d) // tg

    def body(g1_ref, g2_ref, b1_ref, b2_ref, o_ref):
        o_ref[...] = (_sigmoid(g1_ref[...]) * b1_ref[...] + _sigmoid(g2_ref[...]) * b2_ref[...]).astype(BF16)

    blk = pl.BlockSpec((tr, tg), lambda i, j: (i, j))
    return pl.pallas_call(
        body, name=name, grid=(lp // tr, d.d // tg),
        in_specs=[pl.BlockSpec((tr, tg), lambda i, j: (i, o1 + j)), pl.BlockSpec((tr, tg), lambda i, j: (i, o2 + j)),
                  blk, blk],
        out_specs=blk, out_shape=jax.ShapeDtypeStruct((lp, d.d), BF16),
        compiler_params=_cparams(("parallel", "parallel")))(projb, projb, b_sb, b_mla)


def _gate_bwd(dm, projb, b_sb, b_mla, d, name):
    lp, tg = d.lp, d.tg
    tr = _pick(lp, (256, 128))
    o1, o2 = d.g_off // tg, (d.g_off + d.d) // tg

    def body(dm_ref, g1_ref, g2_ref, b1_ref, b2_ref, db1_ref, db2_ref, dg1_ref, dg2_ref):
        dmv = dm_ref[...]
        s1, s2 = _sigmoid(g1_ref[...]), _sigmoid(g2_ref[...])
        db1_ref[...] = (dmv * s1).astype(BF16)
        db2_ref[...] = (dmv * s2).astype(BF16)
        dg1_ref[...] = (dmv * b1_ref[...] * s1 * (1.0 - s1)).astype(BF16)
        dg2_ref[...] = (dmv * b2_ref[...] * s2 * (1.0 - s2)).astype(BF16)

    blk = pl.BlockSpec((tr, tg), lambda i, j: (i, j))
    out = jax.ShapeDtypeStruct((lp, d.d), BF16)
    return pl.pallas_call(
        body, name=name, grid=(lp // tr, d.d // tg),
        in_specs=[blk, pl.BlockSpec((tr, tg), lambda i, j: (i, o1 + j)), pl.BlockSpec((tr, tg), lambda i, j: (i, o2 + j)),
                  blk, blk],
        out_specs=[blk] * 4, out_shape=[out] * 4,
        compiler_params=_cparams(("parallel", "parallel")))(dm, projb, projb, b_sb, b_mla)


HALO = 8


def _conv_tiles(d):
    return _pick(d.lp, (640, 512, 256, 128)), _pick(d.f, (512, 256, 128))


def _convglu_fwd(up, cw, cb, d, name):
    lp, f = d.lp, d.f
    tr, tc = _conv_tiles(d)
    nf = f // tc
    hb = tr // HALO
    pad = d.pad

    def body(a_ref, g_ref, pa_ref, pg_ref, wa_ref, wg_ref, ba_ref, bg_ref, o_ref, xa, xg):
        i = pl.program_id(1)
        keep = (i > 0).astype(F32)
        xa[0:HALO, :] = pa_ref[...] * keep
        xg[0:HALO, :] = pg_ref[...] * keep
        xa[HALO:, :] = a_ref[...]
        xg[HALO:, :] = g_ref[...]

        def conv(x, w_ref, b_ref):
            return (b_ref[...] + x[pl.ds(HALO - 2, tr), :] * w_ref[0:1, :] + x[pl.ds(HALO - 1, tr), :] * w_ref[1:2, :]
                    + x[pl.ds(HALO, tr), :] * w_ref[2:3, :])

        ua = conv(xa, wa_ref, ba_ref)
        ug = conv(xg, wg_ref, bg_ref)
        row = i * tr + lax.broadcasted_iota(jnp.int32, (tr, tc), 0)
        o_ref[...] = jnp.where(row >= pad, ua * _sigmoid(ua) * ug, 0.0).astype(BF16)

    prev = lambda j, i: (jnp.maximum(i * hb - 1, 0), j)
    prevg = lambda j, i: (jnp.maximum(i * hb - 1, 0), nf + j)
    return pl.pallas_call(
        body, name=name, grid=(nf, lp // tr),
        in_specs=[pl.BlockSpec((tr, tc), lambda j, i: (i, j)), pl.BlockSpec((tr, tc), lambda j, i: (i, nf + j)),
                  pl.BlockSpec((HALO, tc), prev), pl.BlockSpec((HALO, tc), prevg),
                  pl.BlockSpec((3, tc), lambda j, i: (0, j)), pl.BlockSpec((3, tc), lambda j, i: (0, nf + j)),
                  pl.BlockSpec((1, tc), lambda j, i: (0, j)), pl.BlockSpec((1, tc), lambda j, i: (0, nf + j))],
        out_specs=pl.BlockSpec((tr, tc), lambda j, i: (i, j)),
        out_shape=jax.ShapeDtypeStruct((lp, f), BF16),
        scratch_shapes=[pltpu.VMEM((tr + HALO, tc), F32), pltpu.VMEM((tr + HALO, tc), F32)],
        compiler_params=_cparams(("parallel", "arbitrary")))(up, up, up, up, cw, cw, cb, cb)


def _convglu_bwd(up, dact, cw, cb, d, name):
    lp, f = d.lp, d.f
    tr, tc = _conv_tiles(d)
    nf = f // tc
    hb = tr // HALO
    nrow = lp // tr
    pad = d.pad
    te = tr + HALO

    def body(a_ref, g_ref, pa_ref, pg_ref, na_ref, ng_ref, da_ref, nd_ref, wa_ref, wg_ref, ba_ref, bg_ref,
             oa_ref, og_ref, sa_ref, sg_ref, xa, xg, xd, ya, yg):
        i = pl.program_id(1)

        @pl.when(i == 0)
        def _():
            sa_ref[...] = jnp.zeros_like(sa_ref)
            sg_ref[...] = jnp.zeros_like(sg_ref)

        keep_p = (i > 0).astype(F32)
        keep_n = (i < nrow - 1).astype(F32)
        xa[0:HALO, :] = pa_ref[...] * keep_p
        xg[0:HALO, :] = pg_ref[...] * keep_p
        xa[HALO:HALO + tr, :] = a_ref[...]
        xg[HALO:HALO + tr, :] = g_ref[...]
        xa[HALO + tr:, :] = na_ref[...] * keep_n
        xg[HALO + tr:, :] = ng_ref[...] * keep_n
        xd[0:tr, :] = da_ref[...]
        xd[tr:, :] = nd_ref[...] * keep_n

        def conv(x, w_ref, b_ref):
            return (b_ref[...] + x[pl.ds(HALO - 2, te), :] * w_ref[0:1, :] + x[pl.ds(HALO - 1, te), :] * w_ref[1:2, :]
                    + x[pl.ds(HALO, te), :] * w_ref[2:3, :])

        ua = conv(xa, wa_ref, ba_ref)
        ug = conv(xg, wg_ref, bg_ref)
        sg = _sigmoid(ua)
        dact = xd[...]
        ya[...] = dact * ug * (sg * (1.0 + ua * (1.0 - sg)))
        yg[...] = dact * (ua * sg)
        row = i * tr + lax.broadcasted_iota(jnp.int32, (tr, tc), 0)

        def back(y, x, w_ref, o_ref, s_ref):
            y0 = y[pl.ds(0, tr), :]
            dup = y0 * w_ref[2:3, :] + y[pl.ds(1, tr), :] * w_ref[1:2, :] + y[pl.ds(2, tr), :] * w_ref[0:1, :]
            o_ref[...] = jnp.where(row >= pad, dup, 0.0).astype(BF16)
            for tap in range(3):
                s_ref[tap:tap + 1, :] += jnp.sum(y0 * x[pl.ds(HALO - 2 + tap, tr), :], axis=0, keepdims=True)
            s_ref[3:4, :] += jnp.sum(y0, axis=0, keepdims=True)

        back(ya, xa, wa_ref, oa_ref, sa_ref)
        back(yg, xg, wg_ref, og_ref, sg_ref)

    last8 = lp // HALO - 1
    prev = lambda j, i: (jnp.maximum(i * hb - 1, 0), j)
    prevg = lambda j, i: (jnp.maximum(i * hb - 1, 0), nf + j)
    nxt = lambda j, i: (jnp.minimum((i + 1) * hb, last8), j)
    nxtg = lambda j, i: (jnp.minimum((i + 1) * hb, last8), nf + j)
    halo = lambda m: pl.BlockSpec((HALO, tc), m)
    main = pl.BlockSpec((tr, tc), lambda j, i: (i, j))
    sums = pl.BlockSpec((8, tc), lambda j, i: (0, j))
    return pl.pallas_call(
        body, name=name, grid=(nf, nrow),
        in_specs=[main, pl.BlockSpec((tr, tc), lambda j, i: (i, nf + j)), halo(prev), halo(prevg), halo(nxt), halo(nxtg),
                  main, halo(nxt),
                  pl.BlockSpec((3, tc), lambda j, i: (0, j)), pl.BlockSpec((3, tc), lambda j, i: (0, nf + j)),
                  pl.BlockSpec((1, tc), lambda j, i: (0, j)), pl.BlockSpec((1, tc), lambda j, i: (0, nf + j))],
        out_specs=[main, main, sums, sums],
        out_shape=[jax.ShapeDtypeStruct((lp, f), BF16), jax.ShapeDtypeStruct((lp, f), BF16),
                   jax.ShapeDtypeStruct((8, f), F32), jax.ShapeDtypeStruct((8, f), F32)],
        scratch_shapes=[pltpu.VMEM((tr + 2 * HALO, tc), F32), pltpu.VMEM((tr + 2 * HALO, tc), F32),
                        pltpu.VMEM((te, tc), F32), pltpu.VMEM((te, tc), F32), pltpu.VMEM((te, tc), F32)],
        compiler_params=_cparams(("parallel", "arbitrary")))(up, up, up, up, up, up, dact, dact, cw, cw, cb, cb)


def _head(h, target, g, d, name):
    lp, dm, t = d.lp, d.d, d.block
    inv_d = 1.0 / dm

    def body(h_ref, t_ref, g_ref, dh_ref, loss_ref, dg_ref):
        i = pl.program_id(0)

        @pl.when(i == 0)
        def _():
            dh_ref[...] = jnp.zeros_like(dh_ref)
            loss_ref[...] = jnp.zeros_like(loss_ref)
            dg_ref[...] = jnp.zeros_like(dg_ref)

        @pl.when(i > 0)
        def _():
            x, gv = h_ref[...], g_ref[...]
            r = lax.rsqrt(jnp.mean(x * x, axis=1, keepdims=True) + EPS)
            xh = x * r
            err = xh * gv - t_ref[...]
            loss_ref[...] += 0.5 * inv_d * jnp.sum(err * err)
            dy = err * inv_d
            gy = dy * gv
            c = jnp.mean(gy * x, axis=1, keepdims=True)
            dh_ref[...] = r * gy - x * (r * r * r) * c
            dg_ref[...] += jnp.sum(dy * xh, axis=0, keepdims=True)

    blk = pl.BlockSpec((t, dm), lambda i: (i, 0))
    return pl.pallas_call(
        body, name=name, grid=(lp // t,),
        in_specs=[blk, pl.BlockSpec((t, dm), lambda i: (jnp.maximum(i - 1, 0), 0)), pl.BlockSpec((1, dm), lambda i: (0, 0))],
        out_specs=[blk, pl.BlockSpec((8, LANES), lambda i: (0, 0)), pl.BlockSpec((1, dm), lambda i: (0, 0))],
        out_shape=[jax.ShapeDtypeStruct((lp, dm), F32), jax.ShapeDtypeStruct((8, LANES), F32),
                   jax.ShapeDtypeStruct((1, dm), F32)],
        compiler_params=_cparams(("arbitrary",)))(h, target, g.reshape(1, dm))


def _exchange(xs, *, scatter, name):
    n = len(xs)
    nf = len(FLIPS)

    def body(*refs):
        ins, outs = refs[:n], refs[n:2 * n]
        send_sems, recv_sems, loc_sems = refs[2 * n:]
        x, y, c = lax.axis_index("x"), lax.axis_index("y"), lax.axis_index("c")
        me = 4 * x + 2 * y + c
        sends, recvs, locs = [], [], []
        for a in range(n):
            src_me = ins[a].at[me] if scatter else ins[a]
            loc = pltpu.make_async_copy(src_me, outs[a].at[me], loc_sems.at[a])
            loc.start()
            locs.append(loc)
            for k, (fx, fy, fc) in enumerate(FLIPS):
                px, py, pc = x ^ fx, y ^ fy, c ^ fc
                peer = 4 * px + 2 * py + pc
                src = ins[a].at[peer] if scatter else ins[a]
                cp = pltpu.make_async_remote_copy(
                    src_ref=src, dst_ref=outs[a].at[me], send_sem=send_sems.at[a * nf + k],
                    recv_sem=recv_sems.at[a * nf + k], device_id=(px, py, pc), device_id_type=pl.DeviceIdType.MESH)
                cp.start()
                sends.append(cp)
                recvs.append(pltpu.make_async_remote_copy(
                    src_ref=src, dst_ref=outs[a].at[peer], send_sem=send_sems.at[a * nf + k],
                    recv_sem=recv_sems.at[a * nf + k], device_id=(px, py, pc), device_id_type=pl.DeviceIdType.MESH))
        for cp in recvs:
            cp.wait_recv()
        for cp in sends:
            cp.wait_send()
        for loc in locs:
            loc.wait()

    hbm = pl.BlockSpec(memory_space=pltpu.HBM)
    out_shape = [jax.ShapeDtypeStruct(((N_DEV,) + tuple(x.shape[1:])) if scatter else ((N_DEV,) + tuple(x.shape)), x.dtype)
                 for x in xs]
    return pl.pallas_call(
        body, name=name, in_specs=[hbm] * n, out_specs=[hbm] * n, out_shape=out_shape,
        scratch_shapes=[pltpu.SemaphoreType.DMA((n * nf,)), pltpu.SemaphoreType.DMA((n * nf,)),
                        pltpu.SemaphoreType.DMA((n,))],
    )(*xs)


def _adamw(parts, w, m, v, name):
    rows = w.shape[0]
    tr = _pick(rows, (128, 64, 32, 16, 8))
    c1 = 1.0 - ADAM_B1 ** ADAM_STEP
    c2 = 1.0 - ADAM_B2 ** ADAM_STEP

    def body(p_ref, w_ref, m_ref, v_ref, g_ref, d_ref, mo_ref, vo_ref):
        g = p_ref[0].astype(F32)
        for q in range(1, N_DEV):
            g = g + p_ref[q].astype(F32)
        mn = ADAM_B1 * m_ref[...] + (1.0 - ADAM_B1) * g
        vn = ADAM_B2 * v_ref[...] + (1.0 - ADAM_B2) * (g * g)
        g_ref[...] = g
        mo_ref[...] = mn
        vo_ref[...] = vn
        d_ref[...] = -ADAM_LR * ((mn / c1) / (jnp.sqrt(vn / c2) + ADAM_EPS) + ADAM_WD * w_ref[...])

    blk = pl.BlockSpec((tr, PACK_W), lambda i: (i, 0))
    out = jax.ShapeDtypeStruct((rows, PACK_W), F32)
    return pl.pallas_call(
        body, name=name, grid=(rows // tr,),
        in_specs=[pl.BlockSpec((N_DEV, tr, PACK_W), lambda i: (0, i, 0)), blk, blk, blk],
        out_specs=[blk] * 4, out_shape=[out] * 4, compiler_params=_cparams(("parallel",)))(parts, w, m, v)


BIG = ["w_in", "w_uq", "w_ukv", "w_sb_out", "w_mla_out", "w_o", "w_up", "w_down"]
ROW_SHARDED = {"w_o", "w_down"}
SMALL = ["conv_w", "meta_tokens"]
SHARDED = BIG + SMALL
REPL = ["norm_mix", "q_norm", "kv_norm", "norm_ffn", "conv_b", "final_norm"]


def _pack_rows(flat, row_mult):
    unit = PACK_W * row_mult
    total = -(-flat.shape[0] // unit) * unit
    return jnp.pad(flat, (0, total - flat.shape[0])).reshape(-1, PACK_W)


def _pack_rows_batched(flat2, row_mult):
    unit = PACK_W * row_mult
    total = -(-flat2.shape[1] // unit) * unit
    return jnp.pad(flat2, ((0, 0), (0, total - flat2.shape[1]))).reshape(flat2.shape[0], -1, PACK_W)


def _padded_cols(c):
    return -(-c // LANES) * LANES


def _pad_block(a, name):
    c = a.shape[-1]
    if name in ROW_SHARDED or c % LANES == 0:
        return a
    return jnp.pad(a, [(0, 0)] * (a.ndim - 1) + [(0, _padded_cols(c) - c)])


def _padded_shape(shape, name):
    if name in ROW_SHARDED:
        return tuple(shape)
    return tuple(shape[:-1]) + (_padded_cols(shape[-1]),)


def _full_from_slots(slots, name, c):
    if name in ROW_SHARDED:
        return jnp.transpose(slots, (1, 0, 2, 3)).reshape(slots.shape[1], -1, slots.shape[3])
    return jnp.concatenate([slots[q][..., :c] for q in range(N_DEV)], axis=-1)


def _slots_from_full(full, name):
    if name in ROW_SHARDED:
        l, rr, n = full.shape
        return jnp.transpose(full.reshape(l, N_DEV, rr // N_DEV, n), (1, 0, 2, 3)).reshape(N_DEV, -1)
    c = full.shape[-1] // N_DEV
    return jnp.stack([_pad_block(full[..., q * c:(q + 1) * c], name).reshape(-1) for q in range(N_DEV)])


def _swap_halves(t):
    half = t.shape[-1] // 2
    return jnp.concatenate([t[..., half:], t[..., :half]], axis=-1)


def _in_offsets(d):
    widths = (d.sbw, d.sbw, d.sbw, d.q_lora, d.kv_lora, d.rope, d.d, d.d)
    offs, o = [], 0
    for w in widths:
        offs.append((o, o + w))
        o += w
    return offs


def _prime_weights(full, layer, d):
    offs = _in_offsets(d)
    w_in = full["w_in"][layer]
    cols = lambda k: w_in[:, offs[k][0]:offs[k][1]]
    kr = cols(5)
    zpad = jnp.zeros((d.d, d.g_off - d.kr_off - LANES), w_in.dtype)
    w_inb = jnp.concatenate([cols(3), cols(4), kr, _swap_halves(kr), zpad, cols(6), cols(7)], axis=1)
    w_ina = w_in[:, :d.wa]
    uq = full["w_uq"][layer].reshape(d.q_lora, d.mla_heads, d.nope + d.rope)
    rope = uq[..., d.nope:]
    w_uq = jnp.concatenate([uq[..., :d.nope], rope, _swap_halves(rope)], axis=-1).reshape(d.q_lora, d.qw)
    ukv = full["w_ukv"][layer].reshape(d.kv_lora, d.mla_heads, d.nope + d.vdim)
    w_ukv = jnp.concatenate([ukv[..., :d.nope].reshape(d.kv_lora, -1), ukv[..., d.nope:].reshape(d.kv_lora, -1)], axis=1)
    return dict(w_ina=w_ina, w_inb=w_inb, w_in=jnp.concatenate([w_ina, w_inb], axis=1), w_uq=w_uq, w_ukv=w_ukv,
                w_sb_out=full["w_sb_out"][layer], w_mla_out=full["w_mla_out"][layer], w_o=full["w_o"][layer],
                w_up=full["w_up"][layer], w_down=full["w_down"][layer])


def _unprime_grads(g, d):
    gi = g["w_in"]
    b = gi[:, d.wa:]
    kr = b[:, d.kr_off:d.kr_off + d.rope] + _swap_halves(b[:, d.kr_off + d.rope:d.kr_off + 2 * d.rope])
    w_in = jnp.concatenate([gi[:, :d.wa], b[:, :d.kr_off], kr, b[:, d.g_off:]], axis=1)
    uq = g["w_uq"].reshape(d.q_lora, d.mla_heads, 2 * LANES)
    rope = uq[..., d.nope:d.nope + d.rope] + _swap_halves(uq[..., d.nope + d.rope:])
    w_uq = jnp.concatenate([uq[..., :d.nope], rope], axis=-1).reshape(d.q_lora, -1)
    hw = d.mla_heads * d.nope
    ukv = g["w_ukv"]
    w_ukv = jnp.concatenate([ukv[:, :hw].reshape(d.kv_lora, d.mla_heads, d.nope),
                             ukv[:, hw:].reshape(d.kv_lora, d.mla_heads, d.vdim)], axis=-1).reshape(d.kv_lora, -1)
    return dict(g, w_in=w_in, w_uq=w_uq, w_ukv=w_ukv)


def _layer_fwd(h, add, w, norm_mix, q_norm, kv_norm, norm_ffn, cw, cb, ctab, stab, d, tag):
    s = {}
    if add is None:
        s["h"] = h
        s["hn"], s["r1"] = _norm_fwd(h, norm_mix, width=d.d, cidx=0, name=f"norm_mix_{tag}")
    else:
        s["h"], s["hn"], s["r1"] = _norm_fwd(h, norm_mix, width=d.d, cidx=0, add=add, name=f"norm_mix_{tag}")
    s["pa"] = _mm(s["hn"], w["w_ina"], out_dtype=BF16, name=f"proj_a_{tag}")
    s["pb"] = _mm(s["hn"], w["w_inb"], name=f"proj_b_{tag}")
    s["o_sb"], s["carry"] = _sb_fwd(s["pa"], d, f"sb_fwd_{tag}")
    s["cqn"], s["rq"] = _norm_fwd(s["pb"], q_norm, width=d.q_lora, cidx=0, name=f"norm_q_{tag}")
    s["ckn"], s["rk"] = _norm_fwd(s["pb"], kv_norm, width=d.kv_lora, cidx=d.q_lora // d.kv_lora, name=f"norm_kv_{tag}")
    qraw = _mm(s["cqn"], w["w_uq"], name=f"uq_{tag}")
    s["kv"] = _mm(s["ckn"], w["w_ukv"], out_dtype=BF16, name=f"ukv_{tag}")
    s["qm"], s["kr"] = _mla_prep_fwd(qraw, s["pb"], ctab, stab, d, f"mla_prep_{tag}")
    s["o_mla"], s["lse"] = _mla_fwd(s["qm"], s["kv"], s["kr"], d, f"mla_fwd_{tag}")
    s["b_sb"] = _mm(s["o_sb"], w["w_sb_out"], name=f"sb_out_{tag}")
    s["b_mla"] = _mm(s["o_mla"], w["w_mla_out"], name=f"mla_out_{tag}")
    s["merged"] = _gate_fwd(s["pb"], s["b_sb"], s["b_mla"], d, f"gate_{tag}")
    mix = _mm(s["merged"], w["w_o"], name=f"w_o_{tag}")
    s["h1"], s["hn2"], s["r2"] = _norm_fwd(s["h"], norm_ffn, width=d.d, cidx=0, add=mix, name=f"norm_ffn_{tag}")
    s["up"] = _mm(s["hn2"], w["w_up"], name=f"w_up_{tag}")
    s["act"] = _convglu_fwd(s["up"], cw, cb, d, f"convglu_{tag}")
    ffn = _mm(s["act"], w["w_down"], name=f"w_down_{tag}")
    return s, ffn


def _layer_bwd(dh2, s, w, norm_mix, q_norm, kv_norm, norm_ffn, cw, cb, ctab, stab, d, tag):
    g = {}
    dact = _mm(dh2, w["w_down"], tb=True, name=f"d_act_{tag}")
    g["w_down"] = _mm(s["act"], dh2, ta=True, name=f"g_w_down_{tag}")
    dup_a, dup_g, sums_a, sums_g = _convglu_bwd(s["up"], dact, cw, cb, d, f"convglu_bwd_{tag}")
    dup = jnp.concatenate([dup_a, dup_g], axis=1)
    g["conv_w"] = jnp.concatenate([sums_a[0:3], sums_g[0:3]], axis=1)
    g["conv_b"] = jnp.concatenate([sums_a[3], sums_g[3]], axis=0)
    g["w_up"] = _mm(s["hn2"], dup, ta=True, name=f"g_w_up_{tag}")
    dhn2 = _mm(dup, w["w_up"], tb=True, name=f"d_hn2_{tag}")
    dh1, g["norm_ffn"] = _norm_bwd(dhn2, s["h1"], s["r2"], norm_ffn, width=d.d, cidx=0, dres=dh2, name=f"norm_ffn_bwd_{tag}")
    dmerged = _mm(dh1, w["w_o"], tb=True, name=f"d_merged_{tag}")
    g["w_o"] = _mm(s["merged"], dh1, ta=True, name=f"g_w_o_{tag}")
    db_sb, db_mla, dg_sb, dg_mla = _gate_bwd(dmerged, s["pb"], s["b_sb"], s["b_mla"], d, f"gate_bwd_{tag}")
    do_sb = _mm(db_sb, w["w_sb_out"], tb=True, name=f"d_o_sb_{tag}")
    g["w_sb_out"] = _mm(s["o_sb"], db_sb, ta=True, name=f"g_w_sb_out_{tag}")
    do_mla = _mm(db_mla, w["w_mla_out"], tb=True, name=f"d_o_mla_{tag}")
    g["w_mla_out"] = _mm(s["o_mla"], db_mla, ta=True, name=f"g_w_mla_out_{tag}")
    dq_sb, dk_sb, dv_sb = _sb_bwd(s["pa"], do_sb, s["carry"], d, f"sb_bwd_{tag}")
    dqm, dkn, dv, dkr = _mla_bwd(s["qm"], s["kv"], s["kr"], s["o_mla"], do_mla, s["lse"], d, f"mla_bwd_{tag}")
    dqraw, dkr128 = _mla_prep_bwd(dqm, dkr, ctab, stab, d, f"mla_prep_bwd_{tag}")
    dkv = jnp.concatenate([dkn.astype(BF16), dv.astype(BF16)], axis=1)
    dcqn = _mm(dqraw, w["w_uq"], tb=True, name=f"d_cq_{tag}")
    g["w_uq"] = _mm(s["cqn"], dqraw, ta=True, name=f"g_w_uq_{tag}")
    dckn = _mm(dkv, w["w_ukv"], tb=True, name=f"d_ckv_{tag}")
    g["w_ukv"] = _mm(s["ckn"], dkv, ta=True, name=f"g_w_ukv_{tag}")
    dcq, g["q_norm"] = _norm_bwd(dcqn, s["pb"], s["rq"], q_norm, width=d.q_lora, cidx=0, out_dtype=BF16, name=f"norm_q_bwd_{tag}")
    dckv, g["kv_norm"] = _norm_bwd(dckn, s["pb"], s["rk"], kv_norm, width=d.kv_lora, cidx=d.q_lora // d.kv_lora,
                                   out_dtype=BF16, name=f"norm_kv_bwd_{tag}")
    zpad = jnp.zeros((d.lp, d.g_off - d.kr_off - LANES), BF16)
    dproj = jnp.concatenate([dq_sb, dk_sb.astype(BF16), dv_sb.astype(BF16), dcq, dckv, dkr128, zpad, dg_sb, dg_mla], axis=1)
    g["w_in"] = _mm(s["hn"], dproj, ta=True, name=f"g_w_in_{tag}")
    dhn = _mm(dproj, w["w_in"], tb=True, name=f"d_hn_{tag}")
    dh, g["norm_mix"] = _norm_bwd(dhn, s["h"], s["r1"], norm_mix, width=d.d, cidx=0, dres=dh1, name=f"norm_mix_bwd_{tag}")
    return dh, g


def _step(d, x, p, m, v, loss_target):
    x = x.reshape(d.seq, d.d)
    target = loss_target.reshape(d.seq, d.d)

    pshape = {n: _padded_shape(p[n].shape, n) for n in SHARDED}
    psize = {n: 1 for n in SHARDED}
    for n in SHARDED:
        for s_ in pshape[n]:
            psize[n] *= s_
    pack_big = lambda t, dt: _pack_rows(jnp.concatenate([_pad_block(t[n], n).astype(dt).reshape(-1) for n in BIG]), 128)
    pack_small = lambda t: _pack_rows(jnp.concatenate([_pad_block(t[n], n).reshape(-1) for n in SMALL]), 8)
    big_all, small_all = _exchange([pack_big(p, BF16), pack_small(p)], scatter=False, name="gather_weights")
    full = {}
    for names, slots in ((BIG, big_all.reshape(N_DEV, -1)), (SMALL, small_all.reshape(N_DEV, -1))):
        off = 0
        for n in names:
            full[n] = _full_from_slots(slots[:, off:off + psize[n]].reshape((N_DEV,) + pshape[n]), n, p[n].shape[-1])
            off += psize[n]

    pos = jnp.arange(d.lp, dtype=F32) - d.pad
    half = d.rope // 2
    freqs = ROPE_THETA ** (-jnp.arange(half, dtype=F32) / half)
    ang = pos[:, None] * freqs[None, :]
    cos, sin = jnp.cos(ang), jnp.sin(ang)
    zero = jnp.zeros((d.lp, LANES - d.rope), F32)
    ctab = jnp.concatenate([cos, cos, zero], axis=1)
    stab = jnp.concatenate([-sin, sin, zero], axis=1)

    h = jnp.concatenate([jnp.zeros((d.pad, d.d), F32), full["meta_tokens"], x], axis=0)
    ws = [_prime_weights(full, l, d) for l in range(d.depth)]
    saved, add = [], None
    for l in range(d.depth):
        s, add = _layer_fwd(h, add, ws[l], p["norm_mix"][l], p["q_norm"][l], p["kv_norm"][l], p["norm_ffn"][l],
                            full["conv_w"][l], p["conv_b"][l].reshape(1, -1), ctab, stab, d, f"l{l}")
        saved.append(s)
        h = s["h1"]
    h_out = _residual_add(h, add, "final_add")
    dh, loss_part, g_final = _head(h_out, target, p["final_norm"], d, "head")

    grads = [None] * d.depth
    for l in reversed(range(d.depth)):
        dh, g = _layer_bwd(dh, saved[l], ws[l], p["norm_mix"][l], p["q_norm"][l], p["kv_norm"][l], p["norm_ffn"][l],
                           full["conv_w"][l], p["conv_b"][l].reshape(1, -1), ctab, stab, d, f"l{l}")
        grads[l] = _unprime_grads(g, d)
    grad_x = dh[d.first_tok:].reshape(1, d.seq, d.d)

    gfull = {n: jnp.stack([grads[l][n] for l in range(d.depth)]) for n in BIG + ["conv_w"]}
    gfull["meta_tokens"] = dh[d.pad:d.first_tok]
    gbig = _pack_rows_batched(jnp.concatenate([_slots_from_full(gfull[n].astype(BF16), n) for n in BIG], axis=1), 128)
    gsmall = _pack_rows_batched(jnp.concatenate([_slots_from_full(gfull[n], n) for n in SMALL], axis=1), 8)
    rbig, rsmall = _exchange([gbig, gsmall], scatter=True, name="scatter_grads")
    outs_big = _adamw(rbig, pack_big(p, F32), pack_big(m, F32), pack_big(v, F32), "adamw_big")
    outs_small = _adamw(rsmall, pack_small(p), pack_small(m), pack_small(v), "adamw_small")

    grep = {n: jnp.stack([grads[l][n].reshape(-1) for l in range(d.depth)]) for n in REPL if n != "final_norm"}
    grep["final_norm"] = g_final.reshape(-1)
    rflat = jnp.concatenate([grep[n].reshape(-1) for n in REPL] + [loss_part[0, 0:1]])
    (rparts,) = _exchange([_pack_rows(rflat, 8)], scatter=False, name="gather_small_grads")
    packr = lambda t: _pack_rows(jnp.concatenate([t[n].reshape(-1) for n in REPL] + [jnp.zeros((1,), F32)]), 8)
    outs_rp = _adamw(rparts, packr(p), packr(m), packr(v), "adamw_replicated")

    def unpack(flat, names, extra=0):
        res, off = {}, 0
        flat = flat.reshape(-1)
        for n in names:
            shape = pshape.get(n, p[n].shape)
            size = psize.get(n, p[n].size)
            res[n] = flat[off:off + size].reshape(shape)[..., :p[n].shape[-1]]
            off += size
        return res, flat[off:off + extra]

    results = []
    loss = None
    for k in range(4):
        big, _ = unpack(outs_big[k], BIG)
        small, _ = unpack(outs_small[k], SMALL)
        rp, tail = unpack(outs_rp[k], REPL, 1)
        if k == 0:
            loss = tail[0]
        results.append({**big, **small, **rp})
    return loss, grad_x, results


def _residual_add(h, add, name):
    rows, width = h.shape
    tr = _pick(rows, (256, 128))

    def body(h_ref, a_ref, o_ref):
        o_ref[...] = h_ref[...] + a_ref[...]

    blk = pl.BlockSpec((tr, width), lambda i: (i, 0))
    return pl.pallas_call(body, name=name, grid=(rows // tr,), in_specs=[blk, blk], out_specs=blk,
                          out_shape=jax.ShapeDtypeStruct((rows, width), F32),
                          compiler_params=_cparams(("parallel",)))(h, add)


WEIGHTS = ["meta_tokens", "norm_mix", "w_in", "q_norm", "w_uq", "kv_norm", "w_ukv", "w_sb_out", "w_mla_out", "w_o",
           "norm_ffn", "w_up", "conv_w", "conv_b", "w_down", "final_norm"]


def _run(d, x, weights, loss_target, moments_m, moments_v):
    p = dict(zip(WEIGHTS, weights))
    m = dict(zip(WEIGHTS, moments_m))
    v = dict(zip(WEIGHTS, moments_v))
    loss, grad_x, res = _step(d, x, p, m, v, loss_target)
    out = [loss, grad_x]
    for k in range(4):
        out += [res[k][n] for n in WEIGHTS]
    return tuple(out)


def kernel(x, meta_tokens, norm_mix, w_in, q_norm, w_uq, kv_norm, w_ukv, w_sb_out, w_mla_out, w_o, norm_ffn, w_up, conv_w, conv_b, w_down, final_norm, loss_target, m_meta_tokens, m_norm_mix, m_w_in, m_q_norm, m_w_uq, m_kv_norm, m_w_ukv, m_w_sb_out, m_w_mla_out, m_w_o, m_norm_ffn, m_w_up, m_conv_w, m_conv_b, m_w_down, m_final_norm, v_meta_tokens, v_norm_mix, v_w_in, v_q_norm, v_w_uq, v_kv_norm, v_w_ukv, v_w_sb_out, v_w_mla_out, v_w_o, v_norm_ffn, v_w_up, v_conv_w, v_conv_b, v_w_down, v_final_norm):
    weights = [meta_tokens, norm_mix, w_in, q_norm, w_uq, kv_norm, w_ukv, w_sb_out, w_mla_out, w_o, norm_ffn, w_up,
               conv_w, conv_b, w_down, final_norm]
    ms = [m_meta_tokens, m_norm_mix, m_w_in, m_q_norm, m_w_uq, m_kv_norm, m_w_ukv, m_w_sb_out, m_w_mla_out, m_w_o,
          m_norm_ffn, m_w_up, m_conv_w, m_conv_b, m_w_down, m_final_norm]
    vs = [v_meta_tokens, v_norm_mix, v_w_in, v_q_norm, v_w_uq, v_kv_norm, v_w_ukv, v_w_sb_out, v_w_mla_out, v_w_o,
          v_norm_ffn, v_w_up, v_conv_w, v_conv_b, v_w_down, v_final_norm]
    return _run(PROD, x, weights, loss_target, ms, vs)
```

```python
import jax
import jax.numpy as jnp
from jax import lax
from jax.experimental import pallas as pl
from jax.experimental.pallas import tpu as pltpu

F32 = jnp.float32
BF16 = jnp.bfloat16

EPS = 1e-6
ROPE_THETA = 10000.0
ADAM_LR = 0.001
ADAM_B1 = 0.9
ADAM_B2 = 0.999
ADAM_EPS = 1e-08
ADAM_WD = 0.01
ADAM_STEP = 10
NEG = -1e30
DEAD = -110.0
LANES = 128
PACK_W = 1024
ADAMW_TILE_ELEMS = 256 * 1024
V7X_VMEM_LIMIT = 48 * 1024 * 1024
V7X_VMEM_LIMIT_BIG = 58 * 1024 * 1024
MESH_AXES = ("x", "y", "c")
N_DEV = 8
FLIPS = [(0, 0, 1), (0, 1, 0), (0, 1, 1), (1, 0, 0), (1, 0, 1), (1, 1, 0), (1, 1, 1)]


class _Dims:
    def __init__(self, d_model=2048, seq=8192, depth=2, n_meta=16, block=128, sb_heads=8, hd=128,
                 mla_heads=8, q_lora=512, kv_lora=256, nope=128, rope=64, vdim=128, d_ff=5632, tq=None):
        self.d, self.seq, self.depth, self.n_meta, self.block = d_model, seq, depth, n_meta, block
        self.sb_heads, self.hd, self.mla_heads = sb_heads, hd, mla_heads
        self.q_lora, self.kv_lora, self.nope, self.rope, self.vdim, self.f = q_lora, kv_lora, nope, rope, vdim, d_ff
        assert hd == LANES and nope == LANES and vdim == LANES and 2 * rope == LANES
        self.pad = block - n_meta
        self.lp = self.pad + n_meta + seq
        self.first_tok = self.pad + n_meta
        assert self.first_tok == block and self.lp % block == 0 and self.lp // block < LANES
        self.tq = tq or next(t for t in (640, 512, 256, 128) if self.lp % t == 0)
        assert self.tq % block == 0 and self.lp % self.tq == 0
        self.sbw = sb_heads * hd
        self.mlaw = mla_heads * vdim
        self.wa = 3 * self.sbw
        self.d_in = 3 * self.sbw + q_lora + kv_lora + rope + 2 * d_model
        self.tg = min(1024, d_model)
        self.kr_off = q_lora + kv_lora
        raw = self.kr_off + LANES
        self.g_off = -(-raw // self.tg) * self.tg
        self.wb = self.g_off + 2 * d_model
        self.qw = mla_heads * 2 * LANES


PROD = _Dims()


def _pick(n, prefs):
    for p in prefs:
        if n % p == 0:
            return p
    return n


def _cparams(sem, limit=V7X_VMEM_LIMIT):
    return pltpu.CompilerParams(dimension_semantics=sem, vmem_limit_bytes=limit)


def _mm(a, b, *, ta=False, tb=False, out_dtype=F32, name):
    if ta:
        kdim, m = a.shape
    else:
        m, kdim = a.shape
    if tb:
        n, k2 = b.shape
    else:
        k2, n = b.shape
    assert kdim == k2, (a.shape, b.shape, ta, tb)
    tm = _pick(m, (640, 512, 256, 128))
    tn = _pick(n, (1024, 512, 384, 256, 128))
    tk = _pick(kdim, (2816, 2048, 1664, 1408, 1024, 640, 512, 256, 128))
    nk = kdim // tk
    dn = (((0 if ta else 1,), (1 if tb else 0,)), ((), ()))

    def dot(a_ref, b_ref):
        return lax.dot_general(a_ref[...].astype(BF16), b_ref[...].astype(BF16), dn, preferred_element_type=F32)

    def body_one(a_ref, b_ref, o_ref):
        o_ref[...] = dot(a_ref, b_ref).astype(out_dtype)

    def body_acc(a_ref, b_ref, o_ref, acc_ref):
        k = pl.program_id(2)

        @pl.when(k == 0)
        def _():
            acc_ref[...] = dot(a_ref, b_ref)

        @pl.when((k > 0) & (k < nk - 1))
        def _():
            acc_ref[...] += dot(a_ref, b_ref)

        @pl.when(k == nk - 1)
        def _():
            o_ref[...] = (acc_ref[...] + dot(a_ref, b_ref)).astype(out_dtype)

    a_spec = pl.BlockSpec((tk, tm), lambda i, j, k: (k, i)) if ta else pl.BlockSpec((tm, tk), lambda i, j, k: (i, k))
    b_spec = pl.BlockSpec((tn, tk), lambda i, j, k: (j, k)) if tb else pl.BlockSpec((tk, tn), lambda i, j, k: (k, j))
    return pl.pallas_call(
        body_one if nk == 1 else body_acc, name=name, grid=(m // tm, n // tn, nk), in_specs=[a_spec, b_spec],
        out_specs=pl.BlockSpec((tm, tn), lambda i, j, k: (i, j)),
        out_shape=jax.ShapeDtypeStruct((m, n), out_dtype),
        scratch_shapes=[] if nk == 1 else [pltpu.VMEM((tm, tn), F32)],
        compiler_params=_cparams(("parallel", "parallel", "arbitrary")),
    )(a, b)


def _norm_fwd(x, g, *, width, cidx, add=None, name):
    rows = x.shape[0]
    tr = _pick(rows, (256, 128))
    has_add = add is not None

    def body(*refs):
        if has_add:
            x_ref, a_ref, g_ref, xn_ref, y_ref, r_ref = refs
            xv = x_ref[...] + a_ref[...]
            xn_ref[...] = xv
        else:
            x_ref, g_ref, y_ref, r_ref = refs
            xv = x_ref[...]
        r = lax.rsqrt(jnp.mean(xv * xv, axis=1, keepdims=True) + EPS)
        y_ref[...] = (xv * r * g_ref[...]).astype(BF16)
        r_ref[...] = r

    blk = pl.BlockSpec((tr, width), lambda i: (i, 0))
    in_specs = [pl.BlockSpec((tr, width), lambda i: (i, cidx))]
    args = [x]
    if has_add:
        in_specs.append(blk)
        args.append(add)
    in_specs.append(pl.BlockSpec((1, width), lambda i: (0, 0)))
    args.append(g.reshape(1, width))
    out_specs = [blk, pl.BlockSpec((tr, 1), lambda i: (i, 0))]
    out_shape = [jax.ShapeDtypeStruct((rows, width), BF16), jax.ShapeDtypeStruct((rows, 1), F32)]
    if has_add:
        out_specs.insert(0, blk)
        out_shape.insert(0, jax.ShapeDtypeStruct((rows, width), F32))
    return pl.pallas_call(body, name=name, grid=(rows // tr,), in_specs=in_specs, out_specs=out_specs,
                          out_shape=out_shape, compiler_params=_cparams(("parallel",)))(*args)


def _norm_bwd(dy, x, r, g, *, width, cidx, dres=None, out_dtype=F32, name):
    rows = x.shape[0]
    tr = _pick(rows, (256, 128))
    has_res = dres is not None

    def body(*refs):
        if has_res:
            dy_ref, x_ref, r_ref, g_ref, dr_ref, dx_ref, dg_ref = refs
        else:
            dy_ref, x_ref, r_ref, g_ref, dx_ref, dg_ref = refs
        i = pl.program_id(0)

        @pl.when(i == 0)
        def _():
            dg_ref[...] = jnp.zeros_like(dg_ref)

        dyv, xv, rv = dy_ref[...], x_ref[...], r_ref[...]
        gy = dyv * g_ref[...]
        c = jnp.mean(gy * xv, axis=1, keepdims=True)
        dx = rv * gy - xv * (rv * rv * rv) * c
        if has_res:
            dx = dx + dr_ref[...]
        dx_ref[...] = dx.astype(out_dtype)
        dg_ref[...] += jnp.sum(dyv * xv * rv, axis=0, keepdims=True)

    blk = pl.BlockSpec((tr, width), lambda i: (i, 0))
    in_specs = [blk, pl.BlockSpec((tr, width), lambda i: (i, cidx)), pl.BlockSpec((tr, 1), lambda i: (i, 0)),
                pl.BlockSpec((1, width), lambda i: (0, 0))]
    args = [dy, x, r, g.reshape(1, width)]
    if has_res:
        in_specs.append(blk)
        args.append(dres)
    return pl.pallas_call(
        body, name=name, grid=(rows // tr,), in_specs=in_specs,
        out_specs=[blk, pl.BlockSpec((1, width), lambda i: (0, 0))],
        out_shape=[jax.ShapeDtypeStruct((rows, width), out_dtype), jax.ShapeDtypeStruct((1, width), F32)],
        compiler_params=_cparams(("arbitrary",)))(*args)


def _split3(x):
    h1 = x.astype(BF16)
    r1 = x - h1.astype(F32)
    h2 = r1.astype(BF16)
    h3 = (r1 - h2.astype(F32)).astype(BF16)
    return h1, h2, h3


def _cum(x, tri):
    h1, h2, h3 = _split3(x)
    dot = lambda h: jnp.dot(h, tri, preferred_element_type=F32)
    return dot(h1) + dot(h2) + dot(h3)


def _dot_nt(a, b):
    return lax.dot_general(a, b, (((1,), (1,)), ((), ())), preferred_element_type=F32)


def _dot_tn(a, b):
    return lax.dot_general(a, b, (((0,), (0,)), ((), ())), preferred_element_type=F32)


def _sb_fwd(qkv, d, name):
    nh, hd, lp, tq, t = d.sb_heads, d.hd, d.lp, d.tq, d.block
    r = tq // t
    scale = hd ** -0.5
    pad = d.pad

    def body(q_ref, k_ref, v_ref, o_ref, c_ref):
        i = pl.program_id(1)
        q = q_ref[...]
        row = i * tq + lax.broadcasted_iota(jnp.int32, (tq, t), 0)
        colb = lax.broadcasted_iota(jnp.int32, (tq, t), 1)
        tri = (lax.broadcasted_iota(jnp.int32, (t, t), 0)
               > lax.broadcasted_iota(jnp.int32, (t, t), 1)).astype(BF16)
        lane = lax.broadcasted_iota(jnp.int32, (tq, LANES), 1)

        c_ref[...] = jnp.zeros_like(c_ref)

        def make_step(masked):
            def step(j, carry):
                acc, run = carry
                off = pl.multiple_of(j * t, t)
                k = k_ref[pl.ds(off, t), :]
                v = v_ref[pl.ds(off, t), :]
                z = _dot_nt(q, k) * scale
                e = jnp.exp(-jnp.abs(z))
                sp = jnp.maximum(z, 0.0) + jnp.log(1.0 + e)
                if masked:
                    col = j * t + colb
                    mask = (col < row) & (col >= pad)
                    spm = jnp.where(mask, sp, 0.0)
                else:
                    spm = sp
                w = jnp.exp(z - sp - _cum(spm, tri) + run)
                if masked:
                    w = jnp.where(mask, w, 0.0)
                acc = acc + jnp.dot(w.astype(BF16), v, preferred_element_type=F32)
                c_ref[...] = jnp.where(lane == j, run, c_ref[...])
                run = run - jnp.sum(spm, axis=1, keepdims=True)
                return acc, run
            return step

        masked, plain = make_step(True), make_step(False)
        hi = (i + 1) * r
        lo = jnp.maximum(i * r, 1)
        carry = (jnp.zeros((tq, hd), F32), jnp.zeros((tq, 1), F32))
        acc, run = lax.fori_loop(0, hi - lo, lambda jj, c: masked(hi - 1 - jj, c), carry)

        def alive(run):
            return (jnp.max(run) >= DEAD).astype(jnp.int32)

        def cond(st):
            return (st[0] >= 1) & (st[1] > 0)

        def wbody(st):
            j, _, acc, run = st
            acc, run = plain(j, (acc, run))
            return j - 1, alive(run), acc, run

        j, live, acc, run = lax.while_loop(cond, wbody, (lo - 1, alive(run), acc, run))
        acc, run = lax.fori_loop(0, live, lambda _, c: masked(0, c), (acc, run))
        o_ref[...] = acc
        first = jnp.where(live > 0, 0, j + 1).astype(F32)
        c_ref[...] = jnp.where(lane == LANES - 1, first, c_ref[...])

    return pl.pallas_call(
        body, name=name, grid=(nh, lp // tq),
        in_specs=[pl.BlockSpec((tq, hd), lambda h, i: (i, h)),
                  pl.BlockSpec((lp, hd), lambda h, i: (0, nh + h)),
                  pl.BlockSpec((lp, hd), lambda h, i: (0, 2 * nh + h))],
        out_specs=[pl.BlockSpec((tq, hd), lambda h, i: (i, h)),
                   pl.BlockSpec((None, tq, LANES), lambda h, i: (h, i, 0))],
        out_shape=[jax.ShapeDtypeStruct((lp, nh * hd), F32), jax.ShapeDtypeStruct((nh, lp, LANES), F32)],
        compiler_params=_cparams(("parallel", "arbitrary")),
    )(qkv, qkv, qkv)


def _sb_bwd(qkv, do, carry, d, name):
    nh, hd, lp, tq, t = d.sb_heads, d.hd, d.lp, d.tq, d.block
    r = tq // t
    scale = hd ** -0.5
    pad = d.pad

    def body(q_ref, k_ref, v_ref, do_ref, c_ref, dq_ref, dk_ref, dv_ref):
        i = pl.program_id(1)

        @pl.when(i == 0)
        def _():
            dk_ref[...] = jnp.zeros_like(dk_ref)
            dv_ref[...] = jnp.zeros_like(dv_ref)

        q = q_ref[...]
        dob = do_ref[...].astype(BF16)
        row = i * tq + lax.broadcasted_iota(jnp.int32, (tq, t), 0)
        colb = lax.broadcasted_iota(jnp.int32, (tq, t), 1)
        ri = lax.broadcasted_iota(jnp.int32, (t, t), 0)
        ci = lax.broadcasted_iota(jnp.int32, (t, t), 1)
        tri_suf = (ri > ci).astype(BF16)
        tri_pre = (ri < ci).astype(BF16)
        lane = lax.broadcasted_iota(jnp.int32, (tq, LANES), 1)

        def make_step(masked):
            def step(j, carry):
                dq, pc = carry
                off = pl.multiple_of(j * t, t)
                k = k_ref[pl.ds(off, t), :]
                v = v_ref[pl.ds(off, t), :]
                z = _dot_nt(q, k) * scale
                e = jnp.exp(-jnp.abs(z))
                sp = jnp.maximum(z, 0.0) + jnp.log(1.0 + e)
                if masked:
                    col = j * t + colb
                    mask = (col < row) & (col >= pad)
                    spm = jnp.where(mask, sp, 0.0)
                else:
                    spm = sp
                run = jnp.sum(jnp.where(lane == j, c_ref[...], 0.0), axis=1, keepdims=True)
                w = jnp.exp(z - sp - _cum(spm, tri_suf) + run)
                if masked:
                    w = jnp.where(mask, w, 0.0)
                g = w * _dot_nt(dob, v)
                p = _cum(g, tri_pre) + pc
                inv = 1.0 / (1.0 + e)
                sig = jnp.where(z >= 0.0, inv, e * inv)
                dz = (g * (1.0 - sig) - sig * p) * scale
                if masked:
                    dz = jnp.where(mask, dz, 0.0)
                dzb = dz.astype(BF16)
                dq = dq + jnp.dot(dzb, k, preferred_element_type=F32)
                dk_ref[pl.ds(off, t), :] += _dot_tn(dzb, q)
                dv_ref[pl.ds(off, t), :] += _dot_tn(w.astype(BF16), dob)
                pc = pc + jnp.sum(g, axis=1, keepdims=True)
                return dq, pc
            return step

        masked, plain = make_step(True), make_step(False)
        lo = jnp.maximum(i * r, 1)
        first = jnp.max(jnp.where(lane == LANES - 1, c_ref[...], 0.0)).astype(jnp.int32)
        carry = (jnp.zeros((tq, hd), F32), jnp.zeros((tq, 1), F32))
        carry = lax.fori_loop(0, (first == 0).astype(jnp.int32), lambda _, c: masked(0, c), carry)
        carry = lax.fori_loop(jnp.maximum(first, 1), lo, plain, carry)
        dq, _ = lax.fori_loop(lo, (i + 1) * r, masked, carry)
        dq_ref[...] = dq.astype(BF16)

    w3 = nh * hd
    return pl.pallas_call(
        body, name=name, grid=(nh, lp // tq),
        in_specs=[pl.BlockSpec((tq, hd), lambda h, i: (i, h)),
                  pl.BlockSpec((lp, hd), lambda h, i: (0, nh + h)),
                  pl.BlockSpec((lp, hd), lambda h, i: (0, 2 * nh + h)),
                  pl.BlockSpec((tq, hd), lambda h, i: (i, h)),
                  pl.BlockSpec((None, tq, LANES), lambda h, i: (h, i, 0))],
        out_specs=[pl.BlockSpec((tq, hd), lambda h, i: (i, h)),
                   pl.BlockSpec((lp, hd), lambda h, i: (0, h)),
                   pl.BlockSpec((lp, hd), lambda h, i: (0, h))],
        out_shape=[jax.ShapeDtypeStruct((lp, w3), BF16), jax.ShapeDtypeStruct((lp, w3), F32),
                   jax.ShapeDtypeStruct((lp, w3), F32)],
        compiler_params=_cparams(("arbitrary", "arbitrary")),
    )(qkv, qkv, qkv, do, carry)


def _mla_prep_fwd(qraw, projb, ctab, stab, d, name):
    lp, nh = d.lp, d.mla_heads
    tr = _pick(lp, (256, 128))
    kidx = d.kr_off // LANES

    def rope(u, c, s):
        return u * c + pltpu.roll(u, LANES // 2, 1) * s

    def body(q_ref, k_ref, c_ref, s_ref, qm_ref, kr_ref):
        c, s = c_ref[...], s_ref[...]
        for h in range(nh):
            base = 2 * LANES * h
            qm_ref[:, base:base + LANES] = q_ref[:, base:base + LANES].astype(BF16)
            qm_ref[:, base + LANES:base + 2 * LANES] = rope(q_ref[:, base + LANES:base + 2 * LANES], c, s).astype(BF16)
        kr_ref[...] = rope(k_ref[...], c, s).astype(BF16)

    tab = pl.BlockSpec((tr, LANES), lambda i: (i, 0))
    return pl.pallas_call(
        body, name=name, grid=(lp // tr,),
        in_specs=[pl.BlockSpec((tr, d.qw), lambda i: (i, 0)), pl.BlockSpec((tr, LANES), lambda i: (i, kidx)), tab, tab],
        out_specs=[pl.BlockSpec((tr, d.qw), lambda i: (i, 0)), tab],
        out_shape=[jax.ShapeDtypeStruct((lp, d.qw), BF16), jax.ShapeDtypeStruct((lp, LANES), BF16)],
        compiler_params=_cparams(("parallel",)))(qraw, projb, ctab, stab)


def _mla_prep_bwd(dqm, dkr, ctab, stab, d, name):
    lp, nh = d.lp, d.mla_heads
    tr = _pick(lp, (256, 128))

    def unrope(g, c, s):
        return g * c + pltpu.roll(g * s, LANES // 2, 1)

    def body(dq_ref, dk_ref, c_ref, s_ref, o_ref, ok_ref):
        c, s = c_ref[...], s_ref[...]
        for h in range(nh):
            base = 2 * LANES * h
            o_ref[:, base:base + LANES] = dq_ref[:, base:base + LANES].astype(BF16)
            o_ref[:, base + LANES:base + 2 * LANES] = unrope(dq_ref[:, base + LANES:base + 2 * LANES], c, s).astype(BF16)
        ok_ref[...] = unrope(dk_ref[...], c, s).astype(BF16)

    tab = pl.BlockSpec((tr, LANES), lambda i: (i, 0))
    wide = pl.BlockSpec((tr, d.qw), lambda i: (i, 0))
    return pl.pallas_call(
        body, name=name, grid=(lp // tr,), in_specs=[wide, tab, tab, tab], out_specs=[wide, tab],
        out_shape=[jax.ShapeDtypeStruct((lp, d.qw), BF16), jax.ShapeDtypeStruct((lp, LANES), BF16)],
        compiler_params=_cparams(("parallel",)))(dqm, dkr, ctab, stab)


def _mla_fwd(qm, kv, kr, d, name):
    nh, lp, t = d.mla_heads, d.lp, d.tq
    scale = (d.nope + d.rope) ** -0.5
    pad = d.pad

    def body(q_ref, kn_ref, v_ref, kr_ref, o_ref, lse_ref):
        i = pl.program_id(1)
        q = q_ref[...]
        row = i * t + lax.broadcasted_iota(jnp.int32, (t, t), 0)
        colb = lax.broadcasted_iota(jnp.int32, (t, t), 1)

        def make_step(masked):
            def step(j, carry):
                acc, m, l = carry
                off = pl.multiple_of(j * t, t)
                kc = jnp.concatenate([kn_ref[pl.ds(off, t), :], kr_ref[pl.ds(off, t), :]], axis=1)
                s = _dot_nt(q, kc) * scale
                if masked:
                    col = j * t + colb
                    s = jnp.where((col <= row) & (col >= pad), s, NEG)
                m_new = jnp.maximum(m, jnp.max(s, axis=1, keepdims=True))
                alpha = jnp.exp(m - m_new)
                p = jnp.exp(s - m_new)
                l = alpha * l + jnp.sum(p, axis=1, keepdims=True)
                acc = alpha * acc + jnp.dot(p.astype(BF16), v_ref[pl.ds(off, t), :], preferred_element_type=F32)
                return acc, m_new, l
            return step

        masked, plain = make_step(True), make_step(False)
        carry = masked(0, (jnp.zeros((t, LANES), F32), jnp.full((t, 1), NEG, F32), jnp.zeros((t, 1), F32)))
        carry = lax.fori_loop(1, i, plain, carry)
        acc, m, l = lax.fori_loop(i, i + jnp.minimum(i, 1), masked, carry)
        rowv = i * t + lax.broadcasted_iota(jnp.int32, (t, LANES), 0)
        o_ref[...] = jnp.where(rowv >= pad, acc / l, 0.0)
        lse_ref[...] = m + jnp.log(l)

    return pl.pallas_call(
        body, name=name, grid=(nh, lp // t),
        in_specs=[pl.BlockSpec((t, 2 * LANES), lambda h, i: (i, h)),
                  pl.BlockSpec((lp, LANES), lambda h, i: (0, h)),
                  pl.BlockSpec((lp, LANES), lambda h, i: (0, nh + h)),
                  pl.BlockSpec((lp, LANES), lambda h, i: (0, 0))],
        out_specs=[pl.BlockSpec((t, LANES), lambda h, i: (i, h)),
                   pl.BlockSpec((None, t, 1), lambda h, i: (h, i, 0))],
        out_shape=[jax.ShapeDtypeStruct((lp, nh * LANES), F32), jax.ShapeDtypeStruct((nh, lp, 1), F32)],
        compiler_params=_cparams(("parallel", "arbitrary")),
    )(qm, kv, kv, kr)


def _mla_bwd(qm, kv, kr, o, do, lse, d, name):
    nh, lp, t = d.mla_heads, d.lp, d.tq
    scale = (d.nope + d.rope) ** -0.5
    pad = d.pad

    def body(q_ref, kn_ref, v_ref, kr_ref, o_ref, do_ref, lse_ref, dq_ref, dkn_ref, dv_ref, dkr_ref):
        h = pl.program_id(0)
        i = pl.program_id(1)

        @pl.when(i == 0)
        def _():
            dkn_ref[...] = jnp.zeros_like(dkn_ref)
            dv_ref[...] = jnp.zeros_like(dv_ref)

        @pl.when((i == 0) & (h == 0))
        def _():
            dkr_ref[...] = jnp.zeros_like(dkr_ref)

        q = q_ref[...]
        dof = do_ref[...]
        dob = dof.astype(BF16)
        delta = jnp.sum(dof * o_ref[...], axis=1, keepdims=True)
        lse = lse_ref[...]
        row = i * t + lax.broadcasted_iota(jnp.int32, (t, t), 0)
        colb = lax.broadcasted_iota(jnp.int32, (t, t), 1)

        def make_step(masked):
            def step(j, dq):
                off = pl.multiple_of(j * t, t)
                kc = jnp.concatenate([kn_ref[pl.ds(off, t), :], kr_ref[pl.ds(off, t), :]], axis=1)
                v = v_ref[pl.ds(off, t), :]
                s = _dot_nt(q, kc) * scale
                if masked:
                    col = j * t + colb
                    mask = (col <= row) & (col >= pad)
                    p = jnp.where(mask, jnp.exp(jnp.where(mask, s, NEG) - lse), 0.0)
                else:
                    p = jnp.exp(s - lse)
                dp = _dot_nt(dob, v)
                dsb = (p * (dp - delta) * scale).astype(BF16)
                dq = dq + jnp.dot(dsb, kc, preferred_element_type=F32)
                dkc = _dot_tn(dsb, q)
                dkn_ref[pl.ds(off, t), :] += dkc[:, :LANES]
                dkr_ref[pl.ds(off, t), :] += dkc[:, LANES:]
                dv_ref[pl.ds(off, t), :] += _dot_tn(p.astype(BF16), dob)
                return dq
            return step

        masked, plain = make_step(True), make_step(False)
        dq = masked(0, jnp.zeros((t, 2 * LANES), F32))
        dq = lax.fori_loop(1, i, plain, dq)
        dq_ref[...] = lax.fori_loop(i, i + jnp.minimum(i, 1), masked, dq)

    return pl.pallas_call(
        body, name=name, grid=(nh, lp // t),
        in_specs=[pl.BlockSpec((t, 2 * LANES), lambda h, i: (i, h)),
                  pl.BlockSpec((lp, LANES), lambda h, i: (0, h), pipeline_mode=pl.Buffered(1)),
                  pl.BlockSpec((lp, LANES), lambda h, i: (0, nh + h), pipeline_mode=pl.Buffered(1)),
                  pl.BlockSpec((lp, LANES), lambda h, i: (0, 0), pipeline_mode=pl.Buffered(1)),
                  pl.BlockSpec((t, LANES), lambda h, i: (i, h)),
                  pl.BlockSpec((t, LANES), lambda h, i: (i, h)),
                  pl.BlockSpec((None, t, 1), lambda h, i: (h, i, 0))],
        out_specs=[pl.BlockSpec((t, 2 * LANES), lambda h, i: (i, h)),
                   pl.BlockSpec((lp, LANES), lambda h, i: (0, h)),
                   pl.BlockSpec((lp, LANES), lambda h, i: (0, h)),
                   pl.BlockSpec((lp, LANES), lambda h, i: (0, 0))],
        out_shape=[jax.ShapeDtypeStruct((lp, nh * 2 * LANES), F32), jax.ShapeDtypeStruct((lp, nh * LANES), F32),
                   jax.ShapeDtypeStruct((lp, nh * LANES), F32), jax.ShapeDtypeStruct((lp, LANES), F32)],
        compiler_params=_cparams(("arbitrary", "arbitrary"), V7X_VMEM_LIMIT_BIG),
    )(qm, kv, kv, kr, o, do, lse)


def _sigmoid(x):
    return 1.0 / (1.0 + jnp.exp(-x))


def _gate_fwd(projb, b_sb, b_mla, d, name):
    lp, tg = d.lp, d.tg
    tr = _pick(lp, (256, 128))
    o1, o2 = d.g_off // tg, (d.g_off + d.d) // tg

    def body(g1_ref, g2_ref, b1_ref, b2_ref, o_ref):
        o_ref[...] = (_sigmoid(g1_ref[...]) * b1_ref[...] + _sigmoid(g2_ref[...]) * b2_ref[...]).astype(BF16)

    blk = pl.BlockSpec((tr, tg), lambda i, j: (i, j))
    return pl.pallas_call(
        body, name=name, grid=(lp // tr, d.d // tg),
        in_specs=[pl.BlockSpec((tr, tg), lambda i, j: (i, o1 + j)), pl.BlockSpec((tr, tg), lambda i, j: (i, o2 + j)),
                  blk, blk],
        out_specs=blk, out_shape=jax.ShapeDtypeStruct((lp, d.d), BF16),
        compiler_params=_cparams(("parallel", "parallel")))(projb, projb, b_sb, b_mla)


def _gate_bwd(dm, projb, b_sb, b_mla, d, name):
    lp, tg = d.lp, d.tg
    tr = _pick(lp, (256, 128))
    o1, o2 = d.g_off // tg, (d.g_off + d.d) // tg

    def body(dm_ref, g1_ref, g2_ref, b1_ref, b2_ref, db1_ref, db2_ref, dg1_ref, dg2_ref):
        dmv = dm_ref[...]
        s1, s2 = _sigmoid(g1_ref[...]), _sigmoid(g2_ref[...])
        db1_ref[...] = (dmv * s1).astype(BF16)
        db2_ref[...] = (dmv * s2).astype(BF16)
        dg1_ref[...] = (dmv * b1_ref[...] * s1 * (1.0 - s1)).astype(BF16)
        dg2_ref[...] = (dmv * b2_ref[...] * s2 * (1.0 - s2)).astype(BF16)

    blk = pl.BlockSpec((tr, tg), lambda i, j: (i, j))
    out = jax.ShapeDtypeStruct((lp, d.d), BF16)
    return pl.pallas_call(
        body, name=name, grid=(lp // tr, d.d // tg),
        in_specs=[blk, pl.BlockSpec((tr, tg), lambda i, j: (i, o1 + j)), pl.BlockSpec((tr, tg), lambda i, j: (i, o2 + j)),
                  blk, blk],
        out_specs=[blk] * 4, out_shape=[out] * 4,
        compiler_params=_cparams(("parallel", "parallel")))(dm, projb, projb, b_sb, b_mla)


HALO = 8


def _conv_tiles(d):
    return _pick(d.lp, (640, 512, 256, 128)), _pick(d.f, (512, 256, 128))


def _convglu_fwd(up, cw, cb, d, name):
    lp, f = d.lp, d.f
    tr, tc = _conv_tiles(d)
    nf = f // tc
    hb = tr // HALO
    pad = d.pad

    def body(a_ref, g_ref, pa_ref, pg_ref, wa_ref, wg_ref, ba_ref, bg_ref, o_ref, xa, xg):
        i = pl.program_id(1)
        keep = (i > 0).astype(F32)
        xa[0:HALO, :] = pa_ref[...] * keep
        xg[0:HALO, :] = pg_ref[...] * keep
        xa[HALO:, :] = a_ref[...]
        xg[HALO:, :] = g_ref[...]

        def conv(x, w_ref, b_ref):
            return (b_ref[...] + x[pl.ds(HALO - 2, tr), :] * w_ref[0:1, :] + x[pl.ds(HALO - 1, tr), :] * w_ref[1:2, :]
                    + x[pl.ds(HALO, tr), :] * w_ref[2:3, :])

        ua = conv(xa, wa_ref, ba_ref)
        ug = conv(xg, wg_ref, bg_ref)
        row = i * tr + lax.broadcasted_iota(jnp.int32, (tr, tc), 0)
        o_ref[...] = jnp.where(row >= pad, ua * _sigmoid(ua) * ug, 0.0).astype(BF16)

    prev = lambda j, i: (jnp.maximum(i * hb - 1, 0), j)
    prevg = lambda j, i: (jnp.maximum(i * hb - 1, 0), nf + j)
    return pl.pallas_call(
        body, name=name, grid=(nf, lp // tr),
        in_specs=[pl.BlockSpec((tr, tc), lambda j, i: (i, j)), pl.BlockSpec((tr, tc), lambda j, i: (i, nf + j)),
                  pl.BlockSpec((HALO, tc), prev), pl.BlockSpec((HALO, tc), prevg),
                  pl.BlockSpec((3, tc), lambda j, i: (0, j)), pl.BlockSpec((3, tc), lambda j, i: (0, nf + j)),
                  pl.BlockSpec((1, tc), lambda j, i: (0, j)), pl.BlockSpec((1, tc), lambda j, i: (0, nf + j))],
        out_specs=pl.BlockSpec((tr, tc), lambda j, i: (i, j)),
        out_shape=jax.ShapeDtypeStruct((lp, f), BF16),
        scratch_shapes=[pltpu.VMEM((tr + HALO, tc), F32), pltpu.VMEM((tr + HALO, tc), F32)],
        compiler_params=_cparams(("parallel", "arbitrary")))(up, up, up, up, cw, cw, cb, cb)


def _convglu_bwd(up, dact, cw, cb, d, name):
    lp, f = d.lp, d.f
    tr, tc = _conv_tiles(d)
    nf = f // tc
    hb = tr // HALO
    nrow = lp // tr
    pad = d.pad
    te = tr + HALO

    def body(a_ref, g_ref, pa_ref, pg_ref, na_ref, ng_ref, da_ref, nd_ref, wa_ref, wg_ref, ba_ref, bg_ref,
             oa_ref, og_ref, sa_ref, sg_ref, xa, xg, xd, ya, yg):
        i = pl.program_id(1)

        @pl.when(i == 0)
        def _():
            sa_ref[...] = jnp.zeros_like(sa_ref)
            sg_ref[...] = jnp.zeros_like(sg_ref)

        keep_p = (i > 0).astype(F32)
        keep_n = (i < nrow - 1).astype(F32)
        xa[0:HALO, :] = pa_ref[...] * keep_p
        xg[0:HALO, :] = pg_ref[...] * keep_p
        xa[HALO:HALO + tr, :] = a_ref[...]
        xg[HALO:HALO + tr, :] = g_ref[...]
        xa[HALO + tr:, :] = na_ref[...] * keep_n
        xg[HALO + tr:, :] = ng_ref[...] * keep_n
        xd[0:tr, :] = da_ref[...]
        xd[tr:, :] = nd_ref[...] * keep_n

        def conv(x, w_ref, b_ref):
            return (b_ref[...] + x[pl.ds(HALO - 2, te), :] * w_ref[0:1, :] + x[pl.ds(HALO - 1, te), :] * w_ref[1:2, :]
                    + x[pl.ds(HALO, te), :] * w_ref[2:3, :])

        ua = conv(xa, wa_ref, ba_ref)
        ug = conv(xg, wg_ref, bg_ref)
        sg = _sigmoid(ua)
        dact = xd[...]
        ya[...] = dact * ug * (sg * (1.0 + ua * (1.0 - sg)))
        yg[...] = dact * (ua * sg)
        row = i * tr + lax.broadcasted_iota(jnp.int32, (tr, tc), 0)

        def back(y, x, w_ref, o_ref, s_ref):
            y0 = y[pl.ds(0, tr), :]
            dup = y0 * w_ref[2:3, :] + y[pl.ds(1, tr), :] * w_ref[1:2, :] + y[pl.ds(2, tr), :] * w_ref[0:1, :]
            o_ref[...] = jnp.where(row >= pad, dup, 0.0).astype(BF16)
            for tap in range(3):
                s_ref[tap:tap + 1, :] += jnp.sum(y0 * x[pl.ds(HALO - 2 + tap, tr), :], axis=0, keepdims=True)
            s_ref[3:4, :] += jnp.sum(y0, axis=0, keepdims=True)

        back(ya, xa, wa_ref, oa_ref, sa_ref)
        back(yg, xg, wg_ref, og_ref, sg_ref)

    last8 = lp // HALO - 1
    prev = lambda j, i: (jnp.maximum(i * hb - 1, 0), j)
    prevg = lambda j, i: (jnp.maximum(i * hb - 1, 0), nf + j)
    nxt = lambda j, i: (jnp.minimum((i + 1) * hb, last8), j)
    nxtg = lambda j, i: (jnp.minimum((i + 1) * hb, last8), nf + j)
    halo = lambda m: pl.BlockSpec((HALO, tc), m)
    main = pl.BlockSpec((tr, tc), lambda j, i: (i, j))
    sums = pl.BlockSpec((8, tc), lambda j, i: (0, j))
    return pl.pallas_call(
        body, name=name, grid=(nf, nrow),
        in_specs=[main, pl.BlockSpec((tr, tc), lambda j, i: (i, nf + j)), halo(prev), halo(prevg), halo(nxt), halo(nxtg),
                  main, halo(nxt),
                  pl.BlockSpec((3, tc), lambda j, i: (0, j)), pl.BlockSpec((3, tc), lambda j, i: (0, nf + j)),
                  pl.BlockSpec((1, tc), lambda j, i: (0, j)), pl.BlockSpec((1, tc), lambda j, i: (0, nf + j))],
        out_specs=[main, main, sums, sums],
        out_shape=[jax.ShapeDtypeStruct((lp, f), BF16), jax.ShapeDtypeStruct((lp, f), BF16),
                   jax.ShapeDtypeStruct((8, f), F32), jax.ShapeDtypeStruct((8, f), F32)],
        scratch_shapes=[pltpu.VMEM((tr + 2 * HALO, tc), F32), pltpu.VMEM((tr + 2 * HALO, tc), F32),
                        pltpu.VMEM((te, tc), F32), pltpu.VMEM((te, tc), F32), pltpu.VMEM((te, tc), F32)],
        compiler_params=_cparams(("parallel", "arbitrary")))(up, up, up, up, up, up, dact, dact, cw, cw, cb, cb)


def _head(h, target, g, d, name):
    lp, dm, t = d.lp, d.d, d.block
    inv_d = 1.0 / dm

    def body(h_ref, t_ref, g_ref, dh_ref, loss_ref, dg_ref):
        i = pl.program_id(0)

        @pl.when(i == 0)
        def _():
            dh_ref[...] = jnp.zeros_like(dh_ref)
            loss_ref[...] = jnp.zeros_like(loss_ref)
            dg_ref[...] = jnp.zeros_like(dg_ref)

        @pl.when(i > 0)
        def _():
            x, gv = h_ref[...], g_ref[...]
            r = lax.rsqrt(jnp.mean(x * x, axis=1, keepdims=True) + EPS)
            xh = x * r
            err = xh * gv - t_ref[...]
            loss_ref[...] += 0.5 * inv_d * jnp.sum(err * err)
            dy = err * inv_d
            gy = dy * gv
            c = jnp.mean(gy * x, axis=1, keepdims=True)
            dh_ref[...] = r * gy - x * (r * r * r) * c
            dg_ref[...] += jnp.sum(dy * xh, axis=0, keepdims=True)

    blk = pl.BlockSpec((t, dm), lambda i: (i, 0))
    return pl.pallas_call(
        body, name=name, grid=(lp // t,),
        in_specs=[blk, pl.BlockSpec((t, dm), lambda i: (jnp.maximum(i - 1, 0), 0)), pl.BlockSpec((1, dm), lambda i: (0, 0))],
        out_specs=[blk, pl.BlockSpec((8, LANES), lambda i: (0, 0)), pl.BlockSpec((1, dm), lambda i: (0, 0))],
        out_shape=[jax.ShapeDtypeStruct((lp, dm), F32), jax.ShapeDtypeStruct((8, LANES), F32),
                   jax.ShapeDtypeStruct((1, dm), F32)],
        compiler_params=_cparams(("arbitrary",)))(h, target, g.reshape(1, dm))


def _exchange(xs, *, scatter, name):
    n = len(xs)
    nf = len(FLIPS)

    def body(*refs):
        ins, outs = refs[:n], refs[n:2 * n]
        send_sems, recv_sems, loc_sems = refs[2 * n:]
        x, y, c = lax.axis_index("x"), lax.axis_index("y"), lax.axis_index("c")
        me = 4 * x + 2 * y + c
        sends, recvs, locs = [], [], []
        for a in range(n):
            src_me = ins[a].at[me] if scatter else ins[a]
            loc = pltpu.make_async_copy(src_me, outs[a].at[me], loc_sems.at[a])
            loc.start()
            locs.append(loc)
            for k, (fx, fy, fc) in enumerate(FLIPS):
                px, py, pc = x ^ fx, y ^ fy, c ^ fc
                peer = 4 * px + 2 * py + pc
                src = ins[a].at[peer] if scatter else ins[a]
                cp = pltpu.make_async_remote_copy(
                    src_ref=src, dst_ref=outs[a].at[me], send_sem=send_sems.at[a * nf + k],
                    recv_sem=recv_sems.at[a * nf + k], device_id=(px, py, pc), device_id_type=pl.DeviceIdType.MESH)
                cp.start()
                sends.append(cp)
                recvs.append(pltpu.make_async_remote_copy(
                    src_ref=src, dst_ref=outs[a].at[peer], send_sem=send_sems.at[a * nf + k],
                    recv_sem=recv_sems.at[a * nf + k], device_id=(px, py, pc), device_id_type=pl.DeviceIdType.MESH))
        for cp in recvs:
            cp.wait_recv()
        for cp in sends:
            cp.wait_send()
        for loc in locs:
            loc.wait()

    hbm = pl.BlockSpec(memory_space=pltpu.HBM)
    out_shape = [jax.ShapeDtypeStruct(((N_DEV,) + tuple(x.shape[1:])) if scatter else ((N_DEV,) + tuple(x.shape)), x.dtype)
                 for x in xs]
    return pl.pallas_call(
        body, name=name, in_specs=[hbm] * n, out_specs=[hbm] * n, out_shape=out_shape,
        scratch_shapes=[pltpu.SemaphoreType.DMA((n * nf,)), pltpu.SemaphoreType.DMA((n * nf,)),
                        pltpu.SemaphoreType.DMA((n,))],
    )(*xs)


def _gather_two_level(xs, name):
    n = len(xs)

    def body(*refs):
        ins, outs = refs[:n], refs[n:2 * n]
        send_sems, recv_sems, loc_sems = refs[2 * n:]
        x, y, c = lax.axis_index("x"), lax.axis_index("y"), lax.axis_index("c")
        me, sibling = (x, y, c), (x, y, 1 - c)
        chips = [(1 - x, y), (x, 1 - y), (1 - x, 1 - y)]

        def slot(a, dev):
            return outs[a].at[4 * dev[0] + 2 * dev[1] + dev[2]]

        def copy(a, k, block, to, src=None):
            return pltpu.make_async_remote_copy(
                src_ref=slot(a, block) if src is None else src, dst_ref=slot(a, block),
                send_sem=send_sems.at[7 * a + k], recv_sem=recv_sems.at[7 * a + k],
                device_id=to, device_id_type=pl.DeviceIdType.MESH)

        locs = [pltpu.make_async_copy(ins[a], slot(a, me), loc_sems.at[a]) for a in range(n)]
        for loc in locs:
            loc.start()
        first = []
        for a in range(n):
            first.append(copy(a, 0, me, sibling, src=ins[a]))
            first += [copy(a, 1 + j, me, (*chip, c), src=ins[a]) for j, chip in enumerate(chips)]
        for cp in first:
            cp.start()
        passed = []
        for j, chip in enumerate(chips):
            for a in range(n):
                copy(a, 1 + j, (*chip, c), me).wait_recv()
                fwd = copy(a, 4 + j, (*chip, c), sibling)
                fwd.start()
                passed.append(fwd)
        for a in range(n):
            copy(a, 0, sibling, me).wait_recv()
            for j, chip in enumerate(chips):
                copy(a, 4 + j, (*chip, 1 - c), me).wait_recv()
        for cp in first + passed:
            cp.wait_send()
        for loc in locs:
            loc.wait()

    hbm = pl.BlockSpec(memory_space=pltpu.HBM)
    return pl.pallas_call(
        body, name=name, in_specs=[hbm] * n, out_specs=[hbm] * n,
        out_shape=[jax.ShapeDtypeStruct((N_DEV,) + tuple(x.shape), x.dtype) for x in xs],
        scratch_shapes=[pltpu.SemaphoreType.DMA((7 * n,)), pltpu.SemaphoreType.DMA((7 * n,)),
                        pltpu.SemaphoreType.DMA((n,))],
    )(*xs)


def _pair_exchange(xs, name):
    n = len(xs)

    def body(*refs):
        ins, outs = refs[:n], refs[n:2 * n]
        send_sems, recv_sems = refs[2 * n:]
        x, y, c = lax.axis_index("x"), lax.axis_index("y"), lax.axis_index("c")
        copies = []
        for a in range(n):
            for q in range(4):
                cp = pltpu.make_async_remote_copy(
                    src_ref=ins[a].at[2 * q + (1 - c)], dst_ref=outs[a].at[q], send_sem=send_sems.at[4 * a + q],
                    recv_sem=recv_sems.at[4 * a + q], device_id=(x, y, 1 - c), device_id_type=pl.DeviceIdType.MESH)
                cp.start()
                copies.append(cp)
        for cp in copies:
            cp.wait_recv()
        for cp in copies:
            cp.wait_send()

    hbm = pl.BlockSpec(memory_space=pltpu.HBM)
    return pl.pallas_call(
        body, name=name, in_specs=[hbm] * n, out_specs=[hbm] * n,
        out_shape=[jax.ShapeDtypeStruct((4,) + tuple(x.shape[1:]), x.dtype) for x in xs],
        scratch_shapes=[pltpu.SemaphoreType.DMA((4 * n,)), pltpu.SemaphoreType.DMA((4 * n,))],
    )(*xs)


def _chip_exchange(xs, name):
    n = len(xs)

    def body(*refs):
        ins, outs = refs[:n], refs[n:2 * n]
        send_sems, recv_sems, loc_sems = refs[2 * n:]
        x, y, c = lax.axis_index("x"), lax.axis_index("y"), lax.axis_index("c")
        mine = 2 * x + y
        sends, recvs, locs = [], [], []
        for a in range(n):
            loc = pltpu.make_async_copy(ins[a].at[mine], outs[a].at[mine], loc_sems.at[a])
            loc.start()
            locs.append(loc)
            for k, (fx, fy) in enumerate([(1, 0), (0, 1), (1, 1)]):
                px, py = x ^ fx, y ^ fy
                peer = 2 * px + py
                sems = dict(send_sem=send_sems.at[3 * a + k], recv_sem=recv_sems.at[3 * a + k],
                            device_id=(px, py, c), device_id_type=pl.DeviceIdType.MESH)
                cp = pltpu.make_async_remote_copy(src_ref=ins[a].at[peer], dst_ref=outs[a].at[mine], **sems)
                cp.start()
                sends.append(cp)
                recvs.append(pltpu.make_async_remote_copy(src_ref=ins[a].at[peer], dst_ref=outs[a].at[peer], **sems))
        for cp in recvs:
            cp.wait_recv()
        for cp in sends:
            cp.wait_send()
        for loc in locs:
            loc.wait()

    hbm = pl.BlockSpec(memory_space=pltpu.HBM)
    return pl.pallas_call(
        body, name=name, in_specs=[hbm] * n, out_specs=[hbm] * n,
        out_shape=[jax.ShapeDtypeStruct(x.shape, x.dtype) for x in xs],
        scratch_shapes=[pltpu.SemaphoreType.DMA((3 * n,)), pltpu.SemaphoreType.DMA((3 * n,)),
                        pltpu.SemaphoreType.DMA((n,))],
    )(*xs)


def _pair_sum(a, b, name):
    shape = a.shape
    cols = shape[-1]
    rows = a.size // cols
    a, b = a.reshape(rows, cols), b.reshape(rows, cols)
    tr = next((t for t in (512, 256, 128, 64, 32, 16) if rows % t == 0 and t * cols <= 2 * ADAMW_TILE_ELEMS), rows)

    def body(a_ref, b_ref, o_ref):
        o_ref[...] = (a_ref[...].astype(F32) + b_ref[...].astype(F32)).astype(o_ref.dtype)

    blk = pl.BlockSpec((tr, cols), lambda i: (i, 0))
    return pl.pallas_call(body, name=name, grid=(rows // tr,), in_specs=[blk, blk], out_specs=blk,
                          out_shape=jax.ShapeDtypeStruct((rows, cols), a.dtype),
                          compiler_params=_cparams(("parallel",)))(a, b).reshape(shape)


def _adamw(parts, w, m, v, name):
    shape = w.shape
    cols = shape[-1]
    rows = w.size // cols
    nparts = parts.shape[0]
    parts, w, m, v = parts.reshape(nparts, rows, cols), w.reshape(rows, cols), m.reshape(rows, cols), v.reshape(rows, cols)
    tr = next((t for t in (256, 128, 64, 32, 16) if rows % t == 0 and t * cols <= ADAMW_TILE_ELEMS), rows)
    c1 = 1.0 - ADAM_B1 ** ADAM_STEP
    c2 = 1.0 - ADAM_B2 ** ADAM_STEP

    def body(p_ref, w_ref, m_ref, v_ref, g_ref, d_ref, mo_ref, vo_ref):
        g = p_ref[0].astype(F32)
        for q in range(1, nparts):
            g = g + p_ref[q].astype(F32)
        mn = ADAM_B1 * m_ref[...] + (1.0 - ADAM_B1) * g
        vn = ADAM_B2 * v_ref[...] + (1.0 - ADAM_B2) * (g * g)
        g_ref[...] = g
        mo_ref[...] = mn
        vo_ref[...] = vn
        d_ref[...] = -ADAM_LR * ((mn / c1) / (jnp.sqrt(vn / c2) + ADAM_EPS) + ADAM_WD * w_ref[...])

    blk = pl.BlockSpec((tr, cols), lambda i: (i, 0))
    out = jax.ShapeDtypeStruct((rows, cols), F32)
    res = pl.pallas_call(
        body, name=name, grid=(rows // tr,),
        in_specs=[pl.BlockSpec((nparts, tr, cols), lambda i: (0, i, 0)), blk, blk, blk],
        out_specs=[blk] * 4, out_shape=[out] * 4, compiler_params=_cparams(("parallel",)))(parts, w, m, v)
    return [r.reshape(shape) for r in res]


BIG = ["w_in", "w_uq", "w_ukv", "w_sb_out", "w_mla_out", "w_o", "w_up", "w_down"]
ROW_SHARDED = {"w_o", "w_down"}
SMALL = ["conv_w", "meta_tokens"]
SHARDED = BIG + SMALL
REPL = ["norm_mix", "q_norm", "kv_norm", "norm_ffn", "conv_b", "final_norm"]


def _pack_rows(flat, row_mult):
    unit = PACK_W * row_mult
    total = -(-flat.shape[0] // unit) * unit
    return jnp.pad(flat, (0, total - flat.shape[0])).reshape(-1, PACK_W)


def _padded_cols(c):
    return -(-c // LANES) * LANES


def _pad_block(a, name):
    c = a.shape[-1]
    if name in ROW_SHARDED or c % LANES == 0:
        return a
    return jnp.pad(a, [(0, 0)] * (a.ndim - 1) + [(0, _padded_cols(c) - c)])


def _full_from_slots(slots, name, c):
    if name in ROW_SHARDED:
        return jnp.transpose(slots, (1, 0, 2, 3)).reshape(slots.shape[1], -1, slots.shape[3])
    return jnp.concatenate([slots[q][..., :c] for q in range(N_DEV)], axis=-1)


def _slots_from_full(full, name):
    if name in ROW_SHARDED:
        l, rr, n = full.shape
        return jnp.transpose(full.reshape(l, N_DEV, rr // N_DEV, n), (1, 0, 2, 3))
    c = full.shape[-1] // N_DEV
    return jnp.stack([_pad_block(full[..., q * c:(q + 1) * c], name) for q in range(N_DEV)])


def _swap_halves(t):
    half = t.shape[-1] // 2
    return jnp.concatenate([t[..., half:], t[..., :half]], axis=-1)


def _in_offsets(d):
    widths = (d.sbw, d.sbw, d.sbw, d.q_lora, d.kv_lora, d.rope, d.d, d.d)
    offs, o = [], 0
    for w in widths:
        offs.append((o, o + w))
        o += w
    return offs


def _prime_weights(full, layer, d):
    offs = _in_offsets(d)
    w_in = full["w_in"][layer]
    cols = lambda k: w_in[:, offs[k][0]:offs[k][1]]
    kr = cols(5)
    zpad = jnp.zeros((d.d, d.g_off - d.kr_off - LANES), w_in.dtype)
    w_inb = jnp.concatenate([cols(3), cols(4), kr, _swap_halves(kr), zpad, cols(6), cols(7)], axis=1)
    w_ina = w_in[:, :d.wa]
    uq = full["w_uq"][layer].reshape(d.q_lora, d.mla_heads, d.nope + d.rope)
    rope = uq[..., d.nope:]
    w_uq = jnp.concatenate([uq[..., :d.nope], rope, _swap_halves(rope)], axis=-1).reshape(d.q_lora, d.qw)
    ukv = full["w_ukv"][layer].reshape(d.kv_lora, d.mla_heads, d.nope + d.vdim)
    w_ukv = jnp.concatenate([ukv[..., :d.nope].reshape(d.kv_lora, -1), ukv[..., d.nope:].reshape(d.kv_lora, -1)], axis=1)
    return dict(w_ina=w_ina, w_inb=w_inb, w_in=jnp.concatenate([w_ina, w_inb], axis=1), w_uq=w_uq, w_ukv=w_ukv,
                w_sb_out=full["w_sb_out"][layer], w_mla_out=full["w_mla_out"][layer], w_o=full["w_o"][layer],
                w_up=full["w_up"][layer], w_down=full["w_down"][layer])


def _unprime_grads(g, d):
    gi = g["w_in"]
    b = gi[:, d.wa:]
    kr = b[:, d.kr_off:d.kr_off + d.rope] + _swap_halves(b[:, d.kr_off + d.rope:d.kr_off + 2 * d.rope])
    w_in = jnp.concatenate([gi[:, :d.wa], b[:, :d.kr_off], kr, b[:, d.g_off:]], axis=1)
    uq = g["w_uq"].reshape(d.q_lora, d.mla_heads, 2 * LANES)
    rope = uq[..., d.nope:d.nope + d.rope] + _swap_halves(uq[..., d.nope + d.rope:])
    w_uq = jnp.concatenate([uq[..., :d.nope], rope], axis=-1).reshape(d.q_lora, -1)
    hw = d.mla_heads * d.nope
    ukv = g["w_ukv"]
    w_ukv = jnp.concatenate([ukv[:, :hw].reshape(d.kv_lora, d.mla_heads, d.nope),
                             ukv[:, hw:].reshape(d.kv_lora, d.mla_heads, d.vdim)], axis=-1).reshape(d.kv_lora, -1)
    return dict(g, w_in=w_in, w_uq=w_uq, w_ukv=w_ukv)


def _layer_fwd(h, add, w, norm_mix, q_norm, kv_norm, norm_ffn, cw, cb, ctab, stab, d, tag):
    s = {}
    if add is None:
        s["h"] = h
        s["hn"], s["r1"] = _norm_fwd(h, norm_mix, width=d.d, cidx=0, name=f"norm_mix_{tag}")
    else:
        s["h"], s["hn"], s["r1"] = _norm_fwd(h, norm_mix, width=d.d, cidx=0, add=add, name=f"norm_mix_{tag}")
    s["pa"] = _mm(s["hn"], w["w_ina"], out_dtype=BF16, name=f"proj_a_{tag}")
    s["pb"] = _mm(s["hn"], w["w_inb"], name=f"proj_b_{tag}")
    s["o_sb"], s["carry"] = _sb_fwd(s["pa"], d, f"sb_fwd_{tag}")
    s["cqn"], s["rq"] = _norm_fwd(s["pb"], q_norm, width=d.q_lora, cidx=0, name=f"norm_q_{tag}")
    s["ckn"], s["rk"] = _norm_fwd(s["pb"], kv_norm, width=d.kv_lora, cidx=d.q_lora // d.kv_lora, name=f"norm_kv_{tag}")
    qraw = _mm(s["cqn"], w["w_uq"], name=f"uq_{tag}")
    s["kv"] = _mm(s["ckn"], w["w_ukv"], out_dtype=BF16, name=f"ukv_{tag}")
    s["qm"], s["kr"] = _mla_prep_fwd(qraw, s["pb"], ctab, stab, d, f"mla_prep_{tag}")
    s["o_mla"], s["lse"] = _mla_fwd(s["qm"], s["kv"], s["kr"], d, f"mla_fwd_{tag}")
    s["b_sb"] = _mm(s["o_sb"], w["w_sb_out"], name=f"sb_out_{tag}")
    s["b_mla"] = _mm(s["o_mla"], w["w_mla_out"], name=f"mla_out_{tag}")
    s["merged"] = _gate_fwd(s["pb"], s["b_sb"], s["b_mla"], d, f"gate_{tag}")
    mix = _mm(s["merged"], w["w_o"], name=f"w_o_{tag}")
    s["h1"], s["hn2"], s["r2"] = _norm_fwd(s["h"], norm_ffn, width=d.d, cidx=0, add=mix, name=f"norm_ffn_{tag}")
    s["up"] = _mm(s["hn2"], w["w_up"], name=f"w_up_{tag}")
    s["act"] = _convglu_fwd(s["up"], cw, cb, d, f"convglu_{tag}")
    ffn = _mm(s["act"], w["w_down"], name=f"w_down_{tag}")
    return s, ffn


def _layer_bwd(dh2, s, w, norm_mix, q_norm, kv_norm, norm_ffn, cw, cb, ctab, stab, d, tag):
    g = {}
    dact = _mm(dh2, w["w_down"], tb=True, name=f"d_act_{tag}")
    g["w_down"] = _mm(s["act"], dh2, ta=True, name=f"g_w_down_{tag}")
    dup_a, dup_g, sums_a, sums_g = _convglu_bwd(s["up"], dact, cw, cb, d, f"convglu_bwd_{tag}")
    dup = jnp.concatenate([dup_a, dup_g], axis=1)
    g["conv_w"] = jnp.concatenate([sums_a[0:3], sums_g[0:3]], axis=1)
    g["conv_b"] = jnp.concatenate([sums_a[3], sums_g[3]], axis=0)
    g["w_up"] = _mm(s["hn2"], dup, ta=True, name=f"g_w_up_{tag}")
    dhn2 = _mm(dup, w["w_up"], tb=True, name=f"d_hn2_{tag}")
    dh1, g["norm_ffn"] = _norm_bwd(dhn2, s["h1"], s["r2"], norm_ffn, width=d.d, cidx=0, dres=dh2, name=f"norm_ffn_bwd_{tag}")
    dmerged = _mm(dh1, w["w_o"], tb=True, name=f"d_merged_{tag}")
    g["w_o"] = _mm(s["merged"], dh1, ta=True, name=f"g_w_o_{tag}")
    db_sb, db_mla, dg_sb, dg_mla = _gate_bwd(dmerged, s["pb"], s["b_sb"], s["b_mla"], d, f"gate_bwd_{tag}")
    do_sb = _mm(db_sb, w["w_sb_out"], tb=True, name=f"d_o_sb_{tag}")
    g["w_sb_out"] = _mm(s["o_sb"], db_sb, ta=True, name=f"g_w_sb_out_{tag}")
    do_mla = _mm(db_mla, w["w_mla_out"], tb=True, name=f"d_o_mla_{tag}")
    g["w_mla_out"] = _mm(s["o_mla"], db_mla, ta=True, name=f"g_w_mla_out_{tag}")
    dq_sb, dk_sb, dv_sb = _sb_bwd(s["pa"], do_sb, s["carry"], d, f"sb_bwd_{tag}")
    dqm, dkn, dv, dkr = _mla_bwd(s["qm"], s["kv"], s["kr"], s["o_mla"], do_mla, s["lse"], d, f"mla_bwd_{tag}")
    dqraw, dkr128 = _mla_prep_bwd(dqm, dkr, ctab, stab, d, f"mla_prep_bwd_{tag}")
    dkv = jnp.concatenate([dkn.astype(BF16), dv.astype(BF16)], axis=1)
    dcqn = _mm(dqraw, w["w_uq"], tb=True, name=f"d_cq_{tag}")
    g["w_uq"] = _mm(s["cqn"], dqraw, ta=True, name=f"g_w_uq_{tag}")
    dckn = _mm(dkv, w["w_ukv"], tb=True, name=f"d_ckv_{tag}")
    g["w_ukv"] = _mm(s["ckn"], dkv, ta=True, name=f"g_w_ukv_{tag}")
    dcq, g["q_norm"] = _norm_bwd(dcqn, s["pb"], s["rq"], q_norm, width=d.q_lora, cidx=0, out_dtype=BF16, name=f"norm_q_bwd_{tag}")
    dckv, g["kv_norm"] = _norm_bwd(dckn, s["pb"], s["rk"], kv_norm, width=d.kv_lora, cidx=d.q_lora // d.kv_lora,
                                   out_dtype=BF16, name=f"norm_kv_bwd_{tag}")
    zpad = jnp.zeros((d.lp, d.g_off - d.kr_off - LANES), BF16)
    dproj = jnp.concatenate([dq_sb, dk_sb.astype(BF16), dv_sb.astype(BF16), dcq, dckv, dkr128, zpad, dg_sb, dg_mla], axis=1)
    g["w_in"] = _mm(s["hn"], dproj, ta=True, name=f"g_w_in_{tag}")
    dhn = _mm(dproj, w["w_in"], tb=True, name=f"d_hn_{tag}")
    dh, g["norm_mix"] = _norm_bwd(dhn, s["h"], s["r1"], norm_mix, width=d.d, cidx=0, dres=dh1, name=f"norm_mix_bwd_{tag}")
    return dh, g


def _step(d, x, p, m, v, loss_target):
    x = x.reshape(d.seq, d.d)
    target = loss_target.reshape(d.seq, d.d)

    blocks = [_pad_block(p[n], n).astype(BF16 if n in BIG else F32) for n in SHARDED]
    gathered = _gather_two_level(blocks, "gather_weights")
    full = {n: _full_from_slots(g_, n, p[n].shape[-1]) for n, g_ in zip(SHARDED, gathered)}

    pos = jnp.arange(d.lp, dtype=F32) - d.pad
    half = d.rope // 2
    freqs = ROPE_THETA ** (-jnp.arange(half, dtype=F32) / half)
    ang = pos[:, None] * freqs[None, :]
    cos, sin = jnp.cos(ang), jnp.sin(ang)
    zero = jnp.zeros((d.lp, LANES - d.rope), F32)
    ctab = jnp.concatenate([cos, cos, zero], axis=1)
    stab = jnp.concatenate([-sin, sin, zero], axis=1)

    h = jnp.concatenate([jnp.zeros((d.pad, d.d), F32), full["meta_tokens"], x], axis=0)
    ws = [_prime_weights(full, l, d) for l in range(d.depth)]
    saved, add = [], None
    for l in range(d.depth):
        s, add = _layer_fwd(h, add, ws[l], p["norm_mix"][l], p["q_norm"][l], p["kv_norm"][l], p["norm_ffn"][l],
                            full["conv_w"][l], p["conv_b"][l].reshape(1, -1), ctab, stab, d, f"l{l}")
        saved.append(s)
        h = s["h1"]
    h_out = _residual_add(h, add, "final_add")
    dh, loss_part, g_final = _head(h_out, target, p["final_norm"], d, "head")

    grads = [None] * d.depth
    for l in reversed(range(d.depth)):
        dh, g = _layer_bwd(dh, saved[l], ws[l], p["norm_mix"][l], p["q_norm"][l], p["kv_norm"][l], p["norm_ffn"][l],
                           full["conv_w"][l], p["conv_b"][l].reshape(1, -1), ctab, stab, d, f"l{l}")
        grads[l] = _unprime_grads(g, d)
    grad_x = dh[d.first_tok:].reshape(1, d.seq, d.d)

    gfull = {n: jnp.stack([grads[l][n] for l in range(d.depth)]) for n in BIG + ["conv_w"]}
    gfull["meta_tokens"] = dh[d.pad:d.first_tok]
    gsend = [_slots_from_full(gfull[n].astype(BF16) if n in BIG else gfull[n], n) for n in SHARDED]
    theirs = _pair_exchange(gsend, "scatter_grads_pair")
    my_c = lax.axis_index("c")
    mine = [lax.dynamic_index_in_dim(g_.reshape((4, 2) + g_.shape[1:]), my_c, axis=1, keepdims=False) for g_ in gsend]
    pairs = [_pair_sum(a_, b_, f"pair_sum_{n}") for n, a_, b_ in zip(SHARDED, mine, theirs)]
    grecv = _chip_exchange(pairs, "scatter_grads_chips")
    outs_sh = {n: _adamw(r_, _pad_block(p[n], n), _pad_block(m[n], n), _pad_block(v[n], n), f"adamw_{n}")
               for n, r_ in zip(SHARDED, grecv)}

    grep = {n: jnp.stack([grads[l][n].reshape(-1) for l in range(d.depth)]) for n in REPL if n != "final_norm"}
    grep["final_norm"] = g_final.reshape(-1)
    rflat = jnp.concatenate([grep[n].reshape(-1) for n in REPL] + [loss_part[0, 0:1]])
    (rparts,) = _exchange([_pack_rows(rflat, 8)], scatter=False, name="gather_small_grads")
    packr = lambda t: _pack_rows(jnp.concatenate([t[n].reshape(-1) for n in REPL] + [jnp.zeros((1,), F32)]), 8)
    outs_rp = _adamw(rparts, packr(p), packr(m), packr(v), "adamw_replicated")

    def unpack(flat, names, extra=0):
        res, off = {}, 0
        flat = flat.reshape(-1)
        for n in names:
            res[n] = flat[off:off + p[n].size].reshape(p[n].shape)
            off += p[n].size
        return res, flat[off:off + extra]

    results = []
    loss = None
    for k in range(4):
        sh = {n: outs_sh[n][k][..., :p[n].shape[-1]] for n in SHARDED}
        rp, tail = unpack(outs_rp[k], REPL, 1)
        if k == 0:
            loss = tail[0]
        results.append({**sh, **rp})
    return loss, grad_x, results


def _residual_add(h, add, name):
    rows, width = h.shape
    tr = _pick(rows, (256, 128))

    def body(h_ref, a_ref, o_ref):
        o_ref[...] = h_ref[...] + a_ref[...]

    blk = pl.BlockSpec((tr, width), lambda i: (i, 0))
    return pl.pallas_call(body, name=name, grid=(rows // tr,), in_specs=[blk, blk], out_specs=blk,
                          out_shape=jax.ShapeDtypeStruct((rows, width), F32),
                          compiler_params=_cparams(("parallel",)))(h, add)


WEIGHTS = ["meta_tokens", "norm_mix", "w_in", "q_norm", "w_uq", "kv_norm", "w_ukv", "w_sb_out", "w_mla_out", "w_o",
           "norm_ffn", "w_up", "conv_w", "conv_b", "w_down", "final_norm"]


def _run(d, x, weights, loss_target, moments_m, moments_v):
    p = dict(zip(WEIGHTS, weights))
    m = dict(zip(WEIGHTS, moments_m))
    v = dict(zip(WEIGHTS, moments_v))
    loss, grad_x, res = _step(d, x, p, m, v, loss_target)
    out = [loss, grad_x]
    for k in range(4):
        out += [res[k][n] for n in WEIGHTS]
    return tuple(out)


def kernel(x, meta_tokens, norm_mix, w_in, q_norm, w_uq, kv_norm, w_ukv, w_sb_out, w_mla_out, w_o, norm_ffn, w_up, conv_w, conv_b, w_down, final_norm, loss_target, m_meta_tokens, m_norm_mix, m_w_in, m_q_norm, m_w_uq, m_kv_norm, m_w_ukv, m_w_sb_out, m_w_mla_out, m_w_o, m_norm_ffn, m_w_up, m_conv_w, m_conv_b, m_w_down, m_final_norm, v_meta_tokens, v_norm_mix, v_w_in, v_q_norm, v_w_uq, v_kv_norm, v_w_ukv, v_w_sb_out, v_w_mla_out, v_w_o, v_norm_ffn, v_w_up, v_conv_w, v_conv_b, v_w_down, v_final_norm):
    weights = [meta_tokens, norm_mix, w_in, q_norm, w_uq, kv_norm, w_ukv, w_sb_out, w_mla_out, w_o, norm_ffn, w_up,
               conv_w, conv_b, w_down, final_norm]
    ms = [m_meta_tokens, m_norm_mix, m_w_in, m_q_norm, m_w_uq, m_kv_norm, m_w_ukv, m_w_sb_out, m_w_mla_out, m_w_o,
          m_norm_ffn, m_w_up, m_conv_w, m_conv_b, m_w_down, m_final_norm]
    vs = [v_meta_tokens, v_norm_mix, v_w_in, v_q_norm, v_w_uq, v_kv_norm, v_w_ukv, v_w_sb_out, v_w_mla_out, v_w_o,
          v_norm_ffn, v_w_up, v_conv_w, v_conv_b, v_w_down, v_final_norm]
    return _run(PROD, x, weights, loss_target, ms, vs)
```

```python
import jax
import jax.numpy as jnp
from jax import lax
from jax.experimental import pallas as pl
from jax.experimental.pallas import tpu as pltpu

F32 = jnp.float32
BF16 = jnp.bfloat16

EPS = 1e-6
ROPE_THETA = 10000.0
ADAM_LR = 0.001
ADAM_B1 = 0.9
ADAM_B2 = 0.999
ADAM_EPS = 1e-08
ADAM_WD = 0.01
ADAM_STEP = 10
NEG = -1e30
DEAD = -110.0
LANES = 128
PACK_W = 1024
ADAMW_TILE_ELEMS = 256 * 1024
V7X_VMEM_LIMIT = 48 * 1024 * 1024
V7X_VMEM_LIMIT_BIG = 58 * 1024 * 1024
MESH_AXES = ("x", "y", "c")
N_DEV = 8
FLIPS = [(0, 0, 1), (0, 1, 0), (0, 1, 1), (1, 0, 0), (1, 0, 1), (1, 1, 0), (1, 1, 1)]


class _Dims:
    def __init__(self, d_model=2048, seq=8192, depth=2, n_meta=16, block=128, sb_heads=8, hd=128,
                 mla_heads=8, q_lora=512, kv_lora=256, nope=128, rope=64, vdim=128, d_ff=5632, tq=None):
        self.d, self.seq, self.depth, self.n_meta, self.block = d_model, seq, depth, n_meta, block
        self.sb_heads, self.hd, self.mla_heads = sb_heads, hd, mla_heads
        self.q_lora, self.kv_lora, self.nope, self.rope, self.vdim, self.f = q_lora, kv_lora, nope, rope, vdim, d_ff
        assert hd == LANES and nope == LANES and vdim == LANES and 2 * rope == LANES
        self.pad = block - n_meta
        self.lp = self.pad + n_meta + seq
        self.first_tok = self.pad + n_meta
        assert self.first_tok == block and self.lp % block == 0 and self.lp // block < LANES
        self.tq = tq or next(t for t in (640, 512, 256, 128) if self.lp % t == 0)
        assert self.tq % block == 0 and self.lp % self.tq == 0
        self.sbw = sb_heads * hd
        self.mlaw = mla_heads * vdim
        self.wa = 3 * self.sbw
        self.d_in = 3 * self.sbw + q_lora + kv_lora + rope + 2 * d_model
        self.tg = min(1024, d_model)
        self.kr_off = q_lora + kv_lora
        raw = self.kr_off + LANES
        self.g_off = -(-raw // self.tg) * self.tg
        self.wb = self.g_off + 2 * d_model
        self.qw = mla_heads * 2 * LANES


PROD = _Dims()


def _pick(n, prefs):
    for p in prefs:
        if n % p == 0:
            return p
    return n


def _cparams(sem, limit=V7X_VMEM_LIMIT):
    return pltpu.CompilerParams(dimension_semantics=sem, vmem_limit_bytes=limit)


def _mm(a, b, *, ta=False, tb=False, out_dtype=F32, name):
    if ta:
        kdim, m = a.shape
    else:
        m, kdim = a.shape
    if tb:
        n, k2 = b.shape
    else:
        k2, n = b.shape
    assert kdim == k2, (a.shape, b.shape, ta, tb)
    tm = _pick(m, (640, 512, 256, 128))
    tn = _pick(n, (1024, 512, 384, 256, 128))
    tk = _pick(kdim, (2816, 2048, 1664, 1408, 1024, 640, 512, 256, 128))
    nk = kdim // tk
    dn = (((0 if ta else 1,), (1 if tb else 0,)), ((), ()))

    def dot(a_ref, b_ref):
        return lax.dot_general(a_ref[...].astype(BF16), b_ref[...].astype(BF16), dn, preferred_element_type=F32)

    def body_one(a_ref, b_ref, o_ref):
        o_ref[...] = dot(a_ref, b_ref).astype(out_dtype)

    def body_acc(a_ref, b_ref, o_ref, acc_ref):
        k = pl.program_id(2)

        @pl.when(k == 0)
        def _():
            acc_ref[...] = dot(a_ref, b_ref)

        @pl.when((k > 0) & (k < nk - 1))
        def _():
            acc_ref[...] += dot(a_ref, b_ref)

        @pl.when(k == nk - 1)
        def _():
            o_ref[...] = (acc_ref[...] + dot(a_ref, b_ref)).astype(out_dtype)

    a_spec = pl.BlockSpec((tk, tm), lambda i, j, k: (k, i)) if ta else pl.BlockSpec((tm, tk), lambda i, j, k: (i, k))
    b_spec = pl.BlockSpec((tn, tk), lambda i, j, k: (j, k)) if tb else pl.BlockSpec((tk, tn), lambda i, j, k: (k, j))
    return pl.pallas_call(
        body_one if nk == 1 else body_acc, name=name, grid=(m // tm, n // tn, nk), in_specs=[a_spec, b_spec],
        out_specs=pl.BlockSpec((tm, tn), lambda i, j, k: (i, j)),
        out_shape=jax.ShapeDtypeStruct((m, n), out_dtype),
        scratch_shapes=[] if nk == 1 else [pltpu.VMEM((tm, tn), F32)],
        compiler_params=_cparams(("parallel", "parallel", "arbitrary")),
    )(a, b)


def _norm_fwd(x, g, *, width, cidx, add=None, name):
    rows = x.shape[0]
    tr = _pick(rows, (256, 128))
    has_add = add is not None

    def body(*refs):
        if has_add:
            x_ref, a_ref, g_ref, xn_ref, y_ref, r_ref = refs
            xv = x_ref[...] + a_ref[...]
            xn_ref[...] = xv
        else:
            x_ref, g_ref, y_ref, r_ref = refs
            xv = x_ref[...]
        r = lax.rsqrt(jnp.mean(xv * xv, axis=1, keepdims=True) + EPS)
        y_ref[...] = (xv * r * g_ref[...]).astype(BF16)
        r_ref[...] = r

    blk = pl.BlockSpec((tr, width), lambda i: (i, 0))
    in_specs = [pl.BlockSpec((tr, width), lambda i: (i, cidx))]
    args = [x]
    if has_add:
        in_specs.append(blk)
        args.append(add)
    in_specs.append(pl.BlockSpec((1, width), lambda i: (0, 0)))
    args.append(g.reshape(1, width))
    out_specs = [blk, pl.BlockSpec((tr, 1), lambda i: (i, 0))]
    out_shape = [jax.ShapeDtypeStruct((rows, width), BF16), jax.ShapeDtypeStruct((rows, 1), F32)]
    if has_add:
        out_specs.insert(0, blk)
        out_shape.insert(0, jax.ShapeDtypeStruct((rows, width), F32))
    return pl.pallas_call(body, name=name, grid=(rows // tr,), in_specs=in_specs, out_specs=out_specs,
                          out_shape=out_shape, compiler_params=_cparams(("parallel",)))(*args)


def _norm_bwd(dy, x, r, g, *, width, cidx, dres=None, out_dtype=F32, name):
    rows = x.shape[0]
    tr = _pick(rows, (256, 128))
    has_res = dres is not None

    def body(*refs):
        if has_res:
            dy_ref, x_ref, r_ref, g_ref, dr_ref, dx_ref, dg_ref = refs
        else:
            dy_ref, x_ref, r_ref, g_ref, dx_ref, dg_ref = refs
        i = pl.program_id(0)

        @pl.when(i == 0)
        def _():
            dg_ref[...] = jnp.zeros_like(dg_ref)

        dyv, xv, rv = dy_ref[...], x_ref[...], r_ref[...]
        gy = dyv * g_ref[...]
        c = jnp.mean(gy * xv, axis=1, keepdims=True)
        dx = rv * gy - xv * (rv * rv * rv) * c
        if has_res:
            dx = dx + dr_ref[...]
        dx_ref[...] = dx.astype(out_dtype)
        dg_ref[...] += jnp.sum(dyv * xv * rv, axis=0, keepdims=True)

    blk = pl.BlockSpec((tr, width), lambda i: (i, 0))
    in_specs = [blk, pl.BlockSpec((tr, width), lambda i: (i, cidx)), pl.BlockSpec((tr, 1), lambda i: (i, 0)),
                pl.BlockSpec((1, width), lambda i: (0, 0))]
    args = [dy, x, r, g.reshape(1, width)]
    if has_res:
        in_specs.append(blk)
        args.append(dres)
    return pl.pallas_call(
        body, name=name, grid=(rows // tr,), in_specs=in_specs,
        out_specs=[blk, pl.BlockSpec((1, width), lambda i: (0, 0))],
        out_shape=[jax.ShapeDtypeStruct((rows, width), out_dtype), jax.ShapeDtypeStruct((1, width), F32)],
        compiler_params=_cparams(("arbitrary",)))(*args)


def _split3(x):
    h1 = x.astype(BF16)
    r1 = x - h1.astype(F32)
    h2 = r1.astype(BF16)
    h3 = (r1 - h2.astype(F32)).astype(BF16)
    return h1, h2, h3


def _cum(x, tri):
    h1, h2, h3 = _split3(x)
    dot = lambda h: jnp.dot(h, tri, preferred_element_type=F32)
    return dot(h1) + dot(h2) + dot(h3)


def _dot_nt(a, b):
    return lax.dot_general(a, b, (((1,), (1,)), ((), ())), preferred_element_type=F32)


def _dot_tn(a, b):
    return lax.dot_general(a, b, (((0,), (0,)), ((), ())), preferred_element_type=F32)


def _sb_geometry(tq, t):
    assert t & (t - 1) == 0
    ri = lax.broadcasted_iota(jnp.int32, (tq, t), 0)
    return jnp.bitwise_and(ri, t - 1), jnp.right_shift(ri, t.bit_length() - 1), lax.broadcasted_iota(jnp.int32, (tq, t), 1)


def _sb_key_blocks(ref, base, r, t, kind):
    if kind == "low":
        return [ref[pl.ds(pl.multiple_of(jnp.maximum(base + g, 0) * t, t), t), :] for g in range(r)]
    slab = ref[pl.ds(pl.multiple_of(base * t, t), r * t), :]
    return [slab[g * t:(g + 1) * t, :] for g in range(r)]


def _sb_mask(geo, base, s, t, pad, kind):
    rowl, grp, col = geo
    if kind == "plain":
        return None
    if kind == "first":
        return col < rowl
    blk = base + grp
    causal = col < rowl + jnp.where(s > 0, t, 0)
    return (blk >= 0) & (blk * t + col >= pad) & causal


def _sb_fwd(qkv, d, name):
    nh, hd, lp, tq, t = d.sb_heads, d.hd, d.lp, d.tq, d.block
    r = tq // t
    scale = hd ** -0.5
    pad = d.pad

    def body(q_ref, k_ref, v_ref, o_ref, c_ref):
        i = pl.program_id(1)
        qs = [q_ref[g * t:(g + 1) * t, :] for g in range(r)]
        geo = _sb_geometry(tq, t)
        tri = (lax.broadcasted_iota(jnp.int32, (t, t), 0)
               > lax.broadcasted_iota(jnp.int32, (t, t), 1)).astype(BF16)
        lane = lax.broadcasted_iota(jnp.int32, (tq, LANES), 1)

        c_ref[...] = jnp.zeros_like(c_ref)

        def make_step(kind):
            def step(s, carry):
                acc, run = carry
                ks = _sb_key_blocks(k_ref, i * r - s, r, t, kind)
                vs = _sb_key_blocks(v_ref, i * r - s, r, t, kind)
                z = jnp.concatenate([_dot_nt(qs[g], ks[g]) for g in range(r)], axis=0) * scale
                e = jnp.exp(-jnp.abs(z))
                sp = jnp.maximum(z, 0.0) + jnp.log(1.0 + e)
                mask = _sb_mask(geo, i * r - s, s, t, pad, kind)
                spm = sp if mask is None else jnp.where(mask, sp, 0.0)
                w = jnp.exp(z - sp - _cum(spm, tri) + run)
                if mask is not None:
                    w = jnp.where(mask, w, 0.0)
                wb = w.astype(BF16)
                acc = acc + jnp.concatenate(
                    [jnp.dot(wb[g * t:(g + 1) * t, :], vs[g], preferred_element_type=F32) for g in range(r)], axis=0)
                c_ref[...] = jnp.where(lane == s, run, c_ref[...])
                run = run - jnp.sum(spm, axis=1, keepdims=True)
                return acc, run
            return step

        first, plain, low = make_step("first"), make_step("plain"), make_step("low")

        def alive(run):
            return (jnp.max(run) >= DEAD).astype(jnp.int32)

        def run_while(step, s, end, live, acc, run):
            def wbody(st):
                s, _, acc, run = st
                acc, run = step(s, (acc, run))
                return s + 1, alive(run), acc, run
            return lax.while_loop(lambda st: (st[0] < end) & (st[1] > 0), wbody, (s, live, acc, run))

        init = (jnp.zeros((tq, hd), F32), jnp.zeros((tq, 1), F32))
        carry = lax.fori_loop(0, jnp.minimum(i, 1), lambda _, cr: first(0, cr), init)
        acc, run = lax.fori_loop(0, 1 - jnp.minimum(i, 1), lambda _, cr: low(0, cr), carry)
        s, live, acc, run = run_while(plain, 1, i * r, alive(run), acc, run)
        end_low = jnp.where(s >= jnp.maximum(i * r, 1), (i + 1) * r, s)
        s, live, acc, run = run_while(low, s, end_low, live, acc, run)
        o_ref[...] = acc
        c_ref[...] = jnp.where(lane == LANES - 1, s.astype(F32), c_ref[...])

    return pl.pallas_call(
        body, name=name, grid=(nh, lp // tq),
        in_specs=[pl.BlockSpec((tq, hd), lambda h, i: (i, h)),
                  pl.BlockSpec((lp, hd), lambda h, i: (0, nh + h)),
                  pl.BlockSpec((lp, hd), lambda h, i: (0, 2 * nh + h))],
        out_specs=[pl.BlockSpec((tq, hd), lambda h, i: (i, h)),
                   pl.BlockSpec((None, tq, LANES), lambda h, i: (h, i, 0))],
        out_shape=[jax.ShapeDtypeStruct((lp, nh * hd), F32), jax.ShapeDtypeStruct((nh, lp, LANES), F32)],
        compiler_params=_cparams(("parallel", "arbitrary")),
    )(qkv, qkv, qkv)


def _sb_bwd(qkv, do, carry, d, name):
    nh, hd, lp, tq, t = d.sb_heads, d.hd, d.lp, d.tq, d.block
    r = tq // t
    scale = hd ** -0.5
    pad = d.pad

    def body(q_ref, k_ref, v_ref, do_ref, c_ref, dq_ref, dk_ref, dv_ref):
        i = pl.program_id(1)

        @pl.when(i == 0)
        def _():
            dk_ref[...] = jnp.zeros_like(dk_ref)
            dv_ref[...] = jnp.zeros_like(dv_ref)

        rows = lambda x, g: x[g * t:(g + 1) * t, :]
        qs = [q_ref[g * t:(g + 1) * t, :] for g in range(r)]
        dobs = [do_ref[g * t:(g + 1) * t, :].astype(BF16) for g in range(r)]
        geo = _sb_geometry(tq, t)
        ri = lax.broadcasted_iota(jnp.int32, (t, t), 0)
        ci = lax.broadcasted_iota(jnp.int32, (t, t), 1)
        tri_suf = (ri > ci).astype(BF16)
        tri_pre = (ri < ci).astype(BF16)
        lane = lax.broadcasted_iota(jnp.int32, (tq, LANES), 1)

        def make_step(kind):
            def step(s, carry):
                dq, pc = carry
                base = i * r - s
                ks = _sb_key_blocks(k_ref, base, r, t, kind)
                vs = _sb_key_blocks(v_ref, base, r, t, kind)
                z = jnp.concatenate([_dot_nt(qs[g], ks[g]) for g in range(r)], axis=0) * scale
                e = jnp.exp(-jnp.abs(z))
                sp = jnp.maximum(z, 0.0) + jnp.log(1.0 + e)
                mask = _sb_mask(geo, base, s, t, pad, kind)
                spm = sp if mask is None else jnp.where(mask, sp, 0.0)
                run = jnp.sum(jnp.where(lane == s, c_ref[...], 0.0), axis=1, keepdims=True)
                w = jnp.exp(z - sp - _cum(spm, tri_suf) + run)
                if mask is not None:
                    w = jnp.where(mask, w, 0.0)
                gw = w * jnp.concatenate([_dot_nt(dobs[g], vs[g]) for g in range(r)], axis=0)
                p = _cum(gw, tri_pre) + pc
                inv = 1.0 / (1.0 + e)
                sig = jnp.where(z >= 0.0, inv, e * inv)
                dz = (gw * (1.0 - sig) - sig * p) * scale
                if mask is not None:
                    dz = jnp.where(mask, dz, 0.0)
                dzb, wb = dz.astype(BF16), w.astype(BF16)
                dq = dq + jnp.concatenate(
                    [jnp.dot(rows(dzb, g), ks[g], preferred_element_type=F32) for g in range(r)], axis=0)
                dks = [_dot_tn(rows(dzb, g), qs[g]) for g in range(r)]
                dvs = [_dot_tn(rows(wb, g), dobs[g]) for g in range(r)]
                if kind == "low":
                    for g in range(r):
                        at = pl.ds(pl.multiple_of(jnp.maximum(base + g, 0) * t, t), t)
                        dk_ref[at, :] += dks[g]
                        dv_ref[at, :] += dvs[g]
                else:
                    at = pl.ds(pl.multiple_of(base * t, t), tq)
                    dk_ref[at, :] += jnp.concatenate(dks, axis=0)
                    dv_ref[at, :] += jnp.concatenate(dvs, axis=0)
                pc = pc + jnp.sum(gw, axis=1, keepdims=True)
                return dq, pc
            return step

        first, plain, low = make_step("first"), make_step("plain"), make_step("low")
        nsteps = jnp.max(jnp.where(lane == LANES - 1, c_ref[...], 0.0)).astype(jnp.int32)
        low_from = jnp.maximum(i * r, 1)
        plain_end = jnp.minimum(nsteps, low_from)
        carry = (jnp.zeros((tq, hd), F32), jnp.zeros((tq, 1), F32))
        carry = lax.fori_loop(0, jnp.maximum(nsteps - low_from, 0), lambda jj, cr: low(nsteps - 1 - jj, cr), carry)
        carry = lax.fori_loop(0, plain_end - 1, lambda jj, cr: plain(plain_end - 1 - jj, cr), carry)
        carry = lax.fori_loop(0, jnp.minimum(i, 1), lambda _, cr: first(0, cr), carry)
        dq, _ = lax.fori_loop(0, 1 - jnp.minimum(i, 1), lambda _, cr: low(0, cr), carry)
        dq_ref[...] = dq.astype(BF16)

    w3 = nh * hd
    return pl.pallas_call(
        body, name=name, grid=(nh, lp // tq),
        in_specs=[pl.BlockSpec((tq, hd), lambda h, i: (i, h)),
                  pl.BlockSpec((lp, hd), lambda h, i: (0, nh + h)),
                  pl.BlockSpec((lp, hd), lambda h, i: (0, 2 * nh + h)),
                  pl.BlockSpec((tq, hd), lambda h, i: (i, h)),
                  pl.BlockSpec((None, tq, LANES), lambda h, i: (h, i, 0))],
        out_specs=[pl.BlockSpec((tq, hd), lambda h, i: (i, h)),
                   pl.BlockSpec((lp, hd), lambda h, i: (0, h)),
                   pl.BlockSpec((lp, hd), lambda h, i: (0, h))],
        out_shape=[jax.ShapeDtypeStruct((lp, w3), BF16), jax.ShapeDtypeStruct((lp, w3), F32),
                   jax.ShapeDtypeStruct((lp, w3), F32)],
        compiler_params=_cparams(("arbitrary", "arbitrary")),
    )(qkv, qkv, qkv, do, carry)


def _mla_prep_fwd(qraw, projb, ctab, stab, d, name):
    lp, nh = d.lp, d.mla_heads
    tr = _pick(lp, (256, 128))
    kidx = d.kr_off // LANES

    def rope(u, c, s):
        return u * c + pltpu.roll(u, LANES // 2, 1) * s

    def body(q_ref, k_ref, c_ref, s_ref, qm_ref, kr_ref):
        c, s = c_ref[...], s_ref[...]
        for h in range(nh):
            base = 2 * LANES * h
            qm_ref[:, base:base + LANES] = q_ref[:, base:base + LANES].astype(BF16)
            qm_ref[:, base + LANES:base + 2 * LANES] = rope(q_ref[:, base + LANES:base + 2 * LANES], c, s).astype(BF16)
        kr_ref[...] = rope(k_ref[...], c, s).astype(BF16)

    tab = pl.BlockSpec((tr, LANES), lambda i: (i, 0))
    return pl.pallas_call(
        body, name=name, grid=(lp // tr,),
        in_specs=[pl.BlockSpec((tr, d.qw), lambda i: (i, 0)), pl.BlockSpec((tr, LANES), lambda i: (i, kidx)), tab, tab],
        out_specs=[pl.BlockSpec((tr, d.qw), lambda i: (i, 0)), tab],
        out_shape=[jax.ShapeDtypeStruct((lp, d.qw), BF16), jax.ShapeDtypeStruct((lp, LANES), BF16)],
        compiler_params=_cparams(("parallel",)))(qraw, projb, ctab, stab)


def _mla_prep_bwd(dqm, dkr, ctab, stab, d, name):
    lp, nh = d.lp, d.mla_heads
    tr = _pick(lp, (256, 128))

    def unrope(g, c, s):
        return g * c + pltpu.roll(g * s, LANES // 2, 1)

    def body(dq_ref, dk_ref, c_ref, s_ref, o_ref, ok_ref):
        c, s = c_ref[...], s_ref[...]
        for h in range(nh):
            base = 2 * LANES * h
            o_ref[:, base:base + LANES] = dq_ref[:, base:base + LANES].astype(BF16)
            o_ref[:, base + LANES:base + 2 * LANES] = unrope(dq_ref[:, base + LANES:base + 2 * LANES], c, s).astype(BF16)
        ok_ref[...] = unrope(dk_ref[...], c, s).astype(BF16)

    tab = pl.BlockSpec((tr, LANES), lambda i: (i, 0))
    wide = pl.BlockSpec((tr, d.qw), lambda i: (i, 0))
    return pl.pallas_call(
        body, name=name, grid=(lp // tr,), in_specs=[wide, tab, tab, tab], out_specs=[wide, tab],
        out_shape=[jax.ShapeDtypeStruct((lp, d.qw), BF16), jax.ShapeDtypeStruct((lp, LANES), BF16)],
        compiler_params=_cparams(("parallel",)))(dqm, dkr, ctab, stab)


def _mla_fwd(qm, kv, kr, d, name):
    nh, lp, t = d.mla_heads, d.lp, d.tq
    scale = (d.nope + d.rope) ** -0.5
    pad = d.pad

    def body(q_ref, kn_ref, v_ref, kr_ref, o_ref, lse_ref):
        i = pl.program_id(1)
        q = q_ref[...]
        row = i * t + lax.broadcasted_iota(jnp.int32, (t, t), 0)
        colb = lax.broadcasted_iota(jnp.int32, (t, t), 1)

        def make_step(masked):
            def step(j, carry):
                acc, m, l = carry
                off = pl.multiple_of(j * t, t)
                kc = jnp.concatenate([kn_ref[pl.ds(off, t), :], kr_ref[pl.ds(off, t), :]], axis=1)
                s = _dot_nt(q, kc) * scale
                if masked:
                    col = j * t + colb
                    s = jnp.where((col <= row) & (col >= pad), s, NEG)
                m_new = jnp.maximum(m, jnp.max(s, axis=1, keepdims=True))
                alpha = jnp.exp(m - m_new)
                p = jnp.exp(s - m_new)
                l = alpha * l + jnp.sum(p, axis=1, keepdims=True)
                acc = alpha * acc + jnp.dot(p.astype(BF16), v_ref[pl.ds(off, t), :], preferred_element_type=F32)
                return acc, m_new, l
            return step

        masked, plain = make_step(True), make_step(False)
        carry = masked(0, (jnp.zeros((t, LANES), F32), jnp.full((t, 1), NEG, F32), jnp.zeros((t, 1), F32)))
        carry = lax.fori_loop(1, i, plain, carry)
        acc, m, l = lax.fori_loop(i, i + jnp.minimum(i, 1), masked, carry)
        rowv = i * t + lax.broadcasted_iota(jnp.int32, (t, LANES), 0)
        o_ref[...] = jnp.where(rowv >= pad, acc / l, 0.0)
        lse_ref[...] = m + jnp.log(l)

    return pl.pallas_call(
        body, name=name, grid=(nh, lp // t),
        in_specs=[pl.BlockSpec((t, 2 * LANES), lambda h, i: (i, h)),
                  pl.BlockSpec((lp, LANES), lambda h, i: (0, h)),
                  pl.BlockSpec((lp, LANES), lambda h, i: (0, nh + h)),
                  pl.BlockSpec((lp, LANES), lambda h, i: (0, 0))],
        out_specs=[pl.BlockSpec((t, LANES), lambda h, i: (i, h)),
                   pl.BlockSpec((None, t, 1), lambda h, i: (h, i, 0))],
        out_shape=[jax.ShapeDtypeStruct((lp, nh * LANES), F32), jax.ShapeDtypeStruct((nh, lp, 1), F32)],
        compiler_params=_cparams(("parallel", "arbitrary")),
    )(qm, kv, kv, kr)


def _mla_bwd(qm, kv, kr, o, do, lse, d, name):
    nh, lp, t = d.mla_heads, d.lp, d.tq
    scale = (d.nope + d.rope) ** -0.5
    pad = d.pad

    def body(q_ref, kn_ref, v_ref, kr_ref, o_ref, do_ref, lse_ref, dq_ref, dkn_ref, dv_ref, dkr_ref):
        h = pl.program_id(0)
        i = pl.program_id(1)

        @pl.when(i == 0)
        def _():
            dkn_ref[...] = jnp.zeros_like(dkn_ref)
            dv_ref[...] = jnp.zeros_like(dv_ref)

        @pl.when((i == 0) & (h == 0))
        def _():
            dkr_ref[...] = jnp.zeros_like(dkr_ref)

        q = q_ref[...]
        dof = do_ref[...]
        dob = dof.astype(BF16)
        delta = jnp.sum(dof * o_ref[...], axis=1, keepdims=True)
        lse = lse_ref[...]
        row = i * t + lax.broadcasted_iota(jnp.int32, (t, t), 0)
        colb = lax.broadcasted_iota(jnp.int32, (t, t), 1)

        def make_step(masked):
            def step(j, dq):
                off = pl.multiple_of(j * t, t)
                kc = jnp.concatenate([kn_ref[pl.ds(off, t), :], kr_ref[pl.ds(off, t), :]], axis=1)
                v = v_ref[pl.ds(off, t), :]
                s = _dot_nt(q, kc) * scale
                if masked:
                    col = j * t + colb
                    mask = (col <= row) & (col >= pad)
                    p = jnp.where(mask, jnp.exp(jnp.where(mask, s, NEG) - lse), 0.0)
                else:
                    p = jnp.exp(s - lse)
                dp = _dot_nt(dob, v)
                dsb = (p * (dp - delta) * scale).astype(BF16)
                dq = dq + jnp.dot(dsb, kc, preferred_element_type=F32)
                dkc = _dot_tn(dsb, q)
                dkn_ref[pl.ds(off, t), :] += dkc[:, :LANES]
                dkr_ref[pl.ds(off, t), :] += dkc[:, LANES:]
                dv_ref[pl.ds(off, t), :] += _dot_tn(p.astype(BF16), dob)
                return dq
            return step

        masked, plain = make_step(True), make_step(False)
        dq = masked(0, jnp.zeros((t, 2 * LANES), F32))
        dq = lax.fori_loop(1, i, plain, dq)
        dq_ref[...] = lax.fori_loop(i, i + jnp.minimum(i, 1), masked, dq)

    return pl.pallas_call(
        body, name=name, grid=(nh, lp // t),
        in_specs=[pl.BlockSpec((t, 2 * LANES), lambda h, i: (i, h)),
                  pl.BlockSpec((lp, LANES), lambda h, i: (0, h), pipeline_mode=pl.Buffered(1)),
                  pl.BlockSpec((lp, LANES), lambda h, i: (0, nh + h), pipeline_mode=pl.Buffered(1)),
                  pl.BlockSpec((lp, LANES), lambda h, i: (0, 0), pipeline_mode=pl.Buffered(1)),
                  pl.BlockSpec((t, LANES), lambda h, i: (i, h)),
                  pl.BlockSpec((t, LANES), lambda h, i: (i, h)),
                  pl.BlockSpec((None, t, 1), lambda h, i: (h, i, 0))],
        out_specs=[pl.BlockSpec((t, 2 * LANES), lambda h, i: (i, h)),
                   pl.BlockSpec((lp, LANES), lambda h, i: (0, h)),
                   pl.BlockSpec((lp, LANES), lambda h, i: (0, h)),
                   pl.BlockSpec((lp, LANES), lambda h, i: (0, 0))],
        out_shape=[jax.ShapeDtypeStruct((lp, nh * 2 * LANES), F32), jax.ShapeDtypeStruct((lp, nh * LANES), F32),
                   jax.ShapeDtypeStruct((lp, nh * LANES), F32), jax.ShapeDtypeStruct((lp, LANES), F32)],
        compiler_params=_cparams(("arbitrary", "arbitrary"), V7X_VMEM_LIMIT_BIG),
    )(qm, kv, kv, kr, o, do, lse)


def _sigmoid(x):
    return 1.0 / (1.0 + jnp.exp(-x))


def _gate_fwd(projb, b_sb, b_mla, d, name):
    lp, tg = d.lp, d.tg
    tr = _pick(lp, (256, 128))
    o1, o2 = d.g_off // tg, (d.g_off + d.d) // tg

    def body(g1_ref, g2_ref, b1_ref, b2_ref, o_ref):
        o_ref[...] = (_sigmoid(g1_ref[...]) * b1_ref[...] + _sigmoid(g2_ref[...]) * b2_ref[...]).astype(BF16)

    blk = pl.BlockSpec((tr, tg), lambda i, j: (i, j))
    return pl.pallas_call(
        body, name=name, grid=(lp // tr, d.d // tg),
        in_specs=[pl.BlockSpec((tr, tg), lambda i, j: (i, o1 + j)), pl.BlockSpec((tr, tg), lambda i, j: (i, o2 + j)),
                  blk, blk],
        out_specs=blk, out_shape=jax.ShapeDtypeStruct((lp, d.d), BF16),
        compiler_params=_cparams(("parallel", "parallel")))(projb, projb, b_sb, b_mla)


def _gate_bwd(dm, projb, b_sb, b_mla, d, name):
    lp, tg = d.lp, d.tg
    tr = _pick(lp, (256, 128))
    o1, o2 = d.g_off // tg, (d.g_off + d.d) // tg

    def body(dm_ref, g1_ref, g2_ref, b1_ref, b2_ref, db1_ref, db2_ref, dg1_ref, dg2_ref):
        dmv = dm_ref[...]
        s1, s2 = _sigmoid(g1_ref[...]), _sigmoid(g2_ref[...])
        db1_ref[...] = (dmv * s1).astype(BF16)
        db2_ref[...] = (dmv * s2).astype(BF16)
        dg1_ref[...] = (dmv * b1_ref[...] * s1 * (1.0 - s1)).astype(BF16)
        dg2_ref[...] = (dmv * b2_ref[...] * s2 * (1.0 - s2)).astype(BF16)

    blk = pl.BlockSpec((tr, tg), lambda i, j: (i, j))
    out = jax.ShapeDtypeStruct((lp, d.d), BF16)
    return pl.pallas_call(
        body, name=name, grid=(lp // tr, d.d // tg),
        in_specs=[blk, pl.BlockSpec((tr, tg), lambda i, j: (i, o1 + j)), pl.BlockSpec((tr, tg), lambda i, j: (i, o2 + j)),
                  blk, blk],
        out_specs=[blk] * 4, out_shape=[out] * 4,
        compiler_params=_cparams(("parallel", "parallel")))(dm, projb, projb, b_sb, b_mla)


HALO = 8


def _conv_tiles(d):
    return _pick(d.lp, (640, 512, 256, 128)), _pick(d.f, (512, 256, 128))


def _convglu_fwd(up, cw, cb, d, name):
    lp, f = d.lp, d.f
    tr, tc = _conv_tiles(d)
    nf = f // tc
    hb = tr // HALO
    pad = d.pad

    def body(a_ref, g_ref, pa_ref, pg_ref, wa_ref, wg_ref, ba_ref, bg_ref, o_ref, xa, xg):
        i = pl.program_id(1)
        keep = (i > 0).astype(F32)
        xa[0:HALO, :] = pa_ref[...] * keep
        xg[0:HALO, :] = pg_ref[...] * keep
        xa[HALO:, :] = a_ref[...]
        xg[HALO:, :] = g_ref[...]

        def conv(x, w_ref, b_ref):
            return (b_ref[...] + x[pl.ds(HALO - 2, tr), :] * w_ref[0:1, :] + x[pl.ds(HALO - 1, tr), :] * w_ref[1:2, :]
                    + x[pl.ds(HALO, tr), :] * w_ref[2:3, :])

        ua = conv(xa, wa_ref, ba_ref)
        ug = conv(xg, wg_ref, bg_ref)
        row = i * tr + lax.broadcasted_iota(jnp.int32, (tr, tc), 0)
        o_ref[...] = jnp.where(row >= pad, ua * _sigmoid(ua) * ug, 0.0).astype(BF16)

    prev = lambda j, i: (jnp.maximum(i * hb - 1, 0), j)
    prevg = lambda j, i: (jnp.maximum(i * hb - 1, 0), nf + j)
    return pl.pallas_call(
        body, name=name, grid=(nf, lp // tr),
        in_specs=[pl.BlockSpec((tr, tc), lambda j, i: (i, j)), pl.BlockSpec((tr, tc), lambda j, i: (i, nf + j)),
                  pl.BlockSpec((HALO, tc), prev), pl.BlockSpec((HALO, tc), prevg),
                  pl.BlockSpec((3, tc), lambda j, i: (0, j)), pl.BlockSpec((3, tc), lambda j, i: (0, nf + j)),
                  pl.BlockSpec((1, tc), lambda j, i: (0, j)), pl.BlockSpec((1, tc), lambda j, i: (0, nf + j))],
        out_specs=pl.BlockSpec((tr, tc), lambda j, i: (i, j)),
        out_shape=jax.ShapeDtypeStruct((lp, f), BF16),
        scratch_shapes=[pltpu.VMEM((tr + HALO, tc), F32), pltpu.VMEM((tr + HALO, tc), F32)],
        compiler_params=_cparams(("parallel", "arbitrary")))(up, up, up, up, cw, cw, cb, cb)


def _convglu_bwd(up, dact, cw, cb, d, name):
    lp, f = d.lp, d.f
    tr, tc = _conv_tiles(d)
    nf = f // tc
    hb = tr // HALO
    nrow = lp // tr
    pad = d.pad
    te = tr + HALO

    def body(a_ref, g_ref, pa_ref, pg_ref, na_ref, ng_ref, da_ref, nd_ref, wa_ref, wg_ref, ba_ref, bg_ref,
             oa_ref, og_ref, sa_ref, sg_ref, xa, xg, xd, ya, yg):
        i = pl.program_id(1)

        @pl.when(i == 0)
        def _():
            sa_ref[...] = jnp.zeros_like(sa_ref)
            sg_ref[...] = jnp.zeros_like(sg_ref)

        keep_p = (i > 0).astype(F32)
        keep_n = (i < nrow - 1).astype(F32)
        xa[0:HALO, :] = pa_ref[...] * keep_p
        xg[0:HALO, :] = pg_ref[...] * keep_p
        xa[HALO:HALO + tr, :] = a_ref[...]
        xg[HALO:HALO + tr, :] = g_ref[...]
        xa[HALO + tr:, :] = na_ref[...] * keep_n
        xg[HALO + tr:, :] = ng_ref[...] * keep_n
        xd[0:tr, :] = da_ref[...]
        xd[tr:, :] = nd_ref[...] * keep_n

        def conv(x, w_ref, b_ref):
            return (b_ref[...] + x[pl.ds(HALO - 2, te), :] * w_ref[0:1, :] + x[pl.ds(HALO - 1, te), :] * w_ref[1:2, :]
                    + x[pl.ds(HALO, te), :] * w_ref[2:3, :])

        ua = conv(xa, wa_ref, ba_ref)
        ug = conv(xg, wg_ref, bg_ref)
        sg = _sigmoid(ua)
        dact = xd[...]
        ya[...] = dact * ug * (sg * (1.0 + ua * (1.0 - sg)))
        yg[...] = dact * (ua * sg)
        row = i * tr + lax.broadcasted_iota(jnp.int32, (tr, tc), 0)

        def back(y, x, w_ref, o_ref, s_ref):
            y0 = y[pl.ds(0, tr), :]
            dup = y0 * w_ref[2:3, :] + y[pl.ds(1, tr), :] * w_ref[1:2, :] + y[pl.ds(2, tr), :] * w_ref[0:1, :]
            o_ref[...] = jnp.where(row >= pad, dup, 0.0).astype(BF16)
            for tap in range(3):
                s_ref[tap:tap + 1, :] += jnp.sum(y0 * x[pl.ds(HALO - 2 + tap, tr), :], axis=0, keepdims=True)
            s_ref[3:4, :] += jnp.sum(y0, axis=0, keepdims=True)

        back(ya, xa, wa_ref, oa_ref, sa_ref)
        back(yg, xg, wg_ref, og_ref, sg_ref)

    last8 = lp // HALO - 1
    prev = lambda j, i: (jnp.maximum(i * hb - 1, 0), j)
    prevg = lambda j, i: (jnp.maximum(i * hb - 1, 0), nf + j)
    nxt = lambda j, i: (jnp.minimum((i + 1) * hb, last8), j)
    nxtg = lambda j, i: (jnp.minimum((i + 1) * hb, last8), nf + j)
    halo = lambda m: pl.BlockSpec((HALO, tc), m)
    main = pl.BlockSpec((tr, tc), lambda j, i: (i, j))
    sums = pl.BlockSpec((8, tc), lambda j, i: (0, j))
    return pl.pallas_call(
        body, name=name, grid=(nf, nrow),
        in_specs=[main, pl.BlockSpec((tr, tc), lambda j, i: (i, nf + j)), halo(prev), halo(prevg), halo(nxt), halo(nxtg),
                  main, halo(nxt),
                  pl.BlockSpec((3, tc), lambda j, i: (0, j)), pl.BlockSpec((3, tc), lambda j, i: (0, nf + j)),
                  pl.BlockSpec((1, tc), lambda j, i: (0, j)), pl.BlockSpec((1, tc), lambda j, i: (0, nf + j))],
        out_specs=[main, main, sums, sums],
        out_shape=[jax.ShapeDtypeStruct((lp, f), BF16), jax.ShapeDtypeStruct((lp, f), BF16),
                   jax.ShapeDtypeStruct((8, f), F32), jax.ShapeDtypeStruct((8, f), F32)],
        scratch_shapes=[pltpu.VMEM((tr + 2 * HALO, tc), F32), pltpu.VMEM((tr + 2 * HALO, tc), F32),
                        pltpu.VMEM((te, tc), F32), pltpu.VMEM((te, tc), F32), pltpu.VMEM((te, tc), F32)],
        compiler_params=_cparams(("parallel", "arbitrary")))(up, up, up, up, up, up, dact, dact, cw, cw, cb, cb)


def _head(h, target, g, d, name):
    lp, dm, t = d.lp, d.d, d.block
    inv_d = 1.0 / dm

    def body(h_ref, t_ref, g_ref, dh_ref, loss_ref, dg_ref):
        i = pl.program_id(0)

        @pl.when(i == 0)
        def _():
            dh_ref[...] = jnp.zeros_like(dh_ref)
            loss_ref[...] = jnp.zeros_like(loss_ref)
            dg_ref[...] = jnp.zeros_like(dg_ref)

        @pl.when(i > 0)
        def _():
            x, gv = h_ref[...], g_ref[...]
            r = lax.rsqrt(jnp.mean(x * x, axis=1, keepdims=True) + EPS)
            xh = x * r
            err = xh * gv - t_ref[...]
            loss_ref[...] += 0.5 * inv_d * jnp.sum(err * err)
            dy = err * inv_d
            gy = dy * gv
            c = jnp.mean(gy * x, axis=1, keepdims=True)
            dh_ref[...] = r * gy - x * (r * r * r) * c
            dg_ref[...] += jnp.sum(dy * xh, axis=0, keepdims=True)

    blk = pl.BlockSpec((t, dm), lambda i: (i, 0))
    return pl.pallas_call(
        body, name=name, grid=(lp // t,),
        in_specs=[blk, pl.BlockSpec((t, dm), lambda i: (jnp.maximum(i - 1, 0), 0)), pl.BlockSpec((1, dm), lambda i: (0, 0))],
        out_specs=[blk, pl.BlockSpec((8, LANES), lambda i: (0, 0)), pl.BlockSpec((1, dm), lambda i: (0, 0))],
        out_shape=[jax.ShapeDtypeStruct((lp, dm), F32), jax.ShapeDtypeStruct((8, LANES), F32),
                   jax.ShapeDtypeStruct((1, dm), F32)],
        compiler_params=_cparams(("arbitrary",)))(h, target, g.reshape(1, dm))


def _exchange(xs, *, scatter, name):
    n = len(xs)
    nf = len(FLIPS)

    def body(*refs):
        ins, outs = refs[:n], refs[n:2 * n]
        send_sems, recv_sems, loc_sems = refs[2 * n:]
        x, y, c = lax.axis_index("x"), lax.axis_index("y"), lax.axis_index("c")
        me = 4 * x + 2 * y + c
        sends, recvs, locs = [], [], []
        for a in range(n):
            src_me = ins[a].at[me] if scatter else ins[a]
            loc = pltpu.make_async_copy(src_me, outs[a].at[me], loc_sems.at[a])
            loc.start()
            locs.append(loc)
            for k, (fx, fy, fc) in enumerate(FLIPS):
                px, py, pc = x ^ fx, y ^ fy, c ^ fc
                peer = 4 * px + 2 * py + pc
                src = ins[a].at[peer] if scatter else ins[a]
                cp = pltpu.make_async_remote_copy(
                    src_ref=src, dst_ref=outs[a].at[me], send_sem=send_sems.at[a * nf + k],
                    recv_sem=recv_sems.at[a * nf + k], device_id=(px, py, pc), device_id_type=pl.DeviceIdType.MESH)
                cp.start()
                sends.append(cp)
                recvs.append(pltpu.make_async_remote_copy(
                    src_ref=src, dst_ref=outs[a].at[peer], send_sem=send_sems.at[a * nf + k],
                    recv_sem=recv_sems.at[a * nf + k], device_id=(px, py, pc), device_id_type=pl.DeviceIdType.MESH))
        for cp in recvs:
            cp.wait_recv()
        for cp in sends:
            cp.wait_send()
        for loc in locs:
            loc.wait()

    hbm = pl.BlockSpec(memory_space=pltpu.HBM)
    out_shape = [jax.ShapeDtypeStruct(((N_DEV,) + tuple(x.shape[1:])) if scatter else ((N_DEV,) + tuple(x.shape)), x.dtype)
                 for x in xs]
    return pl.pallas_call(
        body, name=name, in_specs=[hbm] * n, out_specs=[hbm] * n, out_shape=out_shape,
        scratch_shapes=[pltpu.SemaphoreType.DMA((n * nf,)), pltpu.SemaphoreType.DMA((n * nf,)),
                        pltpu.SemaphoreType.DMA((n,))],
    )(*xs)


def _gather_two_level(xs, name):
    n = len(xs)

    def body(*refs):
        ins, outs = refs[:n], refs[n:2 * n]
        send_sems, recv_sems, loc_sems = refs[2 * n:]
        x, y, c = lax.axis_index("x"), lax.axis_index("y"), lax.axis_index("c")
        me, sibling = (x, y, c), (x, y, 1 - c)
        chips = [(1 - x, y), (x, 1 - y), (1 - x, 1 - y)]

        def slot(a, dev):
            return outs[a].at[4 * dev[0] + 2 * dev[1] + dev[2]]

        def copy(a, k, block, to, src=None):
            return pltpu.make_async_remote_copy(
                src_ref=slot(a, block) if src is None else src, dst_ref=slot(a, block),
                send_sem=send_sems.at[7 * a + k], recv_sem=recv_sems.at[7 * a + k],
                device_id=to, device_id_type=pl.DeviceIdType.MESH)

        locs = [pltpu.make_async_copy(ins[a], slot(a, me), loc_sems.at[a]) for a in range(n)]
        for loc in locs:
            loc.start()
        first = []
        for a in range(n):
            first.append(copy(a, 0, me, sibling, src=ins[a]))
            first += [copy(a, 1 + j, me, (*chip, c), src=ins[a]) for j, chip in enumerate(chips)]
        for cp in first:
            cp.start()
        passed = []
        for j, chip in enumerate(chips):
            for a in range(n):
                copy(a, 1 + j, (*chip, c), me).wait_recv()
                fwd = copy(a, 4 + j, (*chip, c), sibling)
                fwd.start()
                passed.append(fwd)
        for a in range(n):
            copy(a, 0, sibling, me).wait_recv()
            for j, chip in enumerate(chips):
                copy(a, 4 + j, (*chip, 1 - c), me).wait_recv()
        for cp in first + passed:
            cp.wait_send()
        for loc in locs:
            loc.wait()

    hbm = pl.BlockSpec(memory_space=pltpu.HBM)
    return pl.pallas_call(
        body, name=name, in_specs=[hbm] * n, out_specs=[hbm] * n,
        out_shape=[jax.ShapeDtypeStruct((N_DEV,) + tuple(x.shape), x.dtype) for x in xs],
        scratch_shapes=[pltpu.SemaphoreType.DMA((7 * n,)), pltpu.SemaphoreType.DMA((7 * n,)),
                        pltpu.SemaphoreType.DMA((n,))],
    )(*xs)


def _pair_exchange(xs, name):
    n = len(xs)

    def body(*refs):
        ins, outs = refs[:n], refs[n:2 * n]
        send_sems, recv_sems = refs[2 * n:]
        x, y, c = lax.axis_index("x"), lax.axis_index("y"), lax.axis_index("c")
        copies = []
        for a in range(n):
            for q in range(4):
                cp = pltpu.make_async_remote_copy(
                    src_ref=ins[a].at[2 * q + (1 - c)], dst_ref=outs[a].at[q], send_sem=send_sems.at[4 * a + q],
                    recv_sem=recv_sems.at[4 * a + q], device_id=(x, y, 1 - c), device_id_type=pl.DeviceIdType.MESH)
                cp.start()
                copies.append(cp)
        for cp in copies:
            cp.wait_recv()
        for cp in copies:
            cp.wait_send()

    hbm = pl.BlockSpec(memory_space=pltpu.HBM)
    return pl.pallas_call(
        body, name=name, in_specs=[hbm] * n, out_specs=[hbm] * n,
        out_shape=[jax.ShapeDtypeStruct((4,) + tuple(x.shape[1:]), x.dtype) for x in xs],
        scratch_shapes=[pltpu.SemaphoreType.DMA((4 * n,)), pltpu.SemaphoreType.DMA((4 * n,))],
    )(*xs)


def _chip_exchange(xs, name):
    n = len(xs)

    def body(*refs):
        ins, outs = refs[:n], refs[n:2 * n]
        send_sems, recv_sems, loc_sems = refs[2 * n:]
        x, y, c = lax.axis_index("x"), lax.axis_index("y"), lax.axis_index("c")
        mine = 2 * x + y
        sends, recvs, locs = [], [], []
        for a in range(n):
            loc = pltpu.make_async_copy(ins[a].at[mine], outs[a].at[mine], loc_sems.at[a])
            loc.start()
            locs.append(loc)
            for k, (fx, fy) in enumerate([(1, 0), (0, 1), (1, 1)]):
                px, py = x ^ fx, y ^ fy
                peer = 2 * px + py
                sems = dict(send_sem=send_sems.at[3 * a + k], recv_sem=recv_sems.at[3 * a + k],
                            device_id=(px, py, c), device_id_type=pl.DeviceIdType.MESH)
                cp = pltpu.make_async_remote_copy(src_ref=ins[a].at[peer], dst_ref=outs[a].at[mine], **sems)
                cp.start()
                sends.append(cp)
                recvs.append(pltpu.make_async_remote_copy(src_ref=ins[a].at[peer], dst_ref=outs[a].at[peer], **sems))
        for cp in recvs:
            cp.wait_recv()
        for cp in sends:
            cp.wait_send()
        for loc in locs:
            loc.wait()

    hbm = pl.BlockSpec(memory_space=pltpu.HBM)
    return pl.pallas_call(
        body, name=name, in_specs=[hbm] * n, out_specs=[hbm] * n,
        out_shape=[jax.ShapeDtypeStruct(x.shape, x.dtype) for x in xs],
        scratch_shapes=[pltpu.SemaphoreType.DMA((3 * n,)), pltpu.SemaphoreType.DMA((3 * n,)),
                        pltpu.SemaphoreType.DMA((n,))],
    )(*xs)


def _pair_sum(a, b, name):
    shape = a.shape
    cols = shape[-1]
    rows = a.size // cols
    a, b = a.reshape(rows, cols), b.reshape(rows, cols)
    tr = next((t for t in (512, 256, 128, 64, 32, 16) if rows % t == 0 and t * cols <= 2 * ADAMW_TILE_ELEMS), rows)

    def body(a_ref, b_ref, o_ref):
        o_ref[...] = (a_ref[...].astype(F32) + b_ref[...].astype(F32)).astype(o_ref.dtype)

    blk = pl.BlockSpec((tr, cols), lambda i: (i, 0))
    return pl.pallas_call(body, name=name, grid=(rows // tr,), in_specs=[blk, blk], out_specs=blk,
                          out_shape=jax.ShapeDtypeStruct((rows, cols), a.dtype),
                          compiler_params=_cparams(("parallel",)))(a, b).reshape(shape)


def _adamw(parts, w, m, v, name):
    shape = w.shape
    cols = shape[-1]
    rows = w.size // cols
    nparts = parts.shape[0]
    parts, w, m, v = parts.reshape(nparts, rows, cols), w.reshape(rows, cols), m.reshape(rows, cols), v.reshape(rows, cols)
    tr = next((t for t in (256, 128, 64, 32, 16) if rows % t == 0 and t * cols <= ADAMW_TILE_ELEMS), rows)
    c1 = 1.0 - ADAM_B1 ** ADAM_STEP
    c2 = 1.0 - ADAM_B2 ** ADAM_STEP

    def body(p_ref, w_ref, m_ref, v_ref, g_ref, d_ref, mo_ref, vo_ref):
        g = p_ref[0].astype(F32)
        for q in range(1, nparts):
            g = g + p_ref[q].astype(F32)
        mn = ADAM_B1 * m_ref[...] + (1.0 - ADAM_B1) * g
        vn = ADAM_B2 * v_ref[...] + (1.0 - ADAM_B2) * (g * g)
        g_ref[...] = g
        mo_ref[...] = mn
        vo_ref[...] = vn
        d_ref[...] = -ADAM_LR * ((mn / c1) / (jnp.sqrt(vn / c2) + ADAM_EPS) + ADAM_WD * w_ref[...])

    blk = pl.BlockSpec((tr, cols), lambda i: (i, 0))
    out = jax.ShapeDtypeStruct((rows, cols), F32)
    res = pl.pallas_call(
        body, name=name, grid=(rows // tr,),
        in_specs=[pl.BlockSpec((nparts, tr, cols), lambda i: (0, i, 0)), blk, blk, blk],
        out_specs=[blk] * 4, out_shape=[out] * 4, compiler_params=_cparams(("parallel",)))(parts, w, m, v)
    return [r.reshape(shape) for r in res]


BIG = ["w_in", "w_uq", "w_ukv", "w_sb_out", "w_mla_out", "w_o", "w_up", "w_down"]
ROW_SHARDED = {"w_o", "w_down"}
SMALL = ["conv_w", "meta_tokens"]
SHARDED = BIG + SMALL
REPL = ["norm_mix", "q_norm", "kv_norm", "norm_ffn", "conv_b", "final_norm"]


def _pack_rows(flat, row_mult):
    unit = PACK_W * row_mult
    total = -(-flat.shape[0] // unit) * unit
    return jnp.pad(flat, (0, total - flat.shape[0])).reshape(-1, PACK_W)


def _padded_cols(c):
    return -(-c // LANES) * LANES


def _pad_block(a, name):
    c = a.shape[-1]
    if name in ROW_SHARDED or c % LANES == 0:
        return a
    return jnp.pad(a, [(0, 0)] * (a.ndim - 1) + [(0, _padded_cols(c) - c)])


def _full_from_slots(slots, name, c):
    if name in ROW_SHARDED:
        return jnp.transpose(slots, (1, 0, 2, 3)).reshape(slots.shape[1], -1, slots.shape[3])
    return jnp.concatenate([slots[q][..., :c] for q in range(N_DEV)], axis=-1)


def _slots_from_full(full, name):
    if name in ROW_SHARDED:
        l, rr, n = full.shape
        return jnp.transpose(full.reshape(l, N_DEV, rr // N_DEV, n), (1, 0, 2, 3))
    c = full.shape[-1] // N_DEV
    return jnp.stack([_pad_block(full[..., q * c:(q + 1) * c], name) for q in range(N_DEV)])


def _swap_halves(t):
    half = t.shape[-1] // 2
    return jnp.concatenate([t[..., half:], t[..., :half]], axis=-1)


def _in_offsets(d):
    widths = (d.sbw, d.sbw, d.sbw, d.q_lora, d.kv_lora, d.rope, d.d, d.d)
    offs, o = [], 0
    for w in widths:
        offs.append((o, o + w))
        o += w
    return offs


def _prime_weights(full, layer, d):
    offs = _in_offsets(d)
    w_in = full["w_in"][layer]
    cols = lambda k: w_in[:, offs[k][0]:offs[k][1]]
    kr = cols(5)
    zpad = jnp.zeros((d.d, d.g_off - d.kr_off - LANES), w_in.dtype)
    w_inb = jnp.concatenate([cols(3), cols(4), kr, _swap_halves(kr), zpad, cols(6), cols(7)], axis=1)
    w_ina = w_in[:, :d.wa]
    uq = full["w_uq"][layer].reshape(d.q_lora, d.mla_heads, d.nope + d.rope)
    rope = uq[..., d.nope:]
    w_uq = jnp.concatenate([uq[..., :d.nope], rope, _swap_halves(rope)], axis=-1).reshape(d.q_lora, d.qw)
    ukv = full["w_ukv"][layer].reshape(d.kv_lora, d.mla_heads, d.nope + d.vdim)
    w_ukv = jnp.concatenate([ukv[..., :d.nope].reshape(d.kv_lora, -1), ukv[..., d.nope:].reshape(d.kv_lora, -1)], axis=1)
    return dict(w_ina=w_ina, w_inb=w_inb, w_in=jnp.concatenate([w_ina, w_inb], axis=1), w_uq=w_uq, w_ukv=w_ukv,
                w_sb_out=full["w_sb_out"][layer], w_mla_out=full["w_mla_out"][layer], w_o=full["w_o"][layer],
                w_up=full["w_up"][layer], w_down=full["w_down"][layer])


def _unprime_grads(g, d):
    gi = g["w_in"]
    b = gi[:, d.wa:]
    kr = b[:, d.kr_off:d.kr_off + d.rope] + _swap_halves(b[:, d.kr_off + d.rope:d.kr_off + 2 * d.rope])
    w_in = jnp.concatenate([gi[:, :d.wa], b[:, :d.kr_off], kr, b[:, d.g_off:]], axis=1)
    uq = g["w_uq"].reshape(d.q_lora, d.mla_heads, 2 * LANES)
    rope = uq[..., d.nope:d.nope + d.rope] + _swap_halves(uq[..., d.nope + d.rope:])
    w_uq = jnp.concatenate([uq[..., :d.nope], rope], axis=-1).reshape(d.q_lora, -1)
    hw = d.mla_heads * d.nope
    ukv = g["w_ukv"]
    w_ukv = jnp.concatenate([ukv[:, :hw].reshape(d.kv_lora, d.mla_heads, d.nope),
                             ukv[:, hw:].reshape(d.kv_lora, d.mla_heads, d.vdim)], axis=-1).reshape(d.kv_lora, -1)
    return dict(g, w_in=w_in, w_uq=w_uq, w_ukv=w_ukv)


def _layer_fwd(h, add, w, norm_mix, q_norm, kv_norm, norm_ffn, cw, cb, ctab, stab, d, tag):
    s = {}
    if add is None:
        s["h"] = h
        s["hn"], s["r1"] = _norm_fwd(h, norm_mix, width=d.d, cidx=0, name=f"norm_mix_{tag}")
    else:
        s["h"], s["hn"], s["r1"] = _norm_fwd(h, norm_mix, width=d.d, cidx=0, add=add, name=f"norm_mix_{tag}")
    s["pa"] = _mm(s["hn"], w["w_ina"], out_dtype=BF16, name=f"proj_a_{tag}")
    s["pb"] = _mm(s["hn"], w["w_inb"], name=f"proj_b_{tag}")
    s["o_sb"], s["carry"] = _sb_fwd(s["pa"], d, f"sb_fwd_{tag}")
    s["cqn"], s["rq"] = _norm_fwd(s["pb"], q_norm, width=d.q_lora, cidx=0, name=f"norm_q_{tag}")
    s["ckn"], s["rk"] = _norm_fwd(s["pb"], kv_norm, width=d.kv_lora, cidx=d.q_lora // d.kv_lora, name=f"norm_kv_{tag}")
    qraw = _mm(s["cqn"], w["w_uq"], name=f"uq_{tag}")
    s["kv"] = _mm(s["ckn"], w["w_ukv"], out_dtype=BF16, name=f"ukv_{tag}")
    s["qm"], s["kr"] = _mla_prep_fwd(qraw, s["pb"], ctab, stab, d, f"mla_prep_{tag}")
    s["o_mla"], s["lse"] = _mla_fwd(s["qm"], s["kv"], s["kr"], d, f"mla_fwd_{tag}")
    s["b_sb"] = _mm(s["o_sb"], w["w_sb_out"], name=f"sb_out_{tag}")
    s["b_mla"] = _mm(s["o_mla"], w["w_mla_out"], name=f"mla_out_{tag}")
    s["merged"] = _gate_fwd(s["pb"], s["b_sb"], s["b_mla"], d, f"gate_{tag}")
    mix = _mm(s["merged"], w["w_o"], name=f"w_o_{tag}")
    s["h1"], s["hn2"], s["r2"] = _norm_fwd(s["h"], norm_ffn, width=d.d, cidx=0, add=mix, name=f"norm_ffn_{tag}")
    s["up"] = _mm(s["hn2"], w["w_up"], name=f"w_up_{tag}")
    s["act"] = _convglu_fwd(s["up"], cw, cb, d, f"convglu_{tag}")
    ffn = _mm(s["act"], w["w_down"], name=f"w_down_{tag}")
    return s, ffn


def _layer_bwd(dh2, s, w, norm_mix, q_norm, kv_norm, norm_ffn, cw, cb, ctab, stab, d, tag):
    g = {}
    dact = _mm(dh2, w["w_down"], tb=True, name=f"d_act_{tag}")
    g["w_down"] = _mm(s["act"], dh2, ta=True, name=f"g_w_down_{tag}")
    dup_a, dup_g, sums_a, sums_g = _convglu_bwd(s["up"], dact, cw, cb, d, f"convglu_bwd_{tag}")
    dup = jnp.concatenate([dup_a, dup_g], axis=1)
    g["conv_w"] = jnp.concatenate([sums_a[0:3], sums_g[0:3]], axis=1)
    g["conv_b"] = jnp.concatenate([sums_a[3], sums_g[3]], axis=0)
    g["w_up"] = _mm(s["hn2"], dup, ta=True, name=f"g_w_up_{tag}")
    dhn2 = _mm(dup, w["w_up"], tb=True, name=f"d_hn2_{tag}")
    dh1, g["norm_ffn"] = _norm_bwd(dhn2, s["h1"], s["r2"], norm_ffn, width=d.d, cidx=0, dres=dh2, name=f"norm_ffn_bwd_{tag}")
    dmerged = _mm(dh1, w["w_o"], tb=True, name=f"d_merged_{tag}")
    g["w_o"] = _mm(s["merged"], dh1, ta=True, name=f"g_w_o_{tag}")
    db_sb, db_mla, dg_sb, dg_mla = _gate_bwd(dmerged, s["pb"], s["b_sb"], s["b_mla"], d, f"gate_bwd_{tag}")
    do_sb = _mm(db_sb, w["w_sb_out"], tb=True, name=f"d_o_sb_{tag}")
    g["w_sb_out"] = _mm(s["o_sb"], db_sb, ta=True, name=f"g_w_sb_out_{tag}")
    do_mla = _mm(db_mla, w["w_mla_out"], tb=True, name=f"d_o_mla_{tag}")
    g["w_mla_out"] = _mm(s["o_mla"], db_mla, ta=True, name=f"g_w_mla_out_{tag}")
    dq_sb, dk_sb, dv_sb = _sb_bwd(s["pa"], do_sb, s["carry"], d, f"sb_bwd_{tag}")
    dqm, dkn, dv, dkr = _mla_bwd(s["qm"], s["kv"], s["kr"], s["o_mla"], do_mla, s["lse"], d, f"mla_bwd_{tag}")
    dqraw, dkr128 = _mla_prep_bwd(dqm, dkr, ctab, stab, d, f"mla_prep_bwd_{tag}")
    dkv = jnp.concatenate([dkn.astype(BF16), dv.astype(BF16)], axis=1)
    dcqn = _mm(dqraw, w["w_uq"], tb=True, name=f"d_cq_{tag}")
    g["w_uq"] = _mm(s["cqn"], dqraw, ta=True, name=f"g_w_uq_{tag}")
    dckn = _mm(dkv, w["w_ukv"], tb=True, name=f"d_ckv_{tag}")
    g["w_ukv"] = _mm(s["ckn"], dkv, ta=True, name=f"g_w_ukv_{tag}")
    dcq, g["q_norm"] = _norm_bwd(dcqn, s["pb"], s["rq"], q_norm, width=d.q_lora, cidx=0, out_dtype=BF16, name=f"norm_q_bwd_{tag}")
    dckv, g["kv_norm"] = _norm_bwd(dckn, s["pb"], s["rk"], kv_norm, width=d.kv_lora, cidx=d.q_lora // d.kv_lora,
                                   out_dtype=BF16, name=f"norm_kv_bwd_{tag}")
    zpad = jnp.zeros((d.lp, d.g_off - d.kr_off - LANES), BF16)
    dproj = jnp.concatenate([dq_sb, dk_sb.astype(BF16), dv_sb.astype(BF16), dcq, dckv, dkr128, zpad, dg_sb, dg_mla], axis=1)
    g["w_in"] = _mm(s["hn"], dproj, ta=True, name=f"g_w_in_{tag}")
    dhn = _mm(dproj, w["w_in"], tb=True, name=f"d_hn_{tag}")
    dh, g["norm_mix"] = _norm_bwd(dhn, s["h"], s["r1"], norm_mix, width=d.d, cidx=0, dres=dh1, name=f"norm_mix_bwd_{tag}")
    return dh, g


def _step(d, x, p, m, v, loss_target):
    x = x.reshape(d.seq, d.d)
    target = loss_target.reshape(d.seq, d.d)

    blocks = [_pad_block(p[n], n).astype(BF16 if n in BIG else F32) for n in SHARDED]
    gathered = _gather_two_level(blocks, "gather_weights")
    full = {n: _full_from_slots(g_, n, p[n].shape[-1]) for n, g_ in zip(SHARDED, gathered)}

    pos = jnp.arange(d.lp, dtype=F32) - d.pad
    half = d.rope // 2
    freqs = ROPE_THETA ** (-jnp.arange(half, dtype=F32) / half)
    ang = pos[:, None] * freqs[None, :]
    cos, sin = jnp.cos(ang), jnp.sin(ang)
    zero = jnp.zeros((d.lp, LANES - d.rope), F32)
    ctab = jnp.concatenate([cos, cos, zero], axis=1)
    stab = jnp.concatenate([-sin, sin, zero], axis=1)

    h = jnp.concatenate([jnp.zeros((d.pad, d.d), F32), full["meta_tokens"], x], axis=0)
    ws = [_prime_weights(full, l, d) for l in range(d.depth)]
    saved, add = [], None
    for l in range(d.depth):
        s, add = _layer_fwd(h, add, ws[l], p["norm_mix"][l], p["q_norm"][l], p["kv_norm"][l], p["norm_ffn"][l],
                            full["conv_w"][l], p["conv_b"][l].reshape(1, -1), ctab, stab, d, f"l{l}")
        saved.append(s)
        h = s["h1"]
    h_out = _residual_add(h, add, "final_add")
    dh, loss_part, g_final = _head(h_out, target, p["final_norm"], d, "head")

    grads = [None] * d.depth
    for l in reversed(range(d.depth)):
        dh, g = _layer_bwd(dh, saved[l], ws[l], p["norm_mix"][l], p["q_norm"][l], p["kv_norm"][l], p["norm_ffn"][l],
                           full["conv_w"][l], p["conv_b"][l].reshape(1, -1), ctab, stab, d, f"l{l}")
        grads[l] = _unprime_grads(g, d)
    grad_x = dh[d.first_tok:].reshape(1, d.seq, d.d)

    gfull = {n: jnp.stack([grads[l][n] for l in range(d.depth)]) for n in BIG + ["conv_w"]}
    gfull["meta_tokens"] = dh[d.pad:d.first_tok]
    gsend = [_slots_from_full(gfull[n].astype(BF16) if n in BIG else gfull[n], n) for n in SHARDED]
    theirs = _pair_exchange(gsend, "scatter_grads_pair")
    my_c = lax.axis_index("c")
    mine = [lax.dynamic_index_in_dim(g_.reshape((4, 2) + g_.shape[1:]), my_c, axis=1, keepdims=False) for g_ in gsend]
    pairs = [_pair_sum(a_, b_, f"pair_sum_{n}") for n, a_, b_ in zip(SHARDED, mine, theirs)]
    grecv = _chip_exchange(pairs, "scatter_grads_chips")
    outs_sh = {n: _adamw(r_, _pad_block(p[n], n), _pad_block(m[n], n), _pad_block(v[n], n), f"adamw_{n}")
               for n, r_ in zip(SHARDED, grecv)}

    grep = {n: jnp.stack([grads[l][n].reshape(-1) for l in range(d.depth)]) for n in REPL if n != "final_norm"}
    grep["final_norm"] = g_final.reshape(-1)
    rflat = jnp.concatenate([grep[n].reshape(-1) for n in REPL] + [loss_part[0, 0:1]])
    (rparts,) = _exchange([_pack_rows(rflat, 8)], scatter=False, name="gather_small_grads")
    packr = lambda t: _pack_rows(jnp.concatenate([t[n].reshape(-1) for n in REPL] + [jnp.zeros((1,), F32)]), 8)
    outs_rp = _adamw(rparts, packr(p), packr(m), packr(v), "adamw_replicated")

    def unpack(flat, names, extra=0):
        res, off = {}, 0
        flat = flat.reshape(-1)
        for n in names:
            res[n] = flat[off:off + p[n].size].reshape(p[n].shape)
            off += p[n].size
        return res, flat[off:off + extra]

    results = []
    loss = None
    for k in range(4):
        sh = {n: outs_sh[n][k][..., :p[n].shape[-1]] for n in SHARDED}
        rp, tail = unpack(outs_rp[k], REPL, 1)
        if k == 0:
            loss = tail[0]
        results.append({**sh, **rp})
    return loss, grad_x, results


def _residual_add(h, add, name):
    rows, width = h.shape
    tr = _pick(rows, (256, 128))

    def body(h_ref, a_ref, o_ref):
        o_ref[...] = h_ref[...] + a_ref[...]

    blk = pl.BlockSpec((tr, width), lambda i: (i, 0))
    return pl.pallas_call(body, name=name, grid=(rows // tr,), in_specs=[blk, blk], out_specs=blk,
                          out_shape=jax.ShapeDtypeStruct((rows, width), F32),
                          compiler_params=_cparams(("parallel",)))(h, add)


WEIGHTS = ["meta_tokens", "norm_mix", "w_in", "q_norm", "w_uq", "kv_norm", "w_ukv", "w_sb_out", "w_mla_out", "w_o",
           "norm_ffn", "w_up", "conv_w", "conv_b", "w_down", "final_norm"]


def _run(d, x, weights, loss_target, moments_m, moments_v):
    p = dict(zip(WEIGHTS, weights))
    m = dict(zip(WEIGHTS, moments_m))
    v = dict(zip(WEIGHTS, moments_v))
    loss, grad_x, res = _step(d, x, p, m, v, loss_target)
    out = [loss, grad_x]
    for k in range(4):
        out += [res[k][n] for n in WEIGHTS]
    return tuple(out)


def kernel(x, meta_tokens, norm_mix, w_in, q_norm, w_uq, kv_norm, w_ukv, w_sb_out, w_mla_out, w_o, norm_ffn, w_up, conv_w, conv_b, w_down, final_norm, loss_target, m_meta_tokens, m_norm_mix, m_w_in, m_q_norm, m_w_uq, m_kv_norm, m_w_ukv, m_w_sb_out, m_w_mla_out, m_w_o, m_norm_ffn, m_w_up, m_conv_w, m_conv_b, m_w_down, m_final_norm, v_meta_tokens, v_norm_mix, v_w_in, v_q_norm, v_w_uq, v_kv_norm, v_w_ukv, v_w_sb_out, v_w_mla_out, v_w_o, v_norm_ffn, v_w_up, v_conv_w, v_conv_b, v_w_down, v_final_norm):
    weights = [meta_tokens, norm_mix, w_in, q_norm, w_uq, kv_norm, w_ukv, w_sb_out, w_mla_out, w_o, norm_ffn, w_up,
               conv_w, conv_b, w_down, final_norm]
    ms = [m_meta_tokens, m_norm_mix, m_w_in, m_q_norm, m_w_uq, m_kv_norm, m_w_ukv, m_w_sb_out, m_w_mla_out, m_w_o,
          m_norm_ffn, m_w_up, m_conv_w, m_conv_b, m_w_down, m_final_norm]
    vs = [v_meta_tokens, v_norm_mix, v_w_in, v_q_norm, v_w_uq, v_kv_norm, v_w_ukv, v_w_sb_out, v_w_mla_out, v_w_o,
          v_norm_ffn, v_w_up, v_conv_w, v_conv_b, v_w_down, v_final_norm]
    return _run(PROD, x, weights, loss_target, ms, vs)
```

```python
import jax
import jax.numpy as jnp
from jax import lax
from jax.experimental import pallas as pl
from jax.experimental.pallas import tpu as pltpu

F32 = jnp.float32
BF16 = jnp.bfloat16

EPS = 1e-6
ROPE_THETA = 10000.0
ADAM_LR = 0.001
ADAM_B1 = 0.9
ADAM_B2 = 0.999
ADAM_EPS = 1e-08
ADAM_WD = 0.01
ADAM_STEP = 10
NEG = -1e30
DEAD = -110.0
LANES = 128
PACK_W = 1024
ADAMW_TILE_ELEMS = 256 * 1024
V7X_VMEM_LIMIT = 48 * 1024 * 1024
V7X_VMEM_LIMIT_BIG = 58 * 1024 * 1024
MESH_AXES = ("x", "y", "c")
N_DEV = 8
FLIPS = [(0, 0, 1), (0, 1, 0), (0, 1, 1), (1, 0, 0), (1, 0, 1), (1, 1, 0), (1, 1, 1)]


class _Dims:
    def __init__(self, d_model=2048, seq=8192, depth=2, n_meta=16, block=128, sb_heads=8, hd=128,
                 mla_heads=8, q_lora=512, kv_lora=256, nope=128, rope=64, vdim=128, d_ff=5632, tq=None):
        self.d, self.seq, self.depth, self.n_meta, self.block = d_model, seq, depth, n_meta, block
        self.sb_heads, self.hd, self.mla_heads = sb_heads, hd, mla_heads
        self.q_lora, self.kv_lora, self.nope, self.rope, self.vdim, self.f = q_lora, kv_lora, nope, rope, vdim, d_ff
        assert hd == LANES and nope == LANES and vdim == LANES and 2 * rope == LANES
        self.pad = block - n_meta
        self.lp = self.pad + n_meta + seq
        self.first_tok = self.pad + n_meta
        assert self.first_tok == block and self.lp % block == 0 and self.lp // block < LANES
        self.tq = tq or next(t for t in (640, 512, 256, 128) if self.lp % t == 0)
        assert self.tq % block == 0 and self.lp % self.tq == 0
        self.sbw = sb_heads * hd
        self.mlaw = mla_heads * vdim
        self.wa = 3 * self.sbw
        self.d_in = 3 * self.sbw + q_lora + kv_lora + rope + 2 * d_model
        self.tg = min(1024, d_model)
        self.kr_off = q_lora + kv_lora
        raw = self.kr_off + LANES
        self.g_off = -(-raw // self.tg) * self.tg
        self.wb = self.g_off + 2 * d_model
        self.qw = mla_heads * 2 * LANES


PROD = _Dims()


def _pick(n, prefs):
    for p in prefs:
        if n % p == 0:
            return p
    return n


def _cparams(sem, limit=V7X_VMEM_LIMIT):
    return pltpu.CompilerParams(dimension_semantics=sem, vmem_limit_bytes=limit)


def _mm(a, b, *, ta=False, tb=False, out_dtype=F32, name):
    if ta:
        kdim, m = a.shape
    else:
        m, kdim = a.shape
    if tb:
        n, k2 = b.shape
    else:
        k2, n = b.shape
    assert kdim == k2, (a.shape, b.shape, ta, tb)
    tm = _pick(m, (640, 512, 256, 128))
    tn = _pick(n, (1024, 512, 384, 256, 128))
    tk = _pick(kdim, (2816, 2048, 1664, 1408, 1024, 640, 512, 256, 128))
    nk = kdim // tk
    dn = (((0 if ta else 1,), (1 if tb else 0,)), ((), ()))

    def dot(a_ref, b_ref):
        return lax.dot_general(a_ref[...].astype(BF16), b_ref[...].astype(BF16), dn, preferred_element_type=F32)

    def body_one(a_ref, b_ref, o_ref):
        o_ref[...] = dot(a_ref, b_ref).astype(out_dtype)

    def body_acc(a_ref, b_ref, o_ref, acc_ref):
        k = pl.program_id(2)

        @pl.when(k == 0)
        def _():
            acc_ref[...] = dot(a_ref, b_ref)

        @pl.when((k > 0) & (k < nk - 1))
        def _():
            acc_ref[...] += dot(a_ref, b_ref)

        @pl.when(k == nk - 1)
        def _():
            o_ref[...] = (acc_ref[...] + dot(a_ref, b_ref)).astype(out_dtype)

    a_spec = pl.BlockSpec((tk, tm), lambda i, j, k: (k, i)) if ta else pl.BlockSpec((tm, tk), lambda i, j, k: (i, k))
    b_spec = pl.BlockSpec((tn, tk), lambda i, j, k: (j, k)) if tb else pl.BlockSpec((tk, tn), lambda i, j, k: (k, j))
    return pl.pallas_call(
        body_one if nk == 1 else body_acc, name=name, grid=(m // tm, n // tn, nk), in_specs=[a_spec, b_spec],
        out_specs=pl.BlockSpec((tm, tn), lambda i, j, k: (i, j)),
        out_shape=jax.ShapeDtypeStruct((m, n), out_dtype),
        scratch_shapes=[] if nk == 1 else [pltpu.VMEM((tm, tn), F32)],
        compiler_params=_cparams(("parallel", "parallel", "arbitrary")),
    )(a, b)


def _norm_fwd(x, g, *, width, cidx, add=None, name):
    rows = x.shape[0]
    tr = _pick(rows, (256, 128))
    has_add = add is not None

    def body(*refs):
        if has_add:
            x_ref, a_ref, g_ref, xn_ref, y_ref, r_ref = refs
            xv = x_ref[...] + a_ref[...]
            xn_ref[...] = xv
        else:
            x_ref, g_ref, y_ref, r_ref = refs
            xv = x_ref[...]
        r = lax.rsqrt(jnp.mean(xv * xv, axis=1, keepdims=True) + EPS)
        y_ref[...] = (xv * r * g_ref[...]).astype(BF16)
        r_ref[...] = r

    blk = pl.BlockSpec((tr, width), lambda i: (i, 0))
    in_specs = [pl.BlockSpec((tr, width), lambda i: (i, cidx))]
    args = [x]
    if has_add:
        in_specs.append(blk)
        args.append(add)
    in_specs.append(pl.BlockSpec((1, width), lambda i: (0, 0)))
    args.append(g.reshape(1, width))
    out_specs = [blk, pl.BlockSpec((tr, 1), lambda i: (i, 0))]
    out_shape = [jax.ShapeDtypeStruct((rows, width), BF16), jax.ShapeDtypeStruct((rows, 1), F32)]
    if has_add:
        out_specs.insert(0, blk)
        out_shape.insert(0, jax.ShapeDtypeStruct((rows, width), F32))
    return pl.pallas_call(body, name=name, grid=(rows // tr,), in_specs=in_specs, out_specs=out_specs,
                          out_shape=out_shape, compiler_params=_cparams(("parallel",)))(*args)


def _norm_bwd(dy, x, r, g, *, width, cidx, dres=None, out_dtype=F32, name):
    rows = x.shape[0]
    tr = _pick(rows, (256, 128))
    has_res = dres is not None

    def body(*refs):
        if has_res:
            dy_ref, x_ref, r_ref, g_ref, dr_ref, dx_ref, dxb_ref, dg_ref = refs
        else:
            dy_ref, x_ref, r_ref, g_ref, dx_ref, dg_ref = refs
        i = pl.program_id(0)

        @pl.when(i == 0)
        def _():
            dg_ref[...] = jnp.zeros_like(dg_ref)

        dyv, xv, rv = dy_ref[...], x_ref[...], r_ref[...]
        gy = dyv * g_ref[...]
        c = jnp.mean(gy * xv, axis=1, keepdims=True)
        dx = rv * gy - xv * (rv * rv * rv) * c
        if has_res:
            dx = dx + dr_ref[...]
            dxb_ref[...] = dx.astype(BF16)
        dx_ref[...] = dx.astype(out_dtype)
        dg_ref[...] += jnp.sum(dyv * xv * rv, axis=0, keepdims=True)

    blk = pl.BlockSpec((tr, width), lambda i: (i, 0))
    in_specs = [blk, pl.BlockSpec((tr, width), lambda i: (i, cidx)), pl.BlockSpec((tr, 1), lambda i: (i, 0)),
                pl.BlockSpec((1, width), lambda i: (0, 0))]
    args = [dy, x, r, g.reshape(1, width)]
    out_specs = [blk, pl.BlockSpec((1, width), lambda i: (0, 0))]
    out_shape = [jax.ShapeDtypeStruct((rows, width), out_dtype), jax.ShapeDtypeStruct((1, width), F32)]
    if has_res:
        in_specs.append(blk)
        args.append(dres)
        out_specs.insert(1, blk)
        out_shape.insert(1, jax.ShapeDtypeStruct((rows, width), BF16))
    return pl.pallas_call(
        body, name=name, grid=(rows // tr,), in_specs=in_specs, out_specs=out_specs, out_shape=out_shape,
        compiler_params=_cparams(("arbitrary",)))(*args)


def _split3(x):
    h1 = x.astype(BF16)
    r1 = x - h1.astype(F32)
    h2 = r1.astype(BF16)
    h3 = (r1 - h2.astype(F32)).astype(BF16)
    return h1, h2, h3


def _cum(x, tri):
    h1, h2, h3 = _split3(x)
    dot = lambda h: jnp.dot(h, tri, preferred_element_type=F32)
    return dot(h1) + dot(h2) + dot(h3)


def _dot_nt(a, b):
    return lax.dot_general(a, b, (((1,), (1,)), ((), ())), preferred_element_type=F32)


def _dot_tn(a, b):
    return lax.dot_general(a, b, (((0,), (0,)), ((), ())), preferred_element_type=F32)


def _sb_geometry(tq, t):
    assert t & (t - 1) == 0
    ri = lax.broadcasted_iota(jnp.int32, (tq, t), 0)
    return jnp.bitwise_and(ri, t - 1), jnp.right_shift(ri, t.bit_length() - 1), lax.broadcasted_iota(jnp.int32, (tq, t), 1)


def _sb_key_blocks(ref, base, r, t, kind):
    if kind == "low":
        return [ref[pl.ds(pl.multiple_of(jnp.maximum(base + g, 0) * t, t), t), :] for g in range(r)]
    slab = ref[pl.ds(pl.multiple_of(base * t, t), r * t), :]
    return [slab[g * t:(g + 1) * t, :] for g in range(r)]


def _sb_mask(geo, base, s, t, pad, kind):
    rowl, grp, col = geo
    if kind == "plain":
        return None
    if kind == "first":
        return col < rowl
    blk = base + grp
    causal = col < rowl + jnp.where(s > 0, t, 0)
    return (blk >= 0) & (blk * t + col >= pad) & causal


def _sb_fwd(qkv, d, name):
    nh, hd, lp, tq, t = d.sb_heads, d.hd, d.lp, d.tq, d.block
    r = tq // t
    scale = hd ** -0.5
    pad = d.pad

    def body(q_ref, k_ref, v_ref, o_ref, c_ref):
        i = pl.program_id(1)
        qs = [q_ref[g * t:(g + 1) * t, :] for g in range(r)]
        geo = _sb_geometry(tq, t)
        tri = (lax.broadcasted_iota(jnp.int32, (t, t), 0)
               > lax.broadcasted_iota(jnp.int32, (t, t), 1)).astype(BF16)
        lane = lax.broadcasted_iota(jnp.int32, (tq, LANES), 1)

        c_ref[...] = jnp.zeros_like(c_ref)

        def make_step(kind):
            def step(s, carry):
                acc, run = carry
                ks = _sb_key_blocks(k_ref, i * r - s, r, t, kind)
                vs = _sb_key_blocks(v_ref, i * r - s, r, t, kind)
                z = jnp.concatenate([_dot_nt(qs[g], ks[g]) for g in range(r)], axis=0) * scale
                e = jnp.exp(-jnp.abs(z))
                sp = jnp.maximum(z, 0.0) + jnp.log(1.0 + e)
                mask = _sb_mask(geo, i * r - s, s, t, pad, kind)
                spm = sp if mask is None else jnp.where(mask, sp, 0.0)
                w = jnp.exp(z - sp - _cum(spm, tri) + run)
                if mask is not None:
                    w = jnp.where(mask, w, 0.0)
                wb = w.astype(BF16)
                acc = acc + jnp.concatenate(
                    [jnp.dot(wb[g * t:(g + 1) * t, :], vs[g], preferred_element_type=F32) for g in range(r)], axis=0)
                c_ref[...] = jnp.where(lane == s, run, c_ref[...])
                run = run - jnp.sum(spm, axis=1, keepdims=True)
                return acc, run
            return step

        first, plain, low = make_step("first"), make_step("plain"), make_step("low")

        def alive(run):
            return (jnp.max(run) >= DEAD).astype(jnp.int32)

        def run_while(step, s, end, live, acc, run):
            def wbody(st):
                s, _, acc, run = st
                acc, run = step(s, (acc, run))
                return s + 1, alive(run), acc, run
            return lax.while_loop(lambda st: (st[0] < end) & (st[1] > 0), wbody, (s, live, acc, run))

        init = (jnp.zeros((tq, hd), F32), jnp.zeros((tq, 1), F32))
        carry = lax.fori_loop(0, jnp.minimum(i, 1), lambda _, cr: first(0, cr), init)
        acc, run = lax.fori_loop(0, 1 - jnp.minimum(i, 1), lambda _, cr: low(0, cr), carry)
        s, live, acc, run = run_while(plain, 1, i * r, alive(run), acc, run)
        end_low = jnp.where(s >= jnp.maximum(i * r, 1), (i + 1) * r, s)
        s, live, acc, run = run_while(low, s, end_low, live, acc, run)
        o_ref[...] = acc
        c_ref[...] = jnp.where(lane == LANES - 1, s.astype(F32), c_ref[...])

    return pl.pallas_call(
        body, name=name, grid=(nh, lp // tq),
        in_specs=[pl.BlockSpec((tq, hd), lambda h, i: (i, h)),
                  pl.BlockSpec((lp, hd), lambda h, i: (0, nh + h)),
                  pl.BlockSpec((lp, hd), lambda h, i: (0, 2 * nh + h))],
        out_specs=[pl.BlockSpec((tq, hd), lambda h, i: (i, h)),
                   pl.BlockSpec((None, tq, LANES), lambda h, i: (h, i, 0))],
        out_shape=[jax.ShapeDtypeStruct((lp, nh * hd), F32), jax.ShapeDtypeStruct((nh, lp, LANES), F32)],
        compiler_params=_cparams(("parallel", "arbitrary")),
    )(qkv, qkv, qkv)


def _sb_bwd(qkv, do, carry, d, name):
    nh, hd, lp, tq, t = d.sb_heads, d.hd, d.lp, d.tq, d.block
    r = tq // t
    scale = hd ** -0.5
    pad = d.pad

    def body(q_ref, k_ref, v_ref, do_ref, c_ref, dq_ref, dk_ref, dv_ref):
        i = pl.program_id(1)

        @pl.when(i == 0)
        def _():
            dk_ref[...] = jnp.zeros_like(dk_ref)
            dv_ref[...] = jnp.zeros_like(dv_ref)

        rows = lambda x, g: x[g * t:(g + 1) * t, :]
        qs = [q_ref[g * t:(g + 1) * t, :] for g in range(r)]
        dobs = [do_ref[g * t:(g + 1) * t, :].astype(BF16) for g in range(r)]
        geo = _sb_geometry(tq, t)
        ri = lax.broadcasted_iota(jnp.int32, (t, t), 0)
        ci = lax.broadcasted_iota(jnp.int32, (t, t), 1)
        tri_suf = (ri > ci).astype(BF16)
        tri_pre = (ri < ci).astype(BF16)
        lane = lax.broadcasted_iota(jnp.int32, (tq, LANES), 1)

        def make_step(kind):
            def step(s, carry):
                dq, pc = carry
                base = i * r - s
                ks = _sb_key_blocks(k_ref, base, r, t, kind)
                vs = _sb_key_blocks(v_ref, base, r, t, kind)
                z = jnp.concatenate([_dot_nt(qs[g], ks[g]) for g in range(r)], axis=0) * scale
                e = jnp.exp(-jnp.abs(z))
                sp = jnp.maximum(z, 0.0) + jnp.log(1.0 + e)
                mask = _sb_mask(geo, base, s, t, pad, kind)
                spm = sp if mask is None else jnp.where(mask, sp, 0.0)
                run = jnp.sum(jnp.where(lane == s, c_ref[...], 0.0), axis=1, keepdims=True)
                w = jnp.exp(z - sp - _cum(spm, tri_suf) + run)
                if mask is not None:
                    w = jnp.where(mask, w, 0.0)
                gw = w * jnp.concatenate([_dot_nt(dobs[g], vs[g]) for g in range(r)], axis=0)
                p = _cum(gw, tri_pre) + pc
                inv = 1.0 / (1.0 + e)
                sig = jnp.where(z >= 0.0, inv, e * inv)
                dz = (gw * (1.0 - sig) - sig * p) * scale
                if mask is not None:
                    dz = jnp.where(mask, dz, 0.0)
                dzb, wb = dz.astype(BF16), w.astype(BF16)
                dq = dq + jnp.concatenate(
                    [jnp.dot(rows(dzb, g), ks[g], preferred_element_type=F32) for g in range(r)], axis=0)
                dks = [_dot_tn(rows(dzb, g), qs[g]) for g in range(r)]
                dvs = [_dot_tn(rows(wb, g), dobs[g]) for g in range(r)]
                if kind == "low":
                    for g in range(r):
                        at = pl.ds(pl.multiple_of(jnp.maximum(base + g, 0) * t, t), t)
                        dk_ref[at, :] += dks[g]
                        dv_ref[at, :] += dvs[g]
                else:
                    at = pl.ds(pl.multiple_of(base * t, t), tq)
                    dk_ref[at, :] += jnp.concatenate(dks, axis=0)
                    dv_ref[at, :] += jnp.concatenate(dvs, axis=0)
                pc = pc + jnp.sum(gw, axis=1, keepdims=True)
                return dq, pc
            return step

        first, plain, low = make_step("first"), make_step("plain"), make_step("low")
        nsteps = jnp.max(jnp.where(lane == LANES - 1, c_ref[...], 0.0)).astype(jnp.int32)
        low_from = jnp.maximum(i * r, 1)
        plain_end = jnp.minimum(nsteps, low_from)
        carry = (jnp.zeros((tq, hd), F32), jnp.zeros((tq, 1), F32))
        carry = lax.fori_loop(0, jnp.maximum(nsteps - low_from, 0), lambda jj, cr: low(nsteps - 1 - jj, cr), carry)
        carry = lax.fori_loop(0, plain_end - 1, lambda jj, cr: plain(plain_end - 1 - jj, cr), carry)
        carry = lax.fori_loop(0, jnp.minimum(i, 1), lambda _, cr: first(0, cr), carry)
        dq, _ = lax.fori_loop(0, 1 - jnp.minimum(i, 1), lambda _, cr: low(0, cr), carry)
        dq_ref[...] = dq.astype(BF16)

    w3 = nh * hd
    return pl.pallas_call(
        body, name=name, grid=(nh, lp // tq),
        in_specs=[pl.BlockSpec((tq, hd), lambda h, i: (i, h)),
                  pl.BlockSpec((lp, hd), lambda h, i: (0, nh + h)),
                  pl.BlockSpec((lp, hd), lambda h, i: (0, 2 * nh + h)),
                  pl.BlockSpec((tq, hd), lambda h, i: (i, h)),
                  pl.BlockSpec((None, tq, LANES), lambda h, i: (h, i, 0))],
        out_specs=[pl.BlockSpec((tq, hd), lambda h, i: (i, h)),
                   pl.BlockSpec((lp, hd), lambda h, i: (0, h)),
                   pl.BlockSpec((lp, hd), lambda h, i: (0, h))],
        out_shape=[jax.ShapeDtypeStruct((lp, w3), BF16), jax.ShapeDtypeStruct((lp, w3), F32),
                   jax.ShapeDtypeStruct((lp, w3), F32)],
        compiler_params=_cparams(("arbitrary", "arbitrary")),
    )(qkv, qkv, qkv, do, carry)


def _mla_prep_fwd(qraw, projb, ctab, stab, d, name):
    lp, nh = d.lp, d.mla_heads
    tr = _pick(lp, (256, 128))
    kidx = d.kr_off // LANES

    def rope(u, c, s):
        return u * c + pltpu.roll(u, LANES // 2, 1) * s

    def body(q_ref, k_ref, c_ref, s_ref, qm_ref, kr_ref):
        c, s = c_ref[...], s_ref[...]
        for h in range(nh):
            base = 2 * LANES * h
            qm_ref[:, base:base + LANES] = q_ref[:, base:base + LANES].astype(BF16)
            qm_ref[:, base + LANES:base + 2 * LANES] = rope(q_ref[:, base + LANES:base + 2 * LANES], c, s).astype(BF16)
        kr_ref[...] = rope(k_ref[...], c, s).astype(BF16)

    tab = pl.BlockSpec((tr, LANES), lambda i: (i, 0))
    return pl.pallas_call(
        body, name=name, grid=(lp // tr,),
        in_specs=[pl.BlockSpec((tr, d.qw), lambda i: (i, 0)), pl.BlockSpec((tr, LANES), lambda i: (i, kidx)), tab, tab],
        out_specs=[pl.BlockSpec((tr, d.qw), lambda i: (i, 0)), tab],
        out_shape=[jax.ShapeDtypeStruct((lp, d.qw), BF16), jax.ShapeDtypeStruct((lp, LANES), BF16)],
        compiler_params=_cparams(("parallel",)))(qraw, projb, ctab, stab)


def _mla_prep_bwd(dqm, dkr, ctab, stab, d, name):
    lp, nh = d.lp, d.mla_heads
    tr = _pick(lp, (256, 128))

    def unrope(g, c, s):
        return g * c + pltpu.roll(g * s, LANES // 2, 1)

    def body(dq_ref, dk_ref, c_ref, s_ref, o_ref, ok_ref):
        c, s = c_ref[...], s_ref[...]
        for h in range(nh):
            base = 2 * LANES * h
            o_ref[:, base:base + LANES] = dq_ref[:, base:base + LANES].astype(BF16)
            o_ref[:, base + LANES:base + 2 * LANES] = unrope(dq_ref[:, base + LANES:base + 2 * LANES], c, s).astype(BF16)
        ok_ref[...] = unrope(dk_ref[...], c, s).astype(BF16)

    tab = pl.BlockSpec((tr, LANES), lambda i: (i, 0))
    wide = pl.BlockSpec((tr, d.qw), lambda i: (i, 0))
    return pl.pallas_call(
        body, name=name, grid=(lp // tr,), in_specs=[wide, tab, tab, tab], out_specs=[wide, tab],
        out_shape=[jax.ShapeDtypeStruct((lp, d.qw), BF16), jax.ShapeDtypeStruct((lp, LANES), BF16)],
        compiler_params=_cparams(("parallel",)))(dqm, dkr, ctab, stab)


def _mla_fwd(qm, kv, kr, d, name):
    nh, lp, t = d.mla_heads, d.lp, d.tq
    scale = (d.nope + d.rope) ** -0.5
    pad = d.pad

    def body(q_ref, kn_ref, v_ref, kr_ref, o_ref, lse_ref):
        i = pl.program_id(1)
        q = q_ref[...]
        row = i * t + lax.broadcasted_iota(jnp.int32, (t, t), 0)
        colb = lax.broadcasted_iota(jnp.int32, (t, t), 1)

        def make_step(masked):
            def step(j, carry):
                acc, m, l = carry
                off = pl.multiple_of(j * t, t)
                kc = jnp.concatenate([kn_ref[pl.ds(off, t), :], kr_ref[pl.ds(off, t), :]], axis=1)
                s = _dot_nt(q, kc) * scale
                if masked:
                    col = j * t + colb
                    s = jnp.where((col <= row) & (col >= pad), s, NEG)
                m_new = jnp.maximum(m, jnp.max(s, axis=1, keepdims=True))
                alpha = jnp.exp(m - m_new)
                p = jnp.exp(s - m_new)
                l = alpha * l + jnp.sum(p, axis=1, keepdims=True)
                acc = alpha * acc + jnp.dot(p.astype(BF16), v_ref[pl.ds(off, t), :], preferred_element_type=F32)
                return acc, m_new, l
            return step

        masked, plain = make_step(True), make_step(False)
        carry = masked(0, (jnp.zeros((t, LANES), F32), jnp.full((t, 1), NEG, F32), jnp.zeros((t, 1), F32)))
        carry = lax.fori_loop(1, i, plain, carry)
        acc, m, l = lax.fori_loop(i, i + jnp.minimum(i, 1), masked, carry)
        rowv = i * t + lax.broadcasted_iota(jnp.int32, (t, LANES), 0)
        o_ref[...] = jnp.where(rowv >= pad, acc / l, 0.0)
        lse_ref[...] = m + jnp.log(l)

    return pl.pallas_call(
        body, name=name, grid=(nh, lp // t),
        in_specs=[pl.BlockSpec((t, 2 * LANES), lambda h, i: (i, h)),
                  pl.BlockSpec((lp, LANES), lambda h, i: (0, h)),
                  pl.BlockSpec((lp, LANES), lambda h, i: (0, nh + h)),
                  pl.BlockSpec((lp, LANES), lambda h, i: (0, 0))],
        out_specs=[pl.BlockSpec((t, LANES), lambda h, i: (i, h)),
                   pl.BlockSpec((None, t, 1), lambda h, i: (h, i, 0))],
        out_shape=[jax.ShapeDtypeStruct((lp, nh * LANES), F32), jax.ShapeDtypeStruct((nh, lp, 1), F32)],
        compiler_params=_cparams(("parallel", "arbitrary")),
    )(qm, kv, kv, kr)


def _mla_bwd(qm, kv, kr, o, do, lse, d, name, ride=()):
    nh, lp, t = d.mla_heads, d.lp, d.tq
    scale = (d.nope + d.rope) ** -0.5
    pad = d.pad
    nr = len(ride)

    def body(*refs):
        q_ref, kn_ref, v_ref, kr_ref, o_ref, do_ref, lse_ref = refs[:7]
        dq_ref, dkn_ref, dv_ref, dkr_ref = refs[7 + nr:11 + nr]
        h = pl.program_id(0)
        i = pl.program_id(1)
        if nr:
            ride_refs = (refs[7:7 + nr], refs[11 + nr:11 + 2 * nr]) + tuple(refs[11 + 2 * nr:])

            @pl.when((h == 0) & (i == 0))
            def _():
                _chip_copies_start(_chip_copies(*ride_refs))

        @pl.when(i == 0)
        def _():
            dkn_ref[...] = jnp.zeros_like(dkn_ref)
            dv_ref[...] = jnp.zeros_like(dv_ref)

        @pl.when((i == 0) & (h == 0))
        def _():
            dkr_ref[...] = jnp.zeros_like(dkr_ref)

        q = q_ref[...]
        dof = do_ref[...]
        dob = dof.astype(BF16)
        delta = jnp.sum(dof * o_ref[...], axis=1, keepdims=True)
        lse = lse_ref[...]
        row = i * t + lax.broadcasted_iota(jnp.int32, (t, t), 0)
        colb = lax.broadcasted_iota(jnp.int32, (t, t), 1)

        def make_step(masked):
            def step(j, dq):
                off = pl.multiple_of(j * t, t)
                kc = jnp.concatenate([kn_ref[pl.ds(off, t), :], kr_ref[pl.ds(off, t), :]], axis=1)
                v = v_ref[pl.ds(off, t), :]
                s = _dot_nt(q, kc) * scale
                if masked:
                    col = j * t + colb
                    mask = (col <= row) & (col >= pad)
                    p = jnp.where(mask, jnp.exp(jnp.where(mask, s, NEG) - lse), 0.0)
                else:
                    p = jnp.exp(s - lse)
                dp = _dot_nt(dob, v)
                dsb = (p * (dp - delta) * scale).astype(BF16)
                dq = dq + jnp.dot(dsb, kc, preferred_element_type=F32)
                dkc = _dot_tn(dsb, q)
                dkn_ref[pl.ds(off, t), :] += dkc[:, :LANES]
                dkr_ref[pl.ds(off, t), :] += dkc[:, LANES:]
                dv_ref[pl.ds(off, t), :] += _dot_tn(p.astype(BF16), dob)
                return dq
            return step

        masked, plain = make_step(True), make_step(False)
        dq = masked(0, jnp.zeros((t, 2 * LANES), F32))
        dq = lax.fori_loop(1, i, plain, dq)
        dq_ref[...] = lax.fori_loop(i, i + jnp.minimum(i, 1), masked, dq)
        if nr:
            @pl.when((h == nh - 1) & (i == lp // t - 1))
            def _():
                _chip_copies_wait(_chip_copies(*ride_refs))

    hbm = pl.BlockSpec(memory_space=pltpu.HBM)
    res = pl.pallas_call(
        body, name=name, grid=(nh, lp // t),
        in_specs=[pl.BlockSpec((t, 2 * LANES), lambda h, i: (i, h)),
                  pl.BlockSpec((lp, LANES), lambda h, i: (0, h), pipeline_mode=pl.Buffered(1)),
                  pl.BlockSpec((lp, LANES), lambda h, i: (0, nh + h), pipeline_mode=pl.Buffered(1)),
                  pl.BlockSpec((lp, LANES), lambda h, i: (0, 0), pipeline_mode=pl.Buffered(1)),
                  pl.BlockSpec((t, LANES), lambda h, i: (i, h)),
                  pl.BlockSpec((t, LANES), lambda h, i: (i, h)),
                  pl.BlockSpec((None, t, 1), lambda h, i: (h, i, 0))] + [hbm] * nr,
        out_specs=[pl.BlockSpec((t, 2 * LANES), lambda h, i: (i, h)),
                   pl.BlockSpec((lp, LANES), lambda h, i: (0, h)),
                   pl.BlockSpec((lp, LANES), lambda h, i: (0, h)),
                   pl.BlockSpec((lp, LANES), lambda h, i: (0, 0))] + [hbm] * nr,
        out_shape=[jax.ShapeDtypeStruct((lp, nh * 2 * LANES), F32), jax.ShapeDtypeStruct((lp, nh * LANES), F32),
                   jax.ShapeDtypeStruct((lp, nh * LANES), F32), jax.ShapeDtypeStruct((lp, LANES), F32)]
                  + [jax.ShapeDtypeStruct(x.shape, x.dtype) for x in ride],
        scratch_shapes=_chip_copies_sems(nr) if nr else [],
        compiler_params=_cparams(("arbitrary", "arbitrary"), V7X_VMEM_LIMIT_BIG),
    )(qm, kv, kv, kr, o, do, lse, *ride)
    return res[:4], list(res[4:])


def _sigmoid(x):
    return 1.0 / (1.0 + jnp.exp(-x))


def _gate_fwd(projb, b_sb, b_mla, d, name):
    lp, tg = d.lp, d.tg
    tr = _pick(lp, (256, 128))
    o1, o2 = d.g_off // tg, (d.g_off + d.d) // tg

    def body(g1_ref, g2_ref, b1_ref, b2_ref, o_ref):
        o_ref[...] = (_sigmoid(g1_ref[...]) * b1_ref[...] + _sigmoid(g2_ref[...]) * b2_ref[...]).astype(BF16)

    blk = pl.BlockSpec((tr, tg), lambda i, j: (i, j))
    return pl.pallas_call(
        body, name=name, grid=(lp // tr, d.d // tg),
        in_specs=[pl.BlockSpec((tr, tg), lambda i, j: (i, o1 + j)), pl.BlockSpec((tr, tg), lambda i, j: (i, o2 + j)),
                  blk, blk],
        out_specs=blk, out_shape=jax.ShapeDtypeStruct((lp, d.d), BF16),
        compiler_params=_cparams(("parallel", "parallel")))(projb, projb, b_sb, b_mla)


def _gate_bwd(dm, projb, b_sb, b_mla, d, name):
    lp, tg = d.lp, d.tg
    tr = _pick(lp, (256, 128))
    o1, o2 = d.g_off // tg, (d.g_off + d.d) // tg

    def body(dm_ref, g1_ref, g2_ref, b1_ref, b2_ref, db1_ref, db2_ref, dg1_ref, dg2_ref):
        dmv = dm_ref[...]
        s1, s2 = _sigmoid(g1_ref[...]), _sigmoid(g2_ref[...])
        db1_ref[...] = (dmv * s1).astype(BF16)
        db2_ref[...] = (dmv * s2).astype(BF16)
        dg1_ref[...] = (dmv * b1_ref[...] * s1 * (1.0 - s1)).astype(BF16)
        dg2_ref[...] = (dmv * b2_ref[...] * s2 * (1.0 - s2)).astype(BF16)

    blk = pl.BlockSpec((tr, tg), lambda i, j: (i, j))
    out = jax.ShapeDtypeStruct((lp, d.d), BF16)
    return pl.pallas_call(
        body, name=name, grid=(lp // tr, d.d // tg),
        in_specs=[blk, pl.BlockSpec((tr, tg), lambda i, j: (i, o1 + j)), pl.BlockSpec((tr, tg), lambda i, j: (i, o2 + j)),
                  blk, blk],
        out_specs=[blk] * 4, out_shape=[out] * 4,
        compiler_params=_cparams(("parallel", "parallel")))(dm, projb, projb, b_sb, b_mla)


HALO = 8


def _conv_tiles(d):
    return _pick(d.lp, (640, 512, 256, 128)), _pick(d.f, (512, 256, 128))


def _convglu_fwd(up, cw, cb, d, name):
    lp, f = d.lp, d.f
    tr, tc = _conv_tiles(d)
    nf = f // tc
    hb = tr // HALO
    pad = d.pad

    def body(a_ref, g_ref, pa_ref, pg_ref, wa_ref, wg_ref, ba_ref, bg_ref, o_ref, xa, xg):
        i = pl.program_id(1)
        keep = (i > 0).astype(F32)
        xa[0:HALO, :] = pa_ref[...] * keep
        xg[0:HALO, :] = pg_ref[...] * keep
        xa[HALO:, :] = a_ref[...]
        xg[HALO:, :] = g_ref[...]

        def conv(x, w_ref, b_ref):
            return (b_ref[...] + x[pl.ds(HALO - 2, tr), :] * w_ref[0:1, :] + x[pl.ds(HALO - 1, tr), :] * w_ref[1:2, :]
                    + x[pl.ds(HALO, tr), :] * w_ref[2:3, :])

        ua = conv(xa, wa_ref, ba_ref)
        ug = conv(xg, wg_ref, bg_ref)
        row = i * tr + lax.broadcasted_iota(jnp.int32, (tr, tc), 0)
        o_ref[...] = jnp.where(row >= pad, ua * _sigmoid(ua) * ug, 0.0).astype(BF16)

    prev = lambda j, i: (jnp.maximum(i * hb - 1, 0), j)
    prevg = lambda j, i: (jnp.maximum(i * hb - 1, 0), nf + j)
    return pl.pallas_call(
        body, name=name, grid=(nf, lp // tr),
        in_specs=[pl.BlockSpec((tr, tc), lambda j, i: (i, j)), pl.BlockSpec((tr, tc), lambda j, i: (i, nf + j)),
                  pl.BlockSpec((HALO, tc), prev), pl.BlockSpec((HALO, tc), prevg),
                  pl.BlockSpec((3, tc), lambda j, i: (0, j)), pl.BlockSpec((3, tc), lambda j, i: (0, nf + j)),
                  pl.BlockSpec((1, tc), lambda j, i: (0, j)), pl.BlockSpec((1, tc), lambda j, i: (0, nf + j))],
        out_specs=pl.BlockSpec((tr, tc), lambda j, i: (i, j)),
        out_shape=jax.ShapeDtypeStruct((lp, f), BF16),
        scratch_shapes=[pltpu.VMEM((tr + HALO, tc), F32), pltpu.VMEM((tr + HALO, tc), F32)],
        compiler_params=_cparams(("parallel", "arbitrary")))(up, up, up, up, cw, cw, cb, cb)


def _convglu_bwd(up, dact, cw, cb, d, name):
    lp, f = d.lp, d.f
    tr, tc = _conv_tiles(d)
    nf = f // tc
    hb = tr // HALO
    nrow = lp // tr
    pad = d.pad
    te = tr + HALO

    def body(a_ref, g_ref, pa_ref, pg_ref, na_ref, ng_ref, da_ref, nd_ref, wa_ref, wg_ref, ba_ref, bg_ref,
             oa_ref, og_ref, sa_ref, sg_ref, xa, xg, xd, ya, yg):
        i = pl.program_id(1)

        @pl.when(i == 0)
        def _():
            sa_ref[...] = jnp.zeros_like(sa_ref)
            sg_ref[...] = jnp.zeros_like(sg_ref)

        keep_p = (i > 0).astype(F32)
        keep_n = (i < nrow - 1).astype(F32)
        xa[0:HALO, :] = pa_ref[...] * keep_p
        xg[0:HALO, :] = pg_ref[...] * keep_p
        xa[HALO:HALO + tr, :] = a_ref[...]
        xg[HALO:HALO + tr, :] = g_ref[...]
        xa[HALO + tr:, :] = na_ref[...] * keep_n
        xg[HALO + tr:, :] = ng_ref[...] * keep_n
        xd[0:tr, :] = da_ref[...]
        xd[tr:, :] = nd_ref[...] * keep_n

        def conv(x, w_ref, b_ref):
            taps = [x[pl.ds(HALO - 2 + tap, te), :] for tap in range(3)]
            return b_ref[...] + taps[0] * w_ref[0:1, :] + taps[1] * w_ref[1:2, :] + taps[2] * w_ref[2:3, :], taps

        ua, taps_a = conv(xa, wa_ref, ba_ref)
        ug, taps_g = conv(xg, wg_ref, bg_ref)
        sg = _sigmoid(ua)
        dact = xd[...]
        ya_v = dact * ug * (sg * (1.0 + ua * (1.0 - sg)))
        yg_v = dact * (ua * sg)
        ya[...] = ya_v
        yg[...] = yg_v
        row = i * tr + lax.broadcasted_iota(jnp.int32, (tr, tc), 0)

        def back(y, y_v, taps, w_ref, o_ref, s_ref):
            y0 = y_v[:tr]
            dup = y0 * w_ref[2:3, :] + y[pl.ds(1, tr), :] * w_ref[1:2, :] + y[pl.ds(2, tr), :] * w_ref[0:1, :]
            o_ref[...] = jnp.where(row >= pad, dup, 0.0).astype(BF16)
            for tap in range(3):
                s_ref[tap:tap + 1, :] += jnp.sum(y0 * taps[tap][:tr], axis=0, keepdims=True)
            s_ref[3:4, :] += jnp.sum(y0, axis=0, keepdims=True)

        back(ya, ya_v, taps_a, wa_ref, oa_ref, sa_ref)
        back(yg, yg_v, taps_g, wg_ref, og_ref, sg_ref)

    last8 = lp // HALO - 1
    prev = lambda j, i: (jnp.maximum(i * hb - 1, 0), j)
    prevg = lambda j, i: (jnp.maximum(i * hb - 1, 0), nf + j)
    nxt = lambda j, i: (jnp.minimum((i + 1) * hb, last8), j)
    nxtg = lambda j, i: (jnp.minimum((i + 1) * hb, last8), nf + j)
    halo = lambda m: pl.BlockSpec((HALO, tc), m)
    main = pl.BlockSpec((tr, tc), lambda j, i: (i, j))
    sums = pl.BlockSpec((8, tc), lambda j, i: (0, j))
    return pl.pallas_call(
        body, name=name, grid=(nf, nrow),
        in_specs=[main, pl.BlockSpec((tr, tc), lambda j, i: (i, nf + j)), halo(prev), halo(prevg), halo(nxt), halo(nxtg),
                  main, halo(nxt),
                  pl.BlockSpec((3, tc), lambda j, i: (0, j)), pl.BlockSpec((3, tc), lambda j, i: (0, nf + j)),
                  pl.BlockSpec((1, tc), lambda j, i: (0, j)), pl.BlockSpec((1, tc), lambda j, i: (0, nf + j))],
        out_specs=[main, main, sums, sums],
        out_shape=[jax.ShapeDtypeStruct((lp, f), BF16), jax.ShapeDtypeStruct((lp, f), BF16),
                   jax.ShapeDtypeStruct((8, f), F32), jax.ShapeDtypeStruct((8, f), F32)],
        scratch_shapes=[pltpu.VMEM((tr + 2 * HALO, tc), F32), pltpu.VMEM((tr + 2 * HALO, tc), F32),
                        pltpu.VMEM((te, tc), F32), pltpu.VMEM((te, tc), F32), pltpu.VMEM((te, tc), F32)],
        compiler_params=_cparams(("parallel", "arbitrary")))(up, up, up, up, up, up, dact, dact, cw, cw, cb, cb)


def _head(h, add, target, g, d, name):
    lp, dm, t = d.lp, d.d, d.block
    inv_d = 1.0 / dm

    def body(h_ref, a_ref, t_ref, g_ref, dh_ref, dhb_ref, loss_ref, dg_ref):
        i = pl.program_id(0)

        @pl.when(i == 0)
        def _():
            dh_ref[...] = jnp.zeros_like(dh_ref)
            dhb_ref[...] = jnp.zeros_like(dhb_ref)
            loss_ref[...] = jnp.zeros_like(loss_ref)
            dg_ref[...] = jnp.zeros_like(dg_ref)

        @pl.when(i > 0)
        def _():
            x, gv = h_ref[...] + a_ref[...], g_ref[...]
            r = lax.rsqrt(jnp.mean(x * x, axis=1, keepdims=True) + EPS)
            xh = x * r
            err = xh * gv - t_ref[...]
            loss_ref[...] += 0.5 * inv_d * jnp.sum(err * err)
            dy = err * inv_d
            gy = dy * gv
            c = jnp.mean(gy * x, axis=1, keepdims=True)
            dh = r * gy - x * (r * r * r) * c
            dh_ref[...] = dh
            dhb_ref[...] = dh.astype(BF16)
            dg_ref[...] += jnp.sum(dy * xh, axis=0, keepdims=True)

    blk = pl.BlockSpec((t, dm), lambda i: (i, 0))
    return pl.pallas_call(
        body, name=name, grid=(lp // t,),
        in_specs=[blk, blk, pl.BlockSpec((t, dm), lambda i: (jnp.maximum(i - 1, 0), 0)),
                  pl.BlockSpec((1, dm), lambda i: (0, 0))],
        out_specs=[blk, blk, pl.BlockSpec((8, LANES), lambda i: (0, 0)), pl.BlockSpec((1, dm), lambda i: (0, 0))],
        out_shape=[jax.ShapeDtypeStruct((lp, dm), F32), jax.ShapeDtypeStruct((lp, dm), BF16),
                   jax.ShapeDtypeStruct((8, LANES), F32), jax.ShapeDtypeStruct((1, dm), F32)],
        compiler_params=_cparams(("arbitrary",)))(h, add, target, g.reshape(1, dm))


def _exchange(xs, *, scatter, name):
    n = len(xs)
    nf = len(FLIPS)

    def body(*refs):
        ins, outs = refs[:n], refs[n:2 * n]
        send_sems, recv_sems, loc_sems = refs[2 * n:]
        x, y, c = lax.axis_index("x"), lax.axis_index("y"), lax.axis_index("c")
        me = 4 * x + 2 * y + c
        sends, recvs, locs = [], [], []
        for a in range(n):
            src_me = ins[a].at[me] if scatter else ins[a]
            loc = pltpu.make_async_copy(src_me, outs[a].at[me], loc_sems.at[a])
            loc.start()
            locs.append(loc)
            for k, (fx, fy, fc) in enumerate(FLIPS):
                px, py, pc = x ^ fx, y ^ fy, c ^ fc
                peer = 4 * px + 2 * py + pc
                src = ins[a].at[peer] if scatter else ins[a]
                cp = pltpu.make_async_remote_copy(
                    src_ref=src, dst_ref=outs[a].at[me], send_sem=send_sems.at[a * nf + k],
                    recv_sem=recv_sems.at[a * nf + k], device_id=(px, py, pc), device_id_type=pl.DeviceIdType.MESH)
                cp.start()
                sends.append(cp)
                recvs.append(pltpu.make_async_remote_copy(
                    src_ref=src, dst_ref=outs[a].at[peer], send_sem=send_sems.at[a * nf + k],
                    recv_sem=recv_sems.at[a * nf + k], device_id=(px, py, pc), device_id_type=pl.DeviceIdType.MESH))
        for cp in recvs:
            cp.wait_recv()
        for cp in sends:
            cp.wait_send()
        for loc in locs:
            loc.wait()

    hbm = pl.BlockSpec(memory_space=pltpu.HBM)
    out_shape = [jax.ShapeDtypeStruct(((N_DEV,) + tuple(x.shape[1:])) if scatter else ((N_DEV,) + tuple(x.shape)), x.dtype)
                 for x in xs]
    return pl.pallas_call(
        body, name=name, in_specs=[hbm] * n, out_specs=[hbm] * n, out_shape=out_shape,
        scratch_shapes=[pltpu.SemaphoreType.DMA((n * nf,)), pltpu.SemaphoreType.DMA((n * nf,)),
                        pltpu.SemaphoreType.DMA((n,))],
    )(*xs)


def _gather_two_level(xs, name):
    n = len(xs)

    def body(*refs):
        ins, outs = refs[:n], refs[n:2 * n]
        send_sems, recv_sems, loc_sems = refs[2 * n:]
        x, y, c = lax.axis_index("x"), lax.axis_index("y"), lax.axis_index("c")
        me, sibling = (x, y, c), (x, y, 1 - c)
        chips = [(1 - x, y), (x, 1 - y), (1 - x, 1 - y)]

        def slot(a, dev):
            return outs[a].at[4 * dev[0] + 2 * dev[1] + dev[2]]

        def copy(a, k, block, to, src=None):
            return pltpu.make_async_remote_copy(
                src_ref=slot(a, block) if src is None else src, dst_ref=slot(a, block),
                send_sem=send_sems.at[7 * a + k], recv_sem=recv_sems.at[7 * a + k],
                device_id=to, device_id_type=pl.DeviceIdType.MESH)

        locs = [pltpu.make_async_copy(ins[a], slot(a, me), loc_sems.at[a]) for a in range(n)]
        for loc in locs:
            loc.start()
        first = []
        for a in range(n):
            first.append(copy(a, 0, me, sibling, src=ins[a]))
            first += [copy(a, 1 + j, me, (*chip, c), src=ins[a]) for j, chip in enumerate(chips)]
        for cp in first:
            cp.start()
        passed = []
        for j, chip in enumerate(chips):
            for a in range(n):
                copy(a, 1 + j, (*chip, c), me).wait_recv()
                fwd = copy(a, 4 + j, (*chip, c), sibling)
                fwd.start()
                passed.append(fwd)
        for a in range(n):
            copy(a, 0, sibling, me).wait_recv()
            for j, chip in enumerate(chips):
                copy(a, 4 + j, (*chip, 1 - c), me).wait_recv()
        for cp in first + passed:
            cp.wait_send()
        for loc in locs:
            loc.wait()

    hbm = pl.BlockSpec(memory_space=pltpu.HBM)
    return pl.pallas_call(
        body, name=name, in_specs=[hbm] * n, out_specs=[hbm] * n,
        out_shape=[jax.ShapeDtypeStruct((N_DEV,) + tuple(x.shape), x.dtype) for x in xs],
        scratch_shapes=[pltpu.SemaphoreType.DMA((7 * n,)), pltpu.SemaphoreType.DMA((7 * n,)),
                        pltpu.SemaphoreType.DMA((n,))],
    )(*xs)


def _pair_exchange(xs, name):
    n = len(xs)

    def body(*refs):
        ins, outs = refs[:n], refs[n:2 * n]
        send_sems, recv_sems = refs[2 * n:]
        x, y, c = lax.axis_index("x"), lax.axis_index("y"), lax.axis_index("c")
        copies = []
        for a in range(n):
            for q in range(4):
                cp = pltpu.make_async_remote_copy(
                    src_ref=ins[a].at[2 * q + (1 - c)], dst_ref=outs[a].at[q], send_sem=send_sems.at[4 * a + q],
                    recv_sem=recv_sems.at[4 * a + q], device_id=(x, y, 1 - c), device_id_type=pl.DeviceIdType.MESH)
                cp.start()
                copies.append(cp)
        for cp in copies:
            cp.wait_recv()
        for cp in copies:
            cp.wait_send()

    hbm = pl.BlockSpec(memory_space=pltpu.HBM)
    return pl.pallas_call(
        body, name=name, in_specs=[hbm] * n, out_specs=[hbm] * n,
        out_shape=[jax.ShapeDtypeStruct((4,) + tuple(x.shape[1:]), x.dtype) for x in xs],
        scratch_shapes=[pltpu.SemaphoreType.DMA((4 * n,)), pltpu.SemaphoreType.DMA((4 * n,))],
    )(*xs)


def _chip_exchange(xs, name):
    n = len(xs)

    def body(*refs):
        copies = _chip_copies(refs[:n], refs[n:2 * n], *refs[2 * n:])
        _chip_copies_start(copies)
        _chip_copies_wait(copies)

    hbm = pl.BlockSpec(memory_space=pltpu.HBM)
    return pl.pallas_call(
        body, name=name, in_specs=[hbm] * n, out_specs=[hbm] * n,
        out_shape=[jax.ShapeDtypeStruct(x.shape, x.dtype) for x in xs],
        scratch_shapes=_chip_copies_sems(n),
    )(*xs)


def _chip_copies_sems(n):
    return [pltpu.SemaphoreType.DMA((3 * n,)), pltpu.SemaphoreType.DMA((3 * n,)), pltpu.SemaphoreType.DMA((n,))]


def _chip_copies(ins, outs, send_sems, recv_sems, loc_sems):
    x, y, c = lax.axis_index("x"), lax.axis_index("y"), lax.axis_index("c")
    mine = 2 * x + y
    locs, sends, recvs = [], [], []
    for a in range(len(ins)):
        locs.append(pltpu.make_async_copy(ins[a].at[mine], outs[a].at[mine], loc_sems.at[a]))
        for k, (fx, fy) in enumerate([(1, 0), (0, 1), (1, 1)]):
            px, py = x ^ fx, y ^ fy
            peer = 2 * px + py
            sems = dict(send_sem=send_sems.at[3 * a + k], recv_sem=recv_sems.at[3 * a + k],
                        device_id=(px, py, c), device_id_type=pl.DeviceIdType.MESH)
            sends.append(pltpu.make_async_remote_copy(src_ref=ins[a].at[peer], dst_ref=outs[a].at[mine], **sems))
            recvs.append(pltpu.make_async_remote_copy(src_ref=ins[a].at[peer], dst_ref=outs[a].at[peer], **sems))
    return locs, sends, recvs


def _chip_copies_start(copies):
    locs, sends, _ = copies
    for cp in locs + sends:
        cp.start()


def _chip_copies_wait(copies):
    locs, sends, recvs = copies
    for cp in recvs:
        cp.wait_recv()
    for cp in sends:
        cp.wait_send()
    for cp in locs:
        cp.wait()


def _pair_sum(a, b, name):
    shape = a.shape
    cols = shape[-1]
    rows = a.size // cols
    a, b = a.reshape(rows, cols), b.reshape(rows, cols)
    tr = next((t for t in (512, 256, 128, 64, 32, 16) if rows % t == 0 and t * cols <= 2 * ADAMW_TILE_ELEMS), rows)

    def body(a_ref, b_ref, o_ref):
        o_ref[...] = (a_ref[...].astype(F32) + b_ref[...].astype(F32)).astype(o_ref.dtype)

    blk = pl.BlockSpec((tr, cols), lambda i: (i, 0))
    return pl.pallas_call(body, name=name, grid=(rows // tr,), in_specs=[blk, blk], out_specs=blk,
                          out_shape=jax.ShapeDtypeStruct((rows, cols), a.dtype),
                          compiler_params=_cparams(("parallel",)))(a, b).reshape(shape)


def _adamw(parts, w, m, v, name):
    shape = w.shape
    cols = shape[-1]
    rows = w.size // cols
    nparts = parts.shape[0]
    parts, w, m, v = parts.reshape(nparts, rows, cols), w.reshape(rows, cols), m.reshape(rows, cols), v.reshape(rows, cols)
    tr = next((t for t in (256, 128, 64, 32, 16) if rows % t == 0 and t * cols <= ADAMW_TILE_ELEMS), rows)
    c1 = 1.0 - ADAM_B1 ** ADAM_STEP
    c2 = 1.0 - ADAM_B2 ** ADAM_STEP

    def body(p_ref, w_ref, m_ref, v_ref, g_ref, d_ref, mo_ref, vo_ref):
        g = p_ref[0].astype(F32)
        for q in range(1, nparts):
            g = g + p_ref[q].astype(F32)
        mn = ADAM_B1 * m_ref[...] + (1.0 - ADAM_B1) * g
        vn = ADAM_B2 * v_ref[...] + (1.0 - ADAM_B2) * (g * g)
        g_ref[...] = g
        mo_ref[...] = mn
        vo_ref[...] = vn
        d_ref[...] = -ADAM_LR * ((mn / c1) / (jnp.sqrt(vn / c2) + ADAM_EPS) + ADAM_WD * w_ref[...])

    blk = pl.BlockSpec((tr, cols), lambda i: (i, 0))
    out = jax.ShapeDtypeStruct((rows, cols), F32)
    res = pl.pallas_call(
        body, name=name, grid=(rows // tr,),
        in_specs=[pl.BlockSpec((nparts, tr, cols), lambda i: (0, i, 0)), blk, blk, blk],
        out_specs=[blk] * 4, out_shape=[out] * 4, compiler_params=_cparams(("parallel",)))(parts, w, m, v)
    return [r.reshape(shape) for r in res]


BIG = ["w_in", "w_uq", "w_ukv", "w_sb_out", "w_mla_out", "w_o", "w_up", "w_down"]
ROW_SHARDED = {"w_o", "w_down"}
SMALL = ["conv_w", "meta_tokens"]
SHARDED = BIG + SMALL
REPL = ["norm_mix", "q_norm", "kv_norm", "norm_ffn", "conv_b", "final_norm"]


def _pack_rows(flat, row_mult):
    unit = PACK_W * row_mult
    total = -(-flat.shape[0] // unit) * unit
    return jnp.pad(flat, (0, total - flat.shape[0])).reshape(-1, PACK_W)


def _padded_cols(c):
    return -(-c // LANES) * LANES


def _pad_block(a, name):
    c = a.shape[-1]
    if name in ROW_SHARDED or c % LANES == 0:
        return a
    return jnp.pad(a, [(0, 0)] * (a.ndim - 1) + [(0, _padded_cols(c) - c)])


def _full_from_slots(slots, name, c):
    if name in ROW_SHARDED:
        return jnp.transpose(slots, (1, 0, 2, 3)).reshape(slots.shape[1], -1, slots.shape[3])
    return jnp.concatenate([slots[q][..., :c] for q in range(N_DEV)], axis=-1)


def _slots_from_full(full, name):
    if name in ROW_SHARDED:
        l, rr, n = full.shape
        return jnp.transpose(full.reshape(l, N_DEV, rr // N_DEV, n), (1, 0, 2, 3))
    c = full.shape[-1] // N_DEV
    return jnp.stack([_pad_block(full[..., q * c:(q + 1) * c], name) for q in range(N_DEV)])


def _swap_halves(t):
    half = t.shape[-1] // 2
    return jnp.concatenate([t[..., half:], t[..., :half]], axis=-1)


def _in_offsets(d):
    widths = (d.sbw, d.sbw, d.sbw, d.q_lora, d.kv_lora, d.rope, d.d, d.d)
    offs, o = [], 0
    for w in widths:
        offs.append((o, o + w))
        o += w
    return offs


def _prime_weights(full, layer, d):
    offs = _in_offsets(d)
    w_in = full["w_in"][layer]
    cols = lambda k: w_in[:, offs[k][0]:offs[k][1]]
    kr = cols(5)
    zpad = jnp.zeros((d.d, d.g_off - d.kr_off - LANES), w_in.dtype)
    w_inb = jnp.concatenate([cols(3), cols(4), kr, _swap_halves(kr), zpad, cols(6), cols(7)], axis=1)
    w_ina = w_in[:, :d.wa]
    uq = full["w_uq"][layer].reshape(d.q_lora, d.mla_heads, d.nope + d.rope)
    rope = uq[..., d.nope:]
    w_uq = jnp.concatenate([uq[..., :d.nope], rope, _swap_halves(rope)], axis=-1).reshape(d.q_lora, d.qw)
    ukv = full["w_ukv"][layer].reshape(d.kv_lora, d.mla_heads, d.nope + d.vdim)
    w_ukv = jnp.concatenate([ukv[..., :d.nope].reshape(d.kv_lora, -1), ukv[..., d.nope:].reshape(d.kv_lora, -1)], axis=1)
    return dict(w_ina=w_ina, w_inb=w_inb, w_in=jnp.concatenate([w_ina, w_inb], axis=1), w_uq=w_uq, w_ukv=w_ukv,
                w_sb_out=full["w_sb_out"][layer], w_mla_out=full["w_mla_out"][layer], w_o=full["w_o"][layer],
                w_up=full["w_up"][layer], w_down=full["w_down"][layer])


def _unprime_grads(g, d):
    gi = g["w_in"]
    b = gi[:, d.wa:]
    kr = b[:, d.kr_off:d.kr_off + d.rope] + _swap_halves(b[:, d.kr_off + d.rope:d.kr_off + 2 * d.rope])
    w_in = jnp.concatenate([gi[:, :d.wa], b[:, :d.kr_off], kr, b[:, d.g_off:]], axis=1)
    uq = g["w_uq"].reshape(d.q_lora, d.mla_heads, 2 * LANES)
    rope = uq[..., d.nope:d.nope + d.rope] + _swap_halves(uq[..., d.nope + d.rope:])
    w_uq = jnp.concatenate([uq[..., :d.nope], rope], axis=-1).reshape(d.q_lora, -1)
    hw = d.mla_heads * d.nope
    ukv = g["w_ukv"]
    w_ukv = jnp.concatenate([ukv[:, :hw].reshape(d.kv_lora, d.mla_heads, d.nope),
                             ukv[:, hw:].reshape(d.kv_lora, d.mla_heads, d.vdim)], axis=-1).reshape(d.kv_lora, -1)
    return dict(g, w_in=w_in, w_uq=w_uq, w_ukv=w_ukv)


def _layer_fwd(h, add, w, norm_mix, q_norm, kv_norm, norm_ffn, cw, cb, ctab, stab, d, tag):
    s = {}
    if add is None:
        s["h"] = h
        s["hn"], s["r1"] = _norm_fwd(h, norm_mix, width=d.d, cidx=0, name=f"norm_mix_{tag}")
    else:
        s["h"], s["hn"], s["r1"] = _norm_fwd(h, norm_mix, width=d.d, cidx=0, add=add, name=f"norm_mix_{tag}")
    s["pa"] = _mm(s["hn"], w["w_ina"], out_dtype=BF16, name=f"proj_a_{tag}")
    s["pb"] = _mm(s["hn"], w["w_inb"], name=f"proj_b_{tag}")
    s["o_sb"], s["carry"] = _sb_fwd(s["pa"], d, f"sb_fwd_{tag}")
    s["cqn"], s["rq"] = _norm_fwd(s["pb"], q_norm, width=d.q_lora, cidx=0, name=f"norm_q_{tag}")
    s["ckn"], s["rk"] = _norm_fwd(s["pb"], kv_norm, width=d.kv_lora, cidx=d.q_lora // d.kv_lora, name=f"norm_kv_{tag}")
    qraw = _mm(s["cqn"], w["w_uq"], name=f"uq_{tag}")
    s["kv"] = _mm(s["ckn"], w["w_ukv"], out_dtype=BF16, name=f"ukv_{tag}")
    s["qm"], s["kr"] = _mla_prep_fwd(qraw, s["pb"], ctab, stab, d, f"mla_prep_{tag}")
    s["o_mla"], s["lse"] = _mla_fwd(s["qm"], s["kv"], s["kr"], d, f"mla_fwd_{tag}")
    s["b_sb"] = _mm(s["o_sb"], w["w_sb_out"], name=f"sb_out_{tag}")
    s["b_mla"] = _mm(s["o_mla"], w["w_mla_out"], name=f"mla_out_{tag}")
    s["merged"] = _gate_fwd(s["pb"], s["b_sb"], s["b_mla"], d, f"gate_{tag}")
    mix = _mm(s["merged"], w["w_o"], name=f"w_o_{tag}")
    s["h1"], s["hn2"], s["r2"] = _norm_fwd(s["h"], norm_ffn, width=d.d, cidx=0, add=mix, name=f"norm_ffn_{tag}")
    s["up"] = _mm(s["hn2"], w["w_up"], name=f"w_up_{tag}")
    s["act"] = _convglu_fwd(s["up"], cw, cb, d, f"convglu_{tag}")
    ffn = _mm(s["act"], w["w_down"], name=f"w_down_{tag}")
    return s, ffn


def _layer_bwd(dh2, dh2b, s, w, norm_mix, q_norm, kv_norm, norm_ffn, cw, cb, ctab, stab, d, tag, ride=()):
    g = {}
    dact = _mm(dh2b, w["w_down"], tb=True, name=f"d_act_{tag}")
    g["w_down"] = _mm(s["act"], dh2b, ta=True, name=f"g_w_down_{tag}")
    dup_a, dup_g, sums_a, sums_g = _convglu_bwd(s["up"], dact, cw, cb, d, f"convglu_bwd_{tag}")
    dup = jnp.concatenate([dup_a, dup_g], axis=1)
    g["conv_w"] = jnp.concatenate([sums_a[0:3], sums_g[0:3]], axis=1)
    g["conv_b"] = jnp.concatenate([sums_a[3], sums_g[3]], axis=0)
    g["w_up"] = _mm(s["hn2"], dup, ta=True, name=f"g_w_up_{tag}")
    dhn2 = _mm(dup, w["w_up"], tb=True, name=f"d_hn2_{tag}")
    dh1, dh1b, g["norm_ffn"] = _norm_bwd(dhn2, s["h1"], s["r2"], norm_ffn, width=d.d, cidx=0, dres=dh2,
                                         name=f"norm_ffn_bwd_{tag}")
    dmerged = _mm(dh1b, w["w_o"], tb=True, name=f"d_merged_{tag}")
    g["w_o"] = _mm(s["merged"], dh1b, ta=True, name=f"g_w_o_{tag}")
    db_sb, db_mla, dg_sb, dg_mla = _gate_bwd(dmerged, s["pb"], s["b_sb"], s["b_mla"], d, f"gate_bwd_{tag}")
    do_sb = _mm(db_sb, w["w_sb_out"], tb=True, name=f"d_o_sb_{tag}")
    g["w_sb_out"] = _mm(s["o_sb"], db_sb, ta=True, name=f"g_w_sb_out_{tag}")
    do_mla = _mm(db_mla, w["w_mla_out"], tb=True, name=f"d_o_mla_{tag}")
    g["w_mla_out"] = _mm(s["o_mla"], db_mla, ta=True, name=f"g_w_mla_out_{tag}")
    dq_sb, dk_sb, dv_sb = _sb_bwd(s["pa"], do_sb, s["carry"], d, f"sb_bwd_{tag}")
    (dqm, dkn, dv, dkr), rode = _mla_bwd(s["qm"], s["kv"], s["kr"], s["o_mla"], do_mla, s["lse"], d, f"mla_bwd_{tag}",
                                         ride=ride)
    dqraw, dkr128 = _mla_prep_bwd(dqm, dkr, ctab, stab, d, f"mla_prep_bwd_{tag}")
    dkv = jnp.concatenate([dkn.astype(BF16), dv.astype(BF16)], axis=1)
    dcqn = _mm(dqraw, w["w_uq"], tb=True, name=f"d_cq_{tag}")
    g["w_uq"] = _mm(s["cqn"], dqraw, ta=True, name=f"g_w_uq_{tag}")
    dckn = _mm(dkv, w["w_ukv"], tb=True, name=f"d_ckv_{tag}")
    g["w_ukv"] = _mm(s["ckn"], dkv, ta=True, name=f"g_w_ukv_{tag}")
    dcq, g["q_norm"] = _norm_bwd(dcqn, s["pb"], s["rq"], q_norm, width=d.q_lora, cidx=0, out_dtype=BF16, name=f"norm_q_bwd_{tag}")
    dckv, g["kv_norm"] = _norm_bwd(dckn, s["pb"], s["rk"], kv_norm, width=d.kv_lora, cidx=d.q_lora // d.kv_lora,
                                   out_dtype=BF16, name=f"norm_kv_bwd_{tag}")
    zpad = jnp.zeros((d.lp, d.g_off - d.kr_off - LANES), BF16)
    dproj = jnp.concatenate([dq_sb, dk_sb.astype(BF16), dv_sb.astype(BF16), dcq, dckv, dkr128, zpad, dg_sb, dg_mla], axis=1)
    g["w_in"] = _mm(s["hn"], dproj, ta=True, name=f"g_w_in_{tag}")
    dhn = _mm(dproj, w["w_in"], tb=True, name=f"d_hn_{tag}")
    dh, dhb, g["norm_mix"] = _norm_bwd(dhn, s["h"], s["r1"], norm_mix, width=d.d, cidx=0, dres=dh1,
                                       name=f"norm_mix_bwd_{tag}")
    return dh, dhb, g, rode


def _step(d, x, p, m, v, loss_target):
    x = x.reshape(d.seq, d.d)
    target = loss_target.reshape(d.seq, d.d)

    blocks = [_pad_block(p[n], n).astype(BF16 if n in BIG else F32) for n in SHARDED]
    gathered = _gather_two_level(blocks, "gather_weights")
    full = {n: _full_from_slots(g_, n, p[n].shape[-1]) for n, g_ in zip(SHARDED, gathered)}

    pos = jnp.arange(d.lp, dtype=F32) - d.pad
    half = d.rope // 2
    freqs = ROPE_THETA ** (-jnp.arange(half, dtype=F32) / half)
    ang = pos[:, None] * freqs[None, :]
    cos, sin = jnp.cos(ang), jnp.sin(ang)
    zero = jnp.zeros((d.lp, LANES - d.rope), F32)
    ctab = jnp.concatenate([cos, cos, zero], axis=1)
    stab = jnp.concatenate([-sin, sin, zero], axis=1)

    h = jnp.concatenate([jnp.zeros((d.pad, d.d), F32), full["meta_tokens"], x], axis=0)
    ws = [_prime_weights(full, l, d) for l in range(d.depth)]
    saved, add = [], None
    for l in range(d.depth):
        s, add = _layer_fwd(h, add, ws[l], p["norm_mix"][l], p["q_norm"][l], p["kv_norm"][l], p["norm_ffn"][l],
                            full["conv_w"][l], p["conv_b"][l].reshape(1, -1), ctab, stab, d, f"l{l}")
        saved.append(s)
        h = s["h1"]
    dh, dhb, loss_part, g_final = _head(h, add, target, p["final_norm"], d, "head")

    my_c = lax.axis_index("c")

    def pair_level(names, gfull, tag):
        gsend = [_slots_from_full(gfull[n].astype(BF16) if n in BIG else gfull[n], n) for n in names]
        theirs = _pair_exchange(gsend, f"scatter_grads_pair_{tag}")
        mine = [lax.dynamic_index_in_dim(g_.reshape((4, 2) + g_.shape[1:]), my_c, axis=1, keepdims=False) for g_ in gsend]
        return [_pair_sum(a_, b_, f"pair_sum_{n}_{tag}") for n, a_, b_ in zip(names, mine, theirs)]

    grads = [None] * d.depth
    recv_big = [None] * d.depth
    ride = []
    for l in reversed(range(d.depth)):
        dh, dhb, g, rode = _layer_bwd(dh, dhb, saved[l], ws[l], p["norm_mix"][l], p["q_norm"][l], p["kv_norm"][l],
                                      p["norm_ffn"][l], full["conv_w"][l], p["conv_b"][l].reshape(1, -1), ctab, stab,
                                      d, f"l{l}", ride=ride)
        if ride:
            recv_big[l + 1] = rode
        grads[l] = _unprime_grads(g, d)
        if l > 0:
            ride = pair_level(BIG, {n: grads[l][n][None] for n in BIG}, f"l{l}")
    grad_x = dh[d.first_tok:].reshape(1, d.seq, d.d)

    gfull = {n: grads[0][n][None] for n in BIG}
    gfull["conv_w"] = jnp.stack([grads[l]["conv_w"] for l in range(d.depth)])
    gfull["meta_tokens"] = dh[d.pad:d.first_tok]
    last = list(_chip_exchange(pair_level(SHARDED, gfull, "l0"), "scatter_grads_chips"))
    recv_big[0] = last[:len(BIG)]
    grecv = [jnp.concatenate([recv_big[l][k] for l in range(d.depth)], axis=1) for k in range(len(BIG))] + last[len(BIG):]
    outs_sh = {n: _adamw(r_, _pad_block(p[n], n), _pad_block(m[n], n), _pad_block(v[n], n), f"adamw_{n}")
               for n, r_ in zip(SHARDED, grecv)}

    grep = {n: jnp.stack([grads[l][n].reshape(-1) for l in range(d.depth)]) for n in REPL if n != "final_norm"}
    grep["final_norm"] = g_final.reshape(-1)
    rflat = jnp.concatenate([grep[n].reshape(-1) for n in REPL] + [loss_part[0, 0:1]])
    (rparts,) = _exchange([_pack_rows(rflat, 8)], scatter=False, name="gather_small_grads")
    packr = lambda t: _pack_rows(jnp.concatenate([t[n].reshape(-1) for n in REPL] + [jnp.zeros((1,), F32)]), 8)
    outs_rp = _adamw(rparts, packr(p), packr(m), packr(v), "adamw_replicated")

    def unpack(flat, names, extra=0):
        res, off = {}, 0
        flat = flat.reshape(-1)
        for n in names:
            res[n] = flat[off:off + p[n].size].reshape(p[n].shape)
            off += p[n].size
        return res, flat[off:off + extra]

    results = []
    loss = None
    for k in range(4):
        sh = {n: outs_sh[n][k][..., :p[n].shape[-1]] for n in SHARDED}
        rp, tail = unpack(outs_rp[k], REPL, 1)
        if k == 0:
            loss = tail[0]
        results.append({**sh, **rp})
    return loss, grad_x, results


WEIGHTS = ["meta_tokens", "norm_mix", "w_in", "q_norm", "w_uq", "kv_norm", "w_ukv", "w_sb_out", "w_mla_out", "w_o",
           "norm_ffn", "w_up", "conv_w", "conv_b", "w_down", "final_norm"]


def _run(d, x, weights, loss_target, moments_m, moments_v):
    p = dict(zip(WEIGHTS, weights))
    m = dict(zip(WEIGHTS, moments_m))
    v = dict(zip(WEIGHTS, moments_v))
    loss, grad_x, res = _step(d, x, p, m, v, loss_target)
    out = [loss, grad_x]
    for k in range(4):
        out += [res[k][n] for n in WEIGHTS]
    return tuple(out)


def kernel(x, meta_tokens, norm_mix, w_in, q_norm, w_uq, kv_norm, w_ukv, w_sb_out, w_mla_out, w_o, norm_ffn, w_up, conv_w, conv_b, w_down, final_norm, loss_target, m_meta_tokens, m_norm_mix, m_w_in, m_q_norm, m_w_uq, m_kv_norm, m_w_ukv, m_w_sb_out, m_w_mla_out, m_w_o, m_norm_ffn, m_w_up, m_conv_w, m_conv_b, m_w_down, m_final_norm, v_meta_tokens, v_norm_mix, v_w_in, v_q_norm, v_w_uq, v_kv_norm, v_w_ukv, v_w_sb_out, v_w_mla_out, v_w_o, v_norm_ffn, v_w_up, v_conv_w, v_conv_b, v_w_down, v_final_norm):
    weights = [meta_tokens, norm_mix, w_in, q_norm, w_uq, kv_norm, w_ukv, w_sb_out, w_mla_out, w_o, norm_ffn, w_up,
               conv_w, conv_b, w_down, final_norm]
    ms = [m_meta_tokens, m_norm_mix, m_w_in, m_q_norm, m_w_uq, m_kv_norm, m_w_ukv, m_w_sb_out, m_w_mla_out, m_w_o,
          m_norm_ffn, m_w_up, m_conv_w, m_conv_b, m_w_down, m_final_norm]
    vs = [v_meta_tokens, v_norm_mix, v_w_in, v_q_norm, v_w_uq, v_kv_norm, v_w_ukv, v_w_sb_out, v_w_mla_out, v_w_o,
          v_norm_ffn, v_w_up, v_conv_w, v_conv_b, v_w_down, v_final_norm]
    return _run(PROD, x, weights, loss_target, ms, vs)
```

```python
import jax
import jax.numpy as jnp
from jax import lax
from jax.experimental import pallas as pl
from jax.experimental.pallas import tpu as pltpu

F32 = jnp.float32
BF16 = jnp.bfloat16

EPS = 1e-6
ROPE_THETA = 10000.0
ADAM_LR = 0.001
ADAM_B1 = 0.9
ADAM_B2 = 0.999
ADAM_EPS = 1e-08
ADAM_WD = 0.01
ADAM_STEP = 10
NEG = -1e30
DEAD = -110.0
LANES = 128
PACK_W = 1024
ADAMW_TILE_ELEMS = 256 * 1024
V7X_VMEM_LIMIT = 48 * 1024 * 1024
V7X_VMEM_LIMIT_BIG = 58 * 1024 * 1024
MESH_AXES = ("x", "y", "c")
N_DEV = 8
FLIPS = [(0, 0, 1), (0, 1, 0), (0, 1, 1), (1, 0, 0), (1, 0, 1), (1, 1, 0), (1, 1, 1)]


class _Dims:
    def __init__(self, d_model=2048, seq=8192, depth=2, n_meta=16, block=128, sb_heads=8, hd=128,
                 mla_heads=8, q_lora=512, kv_lora=256, nope=128, rope=64, vdim=128, d_ff=5632, tq=None):
        self.d, self.seq, self.depth, self.n_meta, self.block = d_model, seq, depth, n_meta, block
        self.sb_heads, self.hd, self.mla_heads = sb_heads, hd, mla_heads
        self.q_lora, self.kv_lora, self.nope, self.rope, self.vdim, self.f = q_lora, kv_lora, nope, rope, vdim, d_ff
        assert hd == LANES and nope == LANES and vdim == LANES and 2 * rope == LANES
        self.pad = block - n_meta
        self.lp = self.pad + n_meta + seq
        self.first_tok = self.pad + n_meta
        assert self.first_tok == block and self.lp % block == 0 and self.lp // block < LANES
        self.tq = tq or next(t for t in (640, 512, 256, 128) if self.lp % t == 0)
        assert self.tq % block == 0 and self.lp % self.tq == 0
        self.sbw = sb_heads * hd
        self.mlaw = mla_heads * vdim
        self.wa = 3 * self.sbw
        self.d_in = 3 * self.sbw + q_lora + kv_lora + rope + 2 * d_model
        self.tg = min(1024, d_model)
        self.kr_off = q_lora + kv_lora
        raw = self.kr_off + LANES
        self.g_off = -(-raw // self.tg) * self.tg
        self.wb = self.g_off + 2 * d_model
        self.qw = mla_heads * 2 * LANES


PROD = _Dims()


def _pick(n, prefs):
    for p in prefs:
        if n % p == 0:
            return p
    return n


def _cparams(sem, limit=V7X_VMEM_LIMIT):
    return pltpu.CompilerParams(dimension_semantics=sem, vmem_limit_bytes=limit)


def _mm(a, b, *, ta=False, tb=False, out_dtype=F32, name):
    if ta:
        kdim, m = a.shape
    else:
        m, kdim = a.shape
    if tb:
        n, k2 = b.shape
    else:
        k2, n = b.shape
    assert kdim == k2, (a.shape, b.shape, ta, tb)
    tm = _pick(m, (640, 512, 256, 128))
    tn = _pick(n, (1024, 512, 384, 256, 128))
    tk = _pick(kdim, (2816, 2048, 1664, 1408, 1024, 640, 512, 256, 128))
    nk = kdim // tk
    dn = (((0 if ta else 1,), (1 if tb else 0,)), ((), ()))

    def dot(a_ref, b_ref):
        return lax.dot_general(a_ref[...].astype(BF16), b_ref[...].astype(BF16), dn, preferred_element_type=F32)

    def body_one(a_ref, b_ref, o_ref):
        o_ref[...] = dot(a_ref, b_ref).astype(out_dtype)

    def body_acc(a_ref, b_ref, o_ref, acc_ref):
        k = pl.program_id(2)

        @pl.when(k == 0)
        def _():
            acc_ref[...] = dot(a_ref, b_ref)

        @pl.when((k > 0) & (k < nk - 1))
        def _():
            acc_ref[...] += dot(a_ref, b_ref)

        @pl.when(k == nk - 1)
        def _():
            o_ref[...] = (acc_ref[...] + dot(a_ref, b_ref)).astype(out_dtype)

    a_spec = pl.BlockSpec((tk, tm), lambda i, j, k: (k, i)) if ta else pl.BlockSpec((tm, tk), lambda i, j, k: (i, k))
    b_spec = pl.BlockSpec((tn, tk), lambda i, j, k: (j, k)) if tb else pl.BlockSpec((tk, tn), lambda i, j, k: (k, j))
    return pl.pallas_call(
        body_one if nk == 1 else body_acc, name=name, grid=(m // tm, n // tn, nk), in_specs=[a_spec, b_spec],
        out_specs=pl.BlockSpec((tm, tn), lambda i, j, k: (i, j)),
        out_shape=jax.ShapeDtypeStruct((m, n), out_dtype),
        scratch_shapes=[] if nk == 1 else [pltpu.VMEM((tm, tn), F32)],
        compiler_params=_cparams(("parallel", "parallel", "arbitrary")),
    )(a, b)


def _norm_fwd(x, g, *, width, cidx, add=None, name):
    rows = x.shape[0]
    tr = _pick(rows, (256, 128))
    has_add = add is not None

    def body(*refs):
        if has_add:
            x_ref, a_ref, g_ref, xn_ref, y_ref, r_ref = refs
            xv = x_ref[...] + a_ref[...]
            xn_ref[...] = xv
        else:
            x_ref, g_ref, y_ref, r_ref = refs
            xv = x_ref[...]
        r = lax.rsqrt(jnp.mean(xv * xv, axis=1, keepdims=True) + EPS)
        y_ref[...] = (xv * r * g_ref[...]).astype(BF16)
        r_ref[...] = r

    blk = pl.BlockSpec((tr, width), lambda i: (i, 0))
    in_specs = [pl.BlockSpec((tr, width), lambda i: (i, cidx))]
    args = [x]
    if has_add:
        in_specs.append(blk)
        args.append(add)
    in_specs.append(pl.BlockSpec((1, width), lambda i: (0, 0)))
    args.append(g.reshape(1, width))
    out_specs = [blk, pl.BlockSpec((tr, 1), lambda i: (i, 0))]
    out_shape = [jax.ShapeDtypeStruct((rows, width), BF16), jax.ShapeDtypeStruct((rows, 1), F32)]
    if has_add:
        out_specs.insert(0, blk)
        out_shape.insert(0, jax.ShapeDtypeStruct((rows, width), F32))
    return pl.pallas_call(body, name=name, grid=(rows // tr,), in_specs=in_specs, out_specs=out_specs,
                          out_shape=out_shape, compiler_params=_cparams(("parallel",)))(*args)


def _norm_bwd(dy, x, r, g, *, width, cidx, dres=None, out_dtype=F32, name):
    rows = x.shape[0]
    tr = _pick(rows, (256, 128))
    has_res = dres is not None

    def body(*refs):
        if has_res:
            dy_ref, x_ref, r_ref, g_ref, dr_ref, dx_ref, dxb_ref, dg_ref = refs
        else:
            dy_ref, x_ref, r_ref, g_ref, dx_ref, dg_ref = refs
        i = pl.program_id(0)

        @pl.when(i == 0)
        def _():
            dg_ref[...] = jnp.zeros_like(dg_ref)

        dyv, xv, rv = dy_ref[...], x_ref[...], r_ref[...]
        gy = dyv * g_ref[...]
        c = jnp.mean(gy * xv, axis=1, keepdims=True)
        dx = rv * gy - xv * (rv * rv * rv) * c
        if has_res:
            dx = dx + dr_ref[...]
            dxb_ref[...] = dx.astype(BF16)
        dx_ref[...] = dx.astype(out_dtype)
        dg_ref[...] += jnp.sum(dyv * xv * rv, axis=0, keepdims=True)

    blk = pl.BlockSpec((tr, width), lambda i: (i, 0))
    in_specs = [blk, pl.BlockSpec((tr, width), lambda i: (i, cidx)), pl.BlockSpec((tr, 1), lambda i: (i, 0)),
                pl.BlockSpec((1, width), lambda i: (0, 0))]
    args = [dy, x, r, g.reshape(1, width)]
    out_specs = [blk, pl.BlockSpec((1, width), lambda i: (0, 0))]
    out_shape = [jax.ShapeDtypeStruct((rows, width), out_dtype), jax.ShapeDtypeStruct((1, width), F32)]
    if has_res:
        in_specs.append(blk)
        args.append(dres)
        out_specs.insert(1, blk)
        out_shape.insert(1, jax.ShapeDtypeStruct((rows, width), BF16))
    return pl.pallas_call(
        body, name=name, grid=(rows // tr,), in_specs=in_specs, out_specs=out_specs, out_shape=out_shape,
        compiler_params=_cparams(("arbitrary",)))(*args)


def _split3(x):
    h1 = x.astype(BF16)
    r1 = x - h1.astype(F32)
    h2 = r1.astype(BF16)
    h3 = (r1 - h2.astype(F32)).astype(BF16)
    return h1, h2, h3


def _cum(x, tri):
    h1, h2, h3 = _split3(x)
    dot = lambda h: jnp.dot(h, tri, preferred_element_type=F32)
    return dot(h1) + dot(h2) + dot(h3)


def _dot_nt(a, b):
    return lax.dot_general(a, b, (((1,), (1,)), ((), ())), preferred_element_type=F32)


def _dot_tn(a, b):
    return lax.dot_general(a, b, (((0,), (0,)), ((), ())), preferred_element_type=F32)


def _sb_geometry(tq, t):
    assert t & (t - 1) == 0
    ri = lax.broadcasted_iota(jnp.int32, (tq, t), 0)
    return jnp.bitwise_and(ri, t - 1), jnp.right_shift(ri, t.bit_length() - 1), lax.broadcasted_iota(jnp.int32, (tq, t), 1)


def _sb_key_blocks(ref, base, r, t, kind):
    if kind == "low":
        return [ref[pl.ds(pl.multiple_of(jnp.maximum(base + g, 0) * t, t), t), :] for g in range(r)]
    slab = ref[pl.ds(pl.multiple_of(base * t, t), r * t), :]
    return [slab[g * t:(g + 1) * t, :] for g in range(r)]


def _sb_mask(geo, base, s, t, pad, kind):
    rowl, grp, col = geo
    if kind == "plain":
        return None
    if kind == "first":
        return col < rowl
    blk = base + grp
    causal = col < rowl + jnp.where(s > 0, t, 0)
    return (blk >= 0) & (blk * t + col >= pad) & causal


def _sb_fwd(qkv, d, name):
    nh, hd, lp, tq, t = d.sb_heads, d.hd, d.lp, d.tq, d.block
    r = tq // t
    scale = hd ** -0.5
    pad = d.pad

    def body(q_ref, k_ref, v_ref, o_ref, c_ref):
        i = pl.program_id(1)
        qs = [q_ref[g * t:(g + 1) * t, :] for g in range(r)]
        geo = _sb_geometry(tq, t)
        tri = (lax.broadcasted_iota(jnp.int32, (t, t), 0)
               > lax.broadcasted_iota(jnp.int32, (t, t), 1)).astype(BF16)
        lane = lax.broadcasted_iota(jnp.int32, (tq, LANES), 1)

        c_ref[...] = jnp.zeros_like(c_ref)

        def make_step(kind):
            def step(s, carry):
                acc, run = carry
                ks = _sb_key_blocks(k_ref, i * r - s, r, t, kind)
                vs = _sb_key_blocks(v_ref, i * r - s, r, t, kind)
                z = jnp.concatenate([_dot_nt(qs[g], ks[g]) for g in range(r)], axis=0) * scale
                e = jnp.exp(-jnp.abs(z))
                sp = jnp.maximum(z, 0.0) + jnp.log(1.0 + e)
                mask = _sb_mask(geo, i * r - s, s, t, pad, kind)
                spm = sp if mask is None else jnp.where(mask, sp, 0.0)
                w = jnp.exp(z - sp - _cum(spm, tri) + run)
                if mask is not None:
                    w = jnp.where(mask, w, 0.0)
                wb = w.astype(BF16)
                acc = acc + jnp.concatenate(
                    [jnp.dot(wb[g * t:(g + 1) * t, :], vs[g], preferred_element_type=F32) for g in range(r)], axis=0)
                c_ref[...] = jnp.where(lane == s, run, c_ref[...])
                run = run - jnp.sum(spm, axis=1, keepdims=True)
                return acc, run
            return step

        first, plain, low = make_step("first"), make_step("plain"), make_step("low")

        def alive(run):
            return (jnp.max(run) >= DEAD).astype(jnp.int32)

        def run_while(step, s, end, live, acc, run):
            def wbody(st):
                s, _, acc, run = st
                acc, run = step(s, (acc, run))
                return s + 1, alive(run), acc, run
            return lax.while_loop(lambda st: (st[0] < end) & (st[1] > 0), wbody, (s, live, acc, run))

        init = (jnp.zeros((tq, hd), F32), jnp.zeros((tq, 1), F32))
        carry = lax.fori_loop(0, jnp.minimum(i, 1), lambda _, cr: first(0, cr), init)
        acc, run = lax.fori_loop(0, 1 - jnp.minimum(i, 1), lambda _, cr: low(0, cr), carry)
        s, live, acc, run = run_while(plain, 1, i * r, alive(run), acc, run)
        end_low = jnp.where(s >= jnp.maximum(i * r, 1), (i + 1) * r, s)
        s, live, acc, run = run_while(low, s, end_low, live, acc, run)
        o_ref[...] = acc
        c_ref[...] = jnp.where(lane == LANES - 1, s.astype(F32), c_ref[...])

    return pl.pallas_call(
        body, name=name, grid=(nh, lp // tq),
        in_specs=[pl.BlockSpec((tq, hd), lambda h, i: (i, h)),
                  pl.BlockSpec((lp, hd), lambda h, i: (0, nh + h)),
                  pl.BlockSpec((lp, hd), lambda h, i: (0, 2 * nh + h))],
        out_specs=[pl.BlockSpec((tq, hd), lambda h, i: (i, h)),
                   pl.BlockSpec((None, tq, LANES), lambda h, i: (h, i, 0))],
        out_shape=[jax.ShapeDtypeStruct((lp, nh * hd), F32), jax.ShapeDtypeStruct((nh, lp, LANES), F32)],
        compiler_params=_cparams(("parallel", "arbitrary")),
    )(qkv, qkv, qkv)


def _sb_bwd(qkv, do, carry, d, name):
    nh, hd, lp, tq, t = d.sb_heads, d.hd, d.lp, d.tq, d.block
    r = tq // t
    scale = hd ** -0.5
    pad = d.pad

    def body(q_ref, k_ref, v_ref, do_ref, c_ref, dq_ref, dk_ref, dv_ref):
        i = pl.program_id(1)

        @pl.when(i == 0)
        def _():
            dk_ref[...] = jnp.zeros_like(dk_ref)
            dv_ref[...] = jnp.zeros_like(dv_ref)

        rows = lambda x, g: x[g * t:(g + 1) * t, :]
        qs = [q_ref[g * t:(g + 1) * t, :] for g in range(r)]
        dobs = [do_ref[g * t:(g + 1) * t, :].astype(BF16) for g in range(r)]
        geo = _sb_geometry(tq, t)
        ri = lax.broadcasted_iota(jnp.int32, (t, t), 0)
        ci = lax.broadcasted_iota(jnp.int32, (t, t), 1)
        tri_suf = (ri > ci).astype(BF16)
        tri_pre = (ri < ci).astype(BF16)
        lane = lax.broadcasted_iota(jnp.int32, (tq, LANES), 1)

        def make_step(kind):
            def step(s, carry):
                dq, pc = carry
                base = i * r - s
                ks = _sb_key_blocks(k_ref, base, r, t, kind)
                vs = _sb_key_blocks(v_ref, base, r, t, kind)
                z = jnp.concatenate([_dot_nt(qs[g], ks[g]) for g in range(r)], axis=0) * scale
                e = jnp.exp(-jnp.abs(z))
                sp = jnp.maximum(z, 0.0) + jnp.log(1.0 + e)
                mask = _sb_mask(geo, base, s, t, pad, kind)
                spm = sp if mask is None else jnp.where(mask, sp, 0.0)
                run = jnp.sum(jnp.where(lane == s, c_ref[...], 0.0), axis=1, keepdims=True)
                w = jnp.exp(z - sp - _cum(spm, tri_suf) + run)
                if mask is not None:
                    w = jnp.where(mask, w, 0.0)
                gw = w * jnp.concatenate([_dot_nt(dobs[g], vs[g]) for g in range(r)], axis=0)
                p = _cum(gw, tri_pre) + pc
                inv = 1.0 / (1.0 + e)
                sig = jnp.where(z >= 0.0, inv, e * inv)
                dz = (gw * (1.0 - sig) - sig * p) * scale
                if mask is not None:
                    dz = jnp.where(mask, dz, 0.0)
                dzb, wb = dz.astype(BF16), w.astype(BF16)
                dq = dq + jnp.concatenate(
                    [jnp.dot(rows(dzb, g), ks[g], preferred_element_type=F32) for g in range(r)], axis=0)
                dks = [_dot_tn(rows(dzb, g), qs[g]) for g in range(r)]
                dvs = [_dot_tn(rows(wb, g), dobs[g]) for g in range(r)]
                if kind == "low":
                    for g in range(r):
                        at = pl.ds(pl.multiple_of(jnp.maximum(base + g, 0) * t, t), t)
                        dk_ref[at, :] += dks[g]
                        dv_ref[at, :] += dvs[g]
                else:
                    at = pl.ds(pl.multiple_of(base * t, t), tq)
                    dk_ref[at, :] += jnp.concatenate(dks, axis=0)
                    dv_ref[at, :] += jnp.concatenate(dvs, axis=0)
                pc = pc + jnp.sum(gw, axis=1, keepdims=True)
                return dq, pc
            return step

        first, plain, low = make_step("first"), make_step("plain"), make_step("low")
        nsteps = jnp.max(jnp.where(lane == LANES - 1, c_ref[...], 0.0)).astype(jnp.int32)
        low_from = jnp.maximum(i * r, 1)
        plain_end = jnp.minimum(nsteps, low_from)
        carry = (jnp.zeros((tq, hd), F32), jnp.zeros((tq, 1), F32))
        carry = lax.fori_loop(0, jnp.maximum(nsteps - low_from, 0), lambda jj, cr: low(nsteps - 1 - jj, cr), carry)
        carry = lax.fori_loop(0, plain_end - 1, lambda jj, cr: plain(plain_end - 1 - jj, cr), carry)
        carry = lax.fori_loop(0, jnp.minimum(i, 1), lambda _, cr: first(0, cr), carry)
        dq, _ = lax.fori_loop(0, 1 - jnp.minimum(i, 1), lambda _, cr: low(0, cr), carry)
        dq_ref[...] = dq.astype(BF16)

    w3 = nh * hd
    return pl.pallas_call(
        body, name=name, grid=(nh, lp // tq),
        in_specs=[pl.BlockSpec((tq, hd), lambda h, i: (i, h)),
                  pl.BlockSpec((lp, hd), lambda h, i: (0, nh + h)),
                  pl.BlockSpec((lp, hd), lambda h, i: (0, 2 * nh + h)),
                  pl.BlockSpec((tq, hd), lambda h, i: (i, h)),
                  pl.BlockSpec((None, tq, LANES), lambda h, i: (h, i, 0))],
        out_specs=[pl.BlockSpec((tq, hd), lambda h, i: (i, h)),
                   pl.BlockSpec((lp, hd), lambda h, i: (0, h)),
                   pl.BlockSpec((lp, hd), lambda h, i: (0, h))],
        out_shape=[jax.ShapeDtypeStruct((lp, w3), BF16), jax.ShapeDtypeStruct((lp, w3), F32),
                   jax.ShapeDtypeStruct((lp, w3), F32)],
        compiler_params=_cparams(("arbitrary", "arbitrary")),
    )(qkv, qkv, qkv, do, carry)


def _mla_prep_fwd(qraw, projb, ctab, stab, d, name):
    lp, nh = d.lp, d.mla_heads
    tr = _pick(lp, (256, 128))
    kidx = d.kr_off // LANES

    def rope(u, c, s):
        return u * c + pltpu.roll(u, LANES // 2, 1) * s

    def body(q_ref, k_ref, c_ref, s_ref, qm_ref, kr_ref):
        c, s = c_ref[...], s_ref[...]
        for h in range(nh):
            base = 2 * LANES * h
            qm_ref[:, base:base + LANES] = q_ref[:, base:base + LANES].astype(BF16)
            qm_ref[:, base + LANES:base + 2 * LANES] = rope(q_ref[:, base + LANES:base + 2 * LANES], c, s).astype(BF16)
        kr_ref[...] = rope(k_ref[...], c, s).astype(BF16)

    tab = pl.BlockSpec((tr, LANES), lambda i: (i, 0))
    return pl.pallas_call(
        body, name=name, grid=(lp // tr,),
        in_specs=[pl.BlockSpec((tr, d.qw), lambda i: (i, 0)), pl.BlockSpec((tr, LANES), lambda i: (i, kidx)), tab, tab],
        out_specs=[pl.BlockSpec((tr, d.qw), lambda i: (i, 0)), tab],
        out_shape=[jax.ShapeDtypeStruct((lp, d.qw), BF16), jax.ShapeDtypeStruct((lp, LANES), BF16)],
        compiler_params=_cparams(("parallel",)))(qraw, projb, ctab, stab)


def _mla_prep_bwd(dqm, dkr, ctab, stab, d, name):
    lp, nh = d.lp, d.mla_heads
    tr = _pick(lp, (256, 128))

    def unrope(g, c, s):
        return g * c + pltpu.roll(g * s, LANES // 2, 1)

    def body(dq_ref, dk_ref, c_ref, s_ref, o_ref, ok_ref):
        c, s = c_ref[...], s_ref[...]
        for h in range(nh):
            base = 2 * LANES * h
            o_ref[:, base:base + LANES] = dq_ref[:, base:base + LANES].astype(BF16)
            o_ref[:, base + LANES:base + 2 * LANES] = unrope(dq_ref[:, base + LANES:base + 2 * LANES], c, s).astype(BF16)
        ok_ref[...] = unrope(dk_ref[...], c, s).astype(BF16)

    tab = pl.BlockSpec((tr, LANES), lambda i: (i, 0))
    wide = pl.BlockSpec((tr, d.qw), lambda i: (i, 0))
    return pl.pallas_call(
        body, name=name, grid=(lp // tr,), in_specs=[wide, tab, tab, tab], out_specs=[wide, tab],
        out_shape=[jax.ShapeDtypeStruct((lp, d.qw), BF16), jax.ShapeDtypeStruct((lp, LANES), BF16)],
        compiler_params=_cparams(("parallel",)))(dqm, dkr, ctab, stab)


def _mla_fwd(qm, kv, kr, d, name, ride=()):
    nh, lp, t = d.mla_heads, d.lp, d.tq
    scale = (d.nope + d.rope) ** -0.5
    pad = d.pad
    nr = len(ride)

    def body(*refs):
        q_ref, kn_ref, v_ref, kr_ref = refs[:4]
        o_ref, lse_ref = refs[4 + nr:6 + nr]
        h = pl.program_id(0)
        i = pl.program_id(1)
        if nr:
            ride_refs = (refs[4:4 + nr], refs[6 + nr:6 + 2 * nr]) + tuple(refs[6 + 2 * nr:])

            @pl.when((h == 0) & (i == 0))
            def _():
                _Gather(*ride_refs).start()

        q = q_ref[...]
        row = i * t + lax.broadcasted_iota(jnp.int32, (t, t), 0)
        colb = lax.broadcasted_iota(jnp.int32, (t, t), 1)

        def make_step(masked):
            def step(j, carry):
                acc, m, l = carry
                off = pl.multiple_of(j * t, t)
                kc = jnp.concatenate([kn_ref[pl.ds(off, t), :], kr_ref[pl.ds(off, t), :]], axis=1)
                s = _dot_nt(q, kc) * scale
                if masked:
                    col = j * t + colb
                    s = jnp.where((col <= row) & (col >= pad), s, NEG)
                m_new = jnp.maximum(m, jnp.max(s, axis=1, keepdims=True))
                alpha = jnp.exp(m - m_new)
                p = jnp.exp(s - m_new)
                l = alpha * l + jnp.sum(p, axis=1, keepdims=True)
                acc = alpha * acc + jnp.dot(p.astype(BF16), v_ref[pl.ds(off, t), :], preferred_element_type=F32)
                return acc, m_new, l
            return step

        masked, plain = make_step(True), make_step(False)
        carry = masked(0, (jnp.zeros((t, LANES), F32), jnp.full((t, 1), NEG, F32), jnp.zeros((t, 1), F32)))
        carry = lax.fori_loop(1, i, plain, carry)
        acc, m, l = lax.fori_loop(i, i + jnp.minimum(i, 1), masked, carry)
        rowv = i * t + lax.broadcasted_iota(jnp.int32, (t, LANES), 0)
        o_ref[...] = jnp.where(rowv >= pad, acc / l, 0.0)
        lse_ref[...] = m + jnp.log(l)
        if nr:
            @pl.when((h == nh - 1) & (i == lp // t - 1))
            def _():
                _Gather(*ride_refs).finish()

    hbm = pl.BlockSpec(memory_space=pltpu.HBM)
    res = pl.pallas_call(
        body, name=name, grid=(nh, lp // t),
        in_specs=[pl.BlockSpec((t, 2 * LANES), lambda h, i: (i, h)),
                  pl.BlockSpec((lp, LANES), lambda h, i: (0, h)),
                  pl.BlockSpec((lp, LANES), lambda h, i: (0, nh + h)),
                  pl.BlockSpec((lp, LANES), lambda h, i: (0, 0))] + [hbm] * nr,
        out_specs=[pl.BlockSpec((t, LANES), lambda h, i: (i, h)),
                   pl.BlockSpec((None, t, 1), lambda h, i: (h, i, 0))] + [hbm] * nr,
        out_shape=[jax.ShapeDtypeStruct((lp, nh * LANES), F32), jax.ShapeDtypeStruct((nh, lp, 1), F32)]
                  + _gather_out_shapes(ride),
        scratch_shapes=_gather_sems(nr) if nr else [],
        compiler_params=_cparams(("arbitrary", "arbitrary")),
    )(qm, kv, kv, kr, *ride)
    return res[:2], list(res[2:])


def _mla_bwd(qm, kv, kr, o, do, lse, d, name, ride=()):
    nh, lp, t = d.mla_heads, d.lp, d.tq
    scale = (d.nope + d.rope) ** -0.5
    pad = d.pad
    nr = len(ride)

    def body(*refs):
        q_ref, kn_ref, v_ref, kr_ref, o_ref, do_ref, lse_ref = refs[:7]
        dq_ref, dkn_ref, dv_ref, dkr_ref = refs[7 + nr:11 + nr]
        h = pl.program_id(0)
        i = pl.program_id(1)
        if nr:
            ride_refs = (refs[7:7 + nr], refs[11 + nr:11 + 2 * nr]) + tuple(refs[11 + 2 * nr:])

            @pl.when((h == 0) & (i == 0))
            def _():
                _chip_copies_start(_chip_copies(*ride_refs))

        @pl.when(i == 0)
        def _():
            dkn_ref[...] = jnp.zeros_like(dkn_ref)
            dv_ref[...] = jnp.zeros_like(dv_ref)

        @pl.when((i == 0) & (h == 0))
        def _():
            dkr_ref[...] = jnp.zeros_like(dkr_ref)

        q = q_ref[...]
        dof = do_ref[...]
        dob = dof.astype(BF16)
        delta = jnp.sum(dof * o_ref[...], axis=1, keepdims=True)
        lse = lse_ref[...]
        row = i * t + lax.broadcasted_iota(jnp.int32, (t, t), 0)
        colb = lax.broadcasted_iota(jnp.int32, (t, t), 1)

        def make_step(masked):
            def step(j, dq):
                off = pl.multiple_of(j * t, t)
                kc = jnp.concatenate([kn_ref[pl.ds(off, t), :], kr_ref[pl.ds(off, t), :]], axis=1)
                v = v_ref[pl.ds(off, t), :]
                s = _dot_nt(q, kc) * scale
                if masked:
                    col = j * t + colb
                    mask = (col <= row) & (col >= pad)
                    p = jnp.where(mask, jnp.exp(jnp.where(mask, s, NEG) - lse), 0.0)
                else:
                    p = jnp.exp(s - lse)
                dp = _dot_nt(dob, v)
                dsb = (p * (dp - delta) * scale).astype(BF16)
                dq = dq + jnp.dot(dsb, kc, preferred_element_type=F32)
                dkc = _dot_tn(dsb, q)
                dkn_ref[pl.ds(off, t), :] += dkc[:, :LANES]
                dkr_ref[pl.ds(off, t), :] += dkc[:, LANES:]
                dv_ref[pl.ds(off, t), :] += _dot_tn(p.astype(BF16), dob)
                return dq
            return step

        masked, plain = make_step(True), make_step(False)
        dq = masked(0, jnp.zeros((t, 2 * LANES), F32))
        dq = lax.fori_loop(1, i, plain, dq)
        dq_ref[...] = lax.fori_loop(i, i + jnp.minimum(i, 1), masked, dq)
        if nr:
            @pl.when((h == nh - 1) & (i == lp // t - 1))
            def _():
                _chip_copies_wait(_chip_copies(*ride_refs))

    hbm = pl.BlockSpec(memory_space=pltpu.HBM)
    res = pl.pallas_call(
        body, name=name, grid=(nh, lp // t),
        in_specs=[pl.BlockSpec((t, 2 * LANES), lambda h, i: (i, h)),
                  pl.BlockSpec((lp, LANES), lambda h, i: (0, h), pipeline_mode=pl.Buffered(1)),
                  pl.BlockSpec((lp, LANES), lambda h, i: (0, nh + h), pipeline_mode=pl.Buffered(1)),
                  pl.BlockSpec((lp, LANES), lambda h, i: (0, 0), pipeline_mode=pl.Buffered(1)),
                  pl.BlockSpec((t, LANES), lambda h, i: (i, h)),
                  pl.BlockSpec((t, LANES), lambda h, i: (i, h)),
                  pl.BlockSpec((None, t, 1), lambda h, i: (h, i, 0))] + [hbm] * nr,
        out_specs=[pl.BlockSpec((t, 2 * LANES), lambda h, i: (i, h)),
                   pl.BlockSpec((lp, LANES), lambda h, i: (0, h)),
                   pl.BlockSpec((lp, LANES), lambda h, i: (0, h)),
                   pl.BlockSpec((lp, LANES), lambda h, i: (0, 0))] + [hbm] * nr,
        out_shape=[jax.ShapeDtypeStruct((lp, nh * 2 * LANES), F32), jax.ShapeDtypeStruct((lp, nh * LANES), F32),
                   jax.ShapeDtypeStruct((lp, nh * LANES), F32), jax.ShapeDtypeStruct((lp, LANES), F32)]
                  + [jax.ShapeDtypeStruct(x.shape, x.dtype) for x in ride],
        scratch_shapes=_chip_copies_sems(nr) if nr else [],
        compiler_params=_cparams(("arbitrary", "arbitrary"), V7X_VMEM_LIMIT_BIG),
    )(qm, kv, kv, kr, o, do, lse, *ride)
    return res[:4], list(res[4:])


def _sigmoid(x):
    return 1.0 / (1.0 + jnp.exp(-x))


def _gate_fwd(projb, b_sb, b_mla, d, name):
    lp, tg = d.lp, d.tg
    tr = _pick(lp, (256, 128))
    o1, o2 = d.g_off // tg, (d.g_off + d.d) // tg

    def body(g1_ref, g2_ref, b1_ref, b2_ref, o_ref):
        o_ref[...] = (_sigmoid(g1_ref[...]) * b1_ref[...] + _sigmoid(g2_ref[...]) * b2_ref[...]).astype(BF16)

    blk = pl.BlockSpec((tr, tg), lambda i, j: (i, j))
    return pl.pallas_call(
        body, name=name, grid=(lp // tr, d.d // tg),
        in_specs=[pl.BlockSpec((tr, tg), lambda i, j: (i, o1 + j)), pl.BlockSpec((tr, tg), lambda i, j: (i, o2 + j)),
                  blk, blk],
        out_specs=blk, out_shape=jax.ShapeDtypeStruct((lp, d.d), BF16),
        compiler_params=_cparams(("parallel", "parallel")))(projb, projb, b_sb, b_mla)


def _gate_bwd(dm, projb, b_sb, b_mla, d, name):
    lp, tg = d.lp, d.tg
    tr = _pick(lp, (256, 128))
    o1, o2 = d.g_off // tg, (d.g_off + d.d) // tg

    def body(dm_ref, g1_ref, g2_ref, b1_ref, b2_ref, db1_ref, db2_ref, dg1_ref, dg2_ref):
        dmv = dm_ref[...]
        s1, s2 = _sigmoid(g1_ref[...]), _sigmoid(g2_ref[...])
        db1_ref[...] = (dmv * s1).astype(BF16)
        db2_ref[...] = (dmv * s2).astype(BF16)
        dg1_ref[...] = (dmv * b1_ref[...] * s1 * (1.0 - s1)).astype(BF16)
        dg2_ref[...] = (dmv * b2_ref[...] * s2 * (1.0 - s2)).astype(BF16)

    blk = pl.BlockSpec((tr, tg), lambda i, j: (i, j))
    out = jax.ShapeDtypeStruct((lp, d.d), BF16)
    return pl.pallas_call(
        body, name=name, grid=(lp // tr, d.d // tg),
        in_specs=[blk, pl.BlockSpec((tr, tg), lambda i, j: (i, o1 + j)), pl.BlockSpec((tr, tg), lambda i, j: (i, o2 + j)),
                  blk, blk],
        out_specs=[blk] * 4, out_shape=[out] * 4,
        compiler_params=_cparams(("parallel", "parallel")))(dm, projb, projb, b_sb, b_mla)


HALO = 8


def _conv_tiles(d):
    return _pick(d.lp, (640, 512, 256, 128)), _pick(d.f, (512, 256, 128))


def _convglu_fwd(up, cw, cb, d, name):
    lp, f = d.lp, d.f
    tr, tc = _conv_tiles(d)
    nf = f // tc
    hb = tr // HALO
    pad = d.pad

    def body(a_ref, g_ref, pa_ref, pg_ref, wa_ref, wg_ref, ba_ref, bg_ref, o_ref, xa, xg):
        i = pl.program_id(1)
        keep = (i > 0).astype(F32)
        xa[0:HALO, :] = pa_ref[...] * keep
        xg[0:HALO, :] = pg_ref[...] * keep
        xa[HALO:, :] = a_ref[...]
        xg[HALO:, :] = g_ref[...]

        def conv(x, w_ref, b_ref):
            return (b_ref[...] + x[pl.ds(HALO - 2, tr), :] * w_ref[0:1, :] + x[pl.ds(HALO - 1, tr), :] * w_ref[1:2, :]
                    + x[pl.ds(HALO, tr), :] * w_ref[2:3, :])

        ua = conv(xa, wa_ref, ba_ref)
        ug = conv(xg, wg_ref, bg_ref)
        row = i * tr + lax.broadcasted_iota(jnp.int32, (tr, tc), 0)
        o_ref[...] = jnp.where(row >= pad, ua * _sigmoid(ua) * ug, 0.0).astype(BF16)

    prev = lambda j, i: (jnp.maximum(i * hb - 1, 0), j)
    prevg = lambda j, i: (jnp.maximum(i * hb - 1, 0), nf + j)
    return pl.pallas_call(
        body, name=name, grid=(nf, lp // tr),
        in_specs=[pl.BlockSpec((tr, tc), lambda j, i: (i, j)), pl.BlockSpec((tr, tc), lambda j, i: (i, nf + j)),
                  pl.BlockSpec((HALO, tc), prev), pl.BlockSpec((HALO, tc), prevg),
                  pl.BlockSpec((3, tc), lambda j, i: (0, j)), pl.BlockSpec((3, tc), lambda j, i: (0, nf + j)),
                  pl.BlockSpec((1, tc), lambda j, i: (0, j)), pl.BlockSpec((1, tc), lambda j, i: (0, nf + j))],
        out_specs=pl.BlockSpec((tr, tc), lambda j, i: (i, j)),
        out_shape=jax.ShapeDtypeStruct((lp, f), BF16),
        scratch_shapes=[pltpu.VMEM((tr + HALO, tc), F32), pltpu.VMEM((tr + HALO, tc), F32)],
        compiler_params=_cparams(("parallel", "arbitrary")))(up, up, up, up, cw, cw, cb, cb)


def _convglu_bwd(up, dact, cw, cb, d, name):
    lp, f = d.lp, d.f
    tr, tc = _conv_tiles(d)
    nf = f // tc
    hb = tr // HALO
    nrow = lp // tr
    pad = d.pad
    te = tr + HALO

    def body(a_ref, g_ref, pa_ref, pg_ref, na_ref, ng_ref, da_ref, nd_ref, wa_ref, wg_ref, ba_ref, bg_ref,
             oa_ref, og_ref, sa_ref, sg_ref, xa, xg, xd, ya, yg):
        i = pl.program_id(1)

        @pl.when(i == 0)
        def _():
            sa_ref[...] = jnp.zeros_like(sa_ref)
            sg_ref[...] = jnp.zeros_like(sg_ref)

        keep_p = (i > 0).astype(F32)
        keep_n = (i < nrow - 1).astype(F32)
        xa[0:HALO, :] = pa_ref[...] * keep_p
        xg[0:HALO, :] = pg_ref[...] * keep_p
        xa[HALO:HALO + tr, :] = a_ref[...]
        xg[HALO:HALO + tr, :] = g_ref[...]
        xa[HALO + tr:, :] = na_ref[...] * keep_n
        xg[HALO + tr:, :] = ng_ref[...] * keep_n
        xd[0:tr, :] = da_ref[...]
        xd[tr:, :] = nd_ref[...] * keep_n

        def conv(x, w_ref, b_ref):
            taps = [x[pl.ds(HALO - 2 + tap, te), :] for tap in range(3)]
            return b_ref[...] + taps[0] * w_ref[0:1, :] + taps[1] * w_ref[1:2, :] + taps[2] * w_ref[2:3, :], taps

        ua, taps_a = conv(xa, wa_ref, ba_ref)
        ug, taps_g = conv(xg, wg_ref, bg_ref)
        sg = _sigmoid(ua)
        dact = xd[...]
        ya_v = dact * ug * (sg * (1.0 + ua * (1.0 - sg)))
        yg_v = dact * (ua * sg)
        ya[...] = ya_v
        yg[...] = yg_v
        row = i * tr + lax.broadcasted_iota(jnp.int32, (tr, tc), 0)

        def back(y, y_v, taps, w_ref, o_ref, s_ref):
            y0 = y_v[:tr]
            dup = y0 * w_ref[2:3, :] + y[pl.ds(1, tr), :] * w_ref[1:2, :] + y[pl.ds(2, tr), :] * w_ref[0:1, :]
            o_ref[...] = jnp.where(row >= pad, dup, 0.0).astype(BF16)
            for tap in range(3):
                s_ref[tap:tap + 1, :] += jnp.sum(y0 * taps[tap][:tr], axis=0, keepdims=True)
            s_ref[3:4, :] += jnp.sum(y0, axis=0, keepdims=True)

        back(ya, ya_v, taps_a, wa_ref, oa_ref, sa_ref)
        back(yg, yg_v, taps_g, wg_ref, og_ref, sg_ref)

    last8 = lp // HALO - 1
    prev = lambda j, i: (jnp.maximum(i * hb - 1, 0), j)
    prevg = lambda j, i: (jnp.maximum(i * hb - 1, 0), nf + j)
    nxt = lambda j, i: (jnp.minimum((i + 1) * hb, last8), j)
    nxtg = lambda j, i: (jnp.minimum((i + 1) * hb, last8), nf + j)
    halo = lambda m: pl.BlockSpec((HALO, tc), m)
    main = pl.BlockSpec((tr, tc), lambda j, i: (i, j))
    sums = pl.BlockSpec((8, tc), lambda j, i: (0, j))
    return pl.pallas_call(
        body, name=name, grid=(nf, nrow),
        in_specs=[main, pl.BlockSpec((tr, tc), lambda j, i: (i, nf + j)), halo(prev), halo(prevg), halo(nxt), halo(nxtg),
                  main, halo(nxt),
                  pl.BlockSpec((3, tc), lambda j, i: (0, j)), pl.BlockSpec((3, tc), lambda j, i: (0, nf + j)),
                  pl.BlockSpec((1, tc), lambda j, i: (0, j)), pl.BlockSpec((1, tc), lambda j, i: (0, nf + j))],
        out_specs=[main, main, sums, sums],
        out_shape=[jax.ShapeDtypeStruct((lp, f), BF16), jax.ShapeDtypeStruct((lp, f), BF16),
                   jax.ShapeDtypeStruct((8, f), F32), jax.ShapeDtypeStruct((8, f), F32)],
        scratch_shapes=[pltpu.VMEM((tr + 2 * HALO, tc), F32), pltpu.VMEM((tr + 2 * HALO, tc), F32),
                        pltpu.VMEM((te, tc), F32), pltpu.VMEM((te, tc), F32), pltpu.VMEM((te, tc), F32)],
        compiler_params=_cparams(("parallel", "arbitrary")))(up, up, up, up, up, up, dact, dact, cw, cw, cb, cb)


def _head(h, add, target, g, d, name):
    lp, dm, t = d.lp, d.d, d.block
    inv_d = 1.0 / dm

    def body(h_ref, a_ref, t_ref, g_ref, dh_ref, dhb_ref, loss_ref, dg_ref):
        i = pl.program_id(0)

        @pl.when(i == 0)
        def _():
            dh_ref[...] = jnp.zeros_like(dh_ref)
            dhb_ref[...] = jnp.zeros_like(dhb_ref)
            loss_ref[...] = jnp.zeros_like(loss_ref)
            dg_ref[...] = jnp.zeros_like(dg_ref)

        @pl.when(i > 0)
        def _():
            x, gv = h_ref[...] + a_ref[...], g_ref[...]
            r = lax.rsqrt(jnp.mean(x * x, axis=1, keepdims=True) + EPS)
            xh = x * r
            err = xh * gv - t_ref[...]
            loss_ref[...] += 0.5 * inv_d * jnp.sum(err * err)
            dy = err * inv_d
            gy = dy * gv
            c = jnp.mean(gy * x, axis=1, keepdims=True)
            dh = r * gy - x * (r * r * r) * c
            dh_ref[...] = dh
            dhb_ref[...] = dh.astype(BF16)
            dg_ref[...] += jnp.sum(dy * xh, axis=0, keepdims=True)

    blk = pl.BlockSpec((t, dm), lambda i: (i, 0))
    return pl.pallas_call(
        body, name=name, grid=(lp // t,),
        in_specs=[blk, blk, pl.BlockSpec((t, dm), lambda i: (jnp.maximum(i - 1, 0), 0)),
                  pl.BlockSpec((1, dm), lambda i: (0, 0))],
        out_specs=[blk, blk, pl.BlockSpec((8, LANES), lambda i: (0, 0)), pl.BlockSpec((1, dm), lambda i: (0, 0))],
        out_shape=[jax.ShapeDtypeStruct((lp, dm), F32), jax.ShapeDtypeStruct((lp, dm), BF16),
                   jax.ShapeDtypeStruct((8, LANES), F32), jax.ShapeDtypeStruct((1, dm), F32)],
        compiler_params=_cparams(("arbitrary",)))(h, add, target, g.reshape(1, dm))


def _exchange(xs, *, scatter, name):
    n = len(xs)
    nf = len(FLIPS)

    def body(*refs):
        ins, outs = refs[:n], refs[n:2 * n]
        send_sems, recv_sems, loc_sems = refs[2 * n:]
        x, y, c = lax.axis_index("x"), lax.axis_index("y"), lax.axis_index("c")
        me = 4 * x + 2 * y + c
        sends, recvs, locs = [], [], []
        for a in range(n):
            src_me = ins[a].at[me] if scatter else ins[a]
            loc = pltpu.make_async_copy(src_me, outs[a].at[me], loc_sems.at[a])
            loc.start()
            locs.append(loc)
            for k, (fx, fy, fc) in enumerate(FLIPS):
                px, py, pc = x ^ fx, y ^ fy, c ^ fc
                peer = 4 * px + 2 * py + pc
                src = ins[a].at[peer] if scatter else ins[a]
                cp = pltpu.make_async_remote_copy(
                    src_ref=src, dst_ref=outs[a].at[me], send_sem=send_sems.at[a * nf + k],
                    recv_sem=recv_sems.at[a * nf + k], device_id=(px, py, pc), device_id_type=pl.DeviceIdType.MESH)
                cp.start()
                sends.append(cp)
                recvs.append(pltpu.make_async_remote_copy(
                    src_ref=src, dst_ref=outs[a].at[peer], send_sem=send_sems.at[a * nf + k],
                    recv_sem=recv_sems.at[a * nf + k], device_id=(px, py, pc), device_id_type=pl.DeviceIdType.MESH))
        for cp in recvs:
            cp.wait_recv()
        for cp in sends:
            cp.wait_send()
        for loc in locs:
            loc.wait()

    hbm = pl.BlockSpec(memory_space=pltpu.HBM)
    out_shape = [jax.ShapeDtypeStruct(((N_DEV,) + tuple(x.shape[1:])) if scatter else ((N_DEV,) + tuple(x.shape)), x.dtype)
                 for x in xs]
    return pl.pallas_call(
        body, name=name, in_specs=[hbm] * n, out_specs=[hbm] * n, out_shape=out_shape,
        scratch_shapes=[pltpu.SemaphoreType.DMA((n * nf,)), pltpu.SemaphoreType.DMA((n * nf,)),
                        pltpu.SemaphoreType.DMA((n,))],
    )(*xs)


def _gather_two_level(xs, name):
    n = len(xs)

    def body(*refs):
        gather = _Gather(refs[:n], refs[n:2 * n], *refs[2 * n:])
        gather.start()
        gather.finish()

    hbm = pl.BlockSpec(memory_space=pltpu.HBM)
    return pl.pallas_call(
        body, name=name, in_specs=[hbm] * n, out_specs=[hbm] * n,
        out_shape=_gather_out_shapes(xs), scratch_shapes=_gather_sems(n),
    )(*xs)


def _gather_out_shapes(xs):
    return [jax.ShapeDtypeStruct((N_DEV,) + tuple(x.shape), x.dtype) for x in xs]


def _gather_sems(n):
    return [pltpu.SemaphoreType.DMA((7 * n,)), pltpu.SemaphoreType.DMA((7 * n,)), pltpu.SemaphoreType.DMA((n,))]


class _Gather:
    def __init__(self, ins, outs, send_sems, recv_sems, loc_sems):
        self.ins, self.outs, self.sems = ins, outs, (send_sems, recv_sems, loc_sems)
        self.x, self.y, self.c = lax.axis_index("x"), lax.axis_index("y"), lax.axis_index("c")
        self.me, self.sibling = (self.x, self.y, self.c), (self.x, self.y, 1 - self.c)
        self.chips = [(1 - self.x, self.y), (self.x, 1 - self.y), (1 - self.x, 1 - self.y)]

    def slot(self, a, dev):
        return self.outs[a].at[4 * dev[0] + 2 * dev[1] + dev[2]]

    def copy(self, a, k, block, to, src=None):
        return pltpu.make_async_remote_copy(
            src_ref=self.slot(a, block) if src is None else src, dst_ref=self.slot(a, block),
            send_sem=self.sems[0].at[7 * a + k], recv_sem=self.sems[1].at[7 * a + k],
            device_id=to, device_id_type=pl.DeviceIdType.MESH)

    def local(self, a):
        return pltpu.make_async_copy(self.ins[a], self.slot(a, self.me), self.sems[2].at[a])

    def first(self):
        out = []
        for a in range(len(self.ins)):
            out.append(self.copy(a, 0, self.me, self.sibling, src=self.ins[a]))
            out += [self.copy(a, 1 + j, self.me, (*chip, self.c), src=self.ins[a]) for j, chip in enumerate(self.chips)]
        return out

    def start(self):
        for a in range(len(self.ins)):
            self.local(a).start()
        for cp in self.first():
            cp.start()

    def finish(self):
        n, c = len(self.ins), self.c
        passed = []
        for j, chip in enumerate(self.chips):
            for a in range(n):
                self.copy(a, 1 + j, (*chip, c), self.me).wait_recv()
                fwd = self.copy(a, 4 + j, (*chip, c), self.sibling)
                fwd.start()
                passed.append(fwd)
        for a in range(n):
            self.copy(a, 0, self.sibling, self.me).wait_recv()
            for j, chip in enumerate(self.chips):
                self.copy(a, 4 + j, (*chip, 1 - c), self.me).wait_recv()
        for cp in self.first() + passed:
            cp.wait_send()
        for a in range(n):
            self.local(a).wait()


def _pair_exchange(xs, name):
    n = len(xs)

    def body(*refs):
        ins, outs = refs[:n], refs[n:2 * n]
        send_sems, recv_sems = refs[2 * n:]
        x, y, c = lax.axis_index("x"), lax.axis_index("y"), lax.axis_index("c")
        copies = []
        for a in range(n):
            for q in range(4):
                cp = pltpu.make_async_remote_copy(
                    src_ref=ins[a].at[2 * q + (1 - c)], dst_ref=outs[a].at[q], send_sem=send_sems.at[4 * a + q],
                    recv_sem=recv_sems.at[4 * a + q], device_id=(x, y, 1 - c), device_id_type=pl.DeviceIdType.MESH)
                cp.start()
                copies.append(cp)
        for cp in copies:
            cp.wait_recv()
        for cp in copies:
            cp.wait_send()

    hbm = pl.BlockSpec(memory_space=pltpu.HBM)
    return pl.pallas_call(
        body, name=name, in_specs=[hbm] * n, out_specs=[hbm] * n,
        out_shape=[jax.ShapeDtypeStruct((4,) + tuple(x.shape[1:]), x.dtype) for x in xs],
        scratch_shapes=[pltpu.SemaphoreType.DMA((4 * n,)), pltpu.SemaphoreType.DMA((4 * n,))],
    )(*xs)


def _chip_exchange(xs, name):
    n = len(xs)

    def body(*refs):
        copies = _chip_copies(refs[:n], refs[n:2 * n], *refs[2 * n:])
        _chip_copies_start(copies)
        _chip_copies_wait(copies)

    hbm = pl.BlockSpec(memory_space=pltpu.HBM)
    return pl.pallas_call(
        body, name=name, in_specs=[hbm] * n, out_specs=[hbm] * n,
        out_shape=[jax.ShapeDtypeStruct(x.shape, x.dtype) for x in xs],
        scratch_shapes=_chip_copies_sems(n),
    )(*xs)


def _chip_copies_sems(n):
    return [pltpu.SemaphoreType.DMA((3 * n,)), pltpu.SemaphoreType.DMA((3 * n,)), pltpu.SemaphoreType.DMA((n,))]


def _chip_copies(ins, outs, send_sems, recv_sems, loc_sems):
    x, y, c = lax.axis_index("x"), lax.axis_index("y"), lax.axis_index("c")
    mine = 2 * x + y
    locs, sends, recvs = [], [], []
    for a in range(len(ins)):
        locs.append(pltpu.make_async_copy(ins[a].at[mine], outs[a].at[mine], loc_sems.at[a]))
        for k, (fx, fy) in enumerate([(1, 0), (0, 1), (1, 1)]):
            px, py = x ^ fx, y ^ fy
            peer = 2 * px + py
            sems = dict(send_sem=send_sems.at[3 * a + k], recv_sem=recv_sems.at[3 * a + k],
                        device_id=(px, py, c), device_id_type=pl.DeviceIdType.MESH)
            sends.append(pltpu.make_async_remote_copy(src_ref=ins[a].at[peer], dst_ref=outs[a].at[mine], **sems))
            recvs.append(pltpu.make_async_remote_copy(src_ref=ins[a].at[peer], dst_ref=outs[a].at[peer], **sems))
    return locs, sends, recvs


def _chip_copies_start(copies):
    locs, sends, _ = copies
    for cp in locs + sends:
        cp.start()


def _chip_copies_wait(copies):
    locs, sends, recvs = copies
    for cp in recvs:
        cp.wait_recv()
    for cp in sends:
        cp.wait_send()
    for cp in locs:
        cp.wait()


def _pair_sum(a, b, name):
    shape = a.shape
    cols = shape[-1]
    rows = a.size // cols
    a, b = a.reshape(rows, cols), b.reshape(rows, cols)
    tr = next((t for t in (512, 256, 128, 64, 32, 16) if rows % t == 0 and t * cols <= 2 * ADAMW_TILE_ELEMS), rows)

    def body(a_ref, b_ref, o_ref):
        o_ref[...] = (a_ref[...].astype(F32) + b_ref[...].astype(F32)).astype(o_ref.dtype)

    blk = pl.BlockSpec((tr, cols), lambda i: (i, 0))
    return pl.pallas_call(body, name=name, grid=(rows // tr,), in_specs=[blk, blk], out_specs=blk,
                          out_shape=jax.ShapeDtypeStruct((rows, cols), a.dtype),
                          compiler_params=_cparams(("parallel",)))(a, b).reshape(shape)


def _adamw(parts, w, m, v, name):
    shape = w.shape
    cols = shape[-1]
    rows = w.size // cols
    nparts = parts.shape[0]
    parts, w, m, v = parts.reshape(nparts, rows, cols), w.reshape(rows, cols), m.reshape(rows, cols), v.reshape(rows, cols)
    tr = next((t for t in (256, 128, 64, 32, 16) if rows % t == 0 and t * cols <= ADAMW_TILE_ELEMS), rows)
    c1 = 1.0 - ADAM_B1 ** ADAM_STEP
    c2 = 1.0 - ADAM_B2 ** ADAM_STEP

    def body(p_ref, w_ref, m_ref, v_ref, g_ref, d_ref, mo_ref, vo_ref):
        g = p_ref[0].astype(F32)
        for q in range(1, nparts):
            g = g + p_ref[q].astype(F32)
        mn = ADAM_B1 * m_ref[...] + (1.0 - ADAM_B1) * g
        vn = ADAM_B2 * v_ref[...] + (1.0 - ADAM_B2) * (g * g)
        g_ref[...] = g
        mo_ref[...] = mn
        vo_ref[...] = vn
        d_ref[...] = -ADAM_LR * ((mn / c1) / (jnp.sqrt(vn / c2) + ADAM_EPS) + ADAM_WD * w_ref[...])

    blk = pl.BlockSpec((tr, cols), lambda i: (i, 0))
    out = jax.ShapeDtypeStruct((rows, cols), F32)
    res = pl.pallas_call(
        body, name=name, grid=(rows // tr,),
        in_specs=[pl.BlockSpec((nparts, tr, cols), lambda i: (0, i, 0)), blk, blk, blk],
        out_specs=[blk] * 4, out_shape=[out] * 4, compiler_params=_cparams(("parallel",)))(parts, w, m, v)
    return [r.reshape(shape) for r in res]


BIG = ["w_in", "w_uq", "w_ukv", "w_sb_out", "w_mla_out", "w_o", "w_up", "w_down"]
ROW_SHARDED = {"w_o", "w_down"}
SMALL = ["conv_w", "meta_tokens"]
SHARDED = BIG + SMALL
REPL = ["norm_mix", "q_norm", "kv_norm", "norm_ffn", "conv_b", "final_norm"]


def _pack_rows(flat, row_mult):
    unit = PACK_W * row_mult
    total = -(-flat.shape[0] // unit) * unit
    return jnp.pad(flat, (0, total - flat.shape[0])).reshape(-1, PACK_W)


def _padded_cols(c):
    return -(-c // LANES) * LANES


def _pad_block(a, name):
    c = a.shape[-1]
    if name in ROW_SHARDED or c % LANES == 0:
        return a
    return jnp.pad(a, [(0, 0)] * (a.ndim - 1) + [(0, _padded_cols(c) - c)])


def _full_from_slots(slots, name, c):
    if name in ROW_SHARDED:
        return jnp.transpose(slots, (1, 0, 2, 3)).reshape(slots.shape[1], -1, slots.shape[3])
    return jnp.concatenate([slots[q][..., :c] for q in range(N_DEV)], axis=-1)


def _slots_from_full(full, name):
    if name in ROW_SHARDED:
        l, rr, n = full.shape
        return jnp.transpose(full.reshape(l, N_DEV, rr // N_DEV, n), (1, 0, 2, 3))
    c = full.shape[-1] // N_DEV
    return jnp.stack([_pad_block(full[..., q * c:(q + 1) * c], name) for q in range(N_DEV)])


def _swap_halves(t):
    half = t.shape[-1] // 2
    return jnp.concatenate([t[..., half:], t[..., :half]], axis=-1)


def _in_offsets(d):
    widths = (d.sbw, d.sbw, d.sbw, d.q_lora, d.kv_lora, d.rope, d.d, d.d)
    offs, o = [], 0
    for w in widths:
        offs.append((o, o + w))
        o += w
    return offs


def _prime_weights(w_in, w_uq, w_ukv, d):
    offs = _in_offsets(d)
    cols = lambda k: w_in[:, offs[k][0]:offs[k][1]]
    kr = cols(5)
    zpad = jnp.zeros((d.d, d.g_off - d.kr_off - LANES), w_in.dtype)
    w_inb = jnp.concatenate([cols(3), cols(4), kr, _swap_halves(kr), zpad, cols(6), cols(7)], axis=1)
    w_ina = w_in[:, :d.wa]
    uq = w_uq.reshape(d.q_lora, d.mla_heads, d.nope + d.rope)
    rope = uq[..., d.nope:]
    w_uq = jnp.concatenate([uq[..., :d.nope], rope, _swap_halves(rope)], axis=-1).reshape(d.q_lora, d.qw)
    ukv = w_ukv.reshape(d.kv_lora, d.mla_heads, d.nope + d.vdim)
    w_ukv = jnp.concatenate([ukv[..., :d.nope].reshape(d.kv_lora, -1), ukv[..., d.nope:].reshape(d.kv_lora, -1)], axis=1)
    return dict(w_ina=w_ina, w_inb=w_inb, w_in=jnp.concatenate([w_ina, w_inb], axis=1), w_uq=w_uq, w_ukv=w_ukv)


def _unprime_grads(g, d):
    gi = g["w_in"]
    b = gi[:, d.wa:]
    kr = b[:, d.kr_off:d.kr_off + d.rope] + _swap_halves(b[:, d.kr_off + d.rope:d.kr_off + 2 * d.rope])
    w_in = jnp.concatenate([gi[:, :d.wa], b[:, :d.kr_off], kr, b[:, d.g_off:]], axis=1)
    uq = g["w_uq"].reshape(d.q_lora, d.mla_heads, 2 * LANES)
    rope = uq[..., d.nope:d.nope + d.rope] + _swap_halves(uq[..., d.nope + d.rope:])
    w_uq = jnp.concatenate([uq[..., :d.nope], rope], axis=-1).reshape(d.q_lora, -1)
    hw = d.mla_heads * d.nope
    ukv = g["w_ukv"]
    w_ukv = jnp.concatenate([ukv[:, :hw].reshape(d.kv_lora, d.mla_heads, d.nope),
                             ukv[:, hw:].reshape(d.kv_lora, d.mla_heads, d.vdim)], axis=-1).reshape(d.kv_lora, -1)
    return dict(g, w_in=w_in, w_uq=w_uq, w_ukv=w_ukv)


def _layer_fwd(h, add, w, norm_mix, q_norm, kv_norm, norm_ffn, cw, cb, ctab, stab, d, tag, ride=(), arrived=None):
    s = {}
    if add is None:
        s["h"] = h
        s["hn"], s["r1"] = _norm_fwd(h, norm_mix, width=d.d, cidx=0, name=f"norm_mix_{tag}")
    else:
        s["h"], s["hn"], s["r1"] = _norm_fwd(h, norm_mix, width=d.d, cidx=0, add=add, name=f"norm_mix_{tag}")
    s["pa"] = _mm(s["hn"], w["w_ina"], out_dtype=BF16, name=f"proj_a_{tag}")
    s["pb"] = _mm(s["hn"], w["w_inb"], name=f"proj_b_{tag}")
    s["o_sb"], s["carry"] = _sb_fwd(s["pa"], d, f"sb_fwd_{tag}")
    s["cqn"], s["rq"] = _norm_fwd(s["pb"], q_norm, width=d.q_lora, cidx=0, name=f"norm_q_{tag}")
    s["ckn"], s["rk"] = _norm_fwd(s["pb"], kv_norm, width=d.kv_lora, cidx=d.q_lora // d.kv_lora, name=f"norm_kv_{tag}")
    qraw = _mm(s["cqn"], w["w_uq"], name=f"uq_{tag}")
    s["kv"] = _mm(s["ckn"], w["w_ukv"], out_dtype=BF16, name=f"ukv_{tag}")
    s["qm"], s["kr"] = _mla_prep_fwd(qraw, s["pb"], ctab, stab, d, f"mla_prep_{tag}")
    (s["o_mla"], s["lse"]), gathered = _mla_fwd(s["qm"], s["kv"], s["kr"], d, f"mla_fwd_{tag}", ride=ride)
    if arrived is not None:
        arrived(gathered)
    s["b_sb"] = _mm(s["o_sb"], w["w_sb_out"], name=f"sb_out_{tag}")
    s["b_mla"] = _mm(s["o_mla"], w["w_mla_out"], name=f"mla_out_{tag}")
    s["merged"] = _gate_fwd(s["pb"], s["b_sb"], s["b_mla"], d, f"gate_{tag}")
    mix = _mm(s["merged"], w["w_o"], name=f"w_o_{tag}")
    s["h1"], s["hn2"], s["r2"] = _norm_fwd(s["h"], norm_ffn, width=d.d, cidx=0, add=mix, name=f"norm_ffn_{tag}")
    s["up"] = _mm(s["hn2"], w["w_up"], name=f"w_up_{tag}")
    s["act"] = _convglu_fwd(s["up"], cw, cb, d, f"convglu_{tag}")
    ffn = _mm(s["act"], w["w_down"], name=f"w_down_{tag}")
    return s, ffn


def _layer_bwd(dh2, dh2b, s, w, norm_mix, q_norm, kv_norm, norm_ffn, cw, cb, ctab, stab, d, tag, ride=()):
    g = {}
    dact = _mm(dh2b, w["w_down"], tb=True, name=f"d_act_{tag}")
    g["w_down"] = _mm(s["act"], dh2b, ta=True, name=f"g_w_down_{tag}")
    dup_a, dup_g, sums_a, sums_g = _convglu_bwd(s["up"], dact, cw, cb, d, f"convglu_bwd_{tag}")
    dup = jnp.concatenate([dup_a, dup_g], axis=1)
    g["conv_w"] = jnp.concatenate([sums_a[0:3], sums_g[0:3]], axis=1)
    g["conv_b"] = jnp.concatenate([sums_a[3], sums_g[3]], axis=0)
    g["w_up"] = _mm(s["hn2"], dup, ta=True, name=f"g_w_up_{tag}")
    dhn2 = _mm(dup, w["w_up"], tb=True, name=f"d_hn2_{tag}")
    dh1, dh1b, g["norm_ffn"] = _norm_bwd(dhn2, s["h1"], s["r2"], norm_ffn, width=d.d, cidx=0, dres=dh2,
                                         name=f"norm_ffn_bwd_{tag}")
    dmerged = _mm(dh1b, w["w_o"], tb=True, name=f"d_merged_{tag}")
    g["w_o"] = _mm(s["merged"], dh1b, ta=True, name=f"g_w_o_{tag}")
    db_sb, db_mla, dg_sb, dg_mla = _gate_bwd(dmerged, s["pb"], s["b_sb"], s["b_mla"], d, f"gate_bwd_{tag}")
    do_sb = _mm(db_sb, w["w_sb_out"], tb=True, name=f"d_o_sb_{tag}")
    g["w_sb_out"] = _mm(s["o_sb"], db_sb, ta=True, name=f"g_w_sb_out_{tag}")
    do_mla = _mm(db_mla, w["w_mla_out"], tb=True, name=f"d_o_mla_{tag}")
    g["w_mla_out"] = _mm(s["o_mla"], db_mla, ta=True, name=f"g_w_mla_out_{tag}")
    dq_sb, dk_sb, dv_sb = _sb_bwd(s["pa"], do_sb, s["carry"], d, f"sb_bwd_{tag}")
    (dqm, dkn, dv, dkr), rode = _mla_bwd(s["qm"], s["kv"], s["kr"], s["o_mla"], do_mla, s["lse"], d, f"mla_bwd_{tag}",
                                         ride=ride)
    dqraw, dkr128 = _mla_prep_bwd(dqm, dkr, ctab, stab, d, f"mla_prep_bwd_{tag}")
    dkv = jnp.concatenate([dkn.astype(BF16), dv.astype(BF16)], axis=1)
    dcqn = _mm(dqraw, w["w_uq"], tb=True, name=f"d_cq_{tag}")
    g["w_uq"] = _mm(s["cqn"], dqraw, ta=True, name=f"g_w_uq_{tag}")
    dckn = _mm(dkv, w["w_ukv"], tb=True, name=f"d_ckv_{tag}")
    g["w_ukv"] = _mm(s["ckn"], dkv, ta=True, name=f"g_w_ukv_{tag}")
    dcq, g["q_norm"] = _norm_bwd(dcqn, s["pb"], s["rq"], q_norm, width=d.q_lora, cidx=0, out_dtype=BF16, name=f"norm_q_bwd_{tag}")
    dckv, g["kv_norm"] = _norm_bwd(dckn, s["pb"], s["rk"], kv_norm, width=d.kv_lora, cidx=d.q_lora // d.kv_lora,
                                   out_dtype=BF16, name=f"norm_kv_bwd_{tag}")
    zpad = jnp.zeros((d.lp, d.g_off - d.kr_off - LANES), BF16)
    dproj = jnp.concatenate([dq_sb, dk_sb.astype(BF16), dv_sb.astype(BF16), dcq, dckv, dkr128, zpad, dg_sb, dg_mla], axis=1)
    g["w_in"] = _mm(s["hn"], dproj, ta=True, name=f"g_w_in_{tag}")
    dhn = _mm(dproj, w["w_in"], tb=True, name=f"d_hn_{tag}")
    dh, dhb, g["norm_mix"] = _norm_bwd(dhn, s["h"], s["r1"], norm_mix, width=d.d, cidx=0, dres=dh1,
                                       name=f"norm_mix_bwd_{tag}")
    return dh, dhb, g, rode


def _step(d, x, p, m, v, loss_target):
    x = x.reshape(d.seq, d.d)
    target = loss_target.reshape(d.seq, d.d)

    def block(n, layers=None):
        a = p[n] if layers is None else p[n][layers]
        return _pad_block(a, n).astype(BF16 if n in BIG else F32)

    def whole(n, slots):
        return _full_from_slots(slots, n, p[n].shape[-1])

    first_names = ["w_in", "w_uq", "w_ukv"]
    early = _gather_two_level([block(n, slice(0, 1)) for n in first_names] + [block(n) for n in SMALL], "gather_weights")
    full = {n: whole(n, g_) for n, g_ in zip(first_names + SMALL, early)}
    ws = [_prime_weights(*[full[n][0] for n in first_names], d)] + [dict() for _ in range(1, d.depth)]
    rest_names = [n for n in BIG if n not in first_names]
    late = [block(n, slice(1, d.depth)) for n in first_names] + [block(n) for n in rest_names]

    def arrived(gathered):
        later = {n: whole(n, g_) for n, g_ in zip(first_names, gathered)}
        rest = {n: whole(n, g_) for n, g_ in zip(rest_names, gathered[len(first_names):])}
        for l in range(d.depth):
            if l > 0:
                ws[l].update(_prime_weights(*[later[n][l - 1] for n in first_names], d))
            ws[l].update({n: rest[n][l] for n in rest_names})

    pos = jnp.arange(d.lp, dtype=F32) - d.pad
    half = d.rope // 2
    freqs = ROPE_THETA ** (-jnp.arange(half, dtype=F32) / half)
    ang = pos[:, None] * freqs[None, :]
    cos, sin = jnp.cos(ang), jnp.sin(ang)
    zero = jnp.zeros((d.lp, LANES - d.rope), F32)
    ctab = jnp.concatenate([cos, cos, zero], axis=1)
    stab = jnp.concatenate([-sin, sin, zero], axis=1)

    h = jnp.concatenate([jnp.zeros((d.pad, d.d), F32), full["meta_tokens"], x], axis=0)
    saved, add = [], None
    for l in range(d.depth):
        s, add = _layer_fwd(h, add, ws[l], p["norm_mix"][l], p["q_norm"][l], p["kv_norm"][l], p["norm_ffn"][l],
                            full["conv_w"][l], p["conv_b"][l].reshape(1, -1), ctab, stab, d, f"l{l}",
                            ride=late if l == 0 else (), arrived=arrived if l == 0 else None)
        saved.append(s)
        h = s["h1"]
    dh, dhb, loss_part, g_final = _head(h, add, target, p["final_norm"], d, "head")

    my_c = lax.axis_index("c")

    def pair_level(names, gfull, tag):
        gsend = [_slots_from_full(gfull[n].astype(BF16) if n in BIG else gfull[n], n) for n in names]
        theirs = _pair_exchange(gsend, f"scatter_grads_pair_{tag}")
        mine = [lax.dynamic_index_in_dim(g_.reshape((4, 2) + g_.shape[1:]), my_c, axis=1, keepdims=False) for g_ in gsend]
        return [_pair_sum(a_, b_, f"pair_sum_{n}_{tag}") for n, a_, b_ in zip(names, mine, theirs)]

    grads = [None] * d.depth
    recv_big = [None] * d.depth
    ride = []
    for l in reversed(range(d.depth)):
        dh, dhb, g, rode = _layer_bwd(dh, dhb, saved[l], ws[l], p["norm_mix"][l], p["q_norm"][l], p["kv_norm"][l],
                                      p["norm_ffn"][l], full["conv_w"][l], p["conv_b"][l].reshape(1, -1), ctab, stab,
                                      d, f"l{l}", ride=ride)
        if ride:
            recv_big[l + 1] = rode
        grads[l] = _unprime_grads(g, d)
        if l > 0:
            ride = pair_level(BIG, {n: grads[l][n][None] for n in BIG}, f"l{l}")
    grad_x = dh[d.first_tok:].reshape(1, d.seq, d.d)

    gfull = {n: grads[0][n][None] for n in BIG}
    gfull["conv_w"] = jnp.stack([grads[l]["conv_w"] for l in range(d.depth)])
    gfull["meta_tokens"] = dh[d.pad:d.first_tok]
    last = list(_chip_exchange(pair_level(SHARDED, gfull, "l0"), "scatter_grads_chips"))
    recv_big[0] = last[:len(BIG)]
    grecv = [jnp.concatenate([recv_big[l][k] for l in range(d.depth)], axis=1) for k in range(len(BIG))] + last[len(BIG):]
    outs_sh = {n: _adamw(r_, _pad_block(p[n], n), _pad_block(m[n], n), _pad_block(v[n], n), f"adamw_{n}")
               for n, r_ in zip(SHARDED, grecv)}

    grep = {n: jnp.stack([grads[l][n].reshape(-1) for l in range(d.depth)]) for n in REPL if n != "final_norm"}
    grep["final_norm"] = g_final.reshape(-1)
    rflat = jnp.concatenate([grep[n].reshape(-1) for n in REPL] + [loss_part[0, 0:1]])
    (rparts,) = _exchange([_pack_rows(rflat, 8)], scatter=False, name="gather_small_grads")
    packr = lambda t: _pack_rows(jnp.concatenate([t[n].reshape(-1) for n in REPL] + [jnp.zeros((1,), F32)]), 8)
    outs_rp = _adamw(rparts, packr(p), packr(m), packr(v), "adamw_replicated")

    def unpack(flat, names, extra=0):
        res, off = {}, 0
        flat = flat.reshape(-1)
        for n in names:
            res[n] = flat[off:off + p[n].size].reshape(p[n].shape)
            off += p[n].size
        return res, flat[off:off + extra]

    results = []
    loss = None
    for k in range(4):
        sh = {n: outs_sh[n][k][..., :p[n].shape[-1]] for n in SHARDED}
        rp, tail = unpack(outs_rp[k], REPL, 1)
        if k == 0:
            loss = tail[0]
        results.append({**sh, **rp})
    return loss, grad_x, results


WEIGHTS = ["meta_tokens", "norm_mix", "w_in", "q_norm", "w_uq", "kv_norm", "w_ukv", "w_sb_out", "w_mla_out", "w_o",
           "norm_ffn", "w_up", "conv_w", "conv_b", "w_down", "final_norm"]


def _run(d, x, weights, loss_target, moments_m, moments_v):
    p = dict(zip(WEIGHTS, weights))
    m = dict(zip(WEIGHTS, moments_m))
    v = dict(zip(WEIGHTS, moments_v))
    loss, grad_x, res = _step(d, x, p, m, v, loss_target)
    out = [loss, grad_x]
    for k in range(4):
        out += [res[k][n] for n in WEIGHTS]
    return tuple(out)


def kernel(x, meta_tokens, norm_mix, w_in, q_norm, w_uq, kv_norm, w_ukv, w_sb_out, w_mla_out, w_o, norm_ffn, w_up, conv_w, conv_b, w_down, final_norm, loss_target, m_meta_tokens, m_norm_mix, m_w_in, m_q_norm, m_w_uq, m_kv_norm, m_w_ukv, m_w_sb_out, m_w_mla_out, m_w_o, m_norm_ffn, m_w_up, m_conv_w, m_conv_b, m_w_down, m_final_norm, v_meta_tokens, v_norm_mix, v_w_in, v_q_norm, v_w_uq, v_kv_norm, v_w_ukv, v_w_sb_out, v_w_mla_out, v_w_o, v_norm_ffn, v_w_up, v_conv_w, v_conv_b, v_w_down, v_final_norm):
    weights = [meta_tokens, norm_mix, w_in, q_norm, w_uq, kv_norm, w_ukv, w_sb_out, w_mla_out, w_o, norm_ffn, w_up,
               conv_w, conv_b, w_down, final_norm]
    ms = [m_meta_tokens, m_norm_mix, m_w_in, m_q_norm, m_w_uq, m_kv_norm, m_w_ukv, m_w_sb_out, m_w_mla_out, m_w_o,
          m_norm_ffn, m_w_up, m_conv_w, m_conv_b, m_w_down, m_final_norm]
    vs = [v_meta_tokens, v_norm_mix, v_w_in, v_q_norm, v_w_uq, v_kv_norm, v_w_ukv, v_w_sb_out, v_w_mla_out, v_w_o,
          v_norm_ffn, v_w_up, v_conv_w, v_conv_b, v_w_down, v_final_norm]
    return _run(PROD, x, weights, loss_target, ms, vs)
```

```python
import jax
import jax.numpy as jnp
from jax import lax
from jax.experimental import pallas as pl
from jax.experimental.pallas import tpu as pltpu

F32 = jnp.float32
BF16 = jnp.bfloat16

EPS = 1e-6
ROPE_THETA = 10000.0
ADAM_LR = 0.001
ADAM_B1 = 0.9
ADAM_B2 = 0.999
ADAM_EPS = 1e-08
ADAM_WD = 0.01
ADAM_STEP = 10
NEG = -1e30
DEAD = -110.0
LANES = 128
PACK_W = 1024
ADAMW_TILE_ELEMS = 256 * 1024
V7X_VMEM_LIMIT = 48 * 1024 * 1024
V7X_VMEM_LIMIT_BIG = 58 * 1024 * 1024
MESH_AXES = ("x", "y", "c")
N_DEV = 8
FLIPS = [(0, 0, 1), (0, 1, 0), (0, 1, 1), (1, 0, 0), (1, 0, 1), (1, 1, 0), (1, 1, 1)]


class _Dims:
    def __init__(self, d_model=2048, seq=8192, depth=2, n_meta=16, block=128, sb_heads=8, hd=128,
                 mla_heads=8, q_lora=512, kv_lora=256, nope=128, rope=64, vdim=128, d_ff=5632, tq=None):
        self.d, self.seq, self.depth, self.n_meta, self.block = d_model, seq, depth, n_meta, block
        self.sb_heads, self.hd, self.mla_heads = sb_heads, hd, mla_heads
        self.q_lora, self.kv_lora, self.nope, self.rope, self.vdim, self.f = q_lora, kv_lora, nope, rope, vdim, d_ff
        assert hd == LANES and nope == LANES and vdim == LANES and 2 * rope == LANES
        self.pad = block - n_meta
        self.lp = self.pad + n_meta + seq
        self.first_tok = self.pad + n_meta
        assert self.first_tok == block and self.lp % block == 0 and self.lp // block < LANES
        self.tq = tq or next(t for t in (640, 512, 256, 128) if self.lp % t == 0)
        assert self.tq % block == 0 and self.lp % self.tq == 0
        self.sbw = sb_heads * hd
        self.mlaw = mla_heads * vdim
        self.wa = 3 * self.sbw
        self.d_in = 3 * self.sbw + q_lora + kv_lora + rope + 2 * d_model
        self.tg = min(1024, d_model)
        self.kr_off = q_lora + kv_lora
        raw = self.kr_off + LANES
        self.g_off = -(-raw // self.tg) * self.tg
        self.wb = self.g_off + 2 * d_model
        self.qw = mla_heads * 2 * LANES


PROD = _Dims()


def _pick(n, prefs):
    for p in prefs:
        if n % p == 0:
            return p
    return n


def _cparams(sem, limit=V7X_VMEM_LIMIT):
    return pltpu.CompilerParams(dimension_semantics=sem, vmem_limit_bytes=limit)


def _mm(a, b, *, ta=False, tb=False, out_dtype=F32, name):
    if ta:
        kdim, m = a.shape
    else:
        m, kdim = a.shape
    if tb:
        n, k2 = b.shape
    else:
        k2, n = b.shape
    assert kdim == k2, (a.shape, b.shape, ta, tb)
    tm = _pick(m, (640, 512, 256, 128))
    tn = _pick(n, (1024, 512, 384, 256, 128))
    tk = _pick(kdim, (2816, 2048, 1664, 1408, 1024, 640, 512, 256, 128))
    nk = kdim // tk
    dn = (((0 if ta else 1,), (1 if tb else 0,)), ((), ()))

    def dot(a_ref, b_ref):
        return lax.dot_general(a_ref[...].astype(BF16), b_ref[...].astype(BF16), dn, preferred_element_type=F32)

    def body_one(a_ref, b_ref, o_ref):
        o_ref[...] = dot(a_ref, b_ref).astype(out_dtype)

    def body_acc(a_ref, b_ref, o_ref, acc_ref):
        k = pl.program_id(2)

        @pl.when(k == 0)
        def _():
            acc_ref[...] = dot(a_ref, b_ref)

        @pl.when((k > 0) & (k < nk - 1))
        def _():
            acc_ref[...] += dot(a_ref, b_ref)

        @pl.when(k == nk - 1)
        def _():
            o_ref[...] = (acc_ref[...] + dot(a_ref, b_ref)).astype(out_dtype)

    a_spec = pl.BlockSpec((tk, tm), lambda i, j, k: (k, i)) if ta else pl.BlockSpec((tm, tk), lambda i, j, k: (i, k))
    b_spec = pl.BlockSpec((tn, tk), lambda i, j, k: (j, k)) if tb else pl.BlockSpec((tk, tn), lambda i, j, k: (k, j))
    return pl.pallas_call(
        body_one if nk == 1 else body_acc, name=name, grid=(m // tm, n // tn, nk), in_specs=[a_spec, b_spec],
        out_specs=pl.BlockSpec((tm, tn), lambda i, j, k: (i, j)),
        out_shape=jax.ShapeDtypeStruct((m, n), out_dtype),
        scratch_shapes=[] if nk == 1 else [pltpu.VMEM((tm, tn), F32)],
        compiler_params=_cparams(("parallel", "parallel", "arbitrary")),
    )(a, b)


def _norm_fwd(x, g, *, width, cidx, add=None, name):
    rows = x.shape[0]
    tr = _pick(rows, (256, 128))
    has_add = add is not None

    def body(*refs):
        if has_add:
            x_ref, a_ref, g_ref, xn_ref, y_ref, r_ref = refs
            xv = x_ref[...] + a_ref[...]
            xn_ref[...] = xv
        else:
            x_ref, g_ref, y_ref, r_ref = refs
            xv = x_ref[...]
        r = lax.rsqrt(jnp.mean(xv * xv, axis=1, keepdims=True) + EPS)
        y_ref[...] = (xv * r * g_ref[...]).astype(BF16)
        r_ref[...] = r

    blk = pl.BlockSpec((tr, width), lambda i: (i, 0))
    in_specs = [pl.BlockSpec((tr, width), lambda i: (i, cidx))]
    args = [x]
    if has_add:
        in_specs.append(blk)
        args.append(add)
    in_specs.append(pl.BlockSpec((1, width), lambda i: (0, 0)))
    args.append(g.reshape(1, width))
    out_specs = [blk, pl.BlockSpec((tr, 1), lambda i: (i, 0))]
    out_shape = [jax.ShapeDtypeStruct((rows, width), BF16), jax.ShapeDtypeStruct((rows, 1), F32)]
    if has_add:
        out_specs.insert(0, blk)
        out_shape.insert(0, jax.ShapeDtypeStruct((rows, width), F32))
    return pl.pallas_call(body, name=name, grid=(rows // tr,), in_specs=in_specs, out_specs=out_specs,
                          out_shape=out_shape, compiler_params=_cparams(("parallel",)))(*args)


def _norm_bwd(dy, x, r, g, *, width, cidx, dres=None, out_dtype=F32, name):
    rows = x.shape[0]
    tr = _pick(rows, (256, 128))
    has_res = dres is not None

    def body(*refs):
        if has_res:
            dy_ref, x_ref, r_ref, g_ref, dr_ref, dx_ref, dxb_ref, dg_ref = refs
        else:
            dy_ref, x_ref, r_ref, g_ref, dx_ref, dg_ref = refs
        i = pl.program_id(0)

        @pl.when(i == 0)
        def _():
            dg_ref[...] = jnp.zeros_like(dg_ref)

        dyv, xv, rv = dy_ref[...], x_ref[...], r_ref[...]
        gy = dyv * g_ref[...]
        c = jnp.mean(gy * xv, axis=1, keepdims=True)
        dx = rv * gy - xv * (rv * rv * rv) * c
        if has_res:
            dx = dx + dr_ref[...]
            dxb_ref[...] = dx.astype(BF16)
        dx_ref[...] = dx.astype(out_dtype)
        dg_ref[...] += jnp.sum(dyv * xv * rv, axis=0, keepdims=True)

    blk = pl.BlockSpec((tr, width), lambda i: (i, 0))
    in_specs = [blk, pl.BlockSpec((tr, width), lambda i: (i, cidx)), pl.BlockSpec((tr, 1), lambda i: (i, 0)),
                pl.BlockSpec((1, width), lambda i: (0, 0))]
    args = [dy, x, r, g.reshape(1, width)]
    out_specs = [blk, pl.BlockSpec((1, width), lambda i: (0, 0))]
    out_shape = [jax.ShapeDtypeStruct((rows, width), out_dtype), jax.ShapeDtypeStruct((1, width), F32)]
    if has_res:
        in_specs.append(blk)
        args.append(dres)
        out_specs.insert(1, blk)
        out_shape.insert(1, jax.ShapeDtypeStruct((rows, width), BF16))
    return pl.pallas_call(
        body, name=name, grid=(rows // tr,), in_specs=in_specs, out_specs=out_specs, out_shape=out_shape,
        compiler_params=_cparams(("arbitrary",)))(*args)


def _split3(x):
    h1 = x.astype(BF16)
    r1 = x - h1.astype(F32)
    h2 = r1.astype(BF16)
    h3 = (r1 - h2.astype(F32)).astype(BF16)
    return h1, h2, h3


def _cum(x, tri):
    h1, h2, h3 = _split3(x)
    dot = lambda h: jnp.dot(h, tri, preferred_element_type=F32)
    return dot(h1) + dot(h2) + dot(h3)


def _dot_nt(a, b):
    return lax.dot_general(a, b, (((1,), (1,)), ((), ())), preferred_element_type=F32)


def _dot_tn(a, b):
    return lax.dot_general(a, b, (((0,), (0,)), ((), ())), preferred_element_type=F32)


def _sb_geometry(tq, t):
    assert t & (t - 1) == 0
    ri = lax.broadcasted_iota(jnp.int32, (tq, t), 0)
    return jnp.bitwise_and(ri, t - 1), jnp.right_shift(ri, t.bit_length() - 1), lax.broadcasted_iota(jnp.int32, (tq, t), 1)


def _sb_key_blocks(ref, base, r, t, kind):
    if kind == "low":
        return [ref[pl.ds(pl.multiple_of(jnp.maximum(base + g, 0) * t, t), t), :] for g in range(r)]
    slab = ref[pl.ds(pl.multiple_of(base * t, t), r * t), :]
    return [slab[g * t:(g + 1) * t, :] for g in range(r)]


def _sb_mask(geo, base, s, t, pad, kind):
    rowl, grp, col = geo
    if kind == "plain":
        return None
    if kind == "first":
        return col < rowl
    blk = base + grp
    causal = col < rowl + jnp.where(s > 0, t, 0)
    return (blk >= 0) & (blk * t + col >= pad) & causal


def _sb_fwd(qkv, d, name):
    nh, hd, lp, tq, t = d.sb_heads, d.hd, d.lp, d.tq, d.block
    r = tq // t
    scale = hd ** -0.5
    pad = d.pad

    def body(q_ref, k_ref, v_ref, o_ref, c_ref):
        i = pl.program_id(1)
        qs = [q_ref[g * t:(g + 1) * t, :] for g in range(r)]
        geo = _sb_geometry(tq, t)
        tri = (lax.broadcasted_iota(jnp.int32, (t, t), 0)
               > lax.broadcasted_iota(jnp.int32, (t, t), 1)).astype(BF16)
        lane = lax.broadcasted_iota(jnp.int32, (tq, LANES), 1)

        c_ref[...] = jnp.zeros_like(c_ref)

        def make_step(kind):
            def step(s, carry):
                acc, run = carry
                ks = _sb_key_blocks(k_ref, i * r - s, r, t, kind)
                vs = _sb_key_blocks(v_ref, i * r - s, r, t, kind)
                z = jnp.concatenate([_dot_nt(qs[g], ks[g]) for g in range(r)], axis=0) * scale
                e = jnp.exp(-jnp.abs(z))
                sp = jnp.maximum(z, 0.0) + jnp.log(1.0 + e)
                mask = _sb_mask(geo, i * r - s, s, t, pad, kind)
                spm = sp if mask is None else jnp.where(mask, sp, 0.0)
                w = jnp.exp(z - sp - _cum(spm, tri) + run)
                if mask is not None:
                    w = jnp.where(mask, w, 0.0)
                wb = w.astype(BF16)
                acc = acc + jnp.concatenate(
                    [jnp.dot(wb[g * t:(g + 1) * t, :], vs[g], preferred_element_type=F32) for g in range(r)], axis=0)
                c_ref[...] = jnp.where(lane == s, run, c_ref[...])
                run = run - jnp.sum(spm, axis=1, keepdims=True)
                return acc, run
            return step

        first, plain, low = make_step("first"), make_step("plain"), make_step("low")

        def alive(run):
            return (jnp.max(run) >= DEAD).astype(jnp.int32)

        def run_while(step, s, end, live, acc, run):
            def wbody(st):
                s, _, acc, run = st
                acc, run = step(s, (acc, run))
                return s + 1, alive(run), acc, run
            return lax.while_loop(lambda st: (st[0] < end) & (st[1] > 0), wbody, (s, live, acc, run))

        init = (jnp.zeros((tq, hd), F32), jnp.zeros((tq, 1), F32))
        carry = lax.fori_loop(0, jnp.minimum(i, 1), lambda _, cr: first(0, cr), init)
        acc, run = lax.fori_loop(0, 1 - jnp.minimum(i, 1), lambda _, cr: low(0, cr), carry)
        s, live, acc, run = run_while(plain, 1, i * r, alive(run), acc, run)
        end_low = jnp.where(s >= jnp.maximum(i * r, 1), (i + 1) * r, s)
        s, live, acc, run = run_while(low, s, end_low, live, acc, run)
        o_ref[...] = acc
        c_ref[...] = jnp.where(lane == LANES - 1, s.astype(F32), c_ref[...])

    return pl.pallas_call(
        body, name=name, grid=(nh, lp // tq),
        in_specs=[pl.BlockSpec((tq, hd), lambda h, i: (i, h)),
                  pl.BlockSpec((lp, hd), lambda h, i: (0, nh + h)),
                  pl.BlockSpec((lp, hd), lambda h, i: (0, 2 * nh + h))],
        out_specs=[pl.BlockSpec((tq, hd), lambda h, i: (i, h)),
                   pl.BlockSpec((None, tq, LANES), lambda h, i: (h, i, 0))],
        out_shape=[jax.ShapeDtypeStruct((lp, nh * hd), F32), jax.ShapeDtypeStruct((nh, lp, LANES), F32)],
        compiler_params=_cparams(("parallel", "arbitrary")),
    )(qkv, qkv, qkv)


def _sb_bwd(qkv, do, carry, d, name):
    nh, hd, lp, tq, t = d.sb_heads, d.hd, d.lp, d.tq, d.block
    r = tq // t
    scale = hd ** -0.5
    pad = d.pad

    def body(q_ref, k_ref, v_ref, do_ref, c_ref, dq_ref, dk_ref, dv_ref):
        i = pl.program_id(1)

        @pl.when(i == 0)
        def _():
            dk_ref[...] = jnp.zeros_like(dk_ref)
            dv_ref[...] = jnp.zeros_like(dv_ref)

        rows = lambda x, g: x[g * t:(g + 1) * t, :]
        qs = [q_ref[g * t:(g + 1) * t, :] for g in range(r)]
        dobs = [do_ref[g * t:(g + 1) * t, :].astype(BF16) for g in range(r)]
        geo = _sb_geometry(tq, t)
        ri = lax.broadcasted_iota(jnp.int32, (t, t), 0)
        ci = lax.broadcasted_iota(jnp.int32, (t, t), 1)
        tri_suf = (ri > ci).astype(BF16)
        tri_pre = (ri < ci).astype(BF16)
        lane = lax.broadcasted_iota(jnp.int32, (tq, LANES), 1)

        def make_step(kind):
            def step(s, carry):
                dq, pc = carry
                base = i * r - s
                ks = _sb_key_blocks(k_ref, base, r, t, kind)
                vs = _sb_key_blocks(v_ref, base, r, t, kind)
                z = jnp.concatenate([_dot_nt(qs[g], ks[g]) for g in range(r)], axis=0) * scale
                e = jnp.exp(-jnp.abs(z))
                sp = jnp.maximum(z, 0.0) + jnp.log(1.0 + e)
                mask = _sb_mask(geo, base, s, t, pad, kind)
                spm = sp if mask is None else jnp.where(mask, sp, 0.0)
                run = jnp.sum(jnp.where(lane == s, c_ref[...], 0.0), axis=1, keepdims=True)
                w = jnp.exp(z - sp - _cum(spm, tri_suf) + run)
                if mask is not None:
                    w = jnp.where(mask, w, 0.0)
                gw = w * jnp.concatenate([_dot_nt(dobs[g], vs[g]) for g in range(r)], axis=0)
                p = _cum(gw, tri_pre) + pc
                inv = 1.0 / (1.0 + e)
                sig = jnp.where(z >= 0.0, inv, e * inv)
                dz = (gw * (1.0 - sig) - sig * p) * scale
                if mask is not None:
                    dz = jnp.where(mask, dz, 0.0)
                dzb, wb = dz.astype(BF16), w.astype(BF16)
                dq = dq + jnp.concatenate(
                    [jnp.dot(rows(dzb, g), ks[g], preferred_element_type=F32) for g in range(r)], axis=0)
                dks = [_dot_tn(rows(dzb, g), qs[g]) for g in range(r)]
                dvs = [_dot_tn(rows(wb, g), dobs[g]) for g in range(r)]
                if kind == "low":
                    for g in range(r):
                        at = pl.ds(pl.multiple_of(jnp.maximum(base + g, 0) * t, t), t)
                        dk_ref[at, :] += dks[g]
                        dv_ref[at, :] += dvs[g]
                else:
                    at = pl.ds(pl.multiple_of(base * t, t), tq)
                    dk_ref[at, :] += jnp.concatenate(dks, axis=0)
                    dv_ref[at, :] += jnp.concatenate(dvs, axis=0)
                pc = pc + jnp.sum(gw, axis=1, keepdims=True)
                return dq, pc
            return step

        first, plain, low = make_step("first"), make_step("plain"), make_step("low")
        nsteps = jnp.max(jnp.where(lane == LANES - 1, c_ref[...], 0.0)).astype(jnp.int32)
        low_from = jnp.maximum(i * r, 1)
        plain_end = jnp.minimum(nsteps, low_from)
        carry = (jnp.zeros((tq, hd), F32), jnp.zeros((tq, 1), F32))
        carry = lax.fori_loop(0, jnp.maximum(nsteps - low_from, 0), lambda jj, cr: low(nsteps - 1 - jj, cr), carry)
        carry = lax.fori_loop(0, plain_end - 1, lambda jj, cr: plain(plain_end - 1 - jj, cr), carry)
        carry = lax.fori_loop(0, jnp.minimum(i, 1), lambda _, cr: first(0, cr), carry)
        dq, _ = lax.fori_loop(0, 1 - jnp.minimum(i, 1), lambda _, cr: low(0, cr), carry)
        dq_ref[...] = dq.astype(BF16)

    w3 = nh * hd
    return pl.pallas_call(
        body, name=name, grid=(nh, lp // tq),
        in_specs=[pl.BlockSpec((tq, hd), lambda h, i: (i, h)),
                  pl.BlockSpec((lp, hd), lambda h, i: (0, nh + h)),
                  pl.BlockSpec((lp, hd), lambda h, i: (0, 2 * nh + h)),
                  pl.BlockSpec((tq, hd), lambda h, i: (i, h)),
                  pl.BlockSpec((None, tq, LANES), lambda h, i: (h, i, 0))],
        out_specs=[pl.BlockSpec((tq, hd), lambda h, i: (i, h)),
                   pl.BlockSpec((lp, hd), lambda h, i: (0, h)),
                   pl.BlockSpec((lp, hd), lambda h, i: (0, h))],
        out_shape=[jax.ShapeDtypeStruct((lp, w3), BF16), jax.ShapeDtypeStruct((lp, w3), F32),
                   jax.ShapeDtypeStruct((lp, w3), F32)],
        compiler_params=_cparams(("arbitrary", "arbitrary")),
    )(qkv, qkv, qkv, do, carry)


def _mla_prep_fwd(qraw, projb, ctab, stab, d, name):
    lp, nh = d.lp, d.mla_heads
    tr = _pick(lp, (256, 128))
    kidx = d.kr_off // LANES

    def rope(u, c, s):
        return u * c + pltpu.roll(u, LANES // 2, 1) * s

    def body(q_ref, k_ref, c_ref, s_ref, qm_ref, kr_ref):
        c, s = c_ref[...], s_ref[...]
        for h in range(nh):
            base = 2 * LANES * h
            qm_ref[:, base:base + LANES] = q_ref[:, base:base + LANES].astype(BF16)
            qm_ref[:, base + LANES:base + 2 * LANES] = rope(q_ref[:, base + LANES:base + 2 * LANES], c, s).astype(BF16)
        kr_ref[...] = rope(k_ref[...], c, s).astype(BF16)

    tab = pl.BlockSpec((tr, LANES), lambda i: (i, 0))
    return pl.pallas_call(
        body, name=name, grid=(lp // tr,),
        in_specs=[pl.BlockSpec((tr, d.qw), lambda i: (i, 0)), pl.BlockSpec((tr, LANES), lambda i: (i, kidx)), tab, tab],
        out_specs=[pl.BlockSpec((tr, d.qw), lambda i: (i, 0)), tab],
        out_shape=[jax.ShapeDtypeStruct((lp, d.qw), BF16), jax.ShapeDtypeStruct((lp, LANES), BF16)],
        compiler_params=_cparams(("parallel",)))(qraw, projb, ctab, stab)


def _mla_prep_bwd(dqm, dkr, ctab, stab, d, name):
    lp, nh = d.lp, d.mla_heads
    tr = _pick(lp, (256, 128))

    def unrope(g, c, s):
        return g * c + pltpu.roll(g * s, LANES // 2, 1)

    def body(dq_ref, dk_ref, c_ref, s_ref, o_ref, ok_ref):
        c, s = c_ref[...], s_ref[...]
        for h in range(nh):
            base = 2 * LANES * h
            o_ref[:, base:base + LANES] = dq_ref[:, base:base + LANES].astype(BF16)
            o_ref[:, base + LANES:base + 2 * LANES] = unrope(dq_ref[:, base + LANES:base + 2 * LANES], c, s).astype(BF16)
        ok_ref[...] = unrope(dk_ref[...], c, s).astype(BF16)

    tab = pl.BlockSpec((tr, LANES), lambda i: (i, 0))
    wide = pl.BlockSpec((tr, d.qw), lambda i: (i, 0))
    return pl.pallas_call(
        body, name=name, grid=(lp // tr,), in_specs=[wide, tab, tab, tab], out_specs=[wide, tab],
        out_shape=[jax.ShapeDtypeStruct((lp, d.qw), BF16), jax.ShapeDtypeStruct((lp, LANES), BF16)],
        compiler_params=_cparams(("parallel",)))(dqm, dkr, ctab, stab)


def _mla_fwd(qm, kv, kr, d, name, ride=()):
    nh, lp, t = d.mla_heads, d.lp, d.tq
    scale = (d.nope + d.rope) ** -0.5
    pad = d.pad
    nr = len(ride)

    def body(*refs):
        q_ref, kn_ref, v_ref, kr_ref = refs[:4]
        o_ref, lse_ref = refs[4 + nr:6 + nr]
        h = pl.program_id(0)
        i = pl.program_id(1)
        if nr:
            ride_refs = (refs[4:4 + nr], refs[6 + nr:6 + 2 * nr]) + tuple(refs[6 + 2 * nr:])

            @pl.when((h == 0) & (i == 0))
            def _():
                _Gather(*ride_refs).start()

        q = q_ref[...]
        row = i * t + lax.broadcasted_iota(jnp.int32, (t, t), 0)
        colb = lax.broadcasted_iota(jnp.int32, (t, t), 1)

        def make_step(masked):
            def step(j, carry):
                acc, m, l = carry
                off = pl.multiple_of(j * t, t)
                kc = jnp.concatenate([kn_ref[pl.ds(off, t), :], kr_ref[pl.ds(off, t), :]], axis=1)
                s = _dot_nt(q, kc) * scale
                if masked:
                    col = j * t + colb
                    s = jnp.where((col <= row) & (col >= pad), s, NEG)
                m_new = jnp.maximum(m, jnp.max(s, axis=1, keepdims=True))
                alpha = jnp.exp(m - m_new)
                p = jnp.exp(s - m_new)
                l = alpha * l + jnp.sum(p, axis=1, keepdims=True)
                acc = alpha * acc + jnp.dot(p.astype(BF16), v_ref[pl.ds(off, t), :], preferred_element_type=F32)
                return acc, m_new, l
            return step

        masked, plain = make_step(True), make_step(False)
        carry = masked(0, (jnp.zeros((t, LANES), F32), jnp.full((t, 1), NEG, F32), jnp.zeros((t, 1), F32)))
        carry = lax.fori_loop(1, i, plain, carry)
        acc, m, l = lax.fori_loop(i, i + jnp.minimum(i, 1), masked, carry)
        rowv = i * t + lax.broadcasted_iota(jnp.int32, (t, LANES), 0)
        o_ref[...] = jnp.where(rowv >= pad, acc / l, 0.0)
        lse_ref[...] = m + jnp.log(l)
        if nr:
            @pl.when((h == nh - 1) & (i == lp // t - 1))
            def _():
                _Gather(*ride_refs).finish()

    hbm = pl.BlockSpec(memory_space=pltpu.HBM)
    res = pl.pallas_call(
        body, name=name, grid=(nh, lp // t),
        in_specs=[pl.BlockSpec((t, 2 * LANES), lambda h, i: (i, h)),
                  pl.BlockSpec((lp, LANES), lambda h, i: (0, h)),
                  pl.BlockSpec((lp, LANES), lambda h, i: (0, nh + h)),
                  pl.BlockSpec((lp, LANES), lambda h, i: (0, 0))] + [hbm] * nr,
        out_specs=[pl.BlockSpec((t, LANES), lambda h, i: (i, h)),
                   pl.BlockSpec((None, t, 1), lambda h, i: (h, i, 0))] + [hbm] * nr,
        out_shape=[jax.ShapeDtypeStruct((lp, nh * LANES), F32), jax.ShapeDtypeStruct((nh, lp, 1), F32)]
                  + _gather_out_shapes(ride),
        scratch_shapes=_gather_sems(nr) if nr else [],
        compiler_params=_cparams(("arbitrary", "arbitrary")),
    )(qm, kv, kv, kr, *ride)
    return res[:2], list(res[2:])


def _mla_bwd(qm, kv, kr, o, do, lse, d, name, ride=()):
    nh, lp, t = d.mla_heads, d.lp, d.tq
    scale = (d.nope + d.rope) ** -0.5
    pad = d.pad
    nr = len(ride)

    def body(*refs):
        q_ref, kn_ref, v_ref, kr_ref, o_ref, do_ref, lse_ref = refs[:7]
        dq_ref, dkn_ref, dv_ref, dkr_ref = refs[7 + nr:11 + nr]
        h = pl.program_id(0)
        i = pl.program_id(1)
        if nr:
            ride_refs = (refs[7:7 + nr], refs[11 + nr:11 + 2 * nr]) + tuple(refs[11 + 2 * nr:])

            @pl.when((h == 0) & (i == 0))
            def _():
                _chip_copies_start(_chip_copies(*ride_refs))

        @pl.when(i == 0)
        def _():
            dkn_ref[...] = jnp.zeros_like(dkn_ref)
            dv_ref[...] = jnp.zeros_like(dv_ref)

        @pl.when((i == 0) & (h == 0))
        def _():
            dkr_ref[...] = jnp.zeros_like(dkr_ref)

        q = q_ref[...]
        dof = do_ref[...]
        dob = dof.astype(BF16)
        delta = jnp.sum(dof * o_ref[...], axis=1, keepdims=True)
        lse = lse_ref[...]
        row = i * t + lax.broadcasted_iota(jnp.int32, (t, t), 0)
        colb = lax.broadcasted_iota(jnp.int32, (t, t), 1)

        def make_step(masked):
            def step(j, dq):
                off = pl.multiple_of(j * t, t)
                kc = jnp.concatenate([kn_ref[pl.ds(off, t), :], kr_ref[pl.ds(off, t), :]], axis=1)
                v = v_ref[pl.ds(off, t), :]
                s = _dot_nt(q, kc) * scale
                if masked:
                    col = j * t + colb
                    mask = (col <= row) & (col >= pad)
                    p = jnp.where(mask, jnp.exp(jnp.where(mask, s, NEG) - lse), 0.0)
                else:
                    p = jnp.exp(s - lse)
                dp = _dot_nt(dob, v)
                dsb = (p * (dp - delta) * scale).astype(BF16)
                dq = dq + jnp.dot(dsb, kc, preferred_element_type=F32)
                dkc = _dot_tn(dsb, q)
                dkn_ref[pl.ds(off, t), :] += dkc[:, :LANES]
                dkr_ref[pl.ds(off, t), :] += dkc[:, LANES:]
                dv_ref[pl.ds(off, t), :] += _dot_tn(p.astype(BF16), dob)
                return dq
            return step

        masked, plain = make_step(True), make_step(False)
        dq = masked(0, jnp.zeros((t, 2 * LANES), F32))
        dq = lax.fori_loop(1, i, plain, dq)
        dq_ref[...] = lax.fori_loop(i, i + jnp.minimum(i, 1), masked, dq)
        if nr:
            @pl.when((h == nh - 1) & (i == lp // t - 1))
            def _():
                _chip_copies_wait(_chip_copies(*ride_refs))

    hbm = pl.BlockSpec(memory_space=pltpu.HBM)
    res = pl.pallas_call(
        body, name=name, grid=(nh, lp // t),
        in_specs=[pl.BlockSpec((t, 2 * LANES), lambda h, i: (i, h)),
                  pl.BlockSpec((lp, LANES), lambda h, i: (0, h), pipeline_mode=pl.Buffered(1)),
                  pl.BlockSpec((lp, LANES), lambda h, i: (0, nh + h), pipeline_mode=pl.Buffered(1)),
                  pl.BlockSpec((lp, LANES), lambda h, i: (0, 0), pipeline_mode=pl.Buffered(1)),
                  pl.BlockSpec((t, LANES), lambda h, i: (i, h)),
                  pl.BlockSpec((t, LANES), lambda h, i: (i, h)),
                  pl.BlockSpec((None, t, 1), lambda h, i: (h, i, 0))] + [hbm] * nr,
        out_specs=[pl.BlockSpec((t, 2 * LANES), lambda h, i: (i, h)),
                   pl.BlockSpec((lp, LANES), lambda h, i: (0, h)),
                   pl.BlockSpec((lp, LANES), lambda h, i: (0, h)),
                   pl.BlockSpec((lp, LANES), lambda h, i: (0, 0))] + [hbm] * nr,
        out_shape=[jax.ShapeDtypeStruct((lp, nh * 2 * LANES), F32), jax.ShapeDtypeStruct((lp, nh * LANES), F32),
                   jax.ShapeDtypeStruct((lp, nh * LANES), F32), jax.ShapeDtypeStruct((lp, LANES), F32)]
                  + [jax.ShapeDtypeStruct(x.shape, x.dtype) for x in ride],
        scratch_shapes=_chip_copies_sems(nr) if nr else [],
        compiler_params=_cparams(("arbitrary", "arbitrary"), V7X_VMEM_LIMIT_BIG),
    )(qm, kv, kv, kr, o, do, lse, *ride)
    return res[:4], list(res[4:])


def _sigmoid(x):
    return 1.0 / (1.0 + jnp.exp(-x))


def _gate_fwd(projb, b_sb, b_mla, d, name):
    lp, tg = d.lp, d.tg
    tr = _pick(lp, (256, 128))
    o1, o2 = d.g_off // tg, (d.g_off + d.d) // tg

    def body(g1_ref, g2_ref, b1_ref, b2_ref, o_ref):
        o_ref[...] = (_sigmoid(g1_ref[...]) * b1_ref[...] + _sigmoid(g2_ref[...]) * b2_ref[...]).astype(BF16)

    blk = pl.BlockSpec((tr, tg), lambda i, j: (i, j))
    return pl.pallas_call(
        body, name=name, grid=(lp // tr, d.d // tg),
        in_specs=[pl.BlockSpec((tr, tg), lambda i, j: (i, o1 + j)), pl.BlockSpec((tr, tg), lambda i, j: (i, o2 + j)),
                  blk, blk],
        out_specs=blk, out_shape=jax.ShapeDtypeStruct((lp, d.d), BF16),
        compiler_params=_cparams(("parallel", "parallel")))(projb, projb, b_sb, b_mla)


def _gate_bwd(dm, projb, b_sb, b_mla, d, name):
    lp, tg = d.lp, d.tg
    tr = _pick(lp, (256, 128))
    o1, o2 = d.g_off // tg, (d.g_off + d.d) // tg

    def body(dm_ref, g1_ref, g2_ref, b1_ref, b2_ref, db1_ref, db2_ref, dg1_ref, dg2_ref):
        dmv = dm_ref[...]
        s1, s2 = _sigmoid(g1_ref[...]), _sigmoid(g2_ref[...])
        db1_ref[...] = (dmv * s1).astype(BF16)
        db2_ref[...] = (dmv * s2).astype(BF16)
        dg1_ref[...] = (dmv * b1_ref[...] * s1 * (1.0 - s1)).astype(BF16)
        dg2_ref[...] = (dmv * b2_ref[...] * s2 * (1.0 - s2)).astype(BF16)

    blk = pl.BlockSpec((tr, tg), lambda i, j: (i, j))
    out = jax.ShapeDtypeStruct((lp, d.d), BF16)
    return pl.pallas_call(
        body, name=name, grid=(lp // tr, d.d // tg),
        in_specs=[blk, pl.BlockSpec((tr, tg), lambda i, j: (i, o1 + j)), pl.BlockSpec((tr, tg), lambda i, j: (i, o2 + j)),
                  blk, blk],
        out_specs=[blk] * 4, out_shape=[out] * 4,
        compiler_params=_cparams(("parallel", "parallel")))(dm, projb, projb, b_sb, b_mla)


HALO = 8


def _conv_tiles(d):
    return _pick(d.lp, (640, 512, 256, 128)), _pick(d.f, (512, 256, 128))


def _convglu_fwd(up, cw, cb, d, name):
    lp, f = d.lp, d.f
    tr, tc = _conv_tiles(d)
    nf = f // tc
    hb = tr // HALO
    pad = d.pad

    def body(a_ref, g_ref, pa_ref, pg_ref, wa_ref, wg_ref, ba_ref, bg_ref, o_ref, xa, xg):
        i = pl.program_id(1)
        keep = (i > 0).astype(F32)
        xa[0:HALO, :] = pa_ref[...] * keep
        xg[0:HALO, :] = pg_ref[...] * keep
        xa[HALO:, :] = a_ref[...]
        xg[HALO:, :] = g_ref[...]

        def conv(x, w_ref, b_ref):
            return (b_ref[...] + x[pl.ds(HALO - 2, tr), :] * w_ref[0:1, :] + x[pl.ds(HALO - 1, tr), :] * w_ref[1:2, :]
                    + x[pl.ds(HALO, tr), :] * w_ref[2:3, :])

        ua = conv(xa, wa_ref, ba_ref)
        ug = conv(xg, wg_ref, bg_ref)
        row = i * tr + lax.broadcasted_iota(jnp.int32, (tr, tc), 0)
        o_ref[...] = jnp.where(row >= pad, ua * _sigmoid(ua) * ug, 0.0).astype(BF16)

    prev = lambda j, i: (jnp.maximum(i * hb - 1, 0), j)
    prevg = lambda j, i: (jnp.maximum(i * hb - 1, 0), nf + j)
    return pl.pallas_call(
        body, name=name, grid=(nf, lp // tr),
        in_specs=[pl.BlockSpec((tr, tc), lambda j, i: (i, j)), pl.BlockSpec((tr, tc), lambda j, i: (i, nf + j)),
                  pl.BlockSpec((HALO, tc), prev), pl.BlockSpec((HALO, tc), prevg),
                  pl.BlockSpec((3, tc), lambda j, i: (0, j)), pl.BlockSpec((3, tc), lambda j, i: (0, nf + j)),
                  pl.BlockSpec((1, tc), lambda j, i: (0, j)), pl.BlockSpec((1, tc), lambda j, i: (0, nf + j))],
        out_specs=pl.BlockSpec((tr, tc), lambda j, i: (i, j)),
        out_shape=jax.ShapeDtypeStruct((lp, f), BF16),
        scratch_shapes=[pltpu.VMEM((tr + HALO, tc), F32), pltpu.VMEM((tr + HALO, tc), F32)],
        compiler_params=_cparams(("parallel", "arbitrary")))(up, up, up, up, cw, cw, cb, cb)


def _convglu_bwd(up, dact, cw, cb, d, name):
    lp, f = d.lp, d.f
    tr, tc = _conv_tiles(d)
    nf = f // tc
    hb = tr // HALO
    nrow = lp // tr
    pad = d.pad
    te = tr + HALO

    def body(a_ref, g_ref, pa_ref, pg_ref, na_ref, ng_ref, da_ref, nd_ref, wa_ref, wg_ref, ba_ref, bg_ref,
             oa_ref, og_ref, sa_ref, sg_ref, xa, xg, xd, ya, yg):
        i = pl.program_id(1)

        @pl.when(i == 0)
        def _():
            sa_ref[...] = jnp.zeros_like(sa_ref)
            sg_ref[...] = jnp.zeros_like(sg_ref)

        keep_p = (i > 0).astype(F32)
        keep_n = (i < nrow - 1).astype(F32)
        xa[0:HALO, :] = pa_ref[...] * keep_p
        xg[0:HALO, :] = pg_ref[...] * keep_p
        xa[HALO:HALO + tr, :] = a_ref[...]
        xg[HALO:HALO + tr, :] = g_ref[...]
        xa[HALO + tr:, :] = na_ref[...] * keep_n
        xg[HALO + tr:, :] = ng_ref[...] * keep_n
        xd[0:tr, :] = da_ref[...]
        xd[tr:, :] = nd_ref[...] * keep_n

        def conv(x, w_ref, b_ref):
            taps = [x[pl.ds(HALO - 2 + tap, te), :] for tap in range(3)]
            return b_ref[...] + taps[0] * w_ref[0:1, :] + taps[1] * w_ref[1:2, :] + taps[2] * w_ref[2:3, :], taps

        ua, taps_a = conv(xa, wa_ref, ba_ref)
        ug, taps_g = conv(xg, wg_ref, bg_ref)
        sg = _sigmoid(ua)
        dact = xd[...]
        ya_v = dact * ug * (sg * (1.0 + ua * (1.0 - sg)))
        yg_v = dact * (ua * sg)
        ya[...] = ya_v
        yg[...] = yg_v
        row = i * tr + lax.broadcasted_iota(jnp.int32, (tr, tc), 0)

        def back(y, y_v, taps, w_ref, o_ref, s_ref):
            y0 = y_v[:tr]
            dup = y0 * w_ref[2:3, :] + y[pl.ds(1, tr), :] * w_ref[1:2, :] + y[pl.ds(2, tr), :] * w_ref[0:1, :]
            o_ref[...] = jnp.where(row >= pad, dup, 0.0).astype(BF16)
            for tap in range(3):
                s_ref[tap:tap + 1, :] += jnp.sum(y0 * taps[tap][:tr], axis=0, keepdims=True)
            s_ref[3:4, :] += jnp.sum(y0, axis=0, keepdims=True)

        back(ya, ya_v, taps_a, wa_ref, oa_ref, sa_ref)
        back(yg, yg_v, taps_g, wg_ref, og_ref, sg_ref)

    last8 = lp // HALO - 1
    prev = lambda j, i: (jnp.maximum(i * hb - 1, 0), j)
    prevg = lambda j, i: (jnp.maximum(i * hb - 1, 0), nf + j)
    nxt = lambda j, i: (jnp.minimum((i + 1) * hb, last8), j)
    nxtg = lambda j, i: (jnp.minimum((i + 1) * hb, last8), nf + j)
    halo = lambda m: pl.BlockSpec((HALO, tc), m)
    main = pl.BlockSpec((tr, tc), lambda j, i: (i, j))
    sums = pl.BlockSpec((8, tc), lambda j, i: (0, j))
    return pl.pallas_call(
        body, name=name, grid=(nf, nrow),
        in_specs=[main, pl.BlockSpec((tr, tc), lambda j, i: (i, nf + j)), halo(prev), halo(prevg), halo(nxt), halo(nxtg),
                  main, halo(nxt),
                  pl.BlockSpec((3, tc), lambda j, i: (0, j)), pl.BlockSpec((3, tc), lambda j, i: (0, nf + j)),
                  pl.BlockSpec((1, tc), lambda j, i: (0, j)), pl.BlockSpec((1, tc), lambda j, i: (0, nf + j))],
        out_specs=[main, main, sums, sums],
        out_shape=[jax.ShapeDtypeStruct((lp, f), BF16), jax.ShapeDtypeStruct((lp, f), BF16),
                   jax.ShapeDtypeStruct((8, f), F32), jax.ShapeDtypeStruct((8, f), F32)],
        scratch_shapes=[pltpu.VMEM((tr + 2 * HALO, tc), F32), pltpu.VMEM((tr + 2 * HALO, tc), F32),
                        pltpu.VMEM((te, tc), F32), pltpu.VMEM((te, tc), F32), pltpu.VMEM((te, tc), F32)],
        compiler_params=_cparams(("parallel", "arbitrary")))(up, up, up, up, up, up, dact, dact, cw, cw, cb, cb)


def _head(h, add, target, g, d, name):
    lp, dm, t = d.lp, d.d, d.block
    inv_d = 1.0 / dm

    def body(h_ref, a_ref, t_ref, g_ref, dh_ref, dhb_ref, loss_ref, dg_ref):
        i = pl.program_id(0)

        @pl.when(i == 0)
        def _():
            dh_ref[...] = jnp.zeros_like(dh_ref)
            dhb_ref[...] = jnp.zeros_like(dhb_ref)
            loss_ref[...] = jnp.zeros_like(loss_ref)
            dg_ref[...] = jnp.zeros_like(dg_ref)

        @pl.when(i > 0)
        def _():
            x, gv = h_ref[...] + a_ref[...], g_ref[...]
            r = lax.rsqrt(jnp.mean(x * x, axis=1, keepdims=True) + EPS)
            xh = x * r
            err = xh * gv - t_ref[...]
            loss_ref[...] += 0.5 * inv_d * jnp.sum(err * err)
            dy = err * inv_d
            gy = dy * gv
            c = jnp.mean(gy * x, axis=1, keepdims=True)
            dh = r * gy - x * (r * r * r) * c
            dh_ref[...] = dh
            dhb_ref[...] = dh.astype(BF16)
            dg_ref[...] += jnp.sum(dy * xh, axis=0, keepdims=True)

    blk = pl.BlockSpec((t, dm), lambda i: (i, 0))
    return pl.pallas_call(
        body, name=name, grid=(lp // t,),
        in_specs=[blk, blk, pl.BlockSpec((t, dm), lambda i: (jnp.maximum(i - 1, 0), 0)),
                  pl.BlockSpec((1, dm), lambda i: (0, 0))],
        out_specs=[blk, blk, pl.BlockSpec((8, LANES), lambda i: (0, 0)), pl.BlockSpec((1, dm), lambda i: (0, 0))],
        out_shape=[jax.ShapeDtypeStruct((lp, dm), F32), jax.ShapeDtypeStruct((lp, dm), BF16),
                   jax.ShapeDtypeStruct((8, LANES), F32), jax.ShapeDtypeStruct((1, dm), F32)],
        compiler_params=_cparams(("arbitrary",)))(h, add, target, g.reshape(1, dm))


def _exchange(xs, *, scatter, name):
    n = len(xs)
    nf = len(FLIPS)

    def body(*refs):
        ins, outs = refs[:n], refs[n:2 * n]
        send_sems, recv_sems, loc_sems = refs[2 * n:]
        x, y, c = lax.axis_index("x"), lax.axis_index("y"), lax.axis_index("c")
        me = 4 * x + 2 * y + c
        sends, recvs, locs = [], [], []
        for a in range(n):
            src_me = ins[a].at[me] if scatter else ins[a]
            loc = pltpu.make_async_copy(src_me, outs[a].at[me], loc_sems.at[a])
            loc.start()
            locs.append(loc)
            for k, (fx, fy, fc) in enumerate(FLIPS):
                px, py, pc = x ^ fx, y ^ fy, c ^ fc
                peer = 4 * px + 2 * py + pc
                src = ins[a].at[peer] if scatter else ins[a]
                cp = pltpu.make_async_remote_copy(
                    src_ref=src, dst_ref=outs[a].at[me], send_sem=send_sems.at[a * nf + k],
                    recv_sem=recv_sems.at[a * nf + k], device_id=(px, py, pc), device_id_type=pl.DeviceIdType.MESH)
                cp.start()
                sends.append(cp)
                recvs.append(pltpu.make_async_remote_copy(
                    src_ref=src, dst_ref=outs[a].at[peer], send_sem=send_sems.at[a * nf + k],
                    recv_sem=recv_sems.at[a * nf + k], device_id=(px, py, pc), device_id_type=pl.DeviceIdType.MESH))
        for cp in recvs:
            cp.wait_recv()
        for cp in sends:
            cp.wait_send()
        for loc in locs:
            loc.wait()

    hbm = pl.BlockSpec(memory_space=pltpu.HBM)
    out_shape = [jax.ShapeDtypeStruct(((N_DEV,) + tuple(x.shape[1:])) if scatter else ((N_DEV,) + tuple(x.shape)), x.dtype)
                 for x in xs]
    return pl.pallas_call(
        body, name=name, in_specs=[hbm] * n, out_specs=[hbm] * n, out_shape=out_shape,
        scratch_shapes=[pltpu.SemaphoreType.DMA((n * nf,)), pltpu.SemaphoreType.DMA((n * nf,)),
                        pltpu.SemaphoreType.DMA((n,))],
    )(*xs)


def _gather_two_level(xs, name):
    n = len(xs)

    def body(*refs):
        gather = _Gather(refs[:n], refs[n:2 * n], *refs[2 * n:])
        gather.start()
        gather.finish()

    hbm = pl.BlockSpec(memory_space=pltpu.HBM)
    return pl.pallas_call(
        body, name=name, in_specs=[hbm] * n, out_specs=[hbm] * n,
        out_shape=_gather_out_shapes(xs), scratch_shapes=_gather_sems(n),
    )(*xs)


def _gather_out_shapes(xs):
    return [jax.ShapeDtypeStruct((N_DEV,) + tuple(x.shape), x.dtype) for x in xs]


def _gather_sems(n):
    return [pltpu.SemaphoreType.DMA((7 * n,)), pltpu.SemaphoreType.DMA((7 * n,)), pltpu.SemaphoreType.DMA((n,))]


class _Gather:
    def __init__(self, ins, outs, send_sems, recv_sems, loc_sems):
        self.ins, self.outs, self.sems = ins, outs, (send_sems, recv_sems, loc_sems)
        self.x, self.y, self.c = lax.axis_index("x"), lax.axis_index("y"), lax.axis_index("c")
        self.me, self.sibling = (self.x, self.y, self.c), (self.x, self.y, 1 - self.c)
        self.chips = [(1 - self.x, self.y), (self.x, 1 - self.y), (1 - self.x, 1 - self.y)]

    def slot(self, a, dev):
        return self.outs[a].at[4 * dev[0] + 2 * dev[1] + dev[2]]

    def copy(self, a, k, block, to, src=None):
        return pltpu.make_async_remote_copy(
            src_ref=self.slot(a, block) if src is None else src, dst_ref=self.slot(a, block),
            send_sem=self.sems[0].at[7 * a + k], recv_sem=self.sems[1].at[7 * a + k],
            device_id=to, device_id_type=pl.DeviceIdType.MESH)

    def local(self, a):
        return pltpu.make_async_copy(self.ins[a], self.slot(a, self.me), self.sems[2].at[a])

    def first(self):
        out = []
        for a in range(len(self.ins)):
            out.append(self.copy(a, 0, self.me, self.sibling, src=self.ins[a]))
            out += [self.copy(a, 1 + j, self.me, (*chip, self.c), src=self.ins[a]) for j, chip in enumerate(self.chips)]
        return out

    def start(self):
        for a in range(len(self.ins)):
            self.local(a).start()
        for cp in self.first():
            cp.start()

    def finish(self):
        n, c = len(self.ins), self.c
        passed = []
        for j, chip in enumerate(self.chips):
            for a in range(n):
                self.copy(a, 1 + j, (*chip, c), self.me).wait_recv()
                fwd = self.copy(a, 4 + j, (*chip, c), self.sibling)
                fwd.start()
                passed.append(fwd)
        for a in range(n):
            self.copy(a, 0, self.sibling, self.me).wait_recv()
            for j, chip in enumerate(self.chips):
                self.copy(a, 4 + j, (*chip, 1 - c), self.me).wait_recv()
        for cp in self.first() + passed:
            cp.wait_send()
        for a in range(n):
            self.local(a).wait()


def _pair_exchange(xs, name):
    n = len(xs)

    def body(*refs):
        ins, outs = refs[:n], refs[n:2 * n]
        send_sems, recv_sems = refs[2 * n:]
        x, y, c = lax.axis_index("x"), lax.axis_index("y"), lax.axis_index("c")
        copies = []
        for a in range(n):
            for q in range(4):
                cp = pltpu.make_async_remote_copy(
                    src_ref=ins[a].at[2 * q + (1 - c)], dst_ref=outs[a].at[q], send_sem=send_sems.at[4 * a + q],
                    recv_sem=recv_sems.at[4 * a + q], device_id=(x, y, 1 - c), device_id_type=pl.DeviceIdType.MESH)
                cp.start()
                copies.append(cp)
        for cp in copies:
            cp.wait_recv()
        for cp in copies:
            cp.wait_send()

    hbm = pl.BlockSpec(memory_space=pltpu.HBM)
    return pl.pallas_call(
        body, name=name, in_specs=[hbm] * n, out_specs=[hbm] * n,
        out_shape=[jax.ShapeDtypeStruct((4,) + tuple(x.shape[1:]), x.dtype) for x in xs],
        scratch_shapes=[pltpu.SemaphoreType.DMA((4 * n,)), pltpu.SemaphoreType.DMA((4 * n,))],
    )(*xs)


def _chip_exchange(xs, name):
    n = len(xs)

    def body(*refs):
        copies = _chip_copies(refs[:n], refs[n:2 * n], *refs[2 * n:])
        _chip_copies_start(copies)
        _chip_copies_wait(copies)

    hbm = pl.BlockSpec(memory_space=pltpu.HBM)
    return pl.pallas_call(
        body, name=name, in_specs=[hbm] * n, out_specs=[hbm] * n,
        out_shape=[jax.ShapeDtypeStruct(x.shape, x.dtype) for x in xs],
        scratch_shapes=_chip_copies_sems(n),
    )(*xs)


def _chip_copies_sems(n):
    return [pltpu.SemaphoreType.DMA((3 * n,)), pltpu.SemaphoreType.DMA((3 * n,)), pltpu.SemaphoreType.DMA((n,))]


def _chip_copies(ins, outs, send_sems, recv_sems, loc_sems):
    x, y, c = lax.axis_index("x"), lax.axis_index("y"), lax.axis_index("c")
    mine = 2 * x + y
    locs, sends, recvs = [], [], []
    for a in range(len(ins)):
        locs.append(pltpu.make_async_copy(ins[a].at[mine], outs[a].at[mine], loc_sems.at[a]))
        for k, (fx, fy) in enumerate([(1, 0), (0, 1), (1, 1)]):
            px, py = x ^ fx, y ^ fy
            peer = 2 * px + py
            sems = dict(send_sem=send_sems.at[3 * a + k], recv_sem=recv_sems.at[3 * a + k],
                        device_id=(px, py, c), device_id_type=pl.DeviceIdType.MESH)
            sends.append(pltpu.make_async_remote_copy(src_ref=ins[a].at[peer], dst_ref=outs[a].at[mine], **sems))
            recvs.append(pltpu.make_async_remote_copy(src_ref=ins[a].at[peer], dst_ref=outs[a].at[peer], **sems))
    return locs, sends, recvs


def _chip_copies_start(copies):
    locs, sends, _ = copies
    for cp in locs + sends:
        cp.start()


def _chip_copies_wait(copies):
    locs, sends, recvs = copies
    for cp in recvs:
        cp.wait_recv()
    for cp in sends:
        cp.wait_send()
    for cp in locs:
        cp.wait()


def _pair_sum(a, b, name):
    shape = a.shape
    cols = shape[-1]
    rows = a.size // cols
    a, b = a.reshape(rows, cols), b.reshape(rows, cols)
    tr = next((t for t in (512, 256, 128, 64, 32, 16) if rows % t == 0 and t * cols <= 2 * ADAMW_TILE_ELEMS), rows)

    def body(a_ref, b_ref, o_ref):
        o_ref[...] = (a_ref[...].astype(F32) + b_ref[...].astype(F32)).astype(o_ref.dtype)

    blk = pl.BlockSpec((tr, cols), lambda i: (i, 0))
    return pl.pallas_call(body, name=name, grid=(rows // tr,), in_specs=[blk, blk], out_specs=blk,
                          out_shape=jax.ShapeDtypeStruct((rows, cols), a.dtype),
                          compiler_params=_cparams(("parallel",)))(a, b).reshape(shape)


def _adamw(parts, w, m, v, name):
    shape = w.shape
    cols = shape[-1]
    rows = w.size // cols
    nparts = parts.shape[0]
    parts, w, m, v = parts.reshape(nparts, rows, cols), w.reshape(rows, cols), m.reshape(rows, cols), v.reshape(rows, cols)
    tr = next((t for t in (256, 128, 64, 32, 16) if rows % t == 0 and t * cols <= ADAMW_TILE_ELEMS), rows)
    c1 = 1.0 - ADAM_B1 ** ADAM_STEP
    c2 = 1.0 - ADAM_B2 ** ADAM_STEP

    def body(p_ref, w_ref, m_ref, v_ref, g_ref, d_ref, mo_ref, vo_ref):
        g = p_ref[0].astype(F32)
        for q in range(1, nparts):
            g = g + p_ref[q].astype(F32)
        mn = ADAM_B1 * m_ref[...] + (1.0 - ADAM_B1) * g
        vn = ADAM_B2 * v_ref[...] + (1.0 - ADAM_B2) * (g * g)
        g_ref[...] = g
        mo_ref[...] = mn
        vo_ref[...] = vn
        d_ref[...] = -ADAM_LR * ((mn / c1) / (jnp.sqrt(vn / c2) + ADAM_EPS) + ADAM_WD * w_ref[...])

    blk = pl.BlockSpec((tr, cols), lambda i: (i, 0))
    out = jax.ShapeDtypeStruct((rows, cols), F32)
    res = pl.pallas_call(
        body, name=name, grid=(rows // tr,),
        in_specs=[pl.BlockSpec((nparts, tr, cols), lambda i: (0, i, 0)), blk, blk, blk],
        out_specs=[blk] * 4, out_shape=[out] * 4, compiler_params=_cparams(("parallel",)))(parts, w, m, v)
    return [r.reshape(shape) for r in res]


BIG = ["w_in", "w_uq", "w_ukv", "w_sb_out", "w_mla_out", "w_o", "w_up", "w_down"]
ROW_SHARDED = {"w_o", "w_down"}
SMALL = ["conv_w", "meta_tokens"]
SHARDED = BIG + SMALL
REPL = ["norm_mix", "q_norm", "kv_norm", "norm_ffn", "conv_b", "final_norm"]


def _pack_rows(flat, row_mult):
    unit = PACK_W * row_mult
    total = -(-flat.shape[0] // unit) * unit
    return jnp.pad(flat, (0, total - flat.shape[0])).reshape(-1, PACK_W)


def _padded_cols(c):
    return -(-c // LANES) * LANES


def _pad_block(a, name):
    c = a.shape[-1]
    if name in ROW_SHARDED or c % LANES == 0:
        return a
    return jnp.pad(a, [(0, 0)] * (a.ndim - 1) + [(0, _padded_cols(c) - c)])


def _full_from_slots(slots, name, c):
    if name in ROW_SHARDED:
        return jnp.transpose(slots, (1, 0, 2, 3)).reshape(slots.shape[1], -1, slots.shape[3])
    return jnp.concatenate([slots[q][..., :c] for q in range(N_DEV)], axis=-1)


def _slots_from_full(full, name):
    if name in ROW_SHARDED:
        l, rr, n = full.shape
        return jnp.transpose(full.reshape(l, N_DEV, rr // N_DEV, n), (1, 0, 2, 3))
    c = full.shape[-1] // N_DEV
    return jnp.stack([_pad_block(full[..., q * c:(q + 1) * c], name) for q in range(N_DEV)])


def _swap_halves(t):
    half = t.shape[-1] // 2
    return jnp.concatenate([t[..., half:], t[..., :half]], axis=-1)


def _in_offsets(d):
    widths = (d.sbw, d.sbw, d.sbw, d.q_lora, d.kv_lora, d.rope, d.d, d.d)
    offs, o = [], 0
    for w in widths:
        offs.append((o, o + w))
        o += w
    return offs


def _prime_weights(w_in, w_uq, w_ukv, d):
    offs = _in_offsets(d)
    cols = lambda k: w_in[:, offs[k][0]:offs[k][1]]
    kr = cols(5)
    zpad = jnp.zeros((d.d, d.g_off - d.kr_off - LANES), w_in.dtype)
    w_inb = jnp.concatenate([cols(3), cols(4), kr, _swap_halves(kr), zpad, cols(6), cols(7)], axis=1)
    w_ina = w_in[:, :d.wa]
    uq = w_uq.reshape(d.q_lora, d.mla_heads, d.nope + d.rope)
    rope = uq[..., d.nope:]
    w_uq = jnp.concatenate([uq[..., :d.nope], rope, _swap_halves(rope)], axis=-1).reshape(d.q_lora, d.qw)
    ukv = w_ukv.reshape(d.kv_lora, d.mla_heads, d.nope + d.vdim)
    w_ukv = jnp.concatenate([ukv[..., :d.nope].reshape(d.kv_lora, -1), ukv[..., d.nope:].reshape(d.kv_lora, -1)], axis=1)
    return dict(w_ina=w_ina, w_inb=w_inb, w_in=jnp.concatenate([w_ina, w_inb], axis=1), w_uq=w_uq, w_ukv=w_ukv)


def _unprime_grads(g, d):
    gi = g["w_in"]
    b = gi[:, d.wa:]
    kr = b[:, d.kr_off:d.kr_off + d.rope] + _swap_halves(b[:, d.kr_off + d.rope:d.kr_off + 2 * d.rope])
    w_in = jnp.concatenate([gi[:, :d.wa], b[:, :d.kr_off], kr, b[:, d.g_off:]], axis=1)
    uq = g["w_uq"].reshape(d.q_lora, d.mla_heads, 2 * LANES)
    rope = uq[..., d.nope:d.nope + d.rope] + _swap_halves(uq[..., d.nope + d.rope:])
    w_uq = jnp.concatenate([uq[..., :d.nope], rope], axis=-1).reshape(d.q_lora, -1)
    hw = d.mla_heads * d.nope
    ukv = g["w_ukv"]
    w_ukv = jnp.concatenate([ukv[:, :hw].reshape(d.kv_lora, d.mla_heads, d.nope),
                             ukv[:, hw:].reshape(d.kv_lora, d.mla_heads, d.vdim)], axis=-1).reshape(d.kv_lora, -1)
    return dict(g, w_in=w_in, w_uq=w_uq, w_ukv=w_ukv)


def _layer_fwd(h, add, w, norm_mix, q_norm, kv_norm, norm_ffn, cw, cb, ctab, stab, d, tag, ride=(), arrived=None):
    s = {}
    if add is None:
        s["h"] = h
        s["hn"], s["r1"] = _norm_fwd(h, norm_mix, width=d.d, cidx=0, name=f"norm_mix_{tag}")
    else:
        s["h"], s["hn"], s["r1"] = _norm_fwd(h, norm_mix, width=d.d, cidx=0, add=add, name=f"norm_mix_{tag}")
    s["pa"] = _mm(s["hn"], w["w_ina"], out_dtype=BF16, name=f"proj_a_{tag}")
    s["pb"] = _mm(s["hn"], w["w_inb"], name=f"proj_b_{tag}")
    s["o_sb"], s["carry"] = _sb_fwd(s["pa"], d, f"sb_fwd_{tag}")
    s["cqn"], s["rq"] = _norm_fwd(s["pb"], q_norm, width=d.q_lora, cidx=0, name=f"norm_q_{tag}")
    s["ckn"], s["rk"] = _norm_fwd(s["pb"], kv_norm, width=d.kv_lora, cidx=d.q_lora // d.kv_lora, name=f"norm_kv_{tag}")
    qraw = _mm(s["cqn"], w["w_uq"], name=f"uq_{tag}")
    s["kv"] = _mm(s["ckn"], w["w_ukv"], out_dtype=BF16, name=f"ukv_{tag}")
    s["qm"], s["kr"] = _mla_prep_fwd(qraw, s["pb"], ctab, stab, d, f"mla_prep_{tag}")
    (s["o_mla"], s["lse"]), gathered = _mla_fwd(s["qm"], s["kv"], s["kr"], d, f"mla_fwd_{tag}", ride=ride)
    if arrived is not None:
        arrived(gathered)
    s["b_sb"] = _mm(s["o_sb"], w["w_sb_out"], name=f"sb_out_{tag}")
    s["b_mla"] = _mm(s["o_mla"], w["w_mla_out"], name=f"mla_out_{tag}")
    s["merged"] = _gate_fwd(s["pb"], s["b_sb"], s["b_mla"], d, f"gate_{tag}")
    mix = _mm(s["merged"], w["w_o"], name=f"w_o_{tag}")
    s["h1"], s["hn2"], s["r2"] = _norm_fwd(s["h"], norm_ffn, width=d.d, cidx=0, add=mix, name=f"norm_ffn_{tag}")
    s["up"] = _mm(s["hn2"], w["w_up"], name=f"w_up_{tag}")
    s["act"] = _convglu_fwd(s["up"], cw, cb, d, f"convglu_{tag}")
    ffn = _mm(s["act"], w["w_down"], name=f"w_down_{tag}")
    return s, ffn


def _layer_bwd(dh2, dh2b, s, w, norm_mix, q_norm, kv_norm, norm_ffn, cw, cb, ctab, stab, d, tag, ride=(),
               ride_more=None):
    g = {}
    dact = _mm(dh2b, w["w_down"], tb=True, name=f"d_act_{tag}")
    g["w_down"] = _mm(s["act"], dh2b, ta=True, name=f"g_w_down_{tag}")
    dup_a, dup_g, sums_a, sums_g = _convglu_bwd(s["up"], dact, cw, cb, d, f"convglu_bwd_{tag}")
    dup = jnp.concatenate([dup_a, dup_g], axis=1)
    g["conv_w"] = jnp.concatenate([sums_a[0:3], sums_g[0:3]], axis=1)
    g["conv_b"] = jnp.concatenate([sums_a[3], sums_g[3]], axis=0)
    g["w_up"] = _mm(s["hn2"], dup, ta=True, name=f"g_w_up_{tag}")
    if ride_more is not None:
        ride = list(ride) + list(ride_more(g))
    dhn2 = _mm(dup, w["w_up"], tb=True, name=f"d_hn2_{tag}")
    dh1, dh1b, g["norm_ffn"] = _norm_bwd(dhn2, s["h1"], s["r2"], norm_ffn, width=d.d, cidx=0, dres=dh2,
                                         name=f"norm_ffn_bwd_{tag}")
    dmerged = _mm(dh1b, w["w_o"], tb=True, name=f"d_merged_{tag}")
    g["w_o"] = _mm(s["merged"], dh1b, ta=True, name=f"g_w_o_{tag}")
    db_sb, db_mla, dg_sb, dg_mla = _gate_bwd(dmerged, s["pb"], s["b_sb"], s["b_mla"], d, f"gate_bwd_{tag}")
    do_sb = _mm(db_sb, w["w_sb_out"], tb=True, name=f"d_o_sb_{tag}")
    g["w_sb_out"] = _mm(s["o_sb"], db_sb, ta=True, name=f"g_w_sb_out_{tag}")
    do_mla = _mm(db_mla, w["w_mla_out"], tb=True, name=f"d_o_mla_{tag}")
    g["w_mla_out"] = _mm(s["o_mla"], db_mla, ta=True, name=f"g_w_mla_out_{tag}")
    dq_sb, dk_sb, dv_sb = _sb_bwd(s["pa"], do_sb, s["carry"], d, f"sb_bwd_{tag}")
    (dqm, dkn, dv, dkr), rode = _mla_bwd(s["qm"], s["kv"], s["kr"], s["o_mla"], do_mla, s["lse"], d, f"mla_bwd_{tag}",
                                         ride=ride)
    dqraw, dkr128 = _mla_prep_bwd(dqm, dkr, ctab, stab, d, f"mla_prep_bwd_{tag}")
    dkv = jnp.concatenate([dkn.astype(BF16), dv.astype(BF16)], axis=1)
    dcqn = _mm(dqraw, w["w_uq"], tb=True, name=f"d_cq_{tag}")
    g["w_uq"] = _mm(s["cqn"], dqraw, ta=True, name=f"g_w_uq_{tag}")
    dckn = _mm(dkv, w["w_ukv"], tb=True, name=f"d_ckv_{tag}")
    g["w_ukv"] = _mm(s["ckn"], dkv, ta=True, name=f"g_w_ukv_{tag}")
    dcq, g["q_norm"] = _norm_bwd(dcqn, s["pb"], s["rq"], q_norm, width=d.q_lora, cidx=0, out_dtype=BF16, name=f"norm_q_bwd_{tag}")
    dckv, g["kv_norm"] = _norm_bwd(dckn, s["pb"], s["rk"], kv_norm, width=d.kv_lora, cidx=d.q_lora // d.kv_lora,
                                   out_dtype=BF16, name=f"norm_kv_bwd_{tag}")
    zpad = jnp.zeros((d.lp, d.g_off - d.kr_off - LANES), BF16)
    dproj = jnp.concatenate([dq_sb, dk_sb.astype(BF16), dv_sb.astype(BF16), dcq, dckv, dkr128, zpad, dg_sb, dg_mla], axis=1)
    g["w_in"] = _mm(s["hn"], dproj, ta=True, name=f"g_w_in_{tag}")
    dhn = _mm(dproj, w["w_in"], tb=True, name=f"d_hn_{tag}")
    dh, dhb, g["norm_mix"] = _norm_bwd(dhn, s["h"], s["r1"], norm_mix, width=d.d, cidx=0, dres=dh1,
                                       name=f"norm_mix_bwd_{tag}")
    return dh, dhb, g, rode


def _step(d, x, p, m, v, loss_target):
    x = x.reshape(d.seq, d.d)
    target = loss_target.reshape(d.seq, d.d)

    def block(n, layers=None):
        a = p[n] if layers is None else p[n][layers]
        return _pad_block(a, n).astype(BF16 if n in BIG else F32)

    def whole(n, slots):
        return _full_from_slots(slots, n, p[n].shape[-1])

    first_names = ["w_in", "w_uq", "w_ukv"]
    early = _gather_two_level([block(n, slice(0, 1)) for n in first_names] + [block(n) for n in SMALL], "gather_weights")
    full = {n: whole(n, g_) for n, g_ in zip(first_names + SMALL, early)}
    ws = [_prime_weights(*[full[n][0] for n in first_names], d)] + [dict() for _ in range(1, d.depth)]
    rest_names = [n for n in BIG if n not in first_names]
    late = [block(n, slice(1, d.depth)) for n in first_names] + [block(n) for n in rest_names]

    def arrived(gathered):
        later = {n: whole(n, g_) for n, g_ in zip(first_names, gathered)}
        rest = {n: whole(n, g_) for n, g_ in zip(rest_names, gathered[len(first_names):])}
        for l in range(d.depth):
            if l > 0:
                ws[l].update(_prime_weights(*[later[n][l - 1] for n in first_names], d))
            ws[l].update({n: rest[n][l] for n in rest_names})

    pos = jnp.arange(d.lp, dtype=F32) - d.pad
    half = d.rope // 2
    freqs = ROPE_THETA ** (-jnp.arange(half, dtype=F32) / half)
    ang = pos[:, None] * freqs[None, :]
    cos, sin = jnp.cos(ang), jnp.sin(ang)
    zero = jnp.zeros((d.lp, LANES - d.rope), F32)
    ctab = jnp.concatenate([cos, cos, zero], axis=1)
    stab = jnp.concatenate([-sin, sin, zero], axis=1)

    h = jnp.concatenate([jnp.zeros((d.pad, d.d), F32), full["meta_tokens"], x], axis=0)
    saved, add = [], None
    for l in range(d.depth):
        s, add = _layer_fwd(h, add, ws[l], p["norm_mix"][l], p["q_norm"][l], p["kv_norm"][l], p["norm_ffn"][l],
                            full["conv_w"][l], p["conv_b"][l].reshape(1, -1), ctab, stab, d, f"l{l}",
                            ride=late if l == 0 else (), arrived=arrived if l == 0 else None)
        saved.append(s)
        h = s["h1"]
    dh, dhb, loss_part, g_final = _head(h, add, target, p["final_norm"], d, "head")

    my_c = lax.axis_index("c")

    def pair_level(names, gfull, tag):
        gsend = [_slots_from_full(gfull[n].astype(BF16) if n in BIG else gfull[n], n) for n in names]
        theirs = _pair_exchange(gsend, f"scatter_grads_pair_{tag}")
        mine = [lax.dynamic_index_in_dim(g_.reshape((4, 2) + g_.shape[1:]), my_c, axis=1, keepdims=False) for g_ in gsend]
        return [_pair_sum(a_, b_, f"pair_sum_{n}_{tag}") for n, a_, b_ in zip(names, mine, theirs)]

    grads = [None] * d.depth
    recv_big = [None] * d.depth
    ride, ffn_names, rode_ffn = [], ["w_up", "w_down"], []
    for l in reversed(range(d.depth)):
        more = None
        if l == 0 and ride:
            more = lambda g_: pair_level(ffn_names, {n: g_[n][None] for n in ffn_names}, "l0_ffn")
        dh, dhb, g, rode = _layer_bwd(dh, dhb, saved[l], ws[l], p["norm_mix"][l], p["q_norm"][l], p["kv_norm"][l],
                                      p["norm_ffn"][l], full["conv_w"][l], p["conv_b"][l].reshape(1, -1), ctab, stab,
                                      d, f"l{l}", ride=ride, ride_more=more)
        if ride:
            recv_big[l + 1] = rode[:len(BIG)]
        if more is not None:
            rode_ffn = rode[len(BIG):]
        grads[l] = _unprime_grads(g, d)
        if l > 0:
            ride = pair_level(BIG, {n: grads[l][n][None] for n in BIG}, f"l{l}")
    grad_x = dh[d.first_tok:].reshape(1, d.seq, d.d)

    rest0 = [n for n in BIG if not (rode_ffn and n in ffn_names)]
    gfull = {n: grads[0][n][None] for n in rest0}
    gfull["conv_w"] = jnp.stack([grads[l]["conv_w"] for l in range(d.depth)])
    gfull["meta_tokens"] = dh[d.pad:d.first_tok]
    last = list(_chip_exchange(pair_level(rest0 + SMALL, gfull, "l0"), "scatter_grads_chips"))
    got0 = dict(zip(rest0, last[:len(rest0)]))
    got0.update(zip(ffn_names, rode_ffn))
    recv_big[0] = [got0[n] for n in BIG]
    grecv = [jnp.concatenate([recv_big[l][k] for l in range(d.depth)], axis=1) for k in range(len(BIG))] + last[len(rest0):]
    outs_sh = {n: _adamw(r_, _pad_block(p[n], n), _pad_block(m[n], n), _pad_block(v[n], n), f"adamw_{n}")
               for n, r_ in zip(SHARDED, grecv)}

    grep = {n: jnp.stack([grads[l][n].reshape(-1) for l in range(d.depth)]) for n in REPL if n != "final_norm"}
    grep["final_norm"] = g_final.reshape(-1)
    rflat = jnp.concatenate([grep[n].reshape(-1) for n in REPL] + [loss_part[0, 0:1]])
    (rparts,) = _exchange([_pack_rows(rflat, 8)], scatter=False, name="gather_small_grads")
    packr = lambda t: _pack_rows(jnp.concatenate([t[n].reshape(-1) for n in REPL] + [jnp.zeros((1,), F32)]), 8)
    outs_rp = _adamw(rparts, packr(p), packr(m), packr(v), "adamw_replicated")

    def unpack(flat, names, extra=0):
        res, off = {}, 0
        flat = flat.reshape(-1)
        for n in names:
            res[n] = flat[off:off + p[n].size].reshape(p[n].shape)
            off += p[n].size
        return res, flat[off:off + extra]

    results = []
    loss = None
    for k in range(4):
        sh = {n: outs_sh[n][k][..., :p[n].shape[-1]] for n in SHARDED}
        rp, tail = unpack(outs_rp[k], REPL, 1)
        if k == 0:
            loss = tail[0]
        results.append({**sh, **rp})
    return loss, grad_x, results


WEIGHTS = ["meta_tokens", "norm_mix", "w_in", "q_norm", "w_uq", "kv_norm", "w_ukv", "w_sb_out", "w_mla_out", "w_o",
           "norm_ffn", "w_up", "conv_w", "conv_b", "w_down", "final_norm"]


def _run(d, x, weights, loss_target, moments_m, moments_v):
    p = dict(zip(WEIGHTS, weights))
    m = dict(zip(WEIGHTS, moments_m))
    v = dict(zip(WEIGHTS, moments_v))
    loss, grad_x, res = _step(d, x, p, m, v, loss_target)
    out = [loss, grad_x]
    for k in range(4):
        out += [res[k][n] for n in WEIGHTS]
    return tuple(out)


def kernel(x, meta_tokens, norm_mix, w_in, q_norm, w_uq, kv_norm, w_ukv, w_sb_out, w_mla_out, w_o, norm_ffn, w_up, conv_w, conv_b, w_down, final_norm, loss_target, m_meta_tokens, m_norm_mix, m_w_in, m_q_norm, m_w_uq, m_kv_norm, m_w_ukv, m_w_sb_out, m_w_mla_out, m_w_o, m_norm_ffn, m_w_up, m_conv_w, m_conv_b, m_w_down, m_final_norm, v_meta_tokens, v_norm_mix, v_w_in, v_q_norm, v_w_uq, v_kv_norm, v_w_ukv, v_w_sb_out, v_w_mla_out, v_w_o, v_norm_ffn, v_w_up, v_conv_w, v_conv_b, v_w_down, v_final_norm):
    weights = [meta_tokens, norm_mix, w_in, q_norm, w_uq, kv_norm, w_ukv, w_sb_out, w_mla_out, w_o, norm_ffn, w_up,
               conv_w, conv_b, w_down, final_norm]
    ms = [m_meta_tokens, m_norm_mix, m_w_in, m_q_norm, m_w_uq, m_kv_norm, m_w_ukv, m_w_sb_out, m_w_mla_out, m_w_o,
          m_norm_ffn, m_w_up, m_conv_w, m_conv_b, m_w_down, m_final_norm]
    vs = [v_meta_tokens, v_norm_mix, v_w_in, v_q_norm, v_w_uq, v_kv_norm, v_w_ukv, v_w_sb_out, v_w_mla_out, v_w_o,
          v_norm_ffn, v_w_up, v_conv_w, v_conv_b, v_w_down, v_final_norm]
    return _run(PROD, x, weights, loss_target, ms, vs)
```

```python
import jax
import jax.numpy as jnp
from jax import lax
from jax.experimental import pallas as pl
from jax.experimental.pallas import tpu as pltpu

F32 = jnp.float32
BF16 = jnp.bfloat16

EPS = 1e-6
ROPE_THETA = 10000.0
ADAM_LR = 0.001
ADAM_B1 = 0.9
ADAM_B2 = 0.999
ADAM_EPS = 1e-08
ADAM_WD = 0.01
ADAM_STEP = 10
NEG = -1e30
DEAD = -110.0
LANES = 128
PACK_W = 1024
ADAMW_TILE_ELEMS = 256 * 1024
V7X_VMEM_LIMIT = 48 * 1024 * 1024
V7X_VMEM_LIMIT_BIG = 58 * 1024 * 1024
MESH_AXES = ("x", "y", "c")
N_DEV = 8
FLIPS = [(0, 0, 1), (0, 1, 0), (0, 1, 1), (1, 0, 0), (1, 0, 1), (1, 1, 0), (1, 1, 1)]


class _Dims:
    def __init__(self, d_model=2048, seq=8192, depth=2, n_meta=16, block=128, sb_heads=8, hd=128,
                 mla_heads=8, q_lora=512, kv_lora=256, nope=128, rope=64, vdim=128, d_ff=5632, tq=None):
        self.d, self.seq, self.depth, self.n_meta, self.block = d_model, seq, depth, n_meta, block
        self.sb_heads, self.hd, self.mla_heads = sb_heads, hd, mla_heads
        self.q_lora, self.kv_lora, self.nope, self.rope, self.vdim, self.f = q_lora, kv_lora, nope, rope, vdim, d_ff
        assert hd == LANES and nope == LANES and vdim == LANES and 2 * rope == LANES
        self.pad = block - n_meta
        self.lp = self.pad + n_meta + seq
        self.first_tok = self.pad + n_meta
        assert self.first_tok == block and self.lp % block == 0 and self.lp // block < LANES
        self.tq = tq or next(t for t in (640, 512, 256, 128) if self.lp % t == 0)
        assert self.tq % block == 0 and self.lp % self.tq == 0
        self.sbw = sb_heads * hd
        self.mlaw = mla_heads * vdim
        self.wa = 3 * self.sbw
        self.d_in = 3 * self.sbw + q_lora + kv_lora + rope + 2 * d_model
        self.tg = min(1024, d_model)
        self.kr_off = q_lora + kv_lora
        raw = self.kr_off + LANES
        self.g_off = -(-raw // self.tg) * self.tg
        self.wb = self.g_off + 2 * d_model
        self.qw = mla_heads * 2 * LANES


PROD = _Dims()


def _pick(n, prefs):
    for p in prefs:
        if n % p == 0:
            return p
    return n


def _cparams(sem, limit=V7X_VMEM_LIMIT):
    return pltpu.CompilerParams(dimension_semantics=sem, vmem_limit_bytes=limit)


def _mm(a, b, *, ta=False, tb=False, out_dtype=F32, name):
    if ta:
        kdim, m = a.shape
    else:
        m, kdim = a.shape
    if tb:
        n, k2 = b.shape
    else:
        k2, n = b.shape
    assert kdim == k2, (a.shape, b.shape, ta, tb)
    tm = _pick(m, (640, 512, 256, 128))
    tn = _pick(n, (1024, 512, 384, 256, 128))
    tk = _pick(kdim, (2816, 2048, 1664, 1408, 1024, 640, 512, 256, 128))
    nk = kdim // tk
    dn = (((0 if ta else 1,), (1 if tb else 0,)), ((), ()))

    def dot(a_ref, b_ref):
        return lax.dot_general(a_ref[...].astype(BF16), b_ref[...].astype(BF16), dn, preferred_element_type=F32)

    def body_one(a_ref, b_ref, o_ref):
        o_ref[...] = dot(a_ref, b_ref).astype(out_dtype)

    def body_acc(a_ref, b_ref, o_ref, acc_ref):
        k = pl.program_id(2)

        @pl.when(k == 0)
        def _():
            acc_ref[...] = dot(a_ref, b_ref)

        @pl.when((k > 0) & (k < nk - 1))
        def _():
            acc_ref[...] += dot(a_ref, b_ref)

        @pl.when(k == nk - 1)
        def _():
            o_ref[...] = (acc_ref[...] + dot(a_ref, b_ref)).astype(out_dtype)

    a_spec = pl.BlockSpec((tk, tm), lambda i, j, k: (k, i)) if ta else pl.BlockSpec((tm, tk), lambda i, j, k: (i, k))
    b_spec = pl.BlockSpec((tn, tk), lambda i, j, k: (j, k)) if tb else pl.BlockSpec((tk, tn), lambda i, j, k: (k, j))
    return pl.pallas_call(
        body_one if nk == 1 else body_acc, name=name, grid=(m // tm, n // tn, nk), in_specs=[a_spec, b_spec],
        out_specs=pl.BlockSpec((tm, tn), lambda i, j, k: (i, j)),
        out_shape=jax.ShapeDtypeStruct((m, n), out_dtype),
        scratch_shapes=[] if nk == 1 else [pltpu.VMEM((tm, tn), F32)],
        compiler_params=_cparams(("parallel", "parallel", "arbitrary")),
    )(a, b)


def _norm_fwd(x, g, *, width, cidx, add=None, name):
    rows = x.shape[0]
    tr = _pick(rows, (256, 128))
    has_add = add is not None

    def body(*refs):
        if has_add:
            x_ref, a_ref, g_ref, xn_ref, y_ref, r_ref = refs
            xv = x_ref[...] + a_ref[...]
            xn_ref[...] = xv
        else:
            x_ref, g_ref, y_ref, r_ref = refs
            xv = x_ref[...]
        r = lax.rsqrt(jnp.mean(xv * xv, axis=1, keepdims=True) + EPS)
        y_ref[...] = (xv * r * g_ref[...]).astype(BF16)
        r_ref[...] = r

    blk = pl.BlockSpec((tr, width), lambda i: (i, 0))
    in_specs = [pl.BlockSpec((tr, width), lambda i: (i, cidx))]
    args = [x]
    if has_add:
        in_specs.append(blk)
        args.append(add)
    in_specs.append(pl.BlockSpec((1, width), lambda i: (0, 0)))
    args.append(g.reshape(1, width))
    out_specs = [blk, pl.BlockSpec((tr, 1), lambda i: (i, 0))]
    out_shape = [jax.ShapeDtypeStruct((rows, width), BF16), jax.ShapeDtypeStruct((rows, 1), F32)]
    if has_add:
        out_specs.insert(0, blk)
        out_shape.insert(0, jax.ShapeDtypeStruct((rows, width), F32))
    return pl.pallas_call(body, name=name, grid=(rows // tr,), in_specs=in_specs, out_specs=out_specs,
                          out_shape=out_shape, compiler_params=_cparams(("parallel",)))(*args)


def _norm_bwd(dy, x, r, g, *, width, cidx, dres=None, out_dtype=F32, name):
    rows = x.shape[0]
    tr = _pick(rows, (256, 128))
    has_res = dres is not None

    def body(*refs):
        if has_res:
            dy_ref, x_ref, r_ref, g_ref, dr_ref, dx_ref, dxb_ref, dg_ref = refs
        else:
            dy_ref, x_ref, r_ref, g_ref, dx_ref, dg_ref = refs
        i = pl.program_id(0)

        @pl.when(i == 0)
        def _():
            dg_ref[...] = jnp.zeros_like(dg_ref)

        dyv, xv, rv = dy_ref[...], x_ref[...], r_ref[...]
        gy = dyv * g_ref[...]
        c = jnp.mean(gy * xv, axis=1, keepdims=True)
        dx = rv * gy - xv * (rv * rv * rv) * c
        if has_res:
            dx = dx + dr_ref[...]
            dxb_ref[...] = dx.astype(BF16)
        dx_ref[...] = dx.astype(out_dtype)
        dg_ref[...] += jnp.sum(dyv * xv * rv, axis=0, keepdims=True)

    blk = pl.BlockSpec((tr, width), lambda i: (i, 0))
    in_specs = [blk, pl.BlockSpec((tr, width), lambda i: (i, cidx)), pl.BlockSpec((tr, 1), lambda i: (i, 0)),
                pl.BlockSpec((1, width), lambda i: (0, 0))]
    args = [dy, x, r, g.reshape(1, width)]
    out_specs = [blk, pl.BlockSpec((1, width), lambda i: (0, 0))]
    out_shape = [jax.ShapeDtypeStruct((rows, width), out_dtype), jax.ShapeDtypeStruct((1, width), F32)]
    if has_res:
        in_specs.append(blk)
        args.append(dres)
        out_specs.insert(1, blk)
        out_shape.insert(1, jax.ShapeDtypeStruct((rows, width), BF16))
    return pl.pallas_call(
        body, name=name, grid=(rows // tr,), in_specs=in_specs, out_specs=out_specs, out_shape=out_shape,
        compiler_params=_cparams(("arbitrary",)))(*args)


def _split3(x):
    h1 = x.astype(BF16)
    r1 = x - h1.astype(F32)
    h2 = r1.astype(BF16)
    h3 = (r1 - h2.astype(F32)).astype(BF16)
    return h1, h2, h3


def _cum(x, tri):
    h1, h2, h3 = _split3(x)
    dot = lambda h: jnp.dot(h, tri, preferred_element_type=F32)
    return dot(h1) + dot(h2) + dot(h3)


def _dot_nt(a, b):
    return lax.dot_general(a, b, (((1,), (1,)), ((), ())), preferred_element_type=F32)


def _dot_tn(a, b):
    return lax.dot_general(a, b, (((0,), (0,)), ((), ())), preferred_element_type=F32)


def _sb_geometry(tq, t):
    assert t & (t - 1) == 0
    ri = lax.broadcasted_iota(jnp.int32, (tq, t), 0)
    return jnp.bitwise_and(ri, t - 1), jnp.right_shift(ri, t.bit_length() - 1), lax.broadcasted_iota(jnp.int32, (tq, t), 1)


def _sb_key_blocks(ref, base, r, t, kind):
    if kind == "low":
        return [ref[pl.ds(pl.multiple_of(jnp.maximum(base + g, 0) * t, t), t), :] for g in range(r)]
    slab = ref[pl.ds(pl.multiple_of(base * t, t), r * t), :]
    return [slab[g * t:(g + 1) * t, :] for g in range(r)]


def _sb_mask(geo, base, s, t, pad, kind):
    rowl, grp, col = geo
    if kind == "plain":
        return None
    if kind == "first":
        return col < rowl
    blk = base + grp
    causal = col < rowl + jnp.where(s > 0, t, 0)
    return (blk >= 0) & (blk * t + col >= pad) & causal


def _sb_fwd(qkv, d, name):
    nh, hd, lp, tq, t = d.sb_heads, d.hd, d.lp, d.tq, d.block
    r = tq // t
    scale = hd ** -0.5
    pad = d.pad

    def body(q_ref, k_ref, v_ref, o_ref, c_ref):
        i = pl.program_id(1)
        qs = [q_ref[g * t:(g + 1) * t, :] for g in range(r)]
        geo = _sb_geometry(tq, t)
        tri = (lax.broadcasted_iota(jnp.int32, (t, t), 0)
               > lax.broadcasted_iota(jnp.int32, (t, t), 1)).astype(BF16)
        lane = lax.broadcasted_iota(jnp.int32, (tq, LANES), 1)

        c_ref[...] = jnp.zeros_like(c_ref)

        def make_step(kind):
            def step(s, carry):
                acc, run = carry
                ks = _sb_key_blocks(k_ref, i * r - s, r, t, kind)
                vs = _sb_key_blocks(v_ref, i * r - s, r, t, kind)
                z = jnp.concatenate([_dot_nt(qs[g], ks[g]) for g in range(r)], axis=0) * scale
                e = jnp.exp(-jnp.abs(z))
                sp = jnp.maximum(z, 0.0) + jnp.log(1.0 + e)
                mask = _sb_mask(geo, i * r - s, s, t, pad, kind)
                spm = sp if mask is None else jnp.where(mask, sp, 0.0)
                w = jnp.exp(z - sp - _cum(spm, tri) + run)
                if mask is not None:
                    w = jnp.where(mask, w, 0.0)
                wb = w.astype(BF16)
                acc = acc + jnp.concatenate(
                    [jnp.dot(wb[g * t:(g + 1) * t, :], vs[g], preferred_element_type=F32) for g in range(r)], axis=0)
                c_ref[...] = jnp.where(lane == s, run, c_ref[...])
                run = run - jnp.sum(spm, axis=1, keepdims=True)
                return acc, run
            return step

        first, plain, low = make_step("first"), make_step("plain"), make_step("low")

        def alive(run):
            return (jnp.max(run) >= DEAD).astype(jnp.int32)

        def run_while(step, s, end, live, acc, run):
            def wbody(st):
                s, _, acc, run = st
                acc, run = step(s, (acc, run))
                return s + 1, alive(run), acc, run
            return lax.while_loop(lambda st: (st[0] < end) & (st[1] > 0), wbody, (s, live, acc, run))

        init = (jnp.zeros((tq, hd), F32), jnp.zeros((tq, 1), F32))
        carry = lax.fori_loop(0, jnp.minimum(i, 1), lambda _, cr: first(0, cr), init)
        acc, run = lax.fori_loop(0, 1 - jnp.minimum(i, 1), lambda _, cr: low(0, cr), carry)
        s, live, acc, run = run_while(plain, 1, i * r, alive(run), acc, run)
        end_low = jnp.where(s >= jnp.maximum(i * r, 1), (i + 1) * r, s)
        s, live, acc, run = run_while(low, s, end_low, live, acc, run)
        o_ref[...] = acc
        c_ref[...] = jnp.where(lane == LANES - 1, s.astype(F32), c_ref[...])

    return pl.pallas_call(
        body, name=name, grid=(nh, lp // tq),
        in_specs=[pl.BlockSpec((tq, hd), lambda h, i: (i, h)),
                  pl.BlockSpec((lp, hd), lambda h, i: (0, nh + h)),
                  pl.BlockSpec((lp, hd), lambda h, i: (0, 2 * nh + h))],
        out_specs=[pl.BlockSpec((tq, hd), lambda h, i: (i, h)),
                   pl.BlockSpec((None, tq, LANES), lambda h, i: (h, i, 0))],
        out_shape=[jax.ShapeDtypeStruct((lp, nh * hd), F32), jax.ShapeDtypeStruct((nh, lp, LANES), F32)],
        compiler_params=_cparams(("parallel", "arbitrary")),
    )(qkv, qkv, qkv)


def _sb_bwd(qkv, do, carry, d, name):
    nh, hd, lp, tq, t = d.sb_heads, d.hd, d.lp, d.tq, d.block
    r = tq // t
    scale = hd ** -0.5
    pad = d.pad

    def body(q_ref, k_ref, v_ref, do_ref, c_ref, dq_ref, dk_ref, dv_ref):
        i = pl.program_id(1)

        @pl.when(i == 0)
        def _():
            dk_ref[...] = jnp.zeros_like(dk_ref)
            dv_ref[...] = jnp.zeros_like(dv_ref)

        rows = lambda x, g: x[g * t:(g + 1) * t, :]
        qs = [q_ref[g * t:(g + 1) * t, :] for g in range(r)]
        dobs = [do_ref[g * t:(g + 1) * t, :].astype(BF16) for g in range(r)]
        geo = _sb_geometry(tq, t)
        ri = lax.broadcasted_iota(jnp.int32, (t, t), 0)
        ci = lax.broadcasted_iota(jnp.int32, (t, t), 1)
        tri_suf = (ri > ci).astype(BF16)
        tri_pre = (ri < ci).astype(BF16)
        lane = lax.broadcasted_iota(jnp.int32, (tq, LANES), 1)

        def make_step(kind):
            def step(s, carry):
                dq, pc = carry
                base = i * r - s
                ks = _sb_key_blocks(k_ref, base, r, t, kind)
                vs = _sb_key_blocks(v_ref, base, r, t, kind)
                z = jnp.concatenate([_dot_nt(qs[g], ks[g]) for g in range(r)], axis=0) * scale
                e = jnp.exp(-jnp.abs(z))
                sp = jnp.maximum(z, 0.0) + jnp.log(1.0 + e)
                mask = _sb_mask(geo, base, s, t, pad, kind)
                spm = sp if mask is None else jnp.where(mask, sp, 0.0)
                run = jnp.sum(jnp.where(lane == s, c_ref[...], 0.0), axis=1, keepdims=True)
                w = jnp.exp(z - sp - _cum(spm, tri_suf) + run)
                if mask is not None:
                    w = jnp.where(mask, w, 0.0)
                gw = w * jnp.concatenate([_dot_nt(dobs[g], vs[g]) for g in range(r)], axis=0)
                p = _cum(gw, tri_pre) + pc
                inv = 1.0 / (1.0 + e)
                sig = jnp.where(z >= 0.0, inv, e * inv)
                dz = (gw * (1.0 - sig) - sig * p) * scale
                if mask is not None:
                    dz = jnp.where(mask, dz, 0.0)
                dzb, wb = dz.astype(BF16), w.astype(BF16)
                dq = dq + jnp.concatenate(
                    [jnp.dot(rows(dzb, g), ks[g], preferred_element_type=F32) for g in range(r)], axis=0)
                dks = [_dot_tn(rows(dzb, g), qs[g]) for g in range(r)]
                dvs = [_dot_tn(rows(wb, g), dobs[g]) for g in range(r)]
                if kind == "low":
                    for g in range(r):
                        at = pl.ds(pl.multiple_of(jnp.maximum(base + g, 0) * t, t), t)
                        dk_ref[at, :] += dks[g]
                        dv_ref[at, :] += dvs[g]
                else:
                    at = pl.ds(pl.multiple_of(base * t, t), tq)
                    dk_ref[at, :] += jnp.concatenate(dks, axis=0)
                    dv_ref[at, :] += jnp.concatenate(dvs, axis=0)
                pc = pc + jnp.sum(gw, axis=1, keepdims=True)
                return dq, pc
            return step

        first, plain, low = make_step("first"), make_step("plain"), make_step("low")
        nsteps = jnp.max(jnp.where(lane == LANES - 1, c_ref[...], 0.0)).astype(jnp.int32)
        low_from = jnp.maximum(i * r, 1)
        plain_end = jnp.minimum(nsteps, low_from)
        carry = (jnp.zeros((tq, hd), F32), jnp.zeros((tq, 1), F32))
        carry = lax.fori_loop(0, jnp.maximum(nsteps - low_from, 0), lambda jj, cr: low(nsteps - 1 - jj, cr), carry)
        carry = lax.fori_loop(0, plain_end - 1, lambda jj, cr: plain(plain_end - 1 - jj, cr), carry)
        carry = lax.fori_loop(0, jnp.minimum(i, 1), lambda _, cr: first(0, cr), carry)
        dq, _ = lax.fori_loop(0, 1 - jnp.minimum(i, 1), lambda _, cr: low(0, cr), carry)
        dq_ref[...] = dq.astype(BF16)

    w3 = nh * hd
    return pl.pallas_call(
        body, name=name, grid=(nh, lp // tq),
        in_specs=[pl.BlockSpec((tq, hd), lambda h, i: (i, h)),
                  pl.BlockSpec((lp, hd), lambda h, i: (0, nh + h)),
                  pl.BlockSpec((lp, hd), lambda h, i: (0, 2 * nh + h)),
                  pl.BlockSpec((tq, hd), lambda h, i: (i, h)),
                  pl.BlockSpec((None, tq, LANES), lambda h, i: (h, i, 0))],
        out_specs=[pl.BlockSpec((tq, hd), lambda h, i: (i, h)),
                   pl.BlockSpec((lp, hd), lambda h, i: (0, h)),
                   pl.BlockSpec((lp, hd), lambda h, i: (0, h))],
        out_shape=[jax.ShapeDtypeStruct((lp, w3), BF16), jax.ShapeDtypeStruct((lp, w3), F32),
                   jax.ShapeDtypeStruct((lp, w3), F32)],
        compiler_params=_cparams(("arbitrary", "arbitrary")),
    )(qkv, qkv, qkv, do, carry)


def _mla_prep_fwd(qraw, projb, ctab, stab, d, name):
    lp, nh = d.lp, d.mla_heads
    tr = _pick(lp, (256, 128))
    kidx = d.kr_off // LANES

    def rope(u, c, s):
        return u * c + pltpu.roll(u, LANES // 2, 1) * s

    def body(q_ref, k_ref, c_ref, s_ref, qm_ref, kr_ref):
        c, s = c_ref[...], s_ref[...]
        for h in range(nh):
            base = 2 * LANES * h
            qm_ref[:, base:base + LANES] = q_ref[:, base:base + LANES].astype(BF16)
            qm_ref[:, base + LANES:base + 2 * LANES] = rope(q_ref[:, base + LANES:base + 2 * LANES], c, s).astype(BF16)
        kr_ref[...] = rope(k_ref[...], c, s).astype(BF16)

    tab = pl.BlockSpec((tr, LANES), lambda i: (i, 0))
    return pl.pallas_call(
        body, name=name, grid=(lp // tr,),
        in_specs=[pl.BlockSpec((tr, d.qw), lambda i: (i, 0)), pl.BlockSpec((tr, LANES), lambda i: (i, kidx)), tab, tab],
        out_specs=[pl.BlockSpec((tr, d.qw), lambda i: (i, 0)), tab],
        out_shape=[jax.ShapeDtypeStruct((lp, d.qw), BF16), jax.ShapeDtypeStruct((lp, LANES), BF16)],
        compiler_params=_cparams(("parallel",)))(qraw, projb, ctab, stab)


def _mla_prep_bwd(dqm, dkr, ctab, stab, d, name):
    lp, nh = d.lp, d.mla_heads
    tr = _pick(lp, (256, 128))

    def unrope(g, c, s):
        return g * c + pltpu.roll(g * s, LANES // 2, 1)

    def body(dq_ref, dk_ref, c_ref, s_ref, o_ref, ok_ref):
        c, s = c_ref[...], s_ref[...]
        for h in range(nh):
            base = 2 * LANES * h
            o_ref[:, base:base + LANES] = dq_ref[:, base:base + LANES].astype(BF16)
            o_ref[:, base + LANES:base + 2 * LANES] = unrope(dq_ref[:, base + LANES:base + 2 * LANES], c, s).astype(BF16)
        ok_ref[...] = unrope(dk_ref[...], c, s).astype(BF16)

    tab = pl.BlockSpec((tr, LANES), lambda i: (i, 0))
    wide = pl.BlockSpec((tr, d.qw), lambda i: (i, 0))
    return pl.pallas_call(
        body, name=name, grid=(lp // tr,), in_specs=[wide, tab, tab, tab], out_specs=[wide, tab],
        out_shape=[jax.ShapeDtypeStruct((lp, d.qw), BF16), jax.ShapeDtypeStruct((lp, LANES), BF16)],
        compiler_params=_cparams(("parallel",)))(dqm, dkr, ctab, stab)


def _mla_fwd(qm, kv, kr, d, name, ride=()):
    nh, lp, t = d.mla_heads, d.lp, d.tq
    scale = (d.nope + d.rope) ** -0.5
    pad = d.pad
    nr = len(ride)

    def body(*refs):
        q_ref, kn_ref, v_ref, kr_ref = refs[:4]
        o_ref, lse_ref = refs[4 + nr:6 + nr]
        h = pl.program_id(0)
        i = pl.program_id(1)
        if nr:
            ride_refs = (refs[4:4 + nr], refs[6 + nr:6 + 2 * nr]) + tuple(refs[6 + 2 * nr:])

            @pl.when((h == 0) & (i == 0))
            def _():
                _Gather(*ride_refs).start()

        q = q_ref[...]
        row = i * t + lax.broadcasted_iota(jnp.int32, (t, t), 0)
        colb = lax.broadcasted_iota(jnp.int32, (t, t), 1)

        def make_step(masked):
            def step(j, carry):
                acc, m, l = carry
                off = pl.multiple_of(j * t, t)
                kc = jnp.concatenate([kn_ref[pl.ds(off, t), :], kr_ref[pl.ds(off, t), :]], axis=1)
                s = _dot_nt(q, kc) * scale
                if masked:
                    col = j * t + colb
                    s = jnp.where((col <= row) & (col >= pad), s, NEG)
                m_new = jnp.maximum(m, jnp.max(s, axis=1, keepdims=True))
                alpha = jnp.exp(m - m_new)
                p = jnp.exp(s - m_new)
                l = alpha * l + jnp.sum(p, axis=1, keepdims=True)
                acc = alpha * acc + jnp.dot(p.astype(BF16), v_ref[pl.ds(off, t), :], preferred_element_type=F32)
                return acc, m_new, l
            return step

        masked, plain = make_step(True), make_step(False)
        carry = masked(0, (jnp.zeros((t, LANES), F32), jnp.full((t, 1), NEG, F32), jnp.zeros((t, 1), F32)))
        carry = lax.fori_loop(1, i, plain, carry)
        acc, m, l = lax.fori_loop(i, i + jnp.minimum(i, 1), masked, carry)
        rowv = i * t + lax.broadcasted_iota(jnp.int32, (t, LANES), 0)
        o_ref[...] = jnp.where(rowv >= pad, acc / l, 0.0)
        lse_ref[...] = m + jnp.log(l)
        if nr:
            @pl.when((h == nh - 1) & (i == lp // t - 1))
            def _():
                _Gather(*ride_refs).finish()

    hbm = pl.BlockSpec(memory_space=pltpu.HBM)
    res = pl.pallas_call(
        body, name=name, grid=(nh, lp // t),
        in_specs=[pl.BlockSpec((t, 2 * LANES), lambda h, i: (i, h)),
                  pl.BlockSpec((lp, LANES), lambda h, i: (0, h)),
                  pl.BlockSpec((lp, LANES), lambda h, i: (0, nh + h)),
                  pl.BlockSpec((lp, LANES), lambda h, i: (0, 0))] + [hbm] * nr,
        out_specs=[pl.BlockSpec((t, LANES), lambda h, i: (i, h)),
                   pl.BlockSpec((None, t, 1), lambda h, i: (h, i, 0))] + [hbm] * nr,
        out_shape=[jax.ShapeDtypeStruct((lp, nh * LANES), F32), jax.ShapeDtypeStruct((nh, lp, 1), F32)]
                  + _gather_out_shapes(ride),
        scratch_shapes=_gather_sems(nr) if nr else [],
        compiler_params=_cparams(("arbitrary", "arbitrary")),
    )(qm, kv, kv, kr, *ride)
    return res[:2], list(res[2:])


def _mla_bwd(qm, kv, kr, o, do, lse, d, name, ride=()):
    nh, lp, t = d.mla_heads, d.lp, d.tq
    scale = (d.nope + d.rope) ** -0.5
    pad = d.pad
    nr = len(ride)

    def body(*refs):
        q_ref, kn_ref, v_ref, kr_ref, o_ref, do_ref, lse_ref = refs[:7]
        dq_ref, dkn_ref, dv_ref, dkr_ref = refs[7 + nr:11 + nr]
        h = pl.program_id(0)
        i = pl.program_id(1)
        if nr:
            ride_refs = (refs[7:7 + nr], refs[11 + nr:11 + 2 * nr]) + tuple(refs[11 + 2 * nr:])

            @pl.when((h == 0) & (i == 0))
            def _():
                _chip_copies_start(_chip_copies(*ride_refs))

        @pl.when(i == 0)
        def _():
            dkn_ref[...] = jnp.zeros_like(dkn_ref)
            dv_ref[...] = jnp.zeros_like(dv_ref)

        @pl.when((i == 0) & (h == 0))
        def _():
            dkr_ref[...] = jnp.zeros_like(dkr_ref)

        q = q_ref[...]
        dof = do_ref[...]
        dob = dof.astype(BF16)
        delta = jnp.sum(dof * o_ref[...], axis=1, keepdims=True)
        lse = lse_ref[...]
        row = i * t + lax.broadcasted_iota(jnp.int32, (t, t), 0)
        colb = lax.broadcasted_iota(jnp.int32, (t, t), 1)

        def make_step(masked):
            def step(j, dq):
                off = pl.multiple_of(j * t, t)
                kc = jnp.concatenate([kn_ref[pl.ds(off, t), :], kr_ref[pl.ds(off, t), :]], axis=1)
                v = v_ref[pl.ds(off, t), :]
                s = _dot_nt(q, kc) * scale
                if masked:
                    col = j * t + colb
                    mask = (col <= row) & (col >= pad)
                    p = jnp.where(mask, jnp.exp(jnp.where(mask, s, NEG) - lse), 0.0)
                else:
                    p = jnp.exp(s - lse)
                dp = _dot_nt(dob, v)
                dsb = (p * (dp - delta) * scale).astype(BF16)
                dq = dq + jnp.dot(dsb, kc, preferred_element_type=F32)
                dkc = _dot_tn(dsb, q)
                dkn_ref[pl.ds(off, t), :] += dkc[:, :LANES]
                dkr_ref[pl.ds(off, t), :] += dkc[:, LANES:]
                dv_ref[pl.ds(off, t), :] += _dot_tn(p.astype(BF16), dob)
                return dq
            return step

        masked, plain = make_step(True), make_step(False)
        dq = masked(0, jnp.zeros((t, 2 * LANES), F32))
        dq = lax.fori_loop(1, i, plain, dq)
        dq_ref[...] = lax.fori_loop(i, i + jnp.minimum(i, 1), masked, dq)
        if nr:
            @pl.when((h == nh - 1) & (i == lp // t - 1))
            def _():
                _chip_copies_wait(_chip_copies(*ride_refs))

    hbm = pl.BlockSpec(memory_space=pltpu.HBM)
    res = pl.pallas_call(
        body, name=name, grid=(nh, lp // t),
        in_specs=[pl.BlockSpec((t, 2 * LANES), lambda h, i: (i, h)),
                  pl.BlockSpec((lp, LANES), lambda h, i: (0, h), pipeline_mode=pl.Buffered(1)),
                  pl.BlockSpec((lp, LANES), lambda h, i: (0, nh + h), pipeline_mode=pl.Buffered(1)),
                  pl.BlockSpec((lp, LANES), lambda h, i: (0, 0), pipeline_mode=pl.Buffered(1)),
                  pl.BlockSpec((t, LANES), lambda h, i: (i, h)),
                  pl.BlockSpec((t, LANES), lambda h, i: (i, h)),
                  pl.BlockSpec((None, t, 1), lambda h, i: (h, i, 0))] + [hbm] * nr,
        out_specs=[pl.BlockSpec((t, 2 * LANES), lambda h, i: (i, h)),
                   pl.BlockSpec((lp, LANES), lambda h, i: (0, h)),
                   pl.BlockSpec((lp, LANES), lambda h, i: (0, h)),
                   pl.BlockSpec((lp, LANES), lambda h, i: (0, 0))] + [hbm] * nr,
        out_shape=[jax.ShapeDtypeStruct((lp, nh * 2 * LANES), F32), jax.ShapeDtypeStruct((lp, nh * LANES), F32),
                   jax.ShapeDtypeStruct((lp, nh * LANES), F32), jax.ShapeDtypeStruct((lp, LANES), F32)]
                  + [jax.ShapeDtypeStruct(x.shape, x.dtype) for x in ride],
        scratch_shapes=_chip_copies_sems(nr) if nr else [],
        compiler_params=_cparams(("arbitrary", "arbitrary"), V7X_VMEM_LIMIT_BIG),
    )(qm, kv, kv, kr, o, do, lse, *ride)
    return res[:4], list(res[4:])


def _sigmoid(x):
    return 1.0 / (1.0 + jnp.exp(-x))


def _gate_fwd(projb, b_sb, b_mla, d, name):
    lp, tg = d.lp, d.tg
    tr = _pick(lp, (256, 128))
    o1, o2 = d.g_off // tg, (d.g_off + d.d) // tg

    def body(g1_ref, g2_ref, b1_ref, b2_ref, o_ref):
        o_ref[...] = (_sigmoid(g1_ref[...]) * b1_ref[...] + _sigmoid(g2_ref[...]) * b2_ref[...]).astype(BF16)

    blk = pl.BlockSpec((tr, tg), lambda i, j: (i, j))
    return pl.pallas_call(
        body, name=name, grid=(lp // tr, d.d // tg),
        in_specs=[pl.BlockSpec((tr, tg), lambda i, j: (i, o1 + j)), pl.BlockSpec((tr, tg), lambda i, j: (i, o2 + j)),
                  blk, blk],
        out_specs=blk, out_shape=jax.ShapeDtypeStruct((lp, d.d), BF16),
        compiler_params=_cparams(("parallel", "parallel")))(projb, projb, b_sb, b_mla)


def _gate_bwd(dm, projb, b_sb, b_mla, d, name):
    lp, tg = d.lp, d.tg
    tr = _pick(lp, (256, 128))
    o1, o2 = d.g_off // tg, (d.g_off + d.d) // tg

    def body(dm_ref, g1_ref, g2_ref, b1_ref, b2_ref, db1_ref, db2_ref, dg1_ref, dg2_ref):
        dmv = dm_ref[...]
        s1, s2 = _sigmoid(g1_ref[...]), _sigmoid(g2_ref[...])
        db1_ref[...] = (dmv * s1).astype(BF16)
        db2_ref[...] = (dmv * s2).astype(BF16)
        dg1_ref[...] = (dmv * b1_ref[...] * s1 * (1.0 - s1)).astype(BF16)
        dg2_ref[...] = (dmv * b2_ref[...] * s2 * (1.0 - s2)).astype(BF16)

    blk = pl.BlockSpec((tr, tg), lambda i, j: (i, j))
    out = jax.ShapeDtypeStruct((lp, d.d), BF16)
    return pl.pallas_call(
        body, name=name, grid=(lp // tr, d.d // tg),
        in_specs=[blk, pl.BlockSpec((tr, tg), lambda i, j: (i, o1 + j)), pl.BlockSpec((tr, tg), lambda i, j: (i, o2 + j)),
                  blk, blk],
        out_specs=[blk] * 4, out_shape=[out] * 4,
        compiler_params=_cparams(("parallel", "parallel")))(dm, projb, projb, b_sb, b_mla)


HALO = 8


def _conv_tiles(d):
    return _pick(d.lp, (640, 512, 256, 128)), _pick(d.f, (512, 256, 128))


def _convglu_fwd(up, cw, cb, d, name):
    lp, f = d.lp, d.f
    tr, tc = _conv_tiles(d)
    nf = f // tc
    hb = tr // HALO
    pad = d.pad

    def body(a_ref, g_ref, pa_ref, pg_ref, wa_ref, wg_ref, ba_ref, bg_ref, o_ref, ua_ref, ug_ref, xa, xg):
        i = pl.program_id(1)
        keep = (i > 0).astype(F32)
        xa[0:HALO, :] = pa_ref[...] * keep
        xg[0:HALO, :] = pg_ref[...] * keep
        xa[HALO:, :] = a_ref[...]
        xg[HALO:, :] = g_ref[...]

        def conv(x, w_ref, b_ref):
            return (b_ref[...] + x[pl.ds(HALO - 2, tr), :] * w_ref[0:1, :] + x[pl.ds(HALO - 1, tr), :] * w_ref[1:2, :]
                    + x[pl.ds(HALO, tr), :] * w_ref[2:3, :])

        ua = conv(xa, wa_ref, ba_ref)
        ug = conv(xg, wg_ref, bg_ref)
        ua_ref[...] = ua
        ug_ref[...] = ug
        row = i * tr + lax.broadcasted_iota(jnp.int32, (tr, tc), 0)
        o_ref[...] = jnp.where(row >= pad, ua * _sigmoid(ua) * ug, 0.0).astype(BF16)

    prev = lambda j, i: (jnp.maximum(i * hb - 1, 0), j)
    prevg = lambda j, i: (jnp.maximum(i * hb - 1, 0), nf + j)
    main = pl.BlockSpec((tr, tc), lambda j, i: (i, j))
    return pl.pallas_call(
        body, name=name, grid=(nf, lp // tr),
        in_specs=[main, pl.BlockSpec((tr, tc), lambda j, i: (i, nf + j)),
                  pl.BlockSpec((HALO, tc), prev), pl.BlockSpec((HALO, tc), prevg),
                  pl.BlockSpec((3, tc), lambda j, i: (0, j)), pl.BlockSpec((3, tc), lambda j, i: (0, nf + j)),
                  pl.BlockSpec((1, tc), lambda j, i: (0, j)), pl.BlockSpec((1, tc), lambda j, i: (0, nf + j))],
        out_specs=[main, main, main],
        out_shape=[jax.ShapeDtypeStruct((lp, f), BF16), jax.ShapeDtypeStruct((lp, f), F32),
                   jax.ShapeDtypeStruct((lp, f), F32)],
        scratch_shapes=[pltpu.VMEM((tr + HALO, tc), F32), pltpu.VMEM((tr + HALO, tc), F32)],
        compiler_params=_cparams(("parallel", "arbitrary")))(up, up, up, up, cw, cw, cb, cb)


def _convglu_bwd(up, ua, ug, dact, cw, d, name):
    lp, f = d.lp, d.f
    tr, tc = _conv_tiles(d)
    nf = f // tc
    hb = tr // HALO
    nrow = lp // tr
    pad = d.pad
    te = tr + HALO

    def body(a_ref, g_ref, ua_ref, ug_ref, nua_ref, nug_ref, da_ref, nd_ref, wa_ref, wg_ref,
             oa_ref, og_ref, sa_ref, sg_ref, ya, yg):
        i = pl.program_id(1)

        @pl.when(i == 0)
        def _():
            sa_ref[...] = jnp.zeros_like(sa_ref)
            sg_ref[...] = jnp.zeros_like(sg_ref)

        def dconv(ua, ug, dact):
            sg = _sigmoid(ua)
            return dact * ug * (sg * (1.0 + ua * (1.0 - sg))), dact * (ua * sg)

        ya_v, yg_v = dconv(ua_ref[...], ug_ref[...], da_ref[...])
        ya[0:tr, :] = ya_v
        yg[0:tr, :] = yg_v
        ya[tr:, :], yg[tr:, :] = dconv(nua_ref[...], nug_ref[...], nd_ref[...] * (i < nrow - 1).astype(F32))
        row = i * tr + lax.broadcasted_iota(jnp.int32, (tr, tc), 0)

        def back(y, y0, x_ref, w_ref, o_ref, s_ref):
            y1, y2 = y[pl.ds(1, tr), :], y[pl.ds(2, tr), :]
            dup = y0 * w_ref[2:3, :] + y1 * w_ref[1:2, :] + y2 * w_ref[0:1, :]
            o_ref[...] = jnp.where(row >= pad, dup, 0.0).astype(BF16)
            xv = x_ref[...]
            for tap, yk in enumerate((y2, y1, y0)):
                s_ref[tap:tap + 1, :] += jnp.sum(yk * xv, axis=0, keepdims=True)
            s_ref[3:4, :] += jnp.sum(y0, axis=0, keepdims=True)

        back(ya, ya_v, a_ref, wa_ref, oa_ref, sa_ref)
        back(yg, yg_v, g_ref, wg_ref, og_ref, sg_ref)

    last8 = lp // HALO - 1
    nxt = pl.BlockSpec((HALO, tc), lambda j, i: (jnp.minimum((i + 1) * hb, last8), j))
    main = pl.BlockSpec((tr, tc), lambda j, i: (i, j))
    sums = pl.BlockSpec((8, tc), lambda j, i: (0, j))
    return pl.pallas_call(
        body, name=name, grid=(nf, nrow),
        in_specs=[main, pl.BlockSpec((tr, tc), lambda j, i: (i, nf + j)), main, main, nxt, nxt, main, nxt,
                  pl.BlockSpec((3, tc), lambda j, i: (0, j)), pl.BlockSpec((3, tc), lambda j, i: (0, nf + j))],
        out_specs=[main, main, sums, sums],
        out_shape=[jax.ShapeDtypeStruct((lp, f), BF16), jax.ShapeDtypeStruct((lp, f), BF16),
                   jax.ShapeDtypeStruct((8, f), F32), jax.ShapeDtypeStruct((8, f), F32)],
        scratch_shapes=[pltpu.VMEM((te, tc), F32), pltpu.VMEM((te, tc), F32)],
        compiler_params=_cparams(("parallel", "arbitrary")))(up, up, ua, ug, ua, ug, dact, dact, cw, cw)


def _head(h, add, target, g, d, name):
    lp, dm, t = d.lp, d.d, d.block
    inv_d = 1.0 / dm

    def body(h_ref, a_ref, t_ref, g_ref, dh_ref, dhb_ref, loss_ref, dg_ref):
        i = pl.program_id(0)

        @pl.when(i == 0)
        def _():
            dh_ref[...] = jnp.zeros_like(dh_ref)
            dhb_ref[...] = jnp.zeros_like(dhb_ref)
            loss_ref[...] = jnp.zeros_like(loss_ref)
            dg_ref[...] = jnp.zeros_like(dg_ref)

        @pl.when(i > 0)
        def _():
            x, gv = h_ref[...] + a_ref[...], g_ref[...]
            r = lax.rsqrt(jnp.mean(x * x, axis=1, keepdims=True) + EPS)
            xh = x * r
            err = xh * gv - t_ref[...]
            loss_ref[...] += 0.5 * inv_d * jnp.sum(err * err)
            dy = err * inv_d
            gy = dy * gv
            c = jnp.mean(gy * x, axis=1, keepdims=True)
            dh = r * gy - x * (r * r * r) * c
            dh_ref[...] = dh
            dhb_ref[...] = dh.astype(BF16)
            dg_ref[...] += jnp.sum(dy * xh, axis=0, keepdims=True)

    blk = pl.BlockSpec((t, dm), lambda i: (i, 0))
    return pl.pallas_call(
        body, name=name, grid=(lp // t,),
        in_specs=[blk, blk, pl.BlockSpec((t, dm), lambda i: (jnp.maximum(i - 1, 0), 0)),
                  pl.BlockSpec((1, dm), lambda i: (0, 0))],
        out_specs=[blk, blk, pl.BlockSpec((8, LANES), lambda i: (0, 0)), pl.BlockSpec((1, dm), lambda i: (0, 0))],
        out_shape=[jax.ShapeDtypeStruct((lp, dm), F32), jax.ShapeDtypeStruct((lp, dm), BF16),
                   jax.ShapeDtypeStruct((8, LANES), F32), jax.ShapeDtypeStruct((1, dm), F32)],
        compiler_params=_cparams(("arbitrary",)))(h, add, target, g.reshape(1, dm))


def _exchange(xs, *, scatter, name):
    n = len(xs)
    nf = len(FLIPS)

    def body(*refs):
        ins, outs = refs[:n], refs[n:2 * n]
        send_sems, recv_sems, loc_sems = refs[2 * n:]
        x, y, c = lax.axis_index("x"), lax.axis_index("y"), lax.axis_index("c")
        me = 4 * x + 2 * y + c
        sends, recvs, locs = [], [], []
        for a in range(n):
            src_me = ins[a].at[me] if scatter else ins[a]
            loc = pltpu.make_async_copy(src_me, outs[a].at[me], loc_sems.at[a])
            loc.start()
            locs.append(loc)
            for k, (fx, fy, fc) in enumerate(FLIPS):
                px, py, pc = x ^ fx, y ^ fy, c ^ fc
                peer = 4 * px + 2 * py + pc
                src = ins[a].at[peer] if scatter else ins[a]
                cp = pltpu.make_async_remote_copy(
                    src_ref=src, dst_ref=outs[a].at[me], send_sem=send_sems.at[a * nf + k],
                    recv_sem=recv_sems.at[a * nf + k], device_id=(px, py, pc), device_id_type=pl.DeviceIdType.MESH)
                cp.start()
                sends.append(cp)
                recvs.append(pltpu.make_async_remote_copy(
                    src_ref=src, dst_ref=outs[a].at[peer], send_sem=send_sems.at[a * nf + k],
                    recv_sem=recv_sems.at[a * nf + k], device_id=(px, py, pc), device_id_type=pl.DeviceIdType.MESH))
        for cp in recvs:
            cp.wait_recv()
        for cp in sends:
            cp.wait_send()
        for loc in locs:
            loc.wait()

    hbm = pl.BlockSpec(memory_space=pltpu.HBM)
    out_shape = [jax.ShapeDtypeStruct(((N_DEV,) + tuple(x.shape[1:])) if scatter else ((N_DEV,) + tuple(x.shape)), x.dtype)
                 for x in xs]
    return pl.pallas_call(
        body, name=name, in_specs=[hbm] * n, out_specs=[hbm] * n, out_shape=out_shape,
        scratch_shapes=[pltpu.SemaphoreType.DMA((n * nf,)), pltpu.SemaphoreType.DMA((n * nf,)),
                        pltpu.SemaphoreType.DMA((n,))],
    )(*xs)


def _gather_two_level(xs, name):
    n = len(xs)

    def body(*refs):
        gather = _Gather(refs[:n], refs[n:2 * n], *refs[2 * n:])
        gather.start()
        gather.finish()

    hbm = pl.BlockSpec(memory_space=pltpu.HBM)
    return pl.pallas_call(
        body, name=name, in_specs=[hbm] * n, out_specs=[hbm] * n,
        out_shape=_gather_out_shapes(xs), scratch_shapes=_gather_sems(n),
    )(*xs)


def _gather_out_shapes(xs):
    return [jax.ShapeDtypeStruct((N_DEV,) + tuple(x.shape), x.dtype) for x in xs]


def _gather_sems(n):
    return [pltpu.SemaphoreType.DMA((7 * n,)), pltpu.SemaphoreType.DMA((7 * n,)), pltpu.SemaphoreType.DMA((n,))]


class _Gather:
    def __init__(self, ins, outs, send_sems, recv_sems, loc_sems):
        self.ins, self.outs, self.sems = ins, outs, (send_sems, recv_sems, loc_sems)
        self.x, self.y, self.c = lax.axis_index("x"), lax.axis_index("y"), lax.axis_index("c")
        self.me, self.sibling = (self.x, self.y, self.c), (self.x, self.y, 1 - self.c)
        self.chips = [(1 - self.x, self.y), (self.x, 1 - self.y), (1 - self.x, 1 - self.y)]

    def slot(self, a, dev):
        return self.outs[a].at[4 * dev[0] + 2 * dev[1] + dev[2]]

    def copy(self, a, k, block, to, src=None):
        return pltpu.make_async_remote_copy(
            src_ref=self.slot(a, block) if src is None else src, dst_ref=self.slot(a, block),
            send_sem=self.sems[0].at[7 * a + k], recv_sem=self.sems[1].at[7 * a + k],
            device_id=to, device_id_type=pl.DeviceIdType.MESH)

    def local(self, a):
        return pltpu.make_async_copy(self.ins[a], self.slot(a, self.me), self.sems[2].at[a])

    def first(self):
        out = []
        for a in range(len(self.ins)):
            out.append(self.copy(a, 0, self.me, self.sibling, src=self.ins[a]))
            out += [self.copy(a, 1 + j, self.me, (*chip, self.c), src=self.ins[a]) for j, chip in enumerate(self.chips)]
        return out

    def start(self):
        for a in range(len(self.ins)):
            self.local(a).start()
        for cp in self.first():
            cp.start()

    def finish(self):
        n, c = len(self.ins), self.c
        passed = []
        for j, chip in enumerate(self.chips):
            for a in range(n):
                self.copy(a, 1 + j, (*chip, c), self.me).wait_recv()
                fwd = self.copy(a, 4 + j, (*chip, c), self.sibling)
                fwd.start()
                passed.append(fwd)
        for a in range(n):
            self.copy(a, 0, self.sibling, self.me).wait_recv()
            for j, chip in enumerate(self.chips):
                self.copy(a, 4 + j, (*chip, 1 - c), self.me).wait_recv()
        for cp in self.first() + passed:
            cp.wait_send()
        for a in range(n):
            self.local(a).wait()


def _pair_exchange(xs, name):
    n = len(xs)

    def body(*refs):
        ins, outs = refs[:n], refs[n:2 * n]
        send_sems, recv_sems = refs[2 * n:]
        x, y, c = lax.axis_index("x"), lax.axis_index("y"), lax.axis_index("c")
        copies = []
        for a in range(n):
            for q in range(4):
                cp = pltpu.make_async_remote_copy(
                    src_ref=ins[a].at[2 * q + (1 - c)], dst_ref=outs[a].at[q], send_sem=send_sems.at[4 * a + q],
                    recv_sem=recv_sems.at[4 * a + q], device_id=(x, y, 1 - c), device_id_type=pl.DeviceIdType.MESH)
                cp.start()
                copies.append(cp)
        for cp in copies:
            cp.wait_recv()
        for cp in copies:
            cp.wait_send()

    hbm = pl.BlockSpec(memory_space=pltpu.HBM)
    return pl.pallas_call(
        body, name=name, in_specs=[hbm] * n, out_specs=[hbm] * n,
        out_shape=[jax.ShapeDtypeStruct((4,) + tuple(x.shape[1:]), x.dtype) for x in xs],
        scratch_shapes=[pltpu.SemaphoreType.DMA((4 * n,)), pltpu.SemaphoreType.DMA((4 * n,))],
    )(*xs)


def _chip_exchange(xs, name):
    n = len(xs)

    def body(*refs):
        copies = _chip_copies(refs[:n], refs[n:2 * n], *refs[2 * n:])
        _chip_copies_start(copies)
        _chip_copies_wait(copies)

    hbm = pl.BlockSpec(memory_space=pltpu.HBM)
    return pl.pallas_call(
        body, name=name, in_specs=[hbm] * n, out_specs=[hbm] * n,
        out_shape=[jax.ShapeDtypeStruct(x.shape, x.dtype) for x in xs],
        scratch_shapes=_chip_copies_sems(n),
    )(*xs)


def _chip_copies_sems(n):
    return [pltpu.SemaphoreType.DMA((3 * n,)), pltpu.SemaphoreType.DMA((3 * n,)), pltpu.SemaphoreType.DMA((n,))]


def _chip_copies(ins, outs, send_sems, recv_sems, loc_sems):
    x, y, c = lax.axis_index("x"), lax.axis_index("y"), lax.axis_index("c")
    mine = 2 * x + y
    locs, sends, recvs = [], [], []
    for a in range(len(ins)):
        locs.append(pltpu.make_async_copy(ins[a].at[mine], outs[a].at[mine], loc_sems.at[a]))
        for k, (fx, fy) in enumerate([(1, 0), (0, 1), (1, 1)]):
            px, py = x ^ fx, y ^ fy
            peer = 2 * px + py
            sems = dict(send_sem=send_sems.at[3 * a + k], recv_sem=recv_sems.at[3 * a + k],
                        device_id=(px, py, c), device_id_type=pl.DeviceIdType.MESH)
            sends.append(pltpu.make_async_remote_copy(src_ref=ins[a].at[peer], dst_ref=outs[a].at[mine], **sems))
            recvs.append(pltpu.make_async_remote_copy(src_ref=ins[a].at[peer], dst_ref=outs[a].at[peer], **sems))
    return locs, sends, recvs


def _chip_copies_start(copies):
    locs, sends, _ = copies
    for cp in locs + sends:
        cp.start()


def _chip_copies_wait(copies):
    locs, sends, recvs = copies
    for cp in recvs:
        cp.wait_recv()
    for cp in sends:
        cp.wait_send()
    for cp in locs:
        cp.wait()


def _pair_sum(a, b, name):
    shape = a.shape
    cols = shape[-1]
    rows = a.size // cols
    a, b = a.reshape(rows, cols), b.reshape(rows, cols)
    tr = next((t for t in (512, 256, 128, 64, 32, 16) if rows % t == 0 and t * cols <= 2 * ADAMW_TILE_ELEMS), rows)

    def body(a_ref, b_ref, o_ref):
        o_ref[...] = (a_ref[...].astype(F32) + b_ref[...].astype(F32)).astype(o_ref.dtype)

    blk = pl.BlockSpec((tr, cols), lambda i: (i, 0))
    return pl.pallas_call(body, name=name, grid=(rows // tr,), in_specs=[blk, blk], out_specs=blk,
                          out_shape=jax.ShapeDtypeStruct((rows, cols), a.dtype),
                          compiler_params=_cparams(("parallel",)))(a, b).reshape(shape)


def _adamw(parts, w, m, v, name):
    shape = w.shape
    cols = shape[-1]
    rows = w.size // cols
    nparts = parts.shape[0]
    parts, w, m, v = parts.reshape(nparts, rows, cols), w.reshape(rows, cols), m.reshape(rows, cols), v.reshape(rows, cols)
    tr = next((t for t in (256, 128, 64, 32, 16) if rows % t == 0 and t * cols <= ADAMW_TILE_ELEMS), rows)
    c1 = 1.0 - ADAM_B1 ** ADAM_STEP
    c2 = 1.0 - ADAM_B2 ** ADAM_STEP

    def body(p_ref, w_ref, m_ref, v_ref, g_ref, d_ref, mo_ref, vo_ref):
        g = p_ref[0].astype(F32)
        for q in range(1, nparts):
            g = g + p_ref[q].astype(F32)
        mn = ADAM_B1 * m_ref[...] + (1.0 - ADAM_B1) * g
        vn = ADAM_B2 * v_ref[...] + (1.0 - ADAM_B2) * (g * g)
        g_ref[...] = g
        mo_ref[...] = mn
        vo_ref[...] = vn
        d_ref[...] = -ADAM_LR * ((mn / c1) / (jnp.sqrt(vn / c2) + ADAM_EPS) + ADAM_WD * w_ref[...])

    blk = pl.BlockSpec((tr, cols), lambda i: (i, 0))
    out = jax.ShapeDtypeStruct((rows, cols), F32)
    res = pl.pallas_call(
        body, name=name, grid=(rows // tr,),
        in_specs=[pl.BlockSpec((nparts, tr, cols), lambda i: (0, i, 0)), blk, blk, blk],
        out_specs=[blk] * 4, out_shape=[out] * 4, compiler_params=_cparams(("parallel",)))(parts, w, m, v)
    return [r.reshape(shape) for r in res]


BIG = ["w_in", "w_uq", "w_ukv", "w_sb_out", "w_mla_out", "w_o", "w_up", "w_down"]
ROW_SHARDED = {"w_o", "w_down"}
SMALL = ["conv_w", "meta_tokens"]
SHARDED = BIG + SMALL
REPL = ["norm_mix", "q_norm", "kv_norm", "norm_ffn", "conv_b", "final_norm"]


def _pack_rows(flat, row_mult):
    unit = PACK_W * row_mult
    total = -(-flat.shape[0] // unit) * unit
    return jnp.pad(flat, (0, total - flat.shape[0])).reshape(-1, PACK_W)


def _padded_cols(c):
    return -(-c // LANES) * LANES


def _pad_block(a, name):
    c = a.shape[-1]
    if name in ROW_SHARDED or c % LANES == 0:
        return a
    return jnp.pad(a, [(0, 0)] * (a.ndim - 1) + [(0, _padded_cols(c) - c)])


def _full_from_slots(slots, name, c):
    if name in ROW_SHARDED:
        return jnp.transpose(slots, (1, 0, 2, 3)).reshape(slots.shape[1], -1, slots.shape[3])
    return jnp.concatenate([slots[q][..., :c] for q in range(N_DEV)], axis=-1)


def _slots_from_full(full, name):
    if name in ROW_SHARDED:
        l, rr, n = full.shape
        return jnp.transpose(full.reshape(l, N_DEV, rr // N_DEV, n), (1, 0, 2, 3))
    c = full.shape[-1] // N_DEV
    return jnp.stack([_pad_block(full[..., q * c:(q + 1) * c], name) for q in range(N_DEV)])


def _swap_halves(t):
    half = t.shape[-1] // 2
    return jnp.concatenate([t[..., half:], t[..., :half]], axis=-1)


def _in_offsets(d):
    widths = (d.sbw, d.sbw, d.sbw, d.q_lora, d.kv_lora, d.rope, d.d, d.d)
    offs, o = [], 0
    for w in widths:
        offs.append((o, o + w))
        o += w
    return offs


def _prime_weights(w_in, w_uq, w_ukv, d):
    offs = _in_offsets(d)
    cols = lambda k: w_in[:, offs[k][0]:offs[k][1]]
    kr = cols(5)
    zpad = jnp.zeros((d.d, d.g_off - d.kr_off - LANES), w_in.dtype)
    w_inb = jnp.concatenate([cols(3), cols(4), kr, _swap_halves(kr), zpad, cols(6), cols(7)], axis=1)
    w_ina = w_in[:, :d.wa]
    uq = w_uq.reshape(d.q_lora, d.mla_heads, d.nope + d.rope)
    rope = uq[..., d.nope:]
    w_uq = jnp.concatenate([uq[..., :d.nope], rope, _swap_halves(rope)], axis=-1).reshape(d.q_lora, d.qw)
    ukv = w_ukv.reshape(d.kv_lora, d.mla_heads, d.nope + d.vdim)
    w_ukv = jnp.concatenate([ukv[..., :d.nope].reshape(d.kv_lora, -1), ukv[..., d.nope:].reshape(d.kv_lora, -1)], axis=1)
    return dict(w_ina=w_ina, w_inb=w_inb, w_in=jnp.concatenate([w_ina, w_inb], axis=1), w_uq=w_uq, w_ukv=w_ukv)


def _unprime_grads(g, d):
    gi = g["w_in"]
    b = gi[:, d.wa:]
    kr = b[:, d.kr_off:d.kr_off + d.rope] + _swap_halves(b[:, d.kr_off + d.rope:d.kr_off + 2 * d.rope])
    w_in = jnp.concatenate([gi[:, :d.wa], b[:, :d.kr_off], kr, b[:, d.g_off:]], axis=1)
    uq = g["w_uq"].reshape(d.q_lora, d.mla_heads, 2 * LANES)
    rope = uq[..., d.nope:d.nope + d.rope] + _swap_halves(uq[..., d.nope + d.rope:])
    w_uq = jnp.concatenate([uq[..., :d.nope], rope], axis=-1).reshape(d.q_lora, -1)
    hw = d.mla_heads * d.nope
    ukv = g["w_ukv"]
    w_ukv = jnp.concatenate([ukv[:, :hw].reshape(d.kv_lora, d.mla_heads, d.nope),
                             ukv[:, hw:].reshape(d.kv_lora, d.mla_heads, d.vdim)], axis=-1).reshape(d.kv_lora, -1)
    return dict(g, w_in=w_in, w_uq=w_uq, w_ukv=w_ukv)


def _layer_fwd(h, add, w, norm_mix, q_norm, kv_norm, norm_ffn, cw, cb, ctab, stab, d, tag, ride=(), arrived=None):
    s = {}
    if add is None:
        s["h"] = h
        s["hn"], s["r1"] = _norm_fwd(h, norm_mix, width=d.d, cidx=0, name=f"norm_mix_{tag}")
    else:
        s["h"], s["hn"], s["r1"] = _norm_fwd(h, norm_mix, width=d.d, cidx=0, add=add, name=f"norm_mix_{tag}")
    s["pa"] = _mm(s["hn"], w["w_ina"], out_dtype=BF16, name=f"proj_a_{tag}")
    s["pb"] = _mm(s["hn"], w["w_inb"], name=f"proj_b_{tag}")
    s["o_sb"], s["carry"] = _sb_fwd(s["pa"], d, f"sb_fwd_{tag}")
    s["cqn"], s["rq"] = _norm_fwd(s["pb"], q_norm, width=d.q_lora, cidx=0, name=f"norm_q_{tag}")
    s["ckn"], s["rk"] = _norm_fwd(s["pb"], kv_norm, width=d.kv_lora, cidx=d.q_lora // d.kv_lora, name=f"norm_kv_{tag}")
    qraw = _mm(s["cqn"], w["w_uq"], name=f"uq_{tag}")
    s["kv"] = _mm(s["ckn"], w["w_ukv"], out_dtype=BF16, name=f"ukv_{tag}")
    s["qm"], s["kr"] = _mla_prep_fwd(qraw, s["pb"], ctab, stab, d, f"mla_prep_{tag}")
    (s["o_mla"], s["lse"]), gathered = _mla_fwd(s["qm"], s["kv"], s["kr"], d, f"mla_fwd_{tag}", ride=ride)
    if arrived is not None:
        arrived(gathered)
    s["b_sb"] = _mm(s["o_sb"], w["w_sb_out"], name=f"sb_out_{tag}")
    s["b_mla"] = _mm(s["o_mla"], w["w_mla_out"], name=f"mla_out_{tag}")
    s["merged"] = _gate_fwd(s["pb"], s["b_sb"], s["b_mla"], d, f"gate_{tag}")
    mix = _mm(s["merged"], w["w_o"], name=f"w_o_{tag}")
    s["h1"], s["hn2"], s["r2"] = _norm_fwd(s["h"], norm_ffn, width=d.d, cidx=0, add=mix, name=f"norm_ffn_{tag}")
    s["up"] = _mm(s["hn2"], w["w_up"], name=f"w_up_{tag}")
    s["act"], s["ua"], s["ug"] = _convglu_fwd(s["up"], cw, cb, d, f"convglu_{tag}")
    ffn = _mm(s["act"], w["w_down"], name=f"w_down_{tag}")
    return s, ffn


def _layer_bwd(dh2, dh2b, s, w, norm_mix, q_norm, kv_norm, norm_ffn, cw, cb, ctab, stab, d, tag, ride=(),
               ride_more=None):
    g = {}
    dact = _mm(dh2b, w["w_down"], tb=True, name=f"d_act_{tag}")
    g["w_down"] = _mm(s["act"], dh2b, ta=True, name=f"g_w_down_{tag}")
    dup_a, dup_g, sums_a, sums_g = _convglu_bwd(s["up"], s["ua"], s["ug"], dact, cw, d, f"convglu_bwd_{tag}")
    dup = jnp.concatenate([dup_a, dup_g], axis=1)
    g["conv_w"] = jnp.concatenate([sums_a[0:3], sums_g[0:3]], axis=1)
    g["conv_b"] = jnp.concatenate([sums_a[3], sums_g[3]], axis=0)
    g["w_up"] = _mm(s["hn2"], dup, ta=True, name=f"g_w_up_{tag}")
    if ride_more is not None:
        ride = list(ride) + list(ride_more(g))
    dhn2 = _mm(dup, w["w_up"], tb=True, name=f"d_hn2_{tag}")
    dh1, dh1b, g["norm_ffn"] = _norm_bwd(dhn2, s["h1"], s["r2"], norm_ffn, width=d.d, cidx=0, dres=dh2,
                                         name=f"norm_ffn_bwd_{tag}")
    dmerged = _mm(dh1b, w["w_o"], tb=True, name=f"d_merged_{tag}")
    g["w_o"] = _mm(s["merged"], dh1b, ta=True, name=f"g_w_o_{tag}")
    db_sb, db_mla, dg_sb, dg_mla = _gate_bwd(dmerged, s["pb"], s["b_sb"], s["b_mla"], d, f"gate_bwd_{tag}")
    do_sb = _mm(db_sb, w["w_sb_out"], tb=True, name=f"d_o_sb_{tag}")
    g["w_sb_out"] = _mm(s["o_sb"], db_sb, ta=True, name=f"g_w_sb_out_{tag}")
    do_mla = _mm(db_mla, w["w_mla_out"], tb=True, name=f"d_o_mla_{tag}")
    g["w_mla_out"] = _mm(s["o_mla"], db_mla, ta=True, name=f"g_w_mla_out_{tag}")
    dq_sb, dk_sb, dv_sb = _sb_bwd(s["pa"], do_sb, s["carry"], d, f"sb_bwd_{tag}")
    (dqm, dkn, dv, dkr), rode = _mla_bwd(s["qm"], s["kv"], s["kr"], s["o_mla"], do_mla, s["lse"], d, f"mla_bwd_{tag}",
                                         ride=ride)
    dqraw, dkr128 = _mla_prep_bwd(dqm, dkr, ctab, stab, d, f"mla_prep_bwd_{tag}")
    dkv = jnp.concatenate([dkn.astype(BF16), dv.astype(BF16)], axis=1)
    dcqn = _mm(dqraw, w["w_uq"], tb=True, name=f"d_cq_{tag}")
    g["w_uq"] = _mm(s["cqn"], dqraw, ta=True, name=f"g_w_uq_{tag}")
    dckn = _mm(dkv, w["w_ukv"], tb=True, name=f"d_ckv_{tag}")
    g["w_ukv"] = _mm(s["ckn"], dkv, ta=True, name=f"g_w_ukv_{tag}")
    dcq, g["q_norm"] = _norm_bwd(dcqn, s["pb"], s["rq"], q_norm, width=d.q_lora, cidx=0, out_dtype=BF16, name=f"norm_q_bwd_{tag}")
    dckv, g["kv_norm"] = _norm_bwd(dckn, s["pb"], s["rk"], kv_norm, width=d.kv_lora, cidx=d.q_lora // d.kv_lora,
                                   out_dtype=BF16, name=f"norm_kv_bwd_{tag}")
    zpad = jnp.zeros((d.lp, d.g_off - d.kr_off - LANES), BF16)
    dproj = jnp.concatenate([dq_sb, dk_sb.astype(BF16), dv_sb.astype(BF16), dcq, dckv, dkr128, zpad, dg_sb, dg_mla], axis=1)
    g["w_in"] = _mm(s["hn"], dproj, ta=True, name=f"g_w_in_{tag}")
    dhn = _mm(dproj, w["w_in"], tb=True, name=f"d_hn_{tag}")
    dh, dhb, g["norm_mix"] = _norm_bwd(dhn, s["h"], s["r1"], norm_mix, width=d.d, cidx=0, dres=dh1,
                                       name=f"norm_mix_bwd_{tag}")
    return dh, dhb, g, rode


def _step(d, x, p, m, v, loss_target):
    x = x.reshape(d.seq, d.d)
    target = loss_target.reshape(d.seq, d.d)

    def block(n, layers=None):
        a = p[n] if layers is None else p[n][layers]
        return _pad_block(a, n).astype(BF16 if n in BIG else F32)

    def whole(n, slots):
        return _full_from_slots(slots, n, p[n].shape[-1])

    first_names = ["w_in", "w_uq", "w_ukv"]
    early = _gather_two_level([block(n, slice(0, 1)) for n in first_names] + [block(n) for n in SMALL], "gather_weights")
    full = {n: whole(n, g_) for n, g_ in zip(first_names + SMALL, early)}
    ws = [_prime_weights(*[full[n][0] for n in first_names], d)] + [dict() for _ in range(1, d.depth)]
    rest_names = [n for n in BIG if n not in first_names]
    late = [block(n, slice(1, d.depth)) for n in first_names] + [block(n) for n in rest_names]

    def arrived(gathered):
        later = {n: whole(n, g_) for n, g_ in zip(first_names, gathered)}
        rest = {n: whole(n, g_) for n, g_ in zip(rest_names, gathered[len(first_names):])}
        for l in range(d.depth):
            if l > 0:
                ws[l].update(_prime_weights(*[later[n][l - 1] for n in first_names], d))
            ws[l].update({n: rest[n][l] for n in rest_names})

    pos = jnp.arange(d.lp, dtype=F32) - d.pad
    half = d.rope // 2
    freqs = ROPE_THETA ** (-jnp.arange(half, dtype=F32) / half)
    ang = pos[:, None] * freqs[None, :]
    cos, sin = jnp.cos(ang), jnp.sin(ang)
    zero = jnp.zeros((d.lp, LANES - d.rope), F32)
    ctab = jnp.concatenate([cos, cos, zero], axis=1)
    stab = jnp.concatenate([-sin, sin, zero], axis=1)

    h = jnp.concatenate([jnp.zeros((d.pad, d.d), F32), full["meta_tokens"], x], axis=0)
    saved, add = [], None
    for l in range(d.depth):
        s, add = _layer_fwd(h, add, ws[l], p["norm_mix"][l], p["q_norm"][l], p["kv_norm"][l], p["norm_ffn"][l],
                            full["conv_w"][l], p["conv_b"][l].reshape(1, -1), ctab, stab, d, f"l{l}",
                            ride=late if l == 0 else (), arrived=arrived if l == 0 else None)
        saved.append(s)
        h = s["h1"]
    dh, dhb, loss_part, g_final = _head(h, add, target, p["final_norm"], d, "head")

    my_c = lax.axis_index("c")

    def pair_level(names, gfull, tag):
        gsend = [_slots_from_full(gfull[n].astype(BF16) if n in BIG else gfull[n], n) for n in names]
        theirs = _pair_exchange(gsend, f"scatter_grads_pair_{tag}")
        mine = [lax.dynamic_index_in_dim(g_.reshape((4, 2) + g_.shape[1:]), my_c, axis=1, keepdims=False) for g_ in gsend]
        return [_pair_sum(a_, b_, f"pair_sum_{n}_{tag}") for n, a_, b_ in zip(names, mine, theirs)]

    grads = [None] * d.depth
    recv_big = [None] * d.depth
    ride, ffn_names, rode_ffn = [], ["w_up", "w_down"], []
    for l in reversed(range(d.depth)):
        more = None
        if l == 0 and ride:
            more = lambda g_: pair_level(ffn_names, {n: g_[n][None] for n in ffn_names}, "l0_ffn")
        dh, dhb, g, rode = _layer_bwd(dh, dhb, saved[l], ws[l], p["norm_mix"][l], p["q_norm"][l], p["kv_norm"][l],
                                      p["norm_ffn"][l], full["conv_w"][l], p["conv_b"][l].reshape(1, -1), ctab, stab,
                                      d, f"l{l}", ride=ride, ride_more=more)
        if ride:
            recv_big[l + 1] = rode[:len(BIG)]
        if more is not None:
            rode_ffn = rode[len(BIG):]
        grads[l] = _unprime_grads(g, d)
        if l > 0:
            ride = pair_level(BIG, {n: grads[l][n][None] for n in BIG}, f"l{l}")
    grad_x = dh[d.first_tok:].reshape(1, d.seq, d.d)

    rest0 = [n for n in BIG if not (rode_ffn and n in ffn_names)]
    gfull = {n: grads[0][n][None] for n in rest0}
    gfull["conv_w"] = jnp.stack([grads[l]["conv_w"] for l in range(d.depth)])
    gfull["meta_tokens"] = dh[d.pad:d.first_tok]
    last = list(_chip_exchange(pair_level(rest0 + SMALL, gfull, "l0"), "scatter_grads_chips"))
    got0 = dict(zip(rest0, last[:len(rest0)]))
    got0.update(zip(ffn_names, rode_ffn))
    recv_big[0] = [got0[n] for n in BIG]
    grecv = [jnp.concatenate([recv_big[l][k] for l in range(d.depth)], axis=1) for k in range(len(BIG))] + last[len(rest0):]
    outs_sh = {n: _adamw(r_, _pad_block(p[n], n), _pad_block(m[n], n), _pad_block(v[n], n), f"adamw_{n}")
               for n, r_ in zip(SHARDED, grecv)}

    grep = {n: jnp.stack([grads[l][n].reshape(-1) for l in range(d.depth)]) for n in REPL if n != "final_norm"}
    grep["final_norm"] = g_final.reshape(-1)
    rflat = jnp.concatenate([grep[n].reshape(-1) for n in REPL] + [loss_part[0, 0:1]])
    (rparts,) = _exchange([_pack_rows(rflat, 8)], scatter=False, name="gather_small_grads")
    packr = lambda t: _pack_rows(jnp.concatenate([t[n].reshape(-1) for n in REPL] + [jnp.zeros((1,), F32)]), 8)
    outs_rp = _adamw(rparts, packr(p), packr(m), packr(v), "adamw_replicated")

    def unpack(flat, names, extra=0):
        res, off = {}, 0
        flat = flat.reshape(-1)
        for n in names:
            res[n] = flat[off:off + p[n].size].reshape(p[n].shape)
            off += p[n].size
        return res, flat[off:off + extra]

    results = []
    loss = None
    for k in range(4):
        sh = {n: outs_sh[n][k][..., :p[n].shape[-1]] for n in SHARDED}
        rp, tail = unpack(outs_rp[k], REPL, 1)
        if k == 0:
            loss = tail[0]
        results.append({**sh, **rp})
    return loss, grad_x, results


WEIGHTS = ["meta_tokens", "norm_mix", "w_in", "q_norm", "w_uq", "kv_norm", "w_ukv", "w_sb_out", "w_mla_out", "w_o",
           "norm_ffn", "w_up", "conv_w", "conv_b", "w_down", "final_norm"]


def _run(d, x, weights, loss_target, moments_m, moments_v):
    p = dict(zip(WEIGHTS, weights))
    m = dict(zip(WEIGHTS, moments_m))
    v = dict(zip(WEIGHTS, moments_v))
    loss, grad_x, res = _step(d, x, p, m, v, loss_target)
    out = [loss, grad_x]
    for k in range(4):
        out += [res[k][n] for n in WEIGHTS]
    return tuple(out)


def kernel(x, meta_tokens, norm_mix, w_in, q_norm, w_uq, kv_norm, w_ukv, w_sb_out, w_mla_out, w_o, norm_ffn, w_up, conv_w, conv_b, w_down, final_norm, loss_target, m_meta_tokens, m_norm_mix, m_w_in, m_q_norm, m_w_uq, m_kv_norm, m_w_ukv, m_w_sb_out, m_w_mla_out, m_w_o, m_norm_ffn, m_w_up, m_conv_w, m_conv_b, m_w_down, m_final_norm, v_meta_tokens, v_norm_mix, v_w_in, v_q_norm, v_w_uq, v_kv_norm, v_w_ukv, v_w_sb_out, v_w_mla_out, v_w_o, v_norm_ffn, v_w_up, v_conv_w, v_conv_b, v_w_down, v_final_norm):
    weights = [meta_tokens, norm_mix, w_in, q_norm, w_uq, kv_norm, w_ukv, w_sb_out, w_mla_out, w_o, norm_ffn, w_up,
               conv_w, conv_b, w_down, final_norm]
    ms = [m_meta_tokens, m_norm_mix, m_w_in, m_q_norm, m_w_uq, m_kv_norm, m_w_ukv, m_w_sb_out, m_w_mla_out, m_w_o,
          m_norm_ffn, m_w_up, m_conv_w, m_conv_b, m_w_down, m_final_norm]
    vs = [v_meta_tokens, v_norm_mix, v_w_in, v_q_norm, v_w_uq, v_kv_norm, v_w_ukv, v_w_sb_out, v_w_mla_out, v_w_o,
          v_norm_ffn, v_w_up, v_conv_w, v_conv_b, v_w_down, v_final_norm]
    return _run(PROD, x, weights, loss_target, ms, vs)
```

```python
import jax
import jax.numpy as jnp
from jax import lax
from jax.experimental import pallas as pl
from jax.experimental.pallas import tpu as pltpu

F32 = jnp.float32
BF16 = jnp.bfloat16

EPS = 1e-6
ROPE_THETA = 10000.0
ADAM_LR = 0.001
ADAM_B1 = 0.9
ADAM_B2 = 0.999
ADAM_EPS = 1e-08
ADAM_WD = 0.01
ADAM_STEP = 10
NEG = -1e30
DEAD = -110.0
LANES = 128
PACK_W = 1024
ADAMW_TILE_ELEMS = 256 * 1024
V7X_VMEM_LIMIT = 48 * 1024 * 1024
V7X_VMEM_LIMIT_BIG = 58 * 1024 * 1024
MESH_AXES = ("x", "y", "c")
N_DEV = 8
FLIPS = [(0, 0, 1), (0, 1, 0), (0, 1, 1), (1, 0, 0), (1, 0, 1), (1, 1, 0), (1, 1, 1)]


class _Dims:
    def __init__(self, d_model=2048, seq=8192, depth=2, n_meta=16, block=128, sb_heads=8, hd=128,
                 mla_heads=8, q_lora=512, kv_lora=256, nope=128, rope=64, vdim=128, d_ff=5632, tq=None):
        self.d, self.seq, self.depth, self.n_meta, self.block = d_model, seq, depth, n_meta, block
        self.sb_heads, self.hd, self.mla_heads = sb_heads, hd, mla_heads
        self.q_lora, self.kv_lora, self.nope, self.rope, self.vdim, self.f = q_lora, kv_lora, nope, rope, vdim, d_ff
        assert hd == LANES and nope == LANES and vdim == LANES and 2 * rope == LANES
        self.pad = block - n_meta
        self.lp = self.pad + n_meta + seq
        self.first_tok = self.pad + n_meta
        assert self.first_tok == block and self.lp % block == 0 and self.lp // block < LANES
        self.tq = tq or next(t for t in (640, 512, 256, 128) if self.lp % t == 0)
        assert self.tq % block == 0 and self.lp % self.tq == 0
        self.sbw = sb_heads * hd
        self.mlaw = mla_heads * vdim
        self.wa = 3 * self.sbw
        self.d_in = 3 * self.sbw + q_lora + kv_lora + rope + 2 * d_model
        self.tg = min(1024, d_model)
        self.kr_off = q_lora + kv_lora
        raw = self.kr_off + LANES
        self.g_off = -(-raw // self.tg) * self.tg
        self.wb = self.g_off + 2 * d_model
        self.qw = mla_heads * 2 * LANES


PROD = _Dims()


def _pick(n, prefs):
    for p in prefs:
        if n % p == 0:
            return p
    return n


def _cparams(sem, limit=V7X_VMEM_LIMIT):
    return pltpu.CompilerParams(dimension_semantics=sem, vmem_limit_bytes=limit)


def _mm(a, b, *, ta=False, tb=False, out_dtype=F32, name):
    if ta:
        kdim, m = a.shape
    else:
        m, kdim = a.shape
    if tb:
        n, k2 = b.shape
    else:
        k2, n = b.shape
    assert kdim == k2, (a.shape, b.shape, ta, tb)
    tm = _pick(m, (640, 512, 256, 128))
    tn = _pick(n, (1024, 512, 384, 256, 128))
    tk = _pick(kdim, (2816, 2048, 1664, 1408, 1024, 640, 512, 256, 128))
    nk = kdim // tk
    dn = (((0 if ta else 1,), (1 if tb else 0,)), ((), ()))

    def dot(a_ref, b_ref):
        return lax.dot_general(a_ref[...].astype(BF16), b_ref[...].astype(BF16), dn, preferred_element_type=F32)

    def body_one(a_ref, b_ref, o_ref):
        o_ref[...] = dot(a_ref, b_ref).astype(out_dtype)

    def body_acc(a_ref, b_ref, o_ref, acc_ref):
        k = pl.program_id(2)

        @pl.when(k == 0)
        def _():
            acc_ref[...] = dot(a_ref, b_ref)

        @pl.when((k > 0) & (k < nk - 1))
        def _():
            acc_ref[...] += dot(a_ref, b_ref)

        @pl.when(k == nk - 1)
        def _():
            o_ref[...] = (acc_ref[...] + dot(a_ref, b_ref)).astype(out_dtype)

    a_spec = pl.BlockSpec((tk, tm), lambda i, j, k: (k, i)) if ta else pl.BlockSpec((tm, tk), lambda i, j, k: (i, k))
    b_spec = pl.BlockSpec((tn, tk), lambda i, j, k: (j, k)) if tb else pl.BlockSpec((tk, tn), lambda i, j, k: (k, j))
    return pl.pallas_call(
        body_one if nk == 1 else body_acc, name=name, grid=(m // tm, n // tn, nk), in_specs=[a_spec, b_spec],
        out_specs=pl.BlockSpec((tm, tn), lambda i, j, k: (i, j)),
        out_shape=jax.ShapeDtypeStruct((m, n), out_dtype),
        scratch_shapes=[] if nk == 1 else [pltpu.VMEM((tm, tn), F32)],
        compiler_params=_cparams(("parallel", "parallel", "arbitrary")),
    )(a, b)


def _norm_fwd(x, g, *, width, cidx, add=None, name):
    rows = x.shape[0]
    tr = _pick(rows, (256, 128))
    has_add = add is not None

    def body(*refs):
        if has_add:
            x_ref, a_ref, g_ref, xn_ref, y_ref, r_ref = refs
            xv = x_ref[...] + a_ref[...]
            xn_ref[...] = xv
        else:
            x_ref, g_ref, y_ref, r_ref = refs
            xv = x_ref[...]
        r = lax.rsqrt(jnp.mean(xv * xv, axis=1, keepdims=True) + EPS)
        y_ref[...] = (xv * r * g_ref[...]).astype(BF16)
        r_ref[...] = r

    blk = pl.BlockSpec((tr, width), lambda i: (i, 0))
    in_specs = [pl.BlockSpec((tr, width), lambda i: (i, cidx))]
    args = [x]
    if has_add:
        in_specs.append(blk)
        args.append(add)
    in_specs.append(pl.BlockSpec((1, width), lambda i: (0, 0)))
    args.append(g.reshape(1, width))
    out_specs = [blk, pl.BlockSpec((tr, 1), lambda i: (i, 0))]
    out_shape = [jax.ShapeDtypeStruct((rows, width), BF16), jax.ShapeDtypeStruct((rows, 1), F32)]
    if has_add:
        out_specs.insert(0, blk)
        out_shape.insert(0, jax.ShapeDtypeStruct((rows, width), F32))
    return pl.pallas_call(body, name=name, grid=(rows // tr,), in_specs=in_specs, out_specs=out_specs,
                          out_shape=out_shape, compiler_params=_cparams(("parallel",)))(*args)


def _norm_bwd(dy, x, r, g, *, width, cidx, dres=None, out_dtype=F32, name):
    rows = x.shape[0]
    tr = _pick(rows, (256, 128))
    has_res = dres is not None

    def body(*refs):
        if has_res:
            dy_ref, x_ref, r_ref, g_ref, dr_ref, dx_ref, dxb_ref, dg_ref = refs
        else:
            dy_ref, x_ref, r_ref, g_ref, dx_ref, dg_ref = refs
        i = pl.program_id(0)

        @pl.when(i == 0)
        def _():
            dg_ref[...] = jnp.zeros_like(dg_ref)

        dyv, xv, rv = dy_ref[...], x_ref[...], r_ref[...]
        gy = dyv * g_ref[...]
        c = jnp.mean(gy * xv, axis=1, keepdims=True)
        dx = rv * gy - xv * (rv * rv * rv) * c
        if has_res:
            dx = dx + dr_ref[...]
            dxb_ref[...] = dx.astype(BF16)
        dx_ref[...] = dx.astype(out_dtype)
        dg_ref[...] += jnp.sum(dyv * xv * rv, axis=0, keepdims=True)

    blk = pl.BlockSpec((tr, width), lambda i: (i, 0))
    in_specs = [blk, pl.BlockSpec((tr, width), lambda i: (i, cidx)), pl.BlockSpec((tr, 1), lambda i: (i, 0)),
                pl.BlockSpec((1, width), lambda i: (0, 0))]
    args = [dy, x, r, g.reshape(1, width)]
    out_specs = [blk, pl.BlockSpec((1, width), lambda i: (0, 0))]
    out_shape = [jax.ShapeDtypeStruct((rows, width), out_dtype), jax.ShapeDtypeStruct((1, width), F32)]
    if has_res:
        in_specs.append(blk)
        args.append(dres)
        out_specs.insert(1, blk)
        out_shape.insert(1, jax.ShapeDtypeStruct((rows, width), BF16))
    return pl.pallas_call(
        body, name=name, grid=(rows // tr,), in_specs=in_specs, out_specs=out_specs, out_shape=out_shape,
        compiler_params=_cparams(("arbitrary",)))(*args)


def _split3(x):
    h1 = x.astype(BF16)
    r1 = x - h1.astype(F32)
    h2 = r1.astype(BF16)
    h3 = (r1 - h2.astype(F32)).astype(BF16)
    return h1, h2, h3


def _cum(x, tri):
    h1, h2, h3 = _split3(x)
    dot = lambda h: jnp.dot(h, tri, preferred_element_type=F32)
    return dot(h1) + dot(h2) + dot(h3)


def _dot_nt(a, b):
    return lax.dot_general(a, b, (((1,), (1,)), ((), ())), preferred_element_type=F32)


def _dot_tn(a, b):
    return lax.dot_general(a, b, (((0,), (0,)), ((), ())), preferred_element_type=F32)


def _sb_geometry(tq, t):
    assert t & (t - 1) == 0
    ri = lax.broadcasted_iota(jnp.int32, (tq, t), 0)
    return jnp.bitwise_and(ri, t - 1), jnp.right_shift(ri, t.bit_length() - 1), lax.broadcasted_iota(jnp.int32, (tq, t), 1)


def _sb_key_blocks(ref, base, r, t, kind):
    if kind == "low":
        return [ref[pl.ds(pl.multiple_of(jnp.maximum(base + g, 0) * t, t), t), :] for g in range(r)]
    slab = ref[pl.ds(pl.multiple_of(base * t, t), r * t), :]
    return [slab[g * t:(g + 1) * t, :] for g in range(r)]


def _sb_mask(geo, base, s, t, pad, kind):
    rowl, grp, col = geo
    if kind == "plain":
        return None
    if kind == "first":
        return col < rowl
    blk = base + grp
    causal = col < rowl + jnp.where(s > 0, t, 0)
    return (blk >= 0) & (blk * t + col >= pad) & causal


def _sb_fwd(qkv, d, name):
    nh, hd, lp, tq, t = d.sb_heads, d.hd, d.lp, d.tq, d.block
    r = tq // t
    scale = hd ** -0.5
    pad = d.pad

    def body(q_ref, k_ref, v_ref, o_ref, c_ref):
        i = pl.program_id(1)
        qs = [q_ref[g * t:(g + 1) * t, :] for g in range(r)]
        geo = _sb_geometry(tq, t)
        tri = (lax.broadcasted_iota(jnp.int32, (t, t), 0)
               > lax.broadcasted_iota(jnp.int32, (t, t), 1)).astype(BF16)
        lane = lax.broadcasted_iota(jnp.int32, (tq, LANES), 1)

        c_ref[...] = jnp.zeros_like(c_ref)

        def make_step(kind):
            def step(s, carry):
                acc, run = carry
                ks = _sb_key_blocks(k_ref, i * r - s, r, t, kind)
                vs = _sb_key_blocks(v_ref, i * r - s, r, t, kind)
                z = jnp.concatenate([_dot_nt(qs[g], ks[g]) for g in range(r)], axis=0) * scale
                e = jnp.exp(-jnp.abs(z))
                sp = jnp.maximum(z, 0.0) + jnp.log(1.0 + e)
                mask = _sb_mask(geo, i * r - s, s, t, pad, kind)
                spm = sp if mask is None else jnp.where(mask, sp, 0.0)
                w = jnp.exp(z - sp - _cum(spm, tri) + run)
                if mask is not None:
                    w = jnp.where(mask, w, 0.0)
                wb = w.astype(BF16)
                acc = acc + jnp.concatenate(
                    [jnp.dot(wb[g * t:(g + 1) * t, :], vs[g], preferred_element_type=F32) for g in range(r)], axis=0)
                c_ref[...] = jnp.where(lane == s, run, c_ref[...])
                run = run - jnp.sum(spm, axis=1, keepdims=True)
                return acc, run
            return step

        first, plain, low = make_step("first"), make_step("plain"), make_step("low")

        def alive(run):
            return (jnp.max(run) >= DEAD).astype(jnp.int32)

        def run_while(step, s, end, live, acc, run):
            def wbody(st):
                s, _, acc, run = st
                acc, run = step(s, (acc, run))
                return s + 1, alive(run), acc, run
            return lax.while_loop(lambda st: (st[0] < end) & (st[1] > 0), wbody, (s, live, acc, run))

        init = (jnp.zeros((tq, hd), F32), jnp.zeros((tq, 1), F32))
        carry = lax.fori_loop(0, jnp.minimum(i, 1), lambda _, cr: first(0, cr), init)
        acc, run = lax.fori_loop(0, 1 - jnp.minimum(i, 1), lambda _, cr: low(0, cr), carry)
        s, live, acc, run = run_while(plain, 1, i * r, alive(run), acc, run)
        end_low = jnp.where(s >= jnp.maximum(i * r, 1), (i + 1) * r, s)
        s, live, acc, run = run_while(low, s, end_low, live, acc, run)
        o_ref[...] = acc
        c_ref[...] = jnp.where(lane == LANES - 1, s.astype(F32), c_ref[...])

    return pl.pallas_call(
        body, name=name, grid=(nh, lp // tq),
        in_specs=[pl.BlockSpec((tq, hd), lambda h, i: (i, h)),
                  pl.BlockSpec((lp, hd), lambda h, i: (0, nh + h)),
                  pl.BlockSpec((lp, hd), lambda h, i: (0, 2 * nh + h))],
        out_specs=[pl.BlockSpec((tq, hd), lambda h, i: (i, h)),
                   pl.BlockSpec((None, tq, LANES), lambda h, i: (h, i, 0))],
        out_shape=[jax.ShapeDtypeStruct((lp, nh * hd), F32), jax.ShapeDtypeStruct((nh, lp, LANES), F32)],
        compiler_params=_cparams(("parallel", "arbitrary")),
    )(qkv, qkv, qkv)


def _sb_bwd(qkv, do, carry, d, name):
    nh, hd, lp, tq, t = d.sb_heads, d.hd, d.lp, d.tq, d.block
    r = tq // t
    scale = hd ** -0.5
    pad = d.pad

    def body(q_ref, k_ref, v_ref, do_ref, c_ref, dq_ref, dk_ref, dv_ref):
        i = pl.program_id(1)

        @pl.when(i == 0)
        def _():
            dk_ref[...] = jnp.zeros_like(dk_ref)
            dv_ref[...] = jnp.zeros_like(dv_ref)

        rows = lambda x, g: x[g * t:(g + 1) * t, :]
        qs = [q_ref[g * t:(g + 1) * t, :] for g in range(r)]
        dobs = [do_ref[g * t:(g + 1) * t, :].astype(BF16) for g in range(r)]
        geo = _sb_geometry(tq, t)
        ri = lax.broadcasted_iota(jnp.int32, (t, t), 0)
        ci = lax.broadcasted_iota(jnp.int32, (t, t), 1)
        tri_suf = (ri > ci).astype(BF16)
        tri_pre = (ri < ci).astype(BF16)
        lane = lax.broadcasted_iota(jnp.int32, (tq, LANES), 1)

        def make_step(kind):
            def step(s, carry):
                dq, pc = carry
                base = i * r - s
                ks = _sb_key_blocks(k_ref, base, r, t, kind)
                vs = _sb_key_blocks(v_ref, base, r, t, kind)
                z = jnp.concatenate([_dot_nt(qs[g], ks[g]) for g in range(r)], axis=0) * scale
                e = jnp.exp(-jnp.abs(z))
                sp = jnp.maximum(z, 0.0) + jnp.log(1.0 + e)
                mask = _sb_mask(geo, base, s, t, pad, kind)
                spm = sp if mask is None else jnp.where(mask, sp, 0.0)
                run = jnp.sum(jnp.where(lane == s, c_ref[...], 0.0), axis=1, keepdims=True)
                w = jnp.exp(z - sp - _cum(spm, tri_suf) + run)
                if mask is not None:
                    w = jnp.where(mask, w, 0.0)
                gw = w * jnp.concatenate([_dot_nt(dobs[g], vs[g]) for g in range(r)], axis=0)
                p = _cum(gw, tri_pre) + pc
                inv = 1.0 / (1.0 + e)
                sig = jnp.where(z >= 0.0, inv, e * inv)
                dz = (gw * (1.0 - sig) - sig * p) * scale
                if mask is not None:
                    dz = jnp.where(mask, dz, 0.0)
                dzb, wb = dz.astype(BF16), w.astype(BF16)
                dq = dq + jnp.concatenate(
                    [jnp.dot(rows(dzb, g), ks[g], preferred_element_type=F32) for g in range(r)], axis=0)
                dks = [_dot_tn(rows(dzb, g), qs[g]) for g in range(r)]
                dvs = [_dot_tn(rows(wb, g), dobs[g]) for g in range(r)]
                if kind == "low":
                    for g in range(r):
                        at = pl.ds(pl.multiple_of(jnp.maximum(base + g, 0) * t, t), t)
                        dk_ref[at, :] += dks[g]
                        dv_ref[at, :] += dvs[g]
                else:
                    at = pl.ds(pl.multiple_of(base * t, t), tq)
                    dk_ref[at, :] += jnp.concatenate(dks, axis=0)
                    dv_ref[at, :] += jnp.concatenate(dvs, axis=0)
                pc = pc + jnp.sum(gw, axis=1, keepdims=True)
                return dq, pc
            return step

        first, plain, low = make_step("first"), make_step("plain"), make_step("low")
        nsteps = jnp.max(jnp.where(lane == LANES - 1, c_ref[...], 0.0)).astype(jnp.int32)
        low_from = jnp.maximum(i * r, 1)
        plain_end = jnp.minimum(nsteps, low_from)
        carry = (jnp.zeros((tq, hd), F32), jnp.zeros((tq, 1), F32))
        carry = lax.fori_loop(0, jnp.maximum(nsteps - low_from, 0), lambda jj, cr: low(nsteps - 1 - jj, cr), carry)
        carry = lax.fori_loop(0, plain_end - 1, lambda jj, cr: plain(plain_end - 1 - jj, cr), carry)
        carry = lax.fori_loop(0, jnp.minimum(i, 1), lambda _, cr: first(0, cr), carry)
        dq, _ = lax.fori_loop(0, 1 - jnp.minimum(i, 1), lambda _, cr: low(0, cr), carry)
        dq_ref[...] = dq.astype(BF16)

    w3 = nh * hd
    return pl.pallas_call(
        body, name=name, grid=(nh, lp // tq),
        in_specs=[pl.BlockSpec((tq, hd), lambda h, i: (i, h)),
                  pl.BlockSpec((lp, hd), lambda h, i: (0, nh + h)),
                  pl.BlockSpec((lp, hd), lambda h, i: (0, 2 * nh + h)),
                  pl.BlockSpec((tq, hd), lambda h, i: (i, h)),
                  pl.BlockSpec((None, tq, LANES), lambda h, i: (h, i, 0))],
        out_specs=[pl.BlockSpec((tq, hd), lambda h, i: (i, h)),
                   pl.BlockSpec((lp, hd), lambda h, i: (0, h)),
                   pl.BlockSpec((lp, hd), lambda h, i: (0, h))],
        out_shape=[jax.ShapeDtypeStruct((lp, w3), BF16), jax.ShapeDtypeStruct((lp, w3), F32),
                   jax.ShapeDtypeStruct((lp, w3), F32)],
        compiler_params=_cparams(("arbitrary", "arbitrary")),
    )(qkv, qkv, qkv, do, carry)


def _mla_prep_fwd(qraw, projb, ctab, stab, d, name):
    lp, nh = d.lp, d.mla_heads
    tr = _pick(lp, (256, 128))
    kidx = d.kr_off // LANES

    def rope(u, c, s):
        return u * c + pltpu.roll(u, LANES // 2, 1) * s

    def body(q_ref, k_ref, c_ref, s_ref, qm_ref, kr_ref):
        c, s = c_ref[...], s_ref[...]
        for h in range(nh):
            base = 2 * LANES * h
            qm_ref[:, base:base + LANES] = q_ref[:, base:base + LANES].astype(BF16)
            qm_ref[:, base + LANES:base + 2 * LANES] = rope(q_ref[:, base + LANES:base + 2 * LANES], c, s).astype(BF16)
        kr_ref[...] = rope(k_ref[...], c, s).astype(BF16)

    tab = pl.BlockSpec((tr, LANES), lambda i: (i, 0))
    return pl.pallas_call(
        body, name=name, grid=(lp // tr,),
        in_specs=[pl.BlockSpec((tr, d.qw), lambda i: (i, 0)), pl.BlockSpec((tr, LANES), lambda i: (i, kidx)), tab, tab],
        out_specs=[pl.BlockSpec((tr, d.qw), lambda i: (i, 0)), tab],
        out_shape=[jax.ShapeDtypeStruct((lp, d.qw), BF16), jax.ShapeDtypeStruct((lp, LANES), BF16)],
        compiler_params=_cparams(("parallel",)))(qraw, projb, ctab, stab)


def _mla_prep_bwd(dqm, dkr, ctab, stab, d, name):
    lp, nh = d.lp, d.mla_heads
    tr = _pick(lp, (256, 128))

    def unrope(g, c, s):
        return g * c + pltpu.roll(g * s, LANES // 2, 1)

    def body(dq_ref, dk_ref, c_ref, s_ref, o_ref, ok_ref):
        c, s = c_ref[...], s_ref[...]
        for h in range(nh):
            base = 2 * LANES * h
            o_ref[:, base:base + LANES] = dq_ref[:, base:base + LANES].astype(BF16)
            o_ref[:, base + LANES:base + 2 * LANES] = unrope(dq_ref[:, base + LANES:base + 2 * LANES], c, s).astype(BF16)
        ok_ref[...] = unrope(dk_ref[...], c, s).astype(BF16)

    tab = pl.BlockSpec((tr, LANES), lambda i: (i, 0))
    wide = pl.BlockSpec((tr, d.qw), lambda i: (i, 0))
    return pl.pallas_call(
        body, name=name, grid=(lp // tr,), in_specs=[wide, tab, tab, tab], out_specs=[wide, tab],
        out_shape=[jax.ShapeDtypeStruct((lp, d.qw), BF16), jax.ShapeDtypeStruct((lp, LANES), BF16)],
        compiler_params=_cparams(("parallel",)))(dqm, dkr, ctab, stab)


def _mla_fwd(qm, kv, kr, d, name, ride=()):
    nh, lp, t = d.mla_heads, d.lp, d.tq
    scale = (d.nope + d.rope) ** -0.5
    pad = d.pad
    nr = len(ride)

    def body(*refs):
        q_ref, kn_ref, v_ref, kr_ref = refs[:4]
        o_ref, lse_ref = refs[4 + nr:6 + nr]
        h = pl.program_id(0)
        i = pl.program_id(1)
        if nr:
            ride_refs = (refs[4:4 + nr], refs[6 + nr:6 + 2 * nr]) + tuple(refs[6 + 2 * nr:])

            @pl.when((h == 0) & (i == 0))
            def _():
                _Gather(*ride_refs).start()

        q = q_ref[...]
        row = i * t + lax.broadcasted_iota(jnp.int32, (t, t), 0)
        colb = lax.broadcasted_iota(jnp.int32, (t, t), 1)

        def make_step(masked):
            def step(j, carry):
                acc, m, l = carry
                off = pl.multiple_of(j * t, t)
                kc = jnp.concatenate([kn_ref[pl.ds(off, t), :], kr_ref[pl.ds(off, t), :]], axis=1)
                s = _dot_nt(q, kc) * scale
                if masked:
                    col = j * t + colb
                    s = jnp.where((col <= row) & (col >= pad), s, NEG)
                m_new = jnp.maximum(m, jnp.max(s, axis=1, keepdims=True))
                alpha = jnp.exp(m - m_new)
                p = jnp.exp(s - m_new)
                l = alpha * l + jnp.sum(p, axis=1, keepdims=True)
                acc = alpha * acc + jnp.dot(p.astype(BF16), v_ref[pl.ds(off, t), :], preferred_element_type=F32)
                return acc, m_new, l
            return step

        masked, plain = make_step(True), make_step(False)
        carry = masked(0, (jnp.zeros((t, LANES), F32), jnp.full((t, 1), NEG, F32), jnp.zeros((t, 1), F32)))
        carry = lax.fori_loop(1, i, plain, carry)
        acc, m, l = lax.fori_loop(i, i + jnp.minimum(i, 1), masked, carry)
        rowv = i * t + lax.broadcasted_iota(jnp.int32, (t, LANES), 0)
        o_ref[...] = jnp.where(rowv >= pad, acc / l, 0.0)
        lse_ref[...] = m + jnp.log(l)
        if nr:
            @pl.when((h == nh - 1) & (i == lp // t - 1))
            def _():
                _Gather(*ride_refs).finish()

    hbm = pl.BlockSpec(memory_space=pltpu.HBM)
    res = pl.pallas_call(
        body, name=name, grid=(nh, lp // t),
        in_specs=[pl.BlockSpec((t, 2 * LANES), lambda h, i: (i, h)),
                  pl.BlockSpec((lp, LANES), lambda h, i: (0, h)),
                  pl.BlockSpec((lp, LANES), lambda h, i: (0, nh + h)),
                  pl.BlockSpec((lp, LANES), lambda h, i: (0, 0))] + [hbm] * nr,
        out_specs=[pl.BlockSpec((t, LANES), lambda h, i: (i, h)),
                   pl.BlockSpec((None, t, 1), lambda h, i: (h, i, 0))] + [hbm] * nr,
        out_shape=[jax.ShapeDtypeStruct((lp, nh * LANES), F32), jax.ShapeDtypeStruct((nh, lp, 1), F32)]
                  + _gather_out_shapes(ride),
        scratch_shapes=_gather_sems(nr) if nr else [],
        compiler_params=_cparams(("arbitrary", "arbitrary")),
    )(qm, kv, kv, kr, *ride)
    return res[:2], list(res[2:])


def _mla_bwd(qm, kv, kr, o, do, lse, d, name, ride=()):
    nh, lp, t = d.mla_heads, d.lp, d.tq
    scale = (d.nope + d.rope) ** -0.5
    pad = d.pad
    nr = len(ride)

    def body(*refs):
        q_ref, kn_ref, v_ref, kr_ref, o_ref, do_ref, lse_ref = refs[:7]
        dq_ref, dkn_ref, dv_ref, dkr_ref = refs[7 + nr:11 + nr]
        h = pl.program_id(0)
        i = pl.program_id(1)
        if nr:
            ride_refs = (refs[7:7 + nr], refs[11 + nr:11 + 2 * nr]) + tuple(refs[11 + 2 * nr:])

            @pl.when((h == 0) & (i == 0))
            def _():
                _chip_copies_start(_chip_copies(*ride_refs))

        @pl.when(i == 0)
        def _():
            dkn_ref[...] = jnp.zeros_like(dkn_ref)
            dv_ref[...] = jnp.zeros_like(dv_ref)

        @pl.when((i == 0) & (h == 0))
        def _():
            dkr_ref[...] = jnp.zeros_like(dkr_ref)

        q = q_ref[...]
        dof = do_ref[...]
        dob = dof.astype(BF16)
        delta = jnp.sum(dof * o_ref[...], axis=1, keepdims=True)
        lse = lse_ref[...]
        row = i * t + lax.broadcasted_iota(jnp.int32, (t, t), 0)
        colb = lax.broadcasted_iota(jnp.int32, (t, t), 1)

        def make_step(masked):
            def step(j, dq):
                off = pl.multiple_of(j * t, t)
                kc = jnp.concatenate([kn_ref[pl.ds(off, t), :], kr_ref[pl.ds(off, t), :]], axis=1)
                v = v_ref[pl.ds(off, t), :]
                s = _dot_nt(q, kc) * scale
                if masked:
                    col = j * t + colb
                    mask = (col <= row) & (col >= pad)
                    p = jnp.where(mask, jnp.exp(jnp.where(mask, s, NEG) - lse), 0.0)
                else:
                    p = jnp.exp(s - lse)
                dp = _dot_nt(dob, v)
                dsb = (p * (dp - delta) * scale).astype(BF16)
                dq = dq + jnp.dot(dsb, kc, preferred_element_type=F32)
                dkc = _dot_tn(dsb, q)
                dkn_ref[pl.ds(off, t), :] += dkc[:, :LANES]
                dkr_ref[pl.ds(off, t), :] += dkc[:, LANES:]
                dv_ref[pl.ds(off, t), :] += _dot_tn(p.astype(BF16), dob)
                return dq
            return step

        masked, plain = make_step(True), make_step(False)
        dq = masked(0, jnp.zeros((t, 2 * LANES), F32))
        dq = lax.fori_loop(1, i, plain, dq)
        dq_ref[...] = lax.fori_loop(i, i + jnp.minimum(i, 1), masked, dq)
        if nr:
            @pl.when((h == nh - 1) & (i == lp // t - 1))
            def _():
                _chip_copies_wait(_chip_copies(*ride_refs))

    hbm = pl.BlockSpec(memory_space=pltpu.HBM)
    res = pl.pallas_call(
        body, name=name, grid=(nh, lp // t),
        in_specs=[pl.BlockSpec((t, 2 * LANES), lambda h, i: (i, h)),
                  pl.BlockSpec((lp, LANES), lambda h, i: (0, h), pipeline_mode=pl.Buffered(1)),
                  pl.BlockSpec((lp, LANES), lambda h, i: (0, nh + h), pipeline_mode=pl.Buffered(1)),
                  pl.BlockSpec((lp, LANES), lambda h, i: (0, 0), pipeline_mode=pl.Buffered(1)),
                  pl.BlockSpec((t, LANES), lambda h, i: (i, h)),
                  pl.BlockSpec((t, LANES), lambda h, i: (i, h)),
                  pl.BlockSpec((None, t, 1), lambda h, i: (h, i, 0))] + [hbm] * nr,
        out_specs=[pl.BlockSpec((t, 2 * LANES), lambda h, i: (i, h)),
                   pl.BlockSpec((lp, LANES), lambda h, i: (0, h)),
                   pl.BlockSpec((lp, LANES), lambda h, i: (0, h)),
                   pl.BlockSpec((lp, LANES), lambda h, i: (0, 0))] + [hbm] * nr,
        out_shape=[jax.ShapeDtypeStruct((lp, nh * 2 * LANES), F32), jax.ShapeDtypeStruct((lp, nh * LANES), F32),
                   jax.ShapeDtypeStruct((lp, nh * LANES), F32), jax.ShapeDtypeStruct((lp, LANES), F32)]
                  + [jax.ShapeDtypeStruct(x.shape, x.dtype) for x in ride],
        scratch_shapes=_chip_copies_sems(nr) if nr else [],
        compiler_params=_cparams(("arbitrary", "arbitrary"), V7X_VMEM_LIMIT_BIG),
    )(qm, kv, kv, kr, o, do, lse, *ride)
    return res[:4], list(res[4:])


def _sigmoid(x):
    return 1.0 / (1.0 + jnp.exp(-x))


def _gate_fwd(projb, b_sb, b_mla, d, name):
    lp, tg = d.lp, d.tg
    tr = _pick(lp, (256, 128))
    o1, o2 = d.g_off // tg, (d.g_off + d.d) // tg

    def body(g1_ref, g2_ref, b1_ref, b2_ref, o_ref):
        o_ref[...] = (_sigmoid(g1_ref[...]) * b1_ref[...] + _sigmoid(g2_ref[...]) * b2_ref[...]).astype(BF16)

    blk = pl.BlockSpec((tr, tg), lambda i, j: (i, j))
    return pl.pallas_call(
        body, name=name, grid=(lp // tr, d.d // tg),
        in_specs=[pl.BlockSpec((tr, tg), lambda i, j: (i, o1 + j)), pl.BlockSpec((tr, tg), lambda i, j: (i, o2 + j)),
                  blk, blk],
        out_specs=blk, out_shape=jax.ShapeDtypeStruct((lp, d.d), BF16),
        compiler_params=_cparams(("parallel", "parallel")))(projb, projb, b_sb, b_mla)


def _gate_bwd(dm, projb, b_sb, b_mla, d, name):
    lp, tg = d.lp, d.tg
    tr = _pick(lp, (256, 128))
    o1, o2 = d.g_off // tg, (d.g_off + d.d) // tg

    def body(dm_ref, g1_ref, g2_ref, b1_ref, b2_ref, db1_ref, db2_ref, dg1_ref, dg2_ref):
        dmv = dm_ref[...]
        s1, s2 = _sigmoid(g1_ref[...]), _sigmoid(g2_ref[...])
        db1_ref[...] = (dmv * s1).astype(BF16)
        db2_ref[...] = (dmv * s2).astype(BF16)
        dg1_ref[...] = (dmv * b1_ref[...] * s1 * (1.0 - s1)).astype(BF16)
        dg2_ref[...] = (dmv * b2_ref[...] * s2 * (1.0 - s2)).astype(BF16)

    blk = pl.BlockSpec((tr, tg), lambda i, j: (i, j))
    out = jax.ShapeDtypeStruct((lp, d.d), BF16)
    return pl.pallas_call(
        body, name=name, grid=(lp // tr, d.d // tg),
        in_specs=[blk, pl.BlockSpec((tr, tg), lambda i, j: (i, o1 + j)), pl.BlockSpec((tr, tg), lambda i, j: (i, o2 + j)),
                  blk, blk],
        out_specs=[blk] * 4, out_shape=[out] * 4,
        compiler_params=_cparams(("parallel", "parallel")))(dm, projb, projb, b_sb, b_mla)


HALO = 8


def _conv_tiles(d):
    return _pick(d.lp, (640, 512, 256, 128)), _pick(d.f, (512, 256, 128))


def _convglu_fwd(up, cw, cb, d, name):
    lp, f = d.lp, d.f
    tr, tc = _conv_tiles(d)
    nf = f // tc
    hb = tr // HALO
    pad = d.pad

    def body(a_ref, g_ref, pa_ref, pg_ref, wa_ref, wg_ref, ba_ref, bg_ref, o_ref, ua_ref, ug_ref, xa, xg):
        i = pl.program_id(1)
        keep = (i > 0).astype(F32)
        xa[0:HALO, :] = pa_ref[...] * keep
        xg[0:HALO, :] = pg_ref[...] * keep
        xa[HALO:, :] = a_ref[...]
        xg[HALO:, :] = g_ref[...]

        def conv(x, w_ref, b_ref):
            return (b_ref[...] + x[pl.ds(HALO - 2, tr), :] * w_ref[0:1, :] + x[pl.ds(HALO - 1, tr), :] * w_ref[1:2, :]
                    + x[pl.ds(HALO, tr), :] * w_ref[2:3, :])

        ua = conv(xa, wa_ref, ba_ref)
        ug = conv(xg, wg_ref, bg_ref)
        ua_ref[...] = ua
        ug_ref[...] = ug
        row = i * tr + lax.broadcasted_iota(jnp.int32, (tr, tc), 0)
        o_ref[...] = jnp.where(row >= pad, ua * _sigmoid(ua) * ug, 0.0).astype(BF16)

    prev = lambda j, i: (jnp.maximum(i * hb - 1, 0), j)
    prevg = lambda j, i: (jnp.maximum(i * hb - 1, 0), nf + j)
    main = pl.BlockSpec((tr, tc), lambda j, i: (i, j))
    return pl.pallas_call(
        body, name=name, grid=(nf, lp // tr),
        in_specs=[main, pl.BlockSpec((tr, tc), lambda j, i: (i, nf + j)),
                  pl.BlockSpec((HALO, tc), prev), pl.BlockSpec((HALO, tc), prevg),
                  pl.BlockSpec((3, tc), lambda j, i: (0, j)), pl.BlockSpec((3, tc), lambda j, i: (0, nf + j)),
                  pl.BlockSpec((1, tc), lambda j, i: (0, j)), pl.BlockSpec((1, tc), lambda j, i: (0, nf + j))],
        out_specs=[main, main, main],
        out_shape=[jax.ShapeDtypeStruct((lp, f), BF16), jax.ShapeDtypeStruct((lp, f), F32),
                   jax.ShapeDtypeStruct((lp, f), F32)],
        scratch_shapes=[pltpu.VMEM((tr + HALO, tc), F32), pltpu.VMEM((tr + HALO, tc), F32)],
        compiler_params=_cparams(("parallel", "arbitrary")))(up, up, up, up, cw, cw, cb, cb)


def _convglu_bwd(up, ua, ug, dact, cw, d, name):
    lp, f = d.lp, d.f
    tr, tc = _conv_tiles(d)
    nf = f // tc
    hb = tr // HALO
    nrow = lp // tr
    pad = d.pad
    te = tr + HALO

    def body(a_ref, g_ref, ua_ref, ug_ref, nua_ref, nug_ref, da_ref, nd_ref, wa_ref, wg_ref,
             oa_ref, og_ref, sa_ref, sg_ref, ya, yg):
        i = pl.program_id(1)

        @pl.when(i == 0)
        def _():
            sa_ref[...] = jnp.zeros_like(sa_ref)
            sg_ref[...] = jnp.zeros_like(sg_ref)

        def dconv(ua, ug, dact):
            sg = _sigmoid(ua)
            return dact * ug * (sg * (1.0 + ua * (1.0 - sg))), dact * (ua * sg)

        ya_v, yg_v = dconv(ua_ref[...], ug_ref[...], da_ref[...])
        ya[0:tr, :] = ya_v
        yg[0:tr, :] = yg_v
        ya[tr:, :], yg[tr:, :] = dconv(nua_ref[...], nug_ref[...], nd_ref[...] * (i < nrow - 1).astype(F32))
        row = i * tr + lax.broadcasted_iota(jnp.int32, (tr, tc), 0)

        def back(y, y0, x_ref, w_ref, o_ref, s_ref):
            y1, y2 = y[pl.ds(1, tr), :], y[pl.ds(2, tr), :]
            dup = y0 * w_ref[2:3, :] + y1 * w_ref[1:2, :] + y2 * w_ref[0:1, :]
            o_ref[...] = jnp.where(row >= pad, dup, 0.0).astype(BF16)
            xv = x_ref[...]
            for tap, yk in enumerate((y2, y1, y0)):
                s_ref[tap:tap + 1, :] += jnp.sum(yk * xv, axis=0, keepdims=True)
            s_ref[3:4, :] += jnp.sum(y0, axis=0, keepdims=True)

        back(ya, ya_v, a_ref, wa_ref, oa_ref, sa_ref)
        back(yg, yg_v, g_ref, wg_ref, og_ref, sg_ref)

    last8 = lp // HALO - 1
    nxt = pl.BlockSpec((HALO, tc), lambda j, i: (jnp.minimum((i + 1) * hb, last8), j))
    main = pl.BlockSpec((tr, tc), lambda j, i: (i, j))
    sums = pl.BlockSpec((8, tc), lambda j, i: (0, j))
    return pl.pallas_call(
        body, name=name, grid=(nf, nrow),
        in_specs=[main, pl.BlockSpec((tr, tc), lambda j, i: (i, nf + j)), main, main, nxt, nxt, main, nxt,
                  pl.BlockSpec((3, tc), lambda j, i: (0, j)), pl.BlockSpec((3, tc), lambda j, i: (0, nf + j))],
        out_specs=[main, main, sums, sums],
        out_shape=[jax.ShapeDtypeStruct((lp, f), BF16), jax.ShapeDtypeStruct((lp, f), BF16),
                   jax.ShapeDtypeStruct((8, f), F32), jax.ShapeDtypeStruct((8, f), F32)],
        scratch_shapes=[pltpu.VMEM((te, tc), F32), pltpu.VMEM((te, tc), F32)],
        compiler_params=_cparams(("parallel", "arbitrary")))(up, up, ua, ug, ua, ug, dact, dact, cw, cw)


def _head(h, add, target, g, d, name):
    lp, dm, t = d.lp, d.d, d.block
    inv_d = 1.0 / dm

    def body(h_ref, a_ref, t_ref, g_ref, dh_ref, dhb_ref, loss_ref, dg_ref):
        i = pl.program_id(0)

        @pl.when(i == 0)
        def _():
            dh_ref[...] = jnp.zeros_like(dh_ref)
            dhb_ref[...] = jnp.zeros_like(dhb_ref)
            loss_ref[...] = jnp.zeros_like(loss_ref)
            dg_ref[...] = jnp.zeros_like(dg_ref)

        @pl.when(i > 0)
        def _():
            x, gv = h_ref[...] + a_ref[...], g_ref[...]
            r = lax.rsqrt(jnp.mean(x * x, axis=1, keepdims=True) + EPS)
            xh = x * r
            err = xh * gv - t_ref[...]
            loss_ref[...] += 0.5 * inv_d * jnp.sum(err * err)
            dy = err * inv_d
            gy = dy * gv
            c = jnp.mean(gy * x, axis=1, keepdims=True)
            dh = r * gy - x * (r * r * r) * c
            dh_ref[...] = dh
            dhb_ref[...] = dh.astype(BF16)
            dg_ref[...] += jnp.sum(dy * xh, axis=0, keepdims=True)

    blk = pl.BlockSpec((t, dm), lambda i: (i, 0))
    return pl.pallas_call(
        body, name=name, grid=(lp // t,),
        in_specs=[blk, blk, pl.BlockSpec((t, dm), lambda i: (jnp.maximum(i - 1, 0), 0)),
                  pl.BlockSpec((1, dm), lambda i: (0, 0))],
        out_specs=[blk, blk, pl.BlockSpec((8, LANES), lambda i: (0, 0)), pl.BlockSpec((1, dm), lambda i: (0, 0))],
        out_shape=[jax.ShapeDtypeStruct((lp, dm), F32), jax.ShapeDtypeStruct((lp, dm), BF16),
                   jax.ShapeDtypeStruct((8, LANES), F32), jax.ShapeDtypeStruct((1, dm), F32)],
        compiler_params=_cparams(("arbitrary",)))(h, add, target, g.reshape(1, dm))


def _exchange(xs, *, scatter, name):
    n = len(xs)
    nf = len(FLIPS)

    def body(*refs):
        ins, outs = refs[:n], refs[n:2 * n]
        send_sems, recv_sems, loc_sems = refs[2 * n:]
        x, y, c = lax.axis_index("x"), lax.axis_index("y"), lax.axis_index("c")
        me = 4 * x + 2 * y + c
        sends, recvs, locs = [], [], []
        for a in range(n):
            src_me = ins[a].at[me] if scatter else ins[a]
            loc = pltpu.make_async_copy(src_me, outs[a].at[me], loc_sems.at[a])
            loc.start()
            locs.append(loc)
            for k, (fx, fy, fc) in enumerate(FLIPS):
                px, py, pc = x ^ fx, y ^ fy, c ^ fc
                peer = 4 * px + 2 * py + pc
                src = ins[a].at[peer] if scatter else ins[a]
                cp = pltpu.make_async_remote_copy(
                    src_ref=src, dst_ref=outs[a].at[me], send_sem=send_sems.at[a * nf + k],
                    recv_sem=recv_sems.at[a * nf + k], device_id=(px, py, pc), device_id_type=pl.DeviceIdType.MESH)
                cp.start()
                sends.append(cp)
                recvs.append(pltpu.make_async_remote_copy(
                    src_ref=src, dst_ref=outs[a].at[peer], send_sem=send_sems.at[a * nf + k],
                    recv_sem=recv_sems.at[a * nf + k], device_id=(px, py, pc), device_id_type=pl.DeviceIdType.MESH))
        for cp in recvs:
            cp.wait_recv()
        for cp in sends:
            cp.wait_send()
        for loc in locs:
            loc.wait()

    hbm = pl.BlockSpec(memory_space=pltpu.HBM)
    out_shape = [jax.ShapeDtypeStruct(((N_DEV,) + tuple(x.shape[1:])) if scatter else ((N_DEV,) + tuple(x.shape)), x.dtype)
                 for x in xs]
    return pl.pallas_call(
        body, name=name, in_specs=[hbm] * n, out_specs=[hbm] * n, out_shape=out_shape,
        scratch_shapes=[pltpu.SemaphoreType.DMA((n * nf,)), pltpu.SemaphoreType.DMA((n * nf,)),
                        pltpu.SemaphoreType.DMA((n,))],
    )(*xs)


def _gather_two_level(xs, name):
    n = len(xs)

    def body(*refs):
        gather = _Gather(refs[:n], refs[n:2 * n], *refs[2 * n:])
        gather.start()
        gather.finish()

    hbm = pl.BlockSpec(memory_space=pltpu.HBM)
    return pl.pallas_call(
        body, name=name, in_specs=[hbm] * n, out_specs=[hbm] * n,
        out_shape=_gather_out_shapes(xs), scratch_shapes=_gather_sems(n),
    )(*xs)


def _gather_out_shapes(xs):
    return [jax.ShapeDtypeStruct((N_DEV,) + tuple(x.shape), x.dtype) for x in xs]


def _gather_sems(n):
    return [pltpu.SemaphoreType.DMA((7 * n,)), pltpu.SemaphoreType.DMA((7 * n,)), pltpu.SemaphoreType.DMA((n,))]


class _Gather:
    def __init__(self, ins, outs, send_sems, recv_sems, loc_sems):
        self.ins, self.outs, self.sems = ins, outs, (send_sems, recv_sems, loc_sems)
        self.x, self.y, self.c = lax.axis_index("x"), lax.axis_index("y"), lax.axis_index("c")
        self.me, self.sibling = (self.x, self.y, self.c), (self.x, self.y, 1 - self.c)
        self.chips = [(1 - self.x, self.y), (self.x, 1 - self.y), (1 - self.x, 1 - self.y)]

    def slot(self, a, dev):
        return self.outs[a].at[4 * dev[0] + 2 * dev[1] + dev[2]]

    def copy(self, a, k, block, to, src=None):
        return pltpu.make_async_remote_copy(
            src_ref=self.slot(a, block) if src is None else src, dst_ref=self.slot(a, block),
            send_sem=self.sems[0].at[7 * a + k], recv_sem=self.sems[1].at[7 * a + k],
            device_id=to, device_id_type=pl.DeviceIdType.MESH)

    def local(self, a):
        return pltpu.make_async_copy(self.ins[a], self.slot(a, self.me), self.sems[2].at[a])

    def first(self):
        out = []
        for a in range(len(self.ins)):
            out.append(self.copy(a, 0, self.me, self.sibling, src=self.ins[a]))
            out += [self.copy(a, 1 + j, self.me, (*chip, self.c), src=self.ins[a]) for j, chip in enumerate(self.chips)]
        return out

    def start(self):
        for a in range(len(self.ins)):
            self.local(a).start()
        for cp in self.first():
            cp.start()

    def finish(self):
        n, c = len(self.ins), self.c
        passed = []
        for j, chip in enumerate(self.chips):
            for a in range(n):
                self.copy(a, 1 + j, (*chip, c), self.me).wait_recv()
                fwd = self.copy(a, 4 + j, (*chip, c), self.sibling)
                fwd.start()
                passed.append(fwd)
        for a in range(n):
            self.copy(a, 0, self.sibling, self.me).wait_recv()
            for j, chip in enumerate(self.chips):
                self.copy(a, 4 + j, (*chip, 1 - c), self.me).wait_recv()
        for cp in self.first() + passed:
            cp.wait_send()
        for a in range(n):
            self.local(a).wait()


def _pair_exchange(xs, name):
    n = len(xs)

    def body(*refs):
        ins, outs = refs[:n], refs[n:2 * n]
        send_sems, recv_sems = refs[2 * n:]
        x, y, c = lax.axis_index("x"), lax.axis_index("y"), lax.axis_index("c")
        copies = []
        for a in range(n):
            for q in range(4):
                cp = pltpu.make_async_remote_copy(
                    src_ref=ins[a].at[2 * q + (1 - c)], dst_ref=outs[a].at[q], send_sem=send_sems.at[4 * a + q],
                    recv_sem=recv_sems.at[4 * a + q], device_id=(x, y, 1 - c), device_id_type=pl.DeviceIdType.MESH)
                cp.start()
                copies.append(cp)
        for cp in copies:
            cp.wait_recv()
        for cp in copies:
            cp.wait_send()

    hbm = pl.BlockSpec(memory_space=pltpu.HBM)
    return pl.pallas_call(
        body, name=name, in_specs=[hbm] * n, out_specs=[hbm] * n,
        out_shape=[jax.ShapeDtypeStruct((4,) + tuple(x.shape[1:]), x.dtype) for x in xs],
        scratch_shapes=[pltpu.SemaphoreType.DMA((4 * n,)), pltpu.SemaphoreType.DMA((4 * n,))],
    )(*xs)


def _chip_exchange(xs, name):
    n = len(xs)

    def body(*refs):
        copies = _chip_copies(refs[:n], refs[n:2 * n], *refs[2 * n:])
        _chip_copies_start(copies)
        _chip_copies_wait(copies)

    hbm = pl.BlockSpec(memory_space=pltpu.HBM)
    return pl.pallas_call(
        body, name=name, in_specs=[hbm] * n, out_specs=[hbm] * n,
        out_shape=[jax.ShapeDtypeStruct(x.shape, x.dtype) for x in xs],
        scratch_shapes=_chip_copies_sems(n),
    )(*xs)


def _chip_copies_sems(n):
    return [pltpu.SemaphoreType.DMA((3 * n,)), pltpu.SemaphoreType.DMA((3 * n,)), pltpu.SemaphoreType.DMA((n,))]


def _chip_copies(ins, outs, send_sems, recv_sems, loc_sems):
    x, y, c = lax.axis_index("x"), lax.axis_index("y"), lax.axis_index("c")
    mine = 2 * x + y
    locs, sends, recvs = [], [], []
    for a in range(len(ins)):
        locs.append(pltpu.make_async_copy(ins[a].at[mine], outs[a].at[mine], loc_sems.at[a]))
        for k, (fx, fy) in enumerate([(1, 0), (0, 1), (1, 1)]):
            px, py = x ^ fx, y ^ fy
            peer = 2 * px + py
            sems = dict(send_sem=send_sems.at[3 * a + k], recv_sem=recv_sems.at[3 * a + k],
                        device_id=(px, py, c), device_id_type=pl.DeviceIdType.MESH)
            sends.append(pltpu.make_async_remote_copy(src_ref=ins[a].at[peer], dst_ref=outs[a].at[mine], **sems))
            recvs.append(pltpu.make_async_remote_copy(src_ref=ins[a].at[peer], dst_ref=outs[a].at[peer], **sems))
    return locs, sends, recvs


def _chip_copies_start(copies):
    locs, sends, _ = copies
    for cp in locs + sends:
        cp.start()


def _chip_copies_wait(copies):
    locs, sends, recvs = copies
    for cp in recvs:
        cp.wait_recv()
    for cp in sends:
        cp.wait_send()
    for cp in locs:
        cp.wait()


def _pair_sum(a, b, name):
    shape = a.shape
    cols = shape[-1]
    rows = a.size // cols
    a, b = a.reshape(rows, cols), b.reshape(rows, cols)
    tr = next((t for t in (512, 256, 128, 64, 32, 16) if rows % t == 0 and t * cols <= 2 * ADAMW_TILE_ELEMS), rows)

    def body(a_ref, b_ref, o_ref):
        o_ref[...] = (a_ref[...].astype(F32) + b_ref[...].astype(F32)).astype(o_ref.dtype)

    blk = pl.BlockSpec((tr, cols), lambda i: (i, 0))
    return pl.pallas_call(body, name=name, grid=(rows // tr,), in_specs=[blk, blk], out_specs=blk,
                          out_shape=jax.ShapeDtypeStruct((rows, cols), a.dtype),
                          compiler_params=_cparams(("parallel",)))(a, b).reshape(shape)


def _adamw(parts, w, m, v, name):
    shape = w.shape
    cols = shape[-1]
    rows = w.size // cols
    nparts = parts.shape[0]
    parts, w, m, v = parts.reshape(nparts, rows, cols), w.reshape(rows, cols), m.reshape(rows, cols), v.reshape(rows, cols)
    tr = next((t for t in (256, 128, 64, 32, 16) if rows % t == 0 and t * cols <= ADAMW_TILE_ELEMS), rows)
    c1 = 1.0 - ADAM_B1 ** ADAM_STEP
    c2 = 1.0 - ADAM_B2 ** ADAM_STEP

    def body(p_ref, w_ref, m_ref, v_ref, g_ref, d_ref, mo_ref, vo_ref):
        g = p_ref[0].astype(F32)
        for q in range(1, nparts):
            g = g + p_ref[q].astype(F32)
        mn = ADAM_B1 * m_ref[...] + (1.0 - ADAM_B1) * g
        vn = ADAM_B2 * v_ref[...] + (1.0 - ADAM_B2) * (g * g)
        g_ref[...] = g
        mo_ref[...] = mn
        vo_ref[...] = vn
        d_ref[...] = -ADAM_LR * ((mn / c1) / (jnp.sqrt(vn / c2) + ADAM_EPS) + ADAM_WD * w_ref[...])

    blk = pl.BlockSpec((tr, cols), lambda i: (i, 0))
    out = jax.ShapeDtypeStruct((rows, cols), F32)
    res = pl.pallas_call(
        body, name=name, grid=(rows // tr,),
        in_specs=[pl.BlockSpec((nparts, tr, cols), lambda i: (0, i, 0)), blk, blk, blk],
        out_specs=[blk] * 4, out_shape=[out] * 4, compiler_params=_cparams(("parallel",)))(parts, w, m, v)
    return [r.reshape(shape) for r in res]


BIG = ["w_in", "w_uq", "w_ukv", "w_sb_out", "w_mla_out", "w_o", "w_up", "w_down"]
ROW_SHARDED = {"w_o", "w_down"}
SMALL = ["conv_w", "meta_tokens"]
SHARDED = BIG + SMALL
REPL = ["norm_mix", "q_norm", "kv_norm", "norm_ffn", "conv_b", "final_norm"]


def _pack_rows(flat, row_mult):
    unit = PACK_W * row_mult
    total = -(-flat.shape[0] // unit) * unit
    return jnp.pad(flat, (0, total - flat.shape[0])).reshape(-1, PACK_W)


def _padded_cols(c):
    return -(-c // LANES) * LANES


def _pad_block(a, name):
    c = a.shape[-1]
    if name in ROW_SHARDED or c % LANES == 0:
        return a
    return jnp.pad(a, [(0, 0)] * (a.ndim - 1) + [(0, _padded_cols(c) - c)])


def _full_from_slots(slots, name, c):
    if name in ROW_SHARDED:
        return jnp.transpose(slots, (1, 0, 2, 3)).reshape(slots.shape[1], -1, slots.shape[3])
    return jnp.concatenate([slots[q][..., :c] for q in range(N_DEV)], axis=-1)


def _slots_from_full(full, name):
    if name in ROW_SHARDED:
        l, rr, n = full.shape
        return jnp.transpose(full.reshape(l, N_DEV, rr // N_DEV, n), (1, 0, 2, 3))
    c = full.shape[-1] // N_DEV
    return jnp.stack([_pad_block(full[..., q * c:(q + 1) * c], name) for q in range(N_DEV)])


def _swap_halves(t):
    half = t.shape[-1] // 2
    return jnp.concatenate([t[..., half:], t[..., :half]], axis=-1)


def _in_offsets(d):
    widths = (d.sbw, d.sbw, d.sbw, d.q_lora, d.kv_lora, d.rope, d.d, d.d)
    offs, o = [], 0
    for w in widths:
        offs.append((o, o + w))
        o += w
    return offs


def _prime_weights(w_in, w_uq, w_ukv, d):
    offs = _in_offsets(d)
    cols = lambda k: w_in[:, offs[k][0]:offs[k][1]]
    kr = cols(5)
    zpad = jnp.zeros((d.d, d.g_off - d.kr_off - LANES), w_in.dtype)
    w_inb = jnp.concatenate([cols(3), cols(4), kr, _swap_halves(kr), zpad, cols(6), cols(7)], axis=1)
    w_ina = w_in[:, :d.wa]
    uq = w_uq.reshape(d.q_lora, d.mla_heads, d.nope + d.rope)
    rope = uq[..., d.nope:]
    w_uq = jnp.concatenate([uq[..., :d.nope], rope, _swap_halves(rope)], axis=-1).reshape(d.q_lora, d.qw)
    ukv = w_ukv.reshape(d.kv_lora, d.mla_heads, d.nope + d.vdim)
    w_ukv = jnp.concatenate([ukv[..., :d.nope].reshape(d.kv_lora, -1), ukv[..., d.nope:].reshape(d.kv_lora, -1)], axis=1)
    return dict(w_ina=w_ina, w_inb=w_inb, w_in=jnp.concatenate([w_ina, w_inb], axis=1), w_uq=w_uq, w_ukv=w_ukv)


def _unprime_grads(g, d):
    gi = g["w_in"]
    b = gi[:, d.wa:]
    kr = b[:, d.kr_off:d.kr_off + d.rope] + _swap_halves(b[:, d.kr_off + d.rope:d.kr_off + 2 * d.rope])
    w_in = jnp.concatenate([gi[:, :d.wa], b[:, :d.kr_off], kr, b[:, d.g_off:]], axis=1)
    uq = g["w_uq"].reshape(d.q_lora, d.mla_heads, 2 * LANES)
    rope = uq[..., d.nope:d.nope + d.rope] + _swap_halves(uq[..., d.nope + d.rope:])
    w_uq = jnp.concatenate([uq[..., :d.nope], rope], axis=-1).reshape(d.q_lora, -1)
    hw = d.mla_heads * d.nope
    ukv = g["w_ukv"]
    w_ukv = jnp.concatenate([ukv[:, :hw].reshape(d.kv_lora, d.mla_heads, d.nope),
                             ukv[:, hw:].reshape(d.kv_lora, d.mla_heads, d.vdim)], axis=-1).reshape(d.kv_lora, -1)
    return dict(g, w_in=w_in, w_uq=w_uq, w_ukv=w_ukv)


def _layer_fwd(h, add, w, norm_mix, q_norm, kv_norm, norm_ffn, cw, cb, ctab, stab, d, tag, ride=(), arrived=None):
    s = {}
    if add is None:
        s["h"] = h
        s["hn"], s["r1"] = _norm_fwd(h, norm_mix, width=d.d, cidx=0, name=f"norm_mix_{tag}")
    else:
        s["h"], s["hn"], s["r1"] = _norm_fwd(h, norm_mix, width=d.d, cidx=0, add=add, name=f"norm_mix_{tag}")
    s["pa"] = _mm(s["hn"], w["w_ina"], out_dtype=BF16, name=f"proj_a_{tag}")
    s["pb"] = _mm(s["hn"], w["w_inb"], name=f"proj_b_{tag}")
    s["o_sb"], s["carry"] = _sb_fwd(s["pa"], d, f"sb_fwd_{tag}")
    s["cqn"], s["rq"] = _norm_fwd(s["pb"], q_norm, width=d.q_lora, cidx=0, name=f"norm_q_{tag}")
    s["ckn"], s["rk"] = _norm_fwd(s["pb"], kv_norm, width=d.kv_lora, cidx=d.q_lora // d.kv_lora, name=f"norm_kv_{tag}")
    qraw = _mm(s["cqn"], w["w_uq"], name=f"uq_{tag}")
    s["kv"] = _mm(s["ckn"], w["w_ukv"], out_dtype=BF16, name=f"ukv_{tag}")
    s["qm"], s["kr"] = _mla_prep_fwd(qraw, s["pb"], ctab, stab, d, f"mla_prep_{tag}")
    (s["o_mla"], s["lse"]), gathered = _mla_fwd(s["qm"], s["kv"], s["kr"], d, f"mla_fwd_{tag}", ride=ride)
    if arrived is not None:
        arrived(gathered)
    s["b_sb"] = _mm(s["o_sb"], w["w_sb_out"], name=f"sb_out_{tag}")
    s["b_mla"] = _mm(s["o_mla"], w["w_mla_out"], name=f"mla_out_{tag}")
    s["merged"] = _gate_fwd(s["pb"], s["b_sb"], s["b_mla"], d, f"gate_{tag}")
    mix = _mm(s["merged"], w["w_o"], name=f"w_o_{tag}")
    s["h1"], s["hn2"], s["r2"] = _norm_fwd(s["h"], norm_ffn, width=d.d, cidx=0, add=mix, name=f"norm_ffn_{tag}")
    s["up"] = _mm(s["hn2"], w["w_up"], name=f"w_up_{tag}")
    s["act"], s["ua"], s["ug"] = _convglu_fwd(s["up"], cw, cb, d, f"convglu_{tag}")
    ffn = _mm(s["act"], w["w_down"], name=f"w_down_{tag}")
    return s, ffn


def _layer_bwd(dh2, dh2b, s, w, norm_mix, q_norm, kv_norm, norm_ffn, cw, cb, ctab, stab, d, tag, ride=(),
               ride_more=None):
    g = {}
    dact = _mm(dh2b, w["w_down"], tb=True, name=f"d_act_{tag}")
    g["w_down"] = _mm(s["act"], dh2b, ta=True, name=f"g_w_down_{tag}")
    dup_a, dup_g, sums_a, sums_g = _convglu_bwd(s["up"], s["ua"], s["ug"], dact, cw, d, f"convglu_bwd_{tag}")
    dup = jnp.concatenate([dup_a, dup_g], axis=1)
    g["conv_w"] = jnp.concatenate([sums_a[0:3], sums_g[0:3]], axis=1)
    g["conv_b"] = jnp.concatenate([sums_a[3], sums_g[3]], axis=0)
    g["w_up"] = _mm(s["hn2"], dup, ta=True, name=f"g_w_up_{tag}")
    dhn2 = _mm(dup, w["w_up"], tb=True, name=f"d_hn2_{tag}")
    dh1, dh1b, g["norm_ffn"] = _norm_bwd(dhn2, s["h1"], s["r2"], norm_ffn, width=d.d, cidx=0, dres=dh2,
                                         name=f"norm_ffn_bwd_{tag}")
    dmerged = _mm(dh1b, w["w_o"], tb=True, name=f"d_merged_{tag}")
    g["w_o"] = _mm(s["merged"], dh1b, ta=True, name=f"g_w_o_{tag}")
    db_sb, db_mla, dg_sb, dg_mla = _gate_bwd(dmerged, s["pb"], s["b_sb"], s["b_mla"], d, f"gate_bwd_{tag}")
    do_sb = _mm(db_sb, w["w_sb_out"], tb=True, name=f"d_o_sb_{tag}")
    g["w_sb_out"] = _mm(s["o_sb"], db_sb, ta=True, name=f"g_w_sb_out_{tag}")
    do_mla = _mm(db_mla, w["w_mla_out"], tb=True, name=f"d_o_mla_{tag}")
    g["w_mla_out"] = _mm(s["o_mla"], db_mla, ta=True, name=f"g_w_mla_out_{tag}")
    dq_sb, dk_sb, dv_sb = _sb_bwd(s["pa"], do_sb, s["carry"], d, f"sb_bwd_{tag}")
    if ride_more is not None:
        ride = list(ride) + list(ride_more(g))
    (dqm, dkn, dv, dkr), rode = _mla_bwd(s["qm"], s["kv"], s["kr"], s["o_mla"], do_mla, s["lse"], d, f"mla_bwd_{tag}",
                                         ride=ride)
    dqraw, dkr128 = _mla_prep_bwd(dqm, dkr, ctab, stab, d, f"mla_prep_bwd_{tag}")
    dkv = jnp.concatenate([dkn.astype(BF16), dv.astype(BF16)], axis=1)
    dcqn = _mm(dqraw, w["w_uq"], tb=True, name=f"d_cq_{tag}")
    g["w_uq"] = _mm(s["cqn"], dqraw, ta=True, name=f"g_w_uq_{tag}")
    dckn = _mm(dkv, w["w_ukv"], tb=True, name=f"d_ckv_{tag}")
    g["w_ukv"] = _mm(s["ckn"], dkv, ta=True, name=f"g_w_ukv_{tag}")
    dcq, g["q_norm"] = _norm_bwd(dcqn, s["pb"], s["rq"], q_norm, width=d.q_lora, cidx=0, out_dtype=BF16, name=f"norm_q_bwd_{tag}")
    dckv, g["kv_norm"] = _norm_bwd(dckn, s["pb"], s["rk"], kv_norm, width=d.kv_lora, cidx=d.q_lora // d.kv_lora,
                                   out_dtype=BF16, name=f"norm_kv_bwd_{tag}")
    zpad = jnp.zeros((d.lp, d.g_off - d.kr_off - LANES), BF16)
    dproj = jnp.concatenate([dq_sb, dk_sb.astype(BF16), dv_sb.astype(BF16), dcq, dckv, dkr128, zpad, dg_sb, dg_mla], axis=1)
    g["w_in"] = _mm(s["hn"], dproj, ta=True, name=f"g_w_in_{tag}")
    dhn = _mm(dproj, w["w_in"], tb=True, name=f"d_hn_{tag}")
    dh, dhb, g["norm_mix"] = _norm_bwd(dhn, s["h"], s["r1"], norm_mix, width=d.d, cidx=0, dres=dh1,
                                       name=f"norm_mix_bwd_{tag}")
    return dh, dhb, g, rode


def _step(d, x, p, m, v, loss_target):
    x = x.reshape(d.seq, d.d)
    target = loss_target.reshape(d.seq, d.d)

    def block(n, layers=None):
        a = p[n] if layers is None else p[n][layers]
        return _pad_block(a, n).astype(BF16 if n in BIG else F32)

    def whole(n, slots):
        return _full_from_slots(slots, n, p[n].shape[-1])

    first_names = ["w_in", "w_uq", "w_ukv"]
    early = _gather_two_level([block(n, slice(0, 1)) for n in first_names] + [block(n) for n in SMALL], "gather_weights")
    full = {n: whole(n, g_) for n, g_ in zip(first_names + SMALL, early)}
    ws = [_prime_weights(*[full[n][0] for n in first_names], d)] + [dict() for _ in range(1, d.depth)]
    rest_names = [n for n in BIG if n not in first_names]
    late = [block(n, slice(1, d.depth)) for n in first_names] + [block(n) for n in rest_names]

    def arrived(gathered):
        later = {n: whole(n, g_) for n, g_ in zip(first_names, gathered)}
        rest = {n: whole(n, g_) for n, g_ in zip(rest_names, gathered[len(first_names):])}
        for l in range(d.depth):
            if l > 0:
                ws[l].update(_prime_weights(*[later[n][l - 1] for n in first_names], d))
            ws[l].update({n: rest[n][l] for n in rest_names})

    pos = jnp.arange(d.lp, dtype=F32) - d.pad
    half = d.rope // 2
    freqs = ROPE_THETA ** (-jnp.arange(half, dtype=F32) / half)
    ang = pos[:, None] * freqs[None, :]
    cos, sin = jnp.cos(ang), jnp.sin(ang)
    zero = jnp.zeros((d.lp, LANES - d.rope), F32)
    ctab = jnp.concatenate([cos, cos, zero], axis=1)
    stab = jnp.concatenate([-sin, sin, zero], axis=1)

    h = jnp.concatenate([jnp.zeros((d.pad, d.d), F32), full["meta_tokens"], x], axis=0)
    saved, add = [], None
    for l in range(d.depth):
        s, add = _layer_fwd(h, add, ws[l], p["norm_mix"][l], p["q_norm"][l], p["kv_norm"][l], p["norm_ffn"][l],
                            full["conv_w"][l], p["conv_b"][l].reshape(1, -1), ctab, stab, d, f"l{l}",
                            ride=late if l == 0 else (), arrived=arrived if l == 0 else None)
        saved.append(s)
        h = s["h1"]
    dh, dhb, loss_part, g_final = _head(h, add, target, p["final_norm"], d, "head")

    my_c = lax.axis_index("c")

    def pair_level(names, gfull, tag):
        gsend = [_slots_from_full(gfull[n].astype(BF16) if n in BIG else gfull[n], n) for n in names]
        theirs = _pair_exchange(gsend, f"scatter_grads_pair_{tag}")
        mine = [lax.dynamic_index_in_dim(g_.reshape((4, 2) + g_.shape[1:]), my_c, axis=1, keepdims=False) for g_ in gsend]
        return [_pair_sum(a_, b_, f"pair_sum_{n}_{tag}") for n, a_, b_ in zip(names, mine, theirs)]

    grads = [None] * d.depth
    recv_big = [None] * d.depth
    ride, ffn_names, rode_ffn = [], ["w_up", "w_down", "w_o", "w_sb_out", "w_mla_out"], []
    for l in reversed(range(d.depth)):
        more = None
        if l == 0 and ride:
            more = lambda g_: pair_level(ffn_names, {n: g_[n][None] for n in ffn_names}, "l0_ffn")
        dh, dhb, g, rode = _layer_bwd(dh, dhb, saved[l], ws[l], p["norm_mix"][l], p["q_norm"][l], p["kv_norm"][l],
                                      p["norm_ffn"][l], full["conv_w"][l], p["conv_b"][l].reshape(1, -1), ctab, stab,
                                      d, f"l{l}", ride=ride, ride_more=more)
        if ride:
            recv_big[l + 1] = rode[:len(BIG)]
        if more is not None:
            rode_ffn = rode[len(BIG):]
        grads[l] = _unprime_grads(g, d)
        if l > 0:
            ride = pair_level(BIG, {n: grads[l][n][None] for n in BIG}, f"l{l}")
    grad_x = dh[d.first_tok:].reshape(1, d.seq, d.d)

    rest0 = [n for n in BIG if not (rode_ffn and n in ffn_names)]
    gfull = {n: grads[0][n][None] for n in rest0}
    gfull["conv_w"] = jnp.stack([grads[l]["conv_w"] for l in range(d.depth)])
    gfull["meta_tokens"] = dh[d.pad:d.first_tok]
    last = list(_chip_exchange(pair_level(rest0 + SMALL, gfull, "l0"), "scatter_grads_chips"))
    got0 = dict(zip(rest0, last[:len(rest0)]))
    got0.update(zip(ffn_names, rode_ffn))
    recv_big[0] = [got0[n] for n in BIG]
    grecv = [jnp.concatenate([recv_big[l][k] for l in range(d.depth)], axis=1) for k in range(len(BIG))] + last[len(rest0):]
    outs_sh = {n: _adamw(r_, _pad_block(p[n], n), _pad_block(m[n], n), _pad_block(v[n], n), f"adamw_{n}")
               for n, r_ in zip(SHARDED, grecv)}

    grep = {n: jnp.stack([grads[l][n].reshape(-1) for l in range(d.depth)]) for n in REPL if n != "final_norm"}
    grep["final_norm"] = g_final.reshape(-1)
    rflat = jnp.concatenate([grep[n].reshape(-1) for n in REPL] + [loss_part[0, 0:1]])
    (rparts,) = _exchange([_pack_rows(rflat, 8)], scatter=False, name="gather_small_grads")
    packr = lambda t: _pack_rows(jnp.concatenate([t[n].reshape(-1) for n in REPL] + [jnp.zeros((1,), F32)]), 8)
    outs_rp = _adamw(rparts, packr(p), packr(m), packr(v), "adamw_replicated")

    def unpack(flat, names, extra=0):
        res, off = {}, 0
        flat = flat.reshape(-1)
        for n in names:
            res[n] = flat[off:off + p[n].size].reshape(p[n].shape)
            off += p[n].size
        return res, flat[off:off + extra]

    results = []
    loss = None
    for k in range(4):
        sh = {n: outs_sh[n][k][..., :p[n].shape[-1]] for n in SHARDED}
        rp, tail = unpack(outs_rp[k], REPL, 1)
        if k == 0:
            loss = tail[0]
        results.append({**sh, **rp})
    return loss, grad_x, results


WEIGHTS = ["meta_tokens", "norm_mix", "w_in", "q_norm", "w_uq", "kv_norm", "w_ukv", "w_sb_out", "w_mla_out", "w_o",
           "norm_ffn", "w_up", "conv_w", "conv_b", "w_down", "final_norm"]


def _run(d, x, weights, loss_target, moments_m, moments_v):
    p = dict(zip(WEIGHTS, weights))
    m = dict(zip(WEIGHTS, moments_m))
    v = dict(zip(WEIGHTS, moments_v))
    loss, grad_x, res = _step(d, x, p, m, v, loss_target)
    out = [loss, grad_x]
    for k in range(4):
        out += [res[k][n] for n in WEIGHTS]
    return tuple(out)


def kernel(x, meta_tokens, norm_mix, w_in, q_norm, w_uq, kv_norm, w_ukv, w_sb_out, w_mla_out, w_o, norm_ffn, w_up, conv_w, conv_b, w_down, final_norm, loss_target, m_meta_tokens, m_norm_mix, m_w_in, m_q_norm, m_w_uq, m_kv_norm, m_w_ukv, m_w_sb_out, m_w_mla_out, m_w_o, m_norm_ffn, m_w_up, m_conv_w, m_conv_b, m_w_down, m_final_norm, v_meta_tokens, v_norm_mix, v_w_in, v_q_norm, v_w_uq, v_kv_norm, v_w_ukv, v_w_sb_out, v_w_mla_out, v_w_o, v_norm_ffn, v_w_up, v_conv_w, v_conv_b, v_w_down, v_final_norm):
    weights = [meta_tokens, norm_mix, w_in, q_norm, w_uq, kv_norm, w_ukv, w_sb_out, w_mla_out, w_o, norm_ffn, w_up,
               conv_w, conv_b, w_down, final_norm]
    ms = [m_meta_tokens, m_norm_mix, m_w_in, m_q_norm, m_w_uq, m_kv_norm, m_w_ukv, m_w_sb_out, m_w_mla_out, m_w_o,
          m_norm_ffn, m_w_up, m_conv_w, m_conv_b, m_w_down, m_final_norm]
    vs = [v_meta_tokens, v_norm_mix, v_w_in, v_q_norm, v_w_uq, v_kv_norm, v_w_ukv, v_w_sb_out, v_w_mla_out, v_w_o,
          v_norm_ffn, v_w_up, v_conv_w, v_conv_b, v_w_down, v_final_norm]
    return _run(PROD, x, weights, loss_target, ms, vs)
```

```python
import jax
import jax.numpy as jnp
from jax import lax
from jax.experimental import pallas as pl
from jax.experimental.pallas import tpu as pltpu

F32 = jnp.float32
BF16 = jnp.bfloat16

EPS = 1e-6
ROPE_THETA = 10000.0
ADAM_LR = 0.001
ADAM_B1 = 0.9
ADAM_B2 = 0.999
ADAM_EPS = 1e-08
ADAM_WD = 0.01
ADAM_STEP = 10
NEG = -1e30
DEAD = -110.0
LANES = 128
PACK_W = 1024
ADAMW_TILE_ELEMS = 256 * 1024
V7X_VMEM_LIMIT = 48 * 1024 * 1024
V7X_VMEM_LIMIT_BIG = 58 * 1024 * 1024
MESH_AXES = ("x", "y", "c")
N_DEV = 8
FLIPS = [(0, 0, 1), (0, 1, 0), (0, 1, 1), (1, 0, 0), (1, 0, 1), (1, 1, 0), (1, 1, 1)]


class _Dims:
    def __init__(self, d_model=2048, seq=8192, depth=2, n_meta=16, block=128, sb_heads=8, hd=128,
                 mla_heads=8, q_lora=512, kv_lora=256, nope=128, rope=64, vdim=128, d_ff=5632, tq=None):
        self.d, self.seq, self.depth, self.n_meta, self.block = d_model, seq, depth, n_meta, block
        self.sb_heads, self.hd, self.mla_heads = sb_heads, hd, mla_heads
        self.q_lora, self.kv_lora, self.nope, self.rope, self.vdim, self.f = q_lora, kv_lora, nope, rope, vdim, d_ff
        assert hd == LANES and nope == LANES and vdim == LANES and 2 * rope == LANES
        self.pad = block - n_meta
        self.lp = self.pad + n_meta + seq
        self.first_tok = self.pad + n_meta
        assert self.first_tok == block and self.lp % block == 0 and self.lp // block < LANES
        self.tq = tq or next(t for t in (640, 512, 256, 128) if self.lp % t == 0)
        assert self.tq % block == 0 and self.lp % self.tq == 0
        self.sbw = sb_heads * hd
        self.mlaw = mla_heads * vdim
        self.wa = 3 * self.sbw
        self.d_in = 3 * self.sbw + q_lora + kv_lora + rope + 2 * d_model
        self.tg = min(1024, d_model)
        self.kr_off = q_lora + kv_lora
        raw = self.kr_off + LANES
        self.g_off = -(-raw // self.tg) * self.tg
        self.wb = self.g_off + 2 * d_model
        self.qw = mla_heads * 2 * LANES


PROD = _Dims()


def _pick(n, prefs):
    for p in prefs:
        if n % p == 0:
            return p
    return n


def _cparams(sem, limit=V7X_VMEM_LIMIT):
    return pltpu.CompilerParams(dimension_semantics=sem, vmem_limit_bytes=limit)


def _mm(a, b, *, ta=False, tb=False, out_dtype=F32, name):
    if ta:
        kdim, m = a.shape
    else:
        m, kdim = a.shape
    if tb:
        n, k2 = b.shape
    else:
        k2, n = b.shape
    assert kdim == k2, (a.shape, b.shape, ta, tb)
    tm = _pick(m, (640, 512, 256, 128))
    tn = _pick(n, (1024, 512, 384, 256, 128))
    tk = _pick(kdim, (2816, 2048, 1664, 1408, 1024, 640, 512, 256, 128))
    nk = kdim // tk
    dn = (((0 if ta else 1,), (1 if tb else 0,)), ((), ()))

    def dot(a_ref, b_ref):
        return lax.dot_general(a_ref[...].astype(BF16), b_ref[...].astype(BF16), dn, preferred_element_type=F32)

    def body_one(a_ref, b_ref, o_ref):
        o_ref[...] = dot(a_ref, b_ref).astype(out_dtype)

    def body_acc(a_ref, b_ref, o_ref, acc_ref):
        k = pl.program_id(2)

        @pl.when(k == 0)
        def _():
            acc_ref[...] = dot(a_ref, b_ref)

        @pl.when((k > 0) & (k < nk - 1))
        def _():
            acc_ref[...] += dot(a_ref, b_ref)

        @pl.when(k == nk - 1)
        def _():
            o_ref[...] = (acc_ref[...] + dot(a_ref, b_ref)).astype(out_dtype)

    a_spec = pl.BlockSpec((tk, tm), lambda i, j, k: (k, i)) if ta else pl.BlockSpec((tm, tk), lambda i, j, k: (i, k))
    b_spec = pl.BlockSpec((tn, tk), lambda i, j, k: (j, k)) if tb else pl.BlockSpec((tk, tn), lambda i, j, k: (k, j))
    return pl.pallas_call(
        body_one if nk == 1 else body_acc, name=name, grid=(m // tm, n // tn, nk), in_specs=[a_spec, b_spec],
        out_specs=pl.BlockSpec((tm, tn), lambda i, j, k: (i, j)),
        out_shape=jax.ShapeDtypeStruct((m, n), out_dtype),
        scratch_shapes=[] if nk == 1 else [pltpu.VMEM((tm, tn), F32)],
        compiler_params=_cparams(("parallel", "parallel", "arbitrary")),
    )(a, b)


def _norm_fwd(x, g, *, width, cidx, add=None, name):
    rows = x.shape[0]
    tr = _pick(rows, (256, 128))
    has_add = add is not None

    def body(*refs):
        if has_add:
            x_ref, a_ref, g_ref, xn_ref, y_ref, r_ref = refs
            xv = x_ref[...] + a_ref[...]
            xn_ref[...] = xv
        else:
            x_ref, g_ref, y_ref, r_ref = refs
            xv = x_ref[...]
        r = lax.rsqrt(jnp.mean(xv * xv, axis=1, keepdims=True) + EPS)
        y_ref[...] = (xv * r * g_ref[...]).astype(BF16)
        r_ref[...] = r

    blk = pl.BlockSpec((tr, width), lambda i: (i, 0))
    in_specs = [pl.BlockSpec((tr, width), lambda i: (i, cidx))]
    args = [x]
    if has_add:
        in_specs.append(blk)
        args.append(add)
    in_specs.append(pl.BlockSpec((1, width), lambda i: (0, 0)))
    args.append(g.reshape(1, width))
    out_specs = [blk, pl.BlockSpec((tr, 1), lambda i: (i, 0))]
    out_shape = [jax.ShapeDtypeStruct((rows, width), BF16), jax.ShapeDtypeStruct((rows, 1), F32)]
    if has_add:
        out_specs.insert(0, blk)
        out_shape.insert(0, jax.ShapeDtypeStruct((rows, width), F32))
    return pl.pallas_call(body, name=name, grid=(rows // tr,), in_specs=in_specs, out_specs=out_specs,
                          out_shape=out_shape, compiler_params=_cparams(("parallel",)))(*args)


def _norm_bwd(dy, x, r, g, *, width, cidx, dres=None, out_dtype=F32, name):
    rows = x.shape[0]
    tr = _pick(rows, (256, 128))
    has_res = dres is not None

    def body(*refs):
        if has_res:
            dy_ref, x_ref, r_ref, g_ref, dr_ref, dx_ref, dxb_ref, dg_ref = refs
        else:
            dy_ref, x_ref, r_ref, g_ref, dx_ref, dg_ref = refs
        i = pl.program_id(0)

        @pl.when(i == 0)
        def _():
            dg_ref[...] = jnp.zeros_like(dg_ref)

        dyv, xv, rv = dy_ref[...], x_ref[...], r_ref[...]
        gy = dyv * g_ref[...]
        c = jnp.mean(gy * xv, axis=1, keepdims=True)
        dx = rv * gy - xv * (rv * rv * rv) * c
        if has_res:
            dx = dx + dr_ref[...]
            dxb_ref[...] = dx.astype(BF16)
        dx_ref[...] = dx.astype(out_dtype)
        dg_ref[...] += jnp.sum(dyv * xv * rv, axis=0, keepdims=True)

    blk = pl.BlockSpec((tr, width), lambda i: (i, 0))
    in_specs = [blk, pl.BlockSpec((tr, width), lambda i: (i, cidx)), pl.BlockSpec((tr, 1), lambda i: (i, 0)),
                pl.BlockSpec((1, width), lambda i: (0, 0))]
    args = [dy, x, r, g.reshape(1, width)]
    out_specs = [blk, pl.BlockSpec((1, width), lambda i: (0, 0))]
    out_shape = [jax.ShapeDtypeStruct((rows, width), out_dtype), jax.ShapeDtypeStruct((1, width), F32)]
    if has_res:
        in_specs.append(blk)
        args.append(dres)
        out_specs.insert(1, blk)
        out_shape.insert(1, jax.ShapeDtypeStruct((rows, width), BF16))
    return pl.pallas_call(
        body, name=name, grid=(rows // tr,), in_specs=in_specs, out_specs=out_specs, out_shape=out_shape,
        compiler_params=_cparams(("arbitrary",)))(*args)


def _split3(x):
    h1 = x.astype(BF16)
    r1 = x - h1.astype(F32)
    h2 = r1.astype(BF16)
    h3 = (r1 - h2.astype(F32)).astype(BF16)
    return h1, h2, h3


def _cum(x, tri):
    h1, h2, h3 = _split3(x)
    dot = lambda h: jnp.dot(h, tri, preferred_element_type=F32)
    return dot(h1) + dot(h2) + dot(h3)


def _dot_nt(a, b):
    return lax.dot_general(a, b, (((1,), (1,)), ((), ())), preferred_element_type=F32)


def _dot_tn(a, b):
    return lax.dot_general(a, b, (((0,), (0,)), ((), ())), preferred_element_type=F32)


def _sb_geometry(tq, t):
    assert t & (t - 1) == 0
    ri = lax.broadcasted_iota(jnp.int32, (tq, t), 0)
    return jnp.bitwise_and(ri, t - 1), jnp.right_shift(ri, t.bit_length() - 1), lax.broadcasted_iota(jnp.int32, (tq, t), 1)


def _sb_key_blocks(ref, base, r, t, kind):
    if kind == "low":
        return [ref[pl.ds(pl.multiple_of(jnp.maximum(base + g, 0) * t, t), t), :] for g in range(r)]
    slab = ref[pl.ds(pl.multiple_of(base * t, t), r * t), :]
    return [slab[g * t:(g + 1) * t, :] for g in range(r)]


def _sb_mask(geo, base, s, t, pad, kind):
    rowl, grp, col = geo
    if kind == "plain":
        return None
    if kind == "first":
        return col < rowl
    blk = base + grp
    causal = col < rowl + jnp.where(s > 0, t, 0)
    return (blk >= 0) & (blk * t + col >= pad) & causal


def _sb_fwd(qkv, d, name):
    nh, hd, lp, tq, t = d.sb_heads, d.hd, d.lp, d.tq, d.block
    r = tq // t
    scale = hd ** -0.5
    pad = d.pad

    def body(q_ref, k_ref, v_ref, o_ref, c_ref):
        i = pl.program_id(1)
        qs = [q_ref[g * t:(g + 1) * t, :] for g in range(r)]
        geo = _sb_geometry(tq, t)
        tri = (lax.broadcasted_iota(jnp.int32, (t, t), 0)
               > lax.broadcasted_iota(jnp.int32, (t, t), 1)).astype(BF16)
        lane = lax.broadcasted_iota(jnp.int32, (tq, LANES), 1)

        c_ref[...] = jnp.zeros_like(c_ref)

        def make_step(kind):
            def step(s, carry):
                acc, run = carry
                ks = _sb_key_blocks(k_ref, i * r - s, r, t, kind)
                vs = _sb_key_blocks(v_ref, i * r - s, r, t, kind)
                z = jnp.concatenate([_dot_nt(qs[g], ks[g]) for g in range(r)], axis=0) * scale
                e = jnp.exp(-jnp.abs(z))
                sp = jnp.maximum(z, 0.0) + jnp.log(1.0 + e)
                mask = _sb_mask(geo, i * r - s, s, t, pad, kind)
                spm = sp if mask is None else jnp.where(mask, sp, 0.0)
                w = jnp.exp(z - sp - _cum(spm, tri) + run)
                if mask is not None:
                    w = jnp.where(mask, w, 0.0)
                wb = w.astype(BF16)
                acc = acc + jnp.concatenate(
                    [jnp.dot(wb[g * t:(g + 1) * t, :], vs[g], preferred_element_type=F32) for g in range(r)], axis=0)
                c_ref[...] = jnp.where(lane == s, run, c_ref[...])
                run = run - jnp.sum(spm, axis=1, keepdims=True)
                return acc, run
            return step

        first, plain, low = make_step("first"), make_step("plain"), make_step("low")

        def alive(run):
            return (jnp.max(run) >= DEAD).astype(jnp.int32)

        def run_while(step, s, end, live, acc, run):
            def wbody(st):
                s, _, acc, run = st
                acc, run = step(s, (acc, run))
                return s + 1, alive(run), acc, run
            return lax.while_loop(lambda st: (st[0] < end) & (st[1] > 0), wbody, (s, live, acc, run))

        init = (jnp.zeros((tq, hd), F32), jnp.zeros((tq, 1), F32))
        carry = lax.fori_loop(0, jnp.minimum(i, 1), lambda _, cr: first(0, cr), init)
        acc, run = lax.fori_loop(0, 1 - jnp.minimum(i, 1), lambda _, cr: low(0, cr), carry)
        s, live, acc, run = run_while(plain, 1, i * r, alive(run), acc, run)
        end_low = jnp.where(s >= jnp.maximum(i * r, 1), (i + 1) * r, s)
        s, live, acc, run = run_while(low, s, end_low, live, acc, run)
        o_ref[...] = acc
        c_ref[...] = jnp.where(lane == LANES - 1, s.astype(F32), c_ref[...])

    return pl.pallas_call(
        body, name=name, grid=(nh, lp // tq),
        in_specs=[pl.BlockSpec((tq, hd), lambda h, i: (i, h)),
                  pl.BlockSpec((lp, hd), lambda h, i: (0, nh + h)),
                  pl.BlockSpec((lp, hd), lambda h, i: (0, 2 * nh + h))],
        out_specs=[pl.BlockSpec((tq, hd), lambda h, i: (i, h)),
                   pl.BlockSpec((None, tq, LANES), lambda h, i: (h, i, 0))],
        out_shape=[jax.ShapeDtypeStruct((lp, nh * hd), F32), jax.ShapeDtypeStruct((nh, lp, LANES), F32)],
        compiler_params=_cparams(("parallel", "arbitrary")),
    )(qkv, qkv, qkv)


def _sb_bwd(qkv, do, carry, d, name):
    nh, hd, lp, tq, t = d.sb_heads, d.hd, d.lp, d.tq, d.block
    r = tq // t
    scale = hd ** -0.5
    pad = d.pad

    def body(q_ref, k_ref, v_ref, do_ref, c_ref, dq_ref, dk_ref, dv_ref):
        i = pl.program_id(1)

        @pl.when(i == 0)
        def _():
            dk_ref[...] = jnp.zeros_like(dk_ref)
            dv_ref[...] = jnp.zeros_like(dv_ref)

        rows = lambda x, g: x[g * t:(g + 1) * t, :]
        qs = [q_ref[g * t:(g + 1) * t, :] for g in range(r)]
        dobs = [do_ref[g * t:(g + 1) * t, :].astype(BF16) for g in range(r)]
        geo = _sb_geometry(tq, t)
        ri = lax.broadcasted_iota(jnp.int32, (t, t), 0)
        ci = lax.broadcasted_iota(jnp.int32, (t, t), 1)
        tri_suf = (ri > ci).astype(BF16)
        tri_pre = (ri < ci).astype(BF16)
        lane = lax.broadcasted_iota(jnp.int32, (tq, LANES), 1)

        def make_step(kind):
            def step(s, carry):
                dq, pc = carry
                base = i * r - s
                ks = _sb_key_blocks(k_ref, base, r, t, kind)
                vs = _sb_key_blocks(v_ref, base, r, t, kind)
                z = jnp.concatenate([_dot_nt(qs[g], ks[g]) for g in range(r)], axis=0) * scale
                e = jnp.exp(-jnp.abs(z))
                sp = jnp.maximum(z, 0.0) + jnp.log(1.0 + e)
                mask = _sb_mask(geo, base, s, t, pad, kind)
                spm = sp if mask is None else jnp.where(mask, sp, 0.0)
                run = jnp.sum(jnp.where(lane == s, c_ref[...], 0.0), axis=1, keepdims=True)
                w = jnp.exp(z - sp - _cum(spm, tri_suf) + run)
                if mask is not None:
                    w = jnp.where(mask, w, 0.0)
                gw = w * jnp.concatenate([_dot_nt(dobs[g], vs[g]) for g in range(r)], axis=0)
                p = _cum(gw, tri_pre) + pc
                inv = 1.0 / (1.0 + e)
                sig = jnp.where(z >= 0.0, inv, e * inv)
                dz = (gw * (1.0 - sig) - sig * p) * scale
                if mask is not None:
                    dz = jnp.where(mask, dz, 0.0)
                dzb, wb = dz.astype(BF16), w.astype(BF16)
                dq = dq + jnp.concatenate(
                    [jnp.dot(rows(dzb, g), ks[g], preferred_element_type=F32) for g in range(r)], axis=0)
                dks = [_dot_tn(rows(dzb, g), qs[g]) for g in range(r)]
                dvs = [_dot_tn(rows(wb, g), dobs[g]) for g in range(r)]
                if kind == "low":
                    for g in range(r):
                        at = pl.ds(pl.multiple_of(jnp.maximum(base + g, 0) * t, t), t)
                        dk_ref[at, :] += dks[g]
                        dv_ref[at, :] += dvs[g]
                else:
                    at = pl.ds(pl.multiple_of(base * t, t), tq)
                    dk_ref[at, :] += jnp.concatenate(dks, axis=0)
                    dv_ref[at, :] += jnp.concatenate(dvs, axis=0)
                pc = pc + jnp.sum(gw, axis=1, keepdims=True)
                return dq, pc
            return step

        first, plain, low = make_step("first"), make_step("plain"), make_step("low")
        nsteps = jnp.max(jnp.where(lane == LANES - 1, c_ref[...], 0.0)).astype(jnp.int32)
        low_from = jnp.maximum(i * r, 1)
        plain_end = jnp.minimum(nsteps, low_from)
        carry = (jnp.zeros((tq, hd), F32), jnp.zeros((tq, 1), F32))
        carry = lax.fori_loop(0, jnp.maximum(nsteps - low_from, 0), lambda jj, cr: low(nsteps - 1 - jj, cr), carry)
        carry = lax.fori_loop(0, plain_end - 1, lambda jj, cr: plain(plain_end - 1 - jj, cr), carry)
        carry = lax.fori_loop(0, jnp.minimum(i, 1), lambda _, cr: first(0, cr), carry)
        dq, _ = lax.fori_loop(0, 1 - jnp.minimum(i, 1), lambda _, cr: low(0, cr), carry)
        dq_ref[...] = dq.astype(BF16)

    w3 = nh * hd
    return pl.pallas_call(
        body, name=name, grid=(nh, lp // tq),
        in_specs=[pl.BlockSpec((tq, hd), lambda h, i: (i, h)),
                  pl.BlockSpec((lp, hd), lambda h, i: (0, nh + h)),
                  pl.BlockSpec((lp, hd), lambda h, i: (0, 2 * nh + h)),
                  pl.BlockSpec((tq, hd), lambda h, i: (i, h)),
                  pl.BlockSpec((None, tq, LANES), lambda h, i: (h, i, 0))],
        out_specs=[pl.BlockSpec((tq, hd), lambda h, i: (i, h)),
                   pl.BlockSpec((lp, hd), lambda h, i: (0, h)),
                   pl.BlockSpec((lp, hd), lambda h, i: (0, h))],
        out_shape=[jax.ShapeDtypeStruct((lp, w3), BF16), jax.ShapeDtypeStruct((lp, w3), F32),
                   jax.ShapeDtypeStruct((lp, w3), F32)],
        compiler_params=_cparams(("arbitrary", "arbitrary")),
    )(qkv, qkv, qkv, do, carry)


def _mla_prep_fwd(qraw, projb, ctab, stab, d, name):
    lp, nh = d.lp, d.mla_heads
    tr = _pick(lp, (256, 128))
    kidx = d.kr_off // LANES

    def rope(u, c, s):
        return u * c + pltpu.roll(u, LANES // 2, 1) * s

    def body(q_ref, k_ref, c_ref, s_ref, qm_ref, kr_ref):
        c, s = c_ref[...], s_ref[...]
        for h in range(nh):
            base = 2 * LANES * h
            qm_ref[:, base:base + LANES] = q_ref[:, base:base + LANES].astype(BF16)
            qm_ref[:, base + LANES:base + 2 * LANES] = rope(q_ref[:, base + LANES:base + 2 * LANES], c, s).astype(BF16)
        kr_ref[...] = rope(k_ref[...], c, s).astype(BF16)

    tab = pl.BlockSpec((tr, LANES), lambda i: (i, 0))
    return pl.pallas_call(
        body, name=name, grid=(lp // tr,),
        in_specs=[pl.BlockSpec((tr, d.qw), lambda i: (i, 0)), pl.BlockSpec((tr, LANES), lambda i: (i, kidx)), tab, tab],
        out_specs=[pl.BlockSpec((tr, d.qw), lambda i: (i, 0)), tab],
        out_shape=[jax.ShapeDtypeStruct((lp, d.qw), BF16), jax.ShapeDtypeStruct((lp, LANES), BF16)],
        compiler_params=_cparams(("parallel",)))(qraw, projb, ctab, stab)


def _mla_prep_bwd(dqm, dkr, ctab, stab, d, name):
    lp, nh = d.lp, d.mla_heads
    tr = _pick(lp, (256, 128))

    def unrope(g, c, s):
        return g * c + pltpu.roll(g * s, LANES // 2, 1)

    def body(dq_ref, dk_ref, c_ref, s_ref, o_ref, ok_ref):
        c, s = c_ref[...], s_ref[...]
        for h in range(nh):
            base = 2 * LANES * h
            o_ref[:, base:base + LANES] = dq_ref[:, base:base + LANES].astype(BF16)
            o_ref[:, base + LANES:base + 2 * LANES] = unrope(dq_ref[:, base + LANES:base + 2 * LANES], c, s).astype(BF16)
        ok_ref[...] = unrope(dk_ref[...], c, s).astype(BF16)

    tab = pl.BlockSpec((tr, LANES), lambda i: (i, 0))
    wide = pl.BlockSpec((tr, d.qw), lambda i: (i, 0))
    return pl.pallas_call(
        body, name=name, grid=(lp // tr,), in_specs=[wide, tab, tab, tab], out_specs=[wide, tab],
        out_shape=[jax.ShapeDtypeStruct((lp, d.qw), BF16), jax.ShapeDtypeStruct((lp, LANES), BF16)],
        compiler_params=_cparams(("parallel",)))(dqm, dkr, ctab, stab)


def _mla_fwd(qm, kv, kr, d, name, ride=()):
    nh, lp, t = d.mla_heads, d.lp, d.tq
    scale = (d.nope + d.rope) ** -0.5
    pad = d.pad
    nr = len(ride)

    def body(*refs):
        q_ref, kn_ref, v_ref, kr_ref = refs[:4]
        o_ref, lse_ref = refs[4 + nr:6 + nr]
        h = pl.program_id(0)
        i = pl.program_id(1)
        if nr:
            ride_refs = (refs[4:4 + nr], refs[6 + nr:6 + 2 * nr]) + tuple(refs[6 + 2 * nr:])

            @pl.when((h == 0) & (i == 0))
            def _():
                _Gather(*ride_refs).start()

        q = q_ref[...]
        row = i * t + lax.broadcasted_iota(jnp.int32, (t, t), 0)
        colb = lax.broadcasted_iota(jnp.int32, (t, t), 1)

        def make_step(masked):
            def step(j, carry):
                acc, m, l = carry
                off = pl.multiple_of(j * t, t)
                kc = jnp.concatenate([kn_ref[pl.ds(off, t), :], kr_ref[pl.ds(off, t), :]], axis=1)
                s = _dot_nt(q, kc) * scale
                if masked:
                    col = j * t + colb
                    s = jnp.where((col <= row) & (col >= pad), s, NEG)
                m_new = jnp.maximum(m, jnp.max(s, axis=1, keepdims=True))
                alpha = jnp.exp(m - m_new)
                p = jnp.exp(s - m_new)
                l = alpha * l + jnp.sum(p, axis=1, keepdims=True)
                acc = alpha * acc + jnp.dot(p.astype(BF16), v_ref[pl.ds(off, t), :], preferred_element_type=F32)
                return acc, m_new, l
            return step

        masked, plain = make_step(True), make_step(False)
        carry = masked(0, (jnp.zeros((t, LANES), F32), jnp.full((t, 1), NEG, F32), jnp.zeros((t, 1), F32)))
        carry = lax.fori_loop(1, i, plain, carry)
        acc, m, l = lax.fori_loop(i, i + jnp.minimum(i, 1), masked, carry)
        rowv = i * t + lax.broadcasted_iota(jnp.int32, (t, LANES), 0)
        o_ref[...] = jnp.where(rowv >= pad, acc / l, 0.0)
        lse_ref[...] = m + jnp.log(l)
        if nr:
            @pl.when((h == nh - 1) & (i == lp // t - 1))
            def _():
                _Gather(*ride_refs).finish()

    hbm = pl.BlockSpec(memory_space=pltpu.HBM)
    res = pl.pallas_call(
        body, name=name, grid=(nh, lp // t),
        in_specs=[pl.BlockSpec((t, 2 * LANES), lambda h, i: (i, h)),
                  pl.BlockSpec((lp, LANES), lambda h, i: (0, h)),
                  pl.BlockSpec((lp, LANES), lambda h, i: (0, nh + h)),
                  pl.BlockSpec((lp, LANES), lambda h, i: (0, 0))] + [hbm] * nr,
        out_specs=[pl.BlockSpec((t, LANES), lambda h, i: (i, h)),
                   pl.BlockSpec((None, t, 1), lambda h, i: (h, i, 0))] + [hbm] * nr,
        out_shape=[jax.ShapeDtypeStruct((lp, nh * LANES), F32), jax.ShapeDtypeStruct((nh, lp, 1), F32)]
                  + _gather_out_shapes(ride),
        scratch_shapes=_gather_sems(nr) if nr else [],
        compiler_params=_cparams(("arbitrary", "arbitrary")),
    )(qm, kv, kv, kr, *ride)
    return res[:2], list(res[2:])


def _mla_bwd(qm, kv, kr, o, do, lse, d, name, ride=()):
    nh, lp, t = d.mla_heads, d.lp, d.tq
    scale = (d.nope + d.rope) ** -0.5
    pad = d.pad
    nr = len(ride)

    def body(*refs):
        q_ref, kn_ref, v_ref, kr_ref, o_ref, do_ref, lse_ref = refs[:7]
        dq_ref, dkn_ref, dv_ref, dkr_ref = refs[7 + nr:11 + nr]
        h = pl.program_id(0)
        i = pl.program_id(1)
        if nr:
            ride_refs = (refs[7:7 + nr], refs[11 + nr:11 + 2 * nr]) + tuple(refs[11 + 2 * nr:])

            @pl.when((h == 0) & (i == 0))
            def _():
                _chip_copies_start(_chip_copies(*ride_refs))

        @pl.when(i == 0)
        def _():
            dkn_ref[...] = jnp.zeros_like(dkn_ref)
            dv_ref[...] = jnp.zeros_like(dv_ref)

        @pl.when((i == 0) & (h == 0))
        def _():
            dkr_ref[...] = jnp.zeros_like(dkr_ref)

        q = q_ref[...]
        dof = do_ref[...]
        dob = dof.astype(BF16)
        delta = jnp.sum(dof * o_ref[...], axis=1, keepdims=True)
        lse = lse_ref[...]
        row = i * t + lax.broadcasted_iota(jnp.int32, (t, t), 0)
        colb = lax.broadcasted_iota(jnp.int32, (t, t), 1)

        def make_step(masked):
            def step(j, dq):
                off = pl.multiple_of(j * t, t)
                kc = jnp.concatenate([kn_ref[pl.ds(off, t), :], kr_ref[pl.ds(off, t), :]], axis=1)
                v = v_ref[pl.ds(off, t), :]
                s = _dot_nt(q, kc) * scale
                if masked:
                    col = j * t + colb
                    mask = (col <= row) & (col >= pad)
                    p = jnp.where(mask, jnp.exp(jnp.where(mask, s, NEG) - lse), 0.0)
                else:
                    p = jnp.exp(s - lse)
                dp = _dot_nt(dob, v)
                dsb = (p * (dp - delta) * scale).astype(BF16)
                dq = dq + jnp.dot(dsb, kc, preferred_element_type=F32)
                dkc = _dot_tn(dsb, q)
                dkn_ref[pl.ds(off, t), :] += dkc[:, :LANES]
                dkr_ref[pl.ds(off, t), :] += dkc[:, LANES:]
                dv_ref[pl.ds(off, t), :] += _dot_tn(p.astype(BF16), dob)
                return dq
            return step

        masked, plain = make_step(True), make_step(False)
        dq = masked(0, jnp.zeros((t, 2 * LANES), F32))
        dq = lax.fori_loop(1, i, plain, dq)
        dq_ref[...] = lax.fori_loop(i, i + jnp.minimum(i, 1), masked, dq)
        if nr:
            @pl.when((h == nh - 1) & (i == lp // t - 1))
            def _():
                _chip_copies_wait(_chip_copies(*ride_refs))

    hbm = pl.BlockSpec(memory_space=pltpu.HBM)
    res = pl.pallas_call(
        body, name=name, grid=(nh, lp // t),
        in_specs=[pl.BlockSpec((t, 2 * LANES), lambda h, i: (i, h)),
                  pl.BlockSpec((lp, LANES), lambda h, i: (0, h), pipeline_mode=pl.Buffered(1)),
                  pl.BlockSpec((lp, LANES), lambda h, i: (0, nh + h), pipeline_mode=pl.Buffered(1)),
                  pl.BlockSpec((lp, LANES), lambda h, i: (0, 0), pipeline_mode=pl.Buffered(1)),
                  pl.BlockSpec((t, LANES), lambda h, i: (i, h)),
                  pl.BlockSpec((t, LANES), lambda h, i: (i, h)),
                  pl.BlockSpec((None, t, 1), lambda h, i: (h, i, 0))] + [hbm] * nr,
        out_specs=[pl.BlockSpec((t, 2 * LANES), lambda h, i: (i, h)),
                   pl.BlockSpec((lp, LANES), lambda h, i: (0, h)),
                   pl.BlockSpec((lp, LANES), lambda h, i: (0, h)),
                   pl.BlockSpec((lp, LANES), lambda h, i: (0, 0))] + [hbm] * nr,
        out_shape=[jax.ShapeDtypeStruct((lp, nh * 2 * LANES), F32), jax.ShapeDtypeStruct((lp, nh * LANES), F32),
                   jax.ShapeDtypeStruct((lp, nh * LANES), F32), jax.ShapeDtypeStruct((lp, LANES), F32)]
                  + [jax.ShapeDtypeStruct(x.shape, x.dtype) for x in ride],
        scratch_shapes=_chip_copies_sems(nr) if nr else [],
        compiler_params=_cparams(("arbitrary", "arbitrary"), V7X_VMEM_LIMIT_BIG),
    )(qm, kv, kv, kr, o, do, lse, *ride)
    return res[:4], list(res[4:])


def _sigmoid(x):
    return 0.5 * jnp.tanh(0.5 * x) + 0.5


def _gate_fwd(projb, b_sb, b_mla, d, name):
    lp, tg = d.lp, d.tg
    tr = _pick(lp, (256, 128))
    o1, o2 = d.g_off // tg, (d.g_off + d.d) // tg

    def body(g1_ref, g2_ref, b1_ref, b2_ref, o_ref):
        o_ref[...] = (_sigmoid(g1_ref[...]) * b1_ref[...] + _sigmoid(g2_ref[...]) * b2_ref[...]).astype(BF16)

    blk = pl.BlockSpec((tr, tg), lambda i, j: (i, j))
    return pl.pallas_call(
        body, name=name, grid=(lp // tr, d.d // tg),
        in_specs=[pl.BlockSpec((tr, tg), lambda i, j: (i, o1 + j)), pl.BlockSpec((tr, tg), lambda i, j: (i, o2 + j)),
                  blk, blk],
        out_specs=blk, out_shape=jax.ShapeDtypeStruct((lp, d.d), BF16),
        compiler_params=_cparams(("parallel", "parallel")))(projb, projb, b_sb, b_mla)


def _gate_bwd(dm, projb, b_sb, b_mla, d, name):
    lp, tg = d.lp, d.tg
    tr = _pick(lp, (256, 128))
    o1, o2 = d.g_off // tg, (d.g_off + d.d) // tg

    def body(dm_ref, g1_ref, g2_ref, b1_ref, b2_ref, db1_ref, db2_ref, dg1_ref, dg2_ref):
        dmv = dm_ref[...]
        s1, s2 = _sigmoid(g1_ref[...]), _sigmoid(g2_ref[...])
        db1_ref[...] = (dmv * s1).astype(BF16)
        db2_ref[...] = (dmv * s2).astype(BF16)
        dg1_ref[...] = (dmv * b1_ref[...] * s1 * (1.0 - s1)).astype(BF16)
        dg2_ref[...] = (dmv * b2_ref[...] * s2 * (1.0 - s2)).astype(BF16)

    blk = pl.BlockSpec((tr, tg), lambda i, j: (i, j))
    out = jax.ShapeDtypeStruct((lp, d.d), BF16)
    return pl.pallas_call(
        body, name=name, grid=(lp // tr, d.d // tg),
        in_specs=[blk, pl.BlockSpec((tr, tg), lambda i, j: (i, o1 + j)), pl.BlockSpec((tr, tg), lambda i, j: (i, o2 + j)),
                  blk, blk],
        out_specs=[blk] * 4, out_shape=[out] * 4,
        compiler_params=_cparams(("parallel", "parallel")))(dm, projb, projb, b_sb, b_mla)


HALO = 8


def _conv_tiles(d):
    return _pick(d.lp, (640, 512, 256, 128)), _pick(d.f, (512, 256, 128))


def _convglu_fwd(up, cw, cb, d, name):
    lp, f = d.lp, d.f
    tr, tc = _conv_tiles(d)
    nf = f // tc
    hb = tr // HALO
    pad = d.pad

    def body(a_ref, g_ref, pa_ref, pg_ref, wa_ref, wg_ref, ba_ref, bg_ref, o_ref, ua_ref, ug_ref, xa, xg):
        i = pl.program_id(1)
        keep = (i > 0).astype(F32)
        xa[0:HALO, :] = pa_ref[...] * keep
        xg[0:HALO, :] = pg_ref[...] * keep
        xa[HALO:, :] = a_ref[...]
        xg[HALO:, :] = g_ref[...]

        def conv(x, w_ref, b_ref):
            return (b_ref[...] + x[pl.ds(HALO - 2, tr), :] * w_ref[0:1, :] + x[pl.ds(HALO - 1, tr), :] * w_ref[1:2, :]
                    + x[pl.ds(HALO, tr), :] * w_ref[2:3, :])

        ua = conv(xa, wa_ref, ba_ref)
        ug = conv(xg, wg_ref, bg_ref)
        ua_ref[...] = ua
        ug_ref[...] = ug
        row = i * tr + lax.broadcasted_iota(jnp.int32, (tr, tc), 0)
        o_ref[...] = jnp.where(row >= pad, ua * _sigmoid(ua) * ug, 0.0).astype(BF16)

    prev = lambda j, i: (jnp.maximum(i * hb - 1, 0), j)
    prevg = lambda j, i: (jnp.maximum(i * hb - 1, 0), nf + j)
    main = pl.BlockSpec((tr, tc), lambda j, i: (i, j))
    return pl.pallas_call(
        body, name=name, grid=(nf, lp // tr),
        in_specs=[main, pl.BlockSpec((tr, tc), lambda j, i: (i, nf + j)),
                  pl.BlockSpec((HALO, tc), prev), pl.BlockSpec((HALO, tc), prevg),
                  pl.BlockSpec((3, tc), lambda j, i: (0, j)), pl.BlockSpec((3, tc), lambda j, i: (0, nf + j)),
                  pl.BlockSpec((1, tc), lambda j, i: (0, j)), pl.BlockSpec((1, tc), lambda j, i: (0, nf + j))],
        out_specs=[main, main, main],
        out_shape=[jax.ShapeDtypeStruct((lp, f), BF16), jax.ShapeDtypeStruct((lp, f), F32),
                   jax.ShapeDtypeStruct((lp, f), F32)],
        scratch_shapes=[pltpu.VMEM((tr + HALO, tc), F32), pltpu.VMEM((tr + HALO, tc), F32)],
        compiler_params=_cparams(("parallel", "arbitrary")))(up, up, up, up, cw, cw, cb, cb)


def _convglu_bwd(up, ua, ug, dact, cw, d, name):
    lp, f = d.lp, d.f
    tr, tc = _conv_tiles(d)
    nf = f // tc
    hb = tr // HALO
    nrow = lp // tr
    pad = d.pad
    te = tr + HALO

    def body(a_ref, g_ref, ua_ref, ug_ref, nua_ref, nug_ref, da_ref, nd_ref, wa_ref, wg_ref,
             oa_ref, og_ref, sa_ref, sg_ref, ya, yg):
        i = pl.program_id(1)

        @pl.when(i == 0)
        def _():
            sa_ref[...] = jnp.zeros_like(sa_ref)
            sg_ref[...] = jnp.zeros_like(sg_ref)

        def dconv(ua, ug, dact):
            sg = _sigmoid(ua)
            return dact * ug * (sg * (1.0 + ua * (1.0 - sg))), dact * (ua * sg)

        ya_v, yg_v = dconv(ua_ref[...], ug_ref[...], da_ref[...])
        ya[0:tr, :] = ya_v
        yg[0:tr, :] = yg_v
        ya[tr:, :], yg[tr:, :] = dconv(nua_ref[...], nug_ref[...], nd_ref[...] * (i < nrow - 1).astype(F32))
        row = i * tr + lax.broadcasted_iota(jnp.int32, (tr, tc), 0)

        def back(y, y0, x_ref, w_ref, o_ref, s_ref):
            y1, y2 = y[pl.ds(1, tr), :], y[pl.ds(2, tr), :]
            dup = y0 * w_ref[2:3, :] + y1 * w_ref[1:2, :] + y2 * w_ref[0:1, :]
            o_ref[...] = jnp.where(row >= pad, dup, 0.0).astype(BF16)
            xv = x_ref[...]
            for tap, yk in enumerate((y2, y1, y0)):
                s_ref[tap:tap + 1, :] += jnp.sum(yk * xv, axis=0, keepdims=True)
            s_ref[3:4, :] += jnp.sum(y0, axis=0, keepdims=True)

        back(ya, ya_v, a_ref, wa_ref, oa_ref, sa_ref)
        back(yg, yg_v, g_ref, wg_ref, og_ref, sg_ref)

    last8 = lp // HALO - 1
    nxt = pl.BlockSpec((HALO, tc), lambda j, i: (jnp.minimum((i + 1) * hb, last8), j))
    main = pl.BlockSpec((tr, tc), lambda j, i: (i, j))
    sums = pl.BlockSpec((8, tc), lambda j, i: (0, j))
    return pl.pallas_call(
        body, name=name, grid=(nf, nrow),
        in_specs=[main, pl.BlockSpec((tr, tc), lambda j, i: (i, nf + j)), main, main, nxt, nxt, main, nxt,
                  pl.BlockSpec((3, tc), lambda j, i: (0, j)), pl.BlockSpec((3, tc), lambda j, i: (0, nf + j))],
        out_specs=[main, main, sums, sums],
        out_shape=[jax.ShapeDtypeStruct((lp, f), BF16), jax.ShapeDtypeStruct((lp, f), BF16),
                   jax.ShapeDtypeStruct((8, f), F32), jax.ShapeDtypeStruct((8, f), F32)],
        scratch_shapes=[pltpu.VMEM((te, tc), F32), pltpu.VMEM((te, tc), F32)],
        compiler_params=_cparams(("parallel", "arbitrary")))(up, up, ua, ug, ua, ug, dact, dact, cw, cw)


def _head(h, add, target, g, d, name):
    lp, dm, t = d.lp, d.d, d.block
    inv_d = 1.0 / dm

    def body(h_ref, a_ref, t_ref, g_ref, dh_ref, dhb_ref, loss_ref, dg_ref):
        i = pl.program_id(0)

        @pl.when(i == 0)
        def _():
            dh_ref[...] = jnp.zeros_like(dh_ref)
            dhb_ref[...] = jnp.zeros_like(dhb_ref)
            loss_ref[...] = jnp.zeros_like(loss_ref)
            dg_ref[...] = jnp.zeros_like(dg_ref)

        @pl.when(i > 0)
        def _():
            x, gv = h_ref[...] + a_ref[...], g_ref[...]
            r = lax.rsqrt(jnp.mean(x * x, axis=1, keepdims=True) + EPS)
            xh = x * r
            err = xh * gv - t_ref[...]
            loss_ref[...] += 0.5 * inv_d * jnp.sum(err * err)
            dy = err * inv_d
            gy = dy * gv
            c = jnp.mean(gy * x, axis=1, keepdims=True)
            dh = r * gy - x * (r * r * r) * c
            dh_ref[...] = dh
            dhb_ref[...] = dh.astype(BF16)
            dg_ref[...] += jnp.sum(dy * xh, axis=0, keepdims=True)

    blk = pl.BlockSpec((t, dm), lambda i: (i, 0))
    return pl.pallas_call(
        body, name=name, grid=(lp // t,),
        in_specs=[blk, blk, pl.BlockSpec((t, dm), lambda i: (jnp.maximum(i - 1, 0), 0)),
                  pl.BlockSpec((1, dm), lambda i: (0, 0))],
        out_specs=[blk, blk, pl.BlockSpec((8, LANES), lambda i: (0, 0)), pl.BlockSpec((1, dm), lambda i: (0, 0))],
        out_shape=[jax.ShapeDtypeStruct((lp, dm), F32), jax.ShapeDtypeStruct((lp, dm), BF16),
                   jax.ShapeDtypeStruct((8, LANES), F32), jax.ShapeDtypeStruct((1, dm), F32)],
        compiler_params=_cparams(("arbitrary",)))(h, add, target, g.reshape(1, dm))


def _exchange(xs, *, scatter, name):
    n = len(xs)
    nf = len(FLIPS)

    def body(*refs):
        ins, outs = refs[:n], refs[n:2 * n]
        send_sems, recv_sems, loc_sems = refs[2 * n:]
        x, y, c = lax.axis_index("x"), lax.axis_index("y"), lax.axis_index("c")
        me = 4 * x + 2 * y + c
        sends, recvs, locs = [], [], []
        for a in range(n):
            src_me = ins[a].at[me] if scatter else ins[a]
            loc = pltpu.make_async_copy(src_me, outs[a].at[me], loc_sems.at[a])
            loc.start()
            locs.append(loc)
            for k, (fx, fy, fc) in enumerate(FLIPS):
                px, py, pc = x ^ fx, y ^ fy, c ^ fc
                peer = 4 * px + 2 * py + pc
                src = ins[a].at[peer] if scatter else ins[a]
                cp = pltpu.make_async_remote_copy(
                    src_ref=src, dst_ref=outs[a].at[me], send_sem=send_sems.at[a * nf + k],
                    recv_sem=recv_sems.at[a * nf + k], device_id=(px, py, pc), device_id_type=pl.DeviceIdType.MESH)
                cp.start()
                sends.append(cp)
                recvs.append(pltpu.make_async_remote_copy(
                    src_ref=src, dst_ref=outs[a].at[peer], send_sem=send_sems.at[a * nf + k],
                    recv_sem=recv_sems.at[a * nf + k], device_id=(px, py, pc), device_id_type=pl.DeviceIdType.MESH))
        for cp in recvs:
            cp.wait_recv()
        for cp in sends:
            cp.wait_send()
        for loc in locs:
            loc.wait()

    hbm = pl.BlockSpec(memory_space=pltpu.HBM)
    out_shape = [jax.ShapeDtypeStruct(((N_DEV,) + tuple(x.shape[1:])) if scatter else ((N_DEV,) + tuple(x.shape)), x.dtype)
                 for x in xs]
    return pl.pallas_call(
        body, name=name, in_specs=[hbm] * n, out_specs=[hbm] * n, out_shape=out_shape,
        scratch_shapes=[pltpu.SemaphoreType.DMA((n * nf,)), pltpu.SemaphoreType.DMA((n * nf,)),
                        pltpu.SemaphoreType.DMA((n,))],
    )(*xs)


def _gather_two_level(xs, name):
    n = len(xs)

    def body(*refs):
        gather = _Gather(refs[:n], refs[n:2 * n], *refs[2 * n:])
        gather.start()
        gather.finish()

    hbm = pl.BlockSpec(memory_space=pltpu.HBM)
    return pl.pallas_call(
        body, name=name, in_specs=[hbm] * n, out_specs=[hbm] * n,
        out_shape=_gather_out_shapes(xs), scratch_shapes=_gather_sems(n),
    )(*xs)


def _gather_out_shapes(xs):
    return [jax.ShapeDtypeStruct((N_DEV,) + tuple(x.shape), x.dtype) for x in xs]


def _gather_sems(n):
    return [pltpu.SemaphoreType.DMA((7 * n,)), pltpu.SemaphoreType.DMA((7 * n,)), pltpu.SemaphoreType.DMA((n,))]


class _Gather:
    def __init__(self, ins, outs, send_sems, recv_sems, loc_sems):
        self.ins, self.outs, self.sems = ins, outs, (send_sems, recv_sems, loc_sems)
        self.x, self.y, self.c = lax.axis_index("x"), lax.axis_index("y"), lax.axis_index("c")
        self.me, self.sibling = (self.x, self.y, self.c), (self.x, self.y, 1 - self.c)
        self.chips = [(1 - self.x, self.y), (self.x, 1 - self.y), (1 - self.x, 1 - self.y)]

    def slot(self, a, dev):
        return self.outs[a].at[4 * dev[0] + 2 * dev[1] + dev[2]]

    def copy(self, a, k, block, to, src=None):
        return pltpu.make_async_remote_copy(
            src_ref=self.slot(a, block) if src is None else src, dst_ref=self.slot(a, block),
            send_sem=self.sems[0].at[7 * a + k], recv_sem=self.sems[1].at[7 * a + k],
            device_id=to, device_id_type=pl.DeviceIdType.MESH)

    def local(self, a):
        return pltpu.make_async_copy(self.ins[a], self.slot(a, self.me), self.sems[2].at[a])

    def first(self):
        out = []
        for a in range(len(self.ins)):
            out.append(self.copy(a, 0, self.me, self.sibling, src=self.ins[a]))
            out += [self.copy(a, 1 + j, self.me, (*chip, self.c), src=self.ins[a]) for j, chip in enumerate(self.chips)]
        return out

    def start(self):
        for a in range(len(self.ins)):
            self.local(a).start()
        for cp in self.first():
            cp.start()

    def finish(self):
        n, c = len(self.ins), self.c
        passed = []
        for j, chip in enumerate(self.chips):
            for a in range(n):
                self.copy(a, 1 + j, (*chip, c), self.me).wait_recv()
                fwd = self.copy(a, 4 + j, (*chip, c), self.sibling)
                fwd.start()
                passed.append(fwd)
        for a in range(n):
            self.copy(a, 0, self.sibling, self.me).wait_recv()
            for j, chip in enumerate(self.chips):
                self.copy(a, 4 + j, (*chip, 1 - c), self.me).wait_recv()
        for cp in self.first() + passed:
            cp.wait_send()
        for a in range(n):
            self.local(a).wait()


def _pair_exchange(xs, name):
    n = len(xs)

    def body(*refs):
        ins, outs = refs[:n], refs[n:2 * n]
        send_sems, recv_sems = refs[2 * n:]
        x, y, c = lax.axis_index("x"), lax.axis_index("y"), lax.axis_index("c")
        copies = []
        for a in range(n):
            for q in range(4):
                cp = pltpu.make_async_remote_copy(
                    src_ref=ins[a].at[2 * q + (1 - c)], dst_ref=outs[a].at[q], send_sem=send_sems.at[4 * a + q],
                    recv_sem=recv_sems.at[4 * a + q], device_id=(x, y, 1 - c), device_id_type=pl.DeviceIdType.MESH)
                cp.start()
                copies.append(cp)
        for cp in copies:
            cp.wait_recv()
        for cp in copies:
            cp.wait_send()

    hbm = pl.BlockSpec(memory_space=pltpu.HBM)
    return pl.pallas_call(
        body, name=name, in_specs=[hbm] * n, out_specs=[hbm] * n,
        out_shape=[jax.ShapeDtypeStruct((4,) + tuple(x.shape[1:]), x.dtype) for x in xs],
        scratch_shapes=[pltpu.SemaphoreType.DMA((4 * n,)), pltpu.SemaphoreType.DMA((4 * n,))],
    )(*xs)


def _chip_exchange(xs, name):
    n = len(xs)

    def body(*refs):
        copies = _chip_copies(refs[:n], refs[n:2 * n], *refs[2 * n:])
        _chip_copies_start(copies)
        _chip_copies_wait(copies)

    hbm = pl.BlockSpec(memory_space=pltpu.HBM)
    return pl.pallas_call(
        body, name=name, in_specs=[hbm] * n, out_specs=[hbm] * n,
        out_shape=[jax.ShapeDtypeStruct(x.shape, x.dtype) for x in xs],
        scratch_shapes=_chip_copies_sems(n),
    )(*xs)


def _chip_copies_sems(n):
    return [pltpu.SemaphoreType.DMA((3 * n,)), pltpu.SemaphoreType.DMA((3 * n,)), pltpu.SemaphoreType.DMA((n,))]


def _chip_copies(ins, outs, send_sems, recv_sems, loc_sems):
    x, y, c = lax.axis_index("x"), lax.axis_index("y"), lax.axis_index("c")
    mine = 2 * x + y
    locs, sends, recvs = [], [], []
    for a in range(len(ins)):
        locs.append(pltpu.make_async_copy(ins[a].at[mine], outs[a].at[mine], loc_sems.at[a]))
        for k, (fx, fy) in enumerate([(1, 0), (0, 1), (1, 1)]):
            px, py = x ^ fx, y ^ fy
            peer = 2 * px + py
            sems = dict(send_sem=send_sems.at[3 * a + k], recv_sem=recv_sems.at[3 * a + k],
                        device_id=(px, py, c), device_id_type=pl.DeviceIdType.MESH)
            sends.append(pltpu.make_async_remote_copy(src_ref=ins[a].at[peer], dst_ref=outs[a].at[mine], **sems))
            recvs.append(pltpu.make_async_remote_copy(src_ref=ins[a].at[peer], dst_ref=outs[a].at[peer], **sems))
    return locs, sends, recvs


def _chip_copies_start(copies):
    locs, sends, _ = copies
    for cp in locs + sends:
        cp.start()


def _chip_copies_wait(copies):
    locs, sends, recvs = copies
    for cp in recvs:
        cp.wait_recv()
    for cp in sends:
        cp.wait_send()
    for cp in locs:
        cp.wait()


def _pair_sum(a, b, name):
    shape = a.shape
    cols = shape[-1]
    rows = a.size // cols
    a, b = a.reshape(rows, cols), b.reshape(rows, cols)
    tr = next((t for t in (512, 256, 128, 64, 32, 16) if rows % t == 0 and t * cols <= 2 * ADAMW_TILE_ELEMS), rows)

    def body(a_ref, b_ref, o_ref):
        o_ref[...] = (a_ref[...].astype(F32) + b_ref[...].astype(F32)).astype(o_ref.dtype)

    blk = pl.BlockSpec((tr, cols), lambda i: (i, 0))
    return pl.pallas_call(body, name=name, grid=(rows // tr,), in_specs=[blk, blk], out_specs=blk,
                          out_shape=jax.ShapeDtypeStruct((rows, cols), a.dtype),
                          compiler_params=_cparams(("parallel",)))(a, b).reshape(shape)


def _adamw(parts, w, m, v, name):
    shape = w.shape
    cols = shape[-1]
    rows = w.size // cols
    nparts = parts.shape[0]
    parts, w, m, v = parts.reshape(nparts, rows, cols), w.reshape(rows, cols), m.reshape(rows, cols), v.reshape(rows, cols)
    tr = next((t for t in (256, 128, 64, 32, 16) if rows % t == 0 and t * cols <= ADAMW_TILE_ELEMS), rows)
    c1 = 1.0 - ADAM_B1 ** ADAM_STEP
    c2 = 1.0 - ADAM_B2 ** ADAM_STEP

    def body(p_ref, w_ref, m_ref, v_ref, g_ref, d_ref, mo_ref, vo_ref):
        g = p_ref[0].astype(F32)
        for q in range(1, nparts):
            g = g + p_ref[q].astype(F32)
        mn = ADAM_B1 * m_ref[...] + (1.0 - ADAM_B1) * g
        vn = ADAM_B2 * v_ref[...] + (1.0 - ADAM_B2) * (g * g)
        g_ref[...] = g
        mo_ref[...] = mn
        vo_ref[...] = vn
        d_ref[...] = -ADAM_LR * ((mn / c1) / (jnp.sqrt(vn / c2) + ADAM_EPS) + ADAM_WD * w_ref[...])

    blk = pl.BlockSpec((tr, cols), lambda i: (i, 0))
    out = jax.ShapeDtypeStruct((rows, cols), F32)
    res = pl.pallas_call(
        body, name=name, grid=(rows // tr,),
        in_specs=[pl.BlockSpec((nparts, tr, cols), lambda i: (0, i, 0)), blk, blk, blk],
        out_specs=[blk] * 4, out_shape=[out] * 4, compiler_params=_cparams(("parallel",)))(parts, w, m, v)
    return [r.reshape(shape) for r in res]


BIG = ["w_in", "w_uq", "w_ukv", "w_sb_out", "w_mla_out", "w_o", "w_up", "w_down"]
ROW_SHARDED = {"w_o", "w_down"}
SMALL = ["conv_w", "meta_tokens"]
SHARDED = BIG + SMALL
REPL = ["norm_mix", "q_norm", "kv_norm", "norm_ffn", "conv_b", "final_norm"]


def _pack_rows(flat, row_mult):
    unit = PACK_W * row_mult
    total = -(-flat.shape[0] // unit) * unit
    return jnp.pad(flat, (0, total - flat.shape[0])).reshape(-1, PACK_W)


def _padded_cols(c):
    return -(-c // LANES) * LANES


def _pad_block(a, name):
    c = a.shape[-1]
    if name in ROW_SHARDED or c % LANES == 0:
        return a
    return jnp.pad(a, [(0, 0)] * (a.ndim - 1) + [(0, _padded_cols(c) - c)])


def _full_from_slots(slots, name, c):
    if name in ROW_SHARDED:
        return jnp.transpose(slots, (1, 0, 2, 3)).reshape(slots.shape[1], -1, slots.shape[3])
    return jnp.concatenate([slots[q][..., :c] for q in range(N_DEV)], axis=-1)


def _slots_from_full(full, name):
    if name in ROW_SHARDED:
        l, rr, n = full.shape
        return jnp.transpose(full.reshape(l, N_DEV, rr // N_DEV, n), (1, 0, 2, 3))
    c = full.shape[-1] // N_DEV
    return jnp.stack([_pad_block(full[..., q * c:(q + 1) * c], name) for q in range(N_DEV)])


def _swap_halves(t):
    half = t.shape[-1] // 2
    return jnp.concatenate([t[..., half:], t[..., :half]], axis=-1)


def _in_offsets(d):
    widths = (d.sbw, d.sbw, d.sbw, d.q_lora, d.kv_lora, d.rope, d.d, d.d)
    offs, o = [], 0
    for w in widths:
        offs.append((o, o + w))
        o += w
    return offs


def _prime_weights(w_in, w_uq, w_ukv, d):
    offs = _in_offsets(d)
    cols = lambda k: w_in[:, offs[k][0]:offs[k][1]]
    kr = cols(5)
    zpad = jnp.zeros((d.d, d.g_off - d.kr_off - LANES), w_in.dtype)
    w_inb = jnp.concatenate([cols(3), cols(4), kr, _swap_halves(kr), zpad, cols(6), cols(7)], axis=1)
    w_ina = w_in[:, :d.wa]
    uq = w_uq.reshape(d.q_lora, d.mla_heads, d.nope + d.rope)
    rope = uq[..., d.nope:]
    w_uq = jnp.concatenate([uq[..., :d.nope], rope, _swap_halves(rope)], axis=-1).reshape(d.q_lora, d.qw)
    ukv = w_ukv.reshape(d.kv_lora, d.mla_heads, d.nope + d.vdim)
    w_ukv = jnp.concatenate([ukv[..., :d.nope].reshape(d.kv_lora, -1), ukv[..., d.nope:].reshape(d.kv_lora, -1)], axis=1)
    return dict(w_ina=w_ina, w_inb=w_inb, w_in=jnp.concatenate([w_ina, w_inb], axis=1), w_uq=w_uq, w_ukv=w_ukv)


def _unprime_grads(g, d):
    gi = g["w_in"]
    b = gi[:, d.wa:]
    kr = b[:, d.kr_off:d.kr_off + d.rope] + _swap_halves(b[:, d.kr_off + d.rope:d.kr_off + 2 * d.rope])
    w_in = jnp.concatenate([gi[:, :d.wa], b[:, :d.kr_off], kr, b[:, d.g_off:]], axis=1)
    uq = g["w_uq"].reshape(d.q_lora, d.mla_heads, 2 * LANES)
    rope = uq[..., d.nope:d.nope + d.rope] + _swap_halves(uq[..., d.nope + d.rope:])
    w_uq = jnp.concatenate([uq[..., :d.nope], rope], axis=-1).reshape(d.q_lora, -1)
    hw = d.mla_heads * d.nope
    ukv = g["w_ukv"]
    w_ukv = jnp.concatenate([ukv[:, :hw].reshape(d.kv_lora, d.mla_heads, d.nope),
                             ukv[:, hw:].reshape(d.kv_lora, d.mla_heads, d.vdim)], axis=-1).reshape(d.kv_lora, -1)
    return dict(g, w_in=w_in, w_uq=w_uq, w_ukv=w_ukv)


def _layer_fwd(h, add, w, norm_mix, q_norm, kv_norm, norm_ffn, cw, cb, ctab, stab, d, tag, ride=(), arrived=None):
    s = {}
    if add is None:
        s["h"] = h
        s["hn"], s["r1"] = _norm_fwd(h, norm_mix, width=d.d, cidx=0, name=f"norm_mix_{tag}")
    else:
        s["h"], s["hn"], s["r1"] = _norm_fwd(h, norm_mix, width=d.d, cidx=0, add=add, name=f"norm_mix_{tag}")
    s["pa"] = _mm(s["hn"], w["w_ina"], out_dtype=BF16, name=f"proj_a_{tag}")
    s["pb"] = _mm(s["hn"], w["w_inb"], name=f"proj_b_{tag}")
    s["o_sb"], s["carry"] = _sb_fwd(s["pa"], d, f"sb_fwd_{tag}")
    s["cqn"], s["rq"] = _norm_fwd(s["pb"], q_norm, width=d.q_lora, cidx=0, name=f"norm_q_{tag}")
    s["ckn"], s["rk"] = _norm_fwd(s["pb"], kv_norm, width=d.kv_lora, cidx=d.q_lora // d.kv_lora, name=f"norm_kv_{tag}")
    qraw = _mm(s["cqn"], w["w_uq"], name=f"uq_{tag}")
    s["kv"] = _mm(s["ckn"], w["w_ukv"], out_dtype=BF16, name=f"ukv_{tag}")
    s["qm"], s["kr"] = _mla_prep_fwd(qraw, s["pb"], ctab, stab, d, f"mla_prep_{tag}")
    (s["o_mla"], s["lse"]), gathered = _mla_fwd(s["qm"], s["kv"], s["kr"], d, f"mla_fwd_{tag}", ride=ride)
    if arrived is not None:
        arrived(gathered)
    s["b_sb"] = _mm(s["o_sb"], w["w_sb_out"], name=f"sb_out_{tag}")
    s["b_mla"] = _mm(s["o_mla"], w["w_mla_out"], name=f"mla_out_{tag}")
    s["merged"] = _gate_fwd(s["pb"], s["b_sb"], s["b_mla"], d, f"gate_{tag}")
    mix = _mm(s["merged"], w["w_o"], name=f"w_o_{tag}")
    s["h1"], s["hn2"], s["r2"] = _norm_fwd(s["h"], norm_ffn, width=d.d, cidx=0, add=mix, name=f"norm_ffn_{tag}")
    s["up"] = _mm(s["hn2"], w["w_up"], name=f"w_up_{tag}")
    s["act"], s["ua"], s["ug"] = _convglu_fwd(s["up"], cw, cb, d, f"convglu_{tag}")
    ffn = _mm(s["act"], w["w_down"], name=f"w_down_{tag}")
    return s, ffn


def _layer_bwd(dh2, dh2b, s, w, norm_mix, q_norm, kv_norm, norm_ffn, cw, cb, ctab, stab, d, tag, ride=(),
               ride_more=None):
    g = {}
    dact = _mm(dh2b, w["w_down"], tb=True, name=f"d_act_{tag}")
    g["w_down"] = _mm(s["act"], dh2b, ta=True, name=f"g_w_down_{tag}")
    dup_a, dup_g, sums_a, sums_g = _convglu_bwd(s["up"], s["ua"], s["ug"], dact, cw, d, f"convglu_bwd_{tag}")
    dup = jnp.concatenate([dup_a, dup_g], axis=1)
    g["conv_w"] = jnp.concatenate([sums_a[0:3], sums_g[0:3]], axis=1)
    g["conv_b"] = jnp.concatenate([sums_a[3], sums_g[3]], axis=0)
    g["w_up"] = _mm(s["hn2"], dup, ta=True, name=f"g_w_up_{tag}")
    dhn2 = _mm(dup, w["w_up"], tb=True, name=f"d_hn2_{tag}")
    dh1, dh1b, g["norm_ffn"] = _norm_bwd(dhn2, s["h1"], s["r2"], norm_ffn, width=d.d, cidx=0, dres=dh2,
                                         name=f"norm_ffn_bwd_{tag}")
    dmerged = _mm(dh1b, w["w_o"], tb=True, name=f"d_merged_{tag}")
    g["w_o"] = _mm(s["merged"], dh1b, ta=True, name=f"g_w_o_{tag}")
    db_sb, db_mla, dg_sb, dg_mla = _gate_bwd(dmerged, s["pb"], s["b_sb"], s["b_mla"], d, f"gate_bwd_{tag}")
    do_sb = _mm(db_sb, w["w_sb_out"], tb=True, name=f"d_o_sb_{tag}")
    g["w_sb_out"] = _mm(s["o_sb"], db_sb, ta=True, name=f"g_w_sb_out_{tag}")
    do_mla = _mm(db_mla, w["w_mla_out"], tb=True, name=f"d_o_mla_{tag}")
    g["w_mla_out"] = _mm(s["o_mla"], db_mla, ta=True, name=f"g_w_mla_out_{tag}")
    dq_sb, dk_sb, dv_sb = _sb_bwd(s["pa"], do_sb, s["carry"], d, f"sb_bwd_{tag}")
    if ride_more is not None:
        ride = list(ride) + list(ride_more(g))
    (dqm, dkn, dv, dkr), rode = _mla_bwd(s["qm"], s["kv"], s["kr"], s["o_mla"], do_mla, s["lse"], d, f"mla_bwd_{tag}",
                                         ride=ride)
    dqraw, dkr128 = _mla_prep_bwd(dqm, dkr, ctab, stab, d, f"mla_prep_bwd_{tag}")
    dkv = jnp.concatenate([dkn.astype(BF16), dv.astype(BF16)], axis=1)
    dcqn = _mm(dqraw, w["w_uq"], tb=True, name=f"d_cq_{tag}")
    g["w_uq"] = _mm(s["cqn"], dqraw, ta=True, name=f"g_w_uq_{tag}")
    dckn = _mm(dkv, w["w_ukv"], tb=True, name=f"d_ckv_{tag}")
    g["w_ukv"] = _mm(s["ckn"], dkv, ta=True, name=f"g_w_ukv_{tag}")
    dcq, g["q_norm"] = _norm_bwd(dcqn, s["pb"], s["rq"], q_norm, width=d.q_lora, cidx=0, out_dtype=BF16, name=f"norm_q_bwd_{tag}")
    dckv, g["kv_norm"] = _norm_bwd(dckn, s["pb"], s["rk"], kv_norm, width=d.kv_lora, cidx=d.q_lora // d.kv_lora,
                                   out_dtype=BF16, name=f"norm_kv_bwd_{tag}")
    zpad = jnp.zeros((d.lp, d.g_off - d.kr_off - LANES), BF16)
    dproj = jnp.concatenate([dq_sb, dk_sb.astype(BF16), dv_sb.astype(BF16), dcq, dckv, dkr128, zpad, dg_sb, dg_mla], axis=1)
    g["w_in"] = _mm(s["hn"], dproj, ta=True, name=f"g_w_in_{tag}")
    dhn = _mm(dproj, w["w_in"], tb=True, name=f"d_hn_{tag}")
    dh, dhb, g["norm_mix"] = _norm_bwd(dhn, s["h"], s["r1"], norm_mix, width=d.d, cidx=0, dres=dh1,
                                       name=f"norm_mix_bwd_{tag}")
    return dh, dhb, g, rode


def _step(d, x, p, m, v, loss_target):
    x = x.reshape(d.seq, d.d)
    target = loss_target.reshape(d.seq, d.d)

    def block(n, layers=None):
        a = p[n] if layers is None else p[n][layers]
        return _pad_block(a, n).astype(BF16 if n in BIG else F32)

    def whole(n, slots):
        return _full_from_slots(slots, n, p[n].shape[-1])

    first_names = ["w_in", "w_uq", "w_ukv"]
    early = _gather_two_level([block(n, slice(0, 1)) for n in first_names] + [block(n) for n in SMALL], "gather_weights")
    full = {n: whole(n, g_) for n, g_ in zip(first_names + SMALL, early)}
    ws = [_prime_weights(*[full[n][0] for n in first_names], d)] + [dict() for _ in range(1, d.depth)]
    rest_names = [n for n in BIG if n not in first_names]
    late = [block(n, slice(1, d.depth)) for n in first_names] + [block(n) for n in rest_names]

    def arrived(gathered):
        later = {n: whole(n, g_) for n, g_ in zip(first_names, gathered)}
        rest = {n: whole(n, g_) for n, g_ in zip(rest_names, gathered[len(first_names):])}
        for l in range(d.depth):
            if l > 0:
                ws[l].update(_prime_weights(*[later[n][l - 1] for n in first_names], d))
            ws[l].update({n: rest[n][l] for n in rest_names})

    pos = jnp.arange(d.lp, dtype=F32) - d.pad
    half = d.rope // 2
    freqs = ROPE_THETA ** (-jnp.arange(half, dtype=F32) / half)
    ang = pos[:, None] * freqs[None, :]
    cos, sin = jnp.cos(ang), jnp.sin(ang)
    zero = jnp.zeros((d.lp, LANES - d.rope), F32)
    ctab = jnp.concatenate([cos, cos, zero], axis=1)
    stab = jnp.concatenate([-sin, sin, zero], axis=1)

    h = jnp.concatenate([jnp.zeros((d.pad, d.d), F32), full["meta_tokens"], x], axis=0)
    saved, add = [], None
    for l in range(d.depth):
        s, add = _layer_fwd(h, add, ws[l], p["norm_mix"][l], p["q_norm"][l], p["kv_norm"][l], p["norm_ffn"][l],
                            full["conv_w"][l], p["conv_b"][l].reshape(1, -1), ctab, stab, d, f"l{l}",
                            ride=late if l == 0 else (), arrived=arrived if l == 0 else None)
        saved.append(s)
        h = s["h1"]
    dh, dhb, loss_part, g_final = _head(h, add, target, p["final_norm"], d, "head")

    my_c = lax.axis_index("c")

    def pair_level(names, gfull, tag):
        gsend = [_slots_from_full(gfull[n].astype(BF16) if n in BIG else gfull[n], n) for n in names]
        theirs = _pair_exchange(gsend, f"scatter_grads_pair_{tag}")
        mine = [lax.dynamic_index_in_dim(g_.reshape((4, 2) + g_.shape[1:]), my_c, axis=1, keepdims=False) for g_ in gsend]
        return [_pair_sum(a_, b_, f"pair_sum_{n}_{tag}") for n, a_, b_ in zip(names, mine, theirs)]

    grads = [None] * d.depth
    recv_big = [None] * d.depth
    ride, ffn_names, rode_ffn = [], ["w_up", "w_down", "w_o", "w_sb_out", "w_mla_out"], []
    for l in reversed(range(d.depth)):
        more = None
        if l == 0 and ride:
            more = lambda g_: pair_level(ffn_names, {n: g_[n][None] for n in ffn_names}, "l0_ffn")
        dh, dhb, g, rode = _layer_bwd(dh, dhb, saved[l], ws[l], p["norm_mix"][l], p["q_norm"][l], p["kv_norm"][l],
                                      p["norm_ffn"][l], full["conv_w"][l], p["conv_b"][l].reshape(1, -1), ctab, stab,
                                      d, f"l{l}", ride=ride, ride_more=more)
        if ride:
            recv_big[l + 1] = rode[:len(BIG)]
        if more is not None:
            rode_ffn = rode[len(BIG):]
        grads[l] = _unprime_grads(g, d)
        if l > 0:
            ride = pair_level(BIG, {n: grads[l][n][None] for n in BIG}, f"l{l}")
    grad_x = dh[d.first_tok:].reshape(1, d.seq, d.d)

    rest0 = [n for n in BIG if not (rode_ffn and n in ffn_names)]
    gfull = {n: grads[0][n][None] for n in rest0}
    gfull["conv_w"] = jnp.stack([grads[l]["conv_w"] for l in range(d.depth)])
    gfull["meta_tokens"] = dh[d.pad:d.first_tok]
    last = list(_chip_exchange(pair_level(rest0 + SMALL, gfull, "l0"), "scatter_grads_chips"))
    got0 = dict(zip(rest0, last[:len(rest0)]))
    got0.update(zip(ffn_names, rode_ffn))
    recv_big[0] = [got0[n] for n in BIG]
    grecv = [jnp.concatenate([recv_big[l][k] for l in range(d.depth)], axis=1) for k in range(len(BIG))] + last[len(rest0):]
    outs_sh = {n: _adamw(r_, _pad_block(p[n], n), _pad_block(m[n], n), _pad_block(v[n], n), f"adamw_{n}")
               for n, r_ in zip(SHARDED, grecv)}

    grep = {n: jnp.stack([grads[l][n].reshape(-1) for l in range(d.depth)]) for n in REPL if n != "final_norm"}
    grep["final_norm"] = g_final.reshape(-1)
    rflat = jnp.concatenate([grep[n].reshape(-1) for n in REPL] + [loss_part[0, 0:1]])
    (rparts,) = _exchange([_pack_rows(rflat, 8)], scatter=False, name="gather_small_grads")
    packr = lambda t: _pack_rows(jnp.concatenate([t[n].reshape(-1) for n in REPL] + [jnp.zeros((1,), F32)]), 8)
    outs_rp = _adamw(rparts, packr(p), packr(m), packr(v), "adamw_replicated")

    def unpack(flat, names, extra=0):
        res, off = {}, 0
        flat = flat.reshape(-1)
        for n in names:
            res[n] = flat[off:off + p[n].size].reshape(p[n].shape)
            off += p[n].size
        return res, flat[off:off + extra]

    results = []
    loss = None
    for k in range(4):
        sh = {n: outs_sh[n][k][..., :p[n].shape[-1]] for n in SHARDED}
        rp, tail = unpack(outs_rp[k], REPL, 1)
        if k == 0:
            loss = tail[0]
        results.append({**sh, **rp})
    return loss, grad_x, results


WEIGHTS = ["meta_tokens", "norm_mix", "w_in", "q_norm", "w_uq", "kv_norm", "w_ukv", "w_sb_out", "w_mla_out", "w_o",
           "norm_ffn", "w_up", "conv_w", "conv_b", "w_down", "final_norm"]


def _run(d, x, weights, loss_target, moments_m, moments_v):
    p = dict(zip(WEIGHTS, weights))
    m = dict(zip(WEIGHTS, moments_m))
    v = dict(zip(WEIGHTS, moments_v))
    loss, grad_x, res = _step(d, x, p, m, v, loss_target)
    out = [loss, grad_x]
    for k in range(4):
        out += [res[k][n] for n in WEIGHTS]
    return tuple(out)


def kernel(x, meta_tokens, norm_mix, w_in, q_norm, w_uq, kv_norm, w_ukv, w_sb_out, w_mla_out, w_o, norm_ffn, w_up, conv_w, conv_b, w_down, final_norm, loss_target, m_meta_tokens, m_norm_mix, m_w_in, m_q_norm, m_w_uq, m_kv_norm, m_w_ukv, m_w_sb_out, m_w_mla_out, m_w_o, m_norm_ffn, m_w_up, m_conv_w, m_conv_b, m_w_down, m_final_norm, v_meta_tokens, v_norm_mix, v_w_in, v_q_norm, v_w_uq, v_kv_norm, v_w_ukv, v_w_sb_out, v_w_mla_out, v_w_o, v_norm_ffn, v_w_up, v_conv_w, v_conv_b, v_w_down, v_final_norm):
    weights = [meta_tokens, norm_mix, w_in, q_norm, w_uq, kv_norm, w_ukv, w_sb_out, w_mla_out, w_o, norm_ffn, w_up,
               conv_w, conv_b, w_down, final_norm]
    ms = [m_meta_tokens, m_norm_mix, m_w_in, m_q_norm, m_w_uq, m_kv_norm, m_w_ukv, m_w_sb_out, m_w_mla_out, m_w_o,
          m_norm_ffn, m_w_up, m_conv_w, m_conv_b, m_w_down, m_final_norm]
    vs = [v_meta_tokens, v_norm_mix, v_w_in, v_q_norm, v_w_uq, v_kv_norm, v_w_ukv, v_w_sb_out, v_w_mla_out, v_w_o,
          v_norm_ffn, v_w_up, v_conv_w, v_conv_b, v_w_down, v_final_norm]
    return _run(PROD, x, weights, loss_target, ms, vs)
```
